```python
import jax, jax.numpy as jnp
from jax import lax
import numpy as np

D_MODEL = 1024
BATCH = 8
SEQ = 2048
DEPTH = 1
DEC_BATCH = 128
DEC_SEQ = 1
PAST_LEN = 16384
PAGE_SIZE = 128

N_META = 16
D_RNN = D_MODEL
N_GATE_BLOCKS = 16
GATE_BW = D_RNN // N_GATE_BLOCKS
CONV_W = 4
LRU_C = 8.0
N_HEADS = 16
HEAD_DIM = 64
N_KV = 2
GROUP = N_HEADS // N_KV
D_ATTN = N_HEADS * HEAD_DIM
D_KV = N_KV * HEAD_DIM
WINDOW = 128
BLOCK = 128
OFF_Q = D_RNN
OFF_K = OFF_Q + D_ATTN
OFF_V = OFF_K + D_KV
OFF_GR = OFF_V + D_KV
OFF_GA = OFF_GR + D_RNN
D_IN = OFF_GA + D_ATTN
D_MIX = D_RNN + D_ATTN
N_EXPERTS = 32
TOP_K = 4
D_FF = D_MODEL
SWIGLU_LIMIT = 7.0
SWIGLU_ALPHA = 1.702
MOE_BLOCK = 128
EPS = 1e-6

kernel_name = "hawk_swa_sink_moe_step"


def rmsnorm(x, g):
    xf = x.astype(jnp.float32)
    y = xf * lax.rsqrt(jnp.mean(xf * xf, axis=-1, keepdims=True) + EPS)
    return (y * g.astype(jnp.float32)).astype(x.dtype)


def alibi_slopes():
    return 2.0 ** (-8.0 * jnp.arange(1, N_HEADS + 1, dtype=jnp.float32) / N_HEADS)


def split_in(h, w_in):
    z = h @ w_in
    return (z[..., :OFF_Q], z[..., OFF_Q:OFF_K], z[..., OFF_K:OFF_V], z[..., OFF_V:OFF_GR], z[..., OFF_GR:OFF_GA], z[..., OFF_GA:D_IN])


def causal_conv(x_ext, w, b):
    T = x_ext.shape[1] - (CONV_W - 1)
    out = b
    for j in range(CONV_W):
        out = out + w[j] * x_ext[:, j:j + T]
    return out


def rg_lru(x, is_first, h0, w_rg, b_rg, w_ig, b_ig, lam):
    f32 = jnp.float32
    xf = x.astype(f32)
    xb = xf.reshape(xf.shape[:-1] + (N_GATE_BLOCKS, GATE_BW))
    r = jax.nn.sigmoid(jnp.einsum('btnc,ncd->btnd', xb, w_rg.astype(f32)).reshape(xf.shape) + b_rg.astype(f32))
    i = jax.nn.sigmoid(jnp.einsum('btnc,ncd->btnd', xb, w_ig.astype(f32)).reshape(xf.shape) + b_ig.astype(f32))
    log_a = -LRU_C * r * jax.nn.softplus(-lam.astype(f32))
    a = jnp.exp(log_a)
    mult = jnp.sqrt(-jnp.expm1(2.0 * log_a))
    mult = jnp.where(is_first[None, :, None], jnp.ones_like(mult), mult)
    b = mult * i * xf

    def step(h, inp):
        a_t, b_t = inp
        h_new = a_t * h + b_t
        return h_new, h_new

    h_last, hs = lax.scan(step, h0.astype(f32), (jnp.swapaxes(a, 0, 1), jnp.swapaxes(b, 0, 1)))
    return jnp.swapaxes(hs, 0, 1).astype(x.dtype), h_last


def window_attend(q, k, v, qpos, kpos, sinks):
    f32 = jnp.float32
    B, N, Q = q.shape[0], q.shape[1], q.shape[2]
    qg = q.reshape(B, N, Q, N_KV, GROUP, HEAD_DIM).astype(f32)
    s = jnp.einsum('bnqkgd,bnskd->bnkgqs', qg, k.astype(f32)) * (HEAD_DIM ** -0.5)
    dist = qpos[:, :, None] - kpos[:, None, :]
    mask = (kpos[:, None, :] >= 0) & (dist >= 0) & (dist < WINDOW)
    slopes = alibi_slopes().reshape(N_KV, GROUP)
    s = s - slopes[None, None, :, :, None, None] * dist.astype(f32)[None, :, None, None]
    s = jnp.where(mask[None, :, None, None], s, -jnp.inf)
    sink = jnp.broadcast_to(sinks.astype(f32).reshape(N_KV, GROUP)[None, None, :, :, None, None], s.shape[:-1] + (1,))
    p = jax.nn.softmax(jnp.concatenate([s, sink], axis=-1), axis=-1)[..., :-1]
    o = jnp.einsum('bnkgqs,bnskd->bnqkgd', p, v.astype(f32))
    return o.reshape(B, N, Q, D_ATTN).astype(q.dtype)


def to_blocks(t, pad_front, pad_end):
    tp = jnp.pad(t, ((0, 0), (pad_front, pad_end), (0, 0), (0, 0)))
    return tp.reshape((t.shape[0], -1, BLOCK) + tuple(t.shape[2:]))


def with_prev_block(tb):
    prev = jnp.concatenate([jnp.zeros_like(tb[:, :1]), tb[:, :-1]], axis=1)
    return jnp.concatenate([prev, tb], axis=2)


def merge_branches(y_rnn, y_attn, g_rnn, g_attn, w_out):
    m = jnp.concatenate([jax.nn.sigmoid(g_rnn) * y_rnn, jax.nn.sigmoid(g_attn) * y_attn], axis=-1)
    return m @ w_out


def moe(h, w_router, b_router, w_up, b_up, w_down, b_down):
    shp = h.shape
    t = h.reshape(-1, D_MODEL)
    T = t.shape[0]
    logits = (t @ w_router + b_router).astype(jnp.float32)
    top_v, top_i = lax.top_k(logits, TOP_K)
    wts = jax.nn.softmax(top_v, axis=-1)
    A = T * TOP_K
    flat_e = top_i.reshape(-1).astype(jnp.int32)
    flat_tok = jnp.arange(A, dtype=jnp.int32) // TOP_K
    flat_w = wts.reshape(-1)
    order = jnp.argsort(flat_e)
    e_sorted = flat_e[order]
    counts = jnp.zeros((N_EXPERTS,), jnp.int32).at[flat_e].add(1)
    padded = ((counts + (MOE_BLOCK - 1)) // MOE_BLOCK) * MOE_BLOCK
    ends = jnp.cumsum(padded).astype(jnp.int32)
    start_padded = ends - padded
    start_sorted = (jnp.cumsum(counts) - counts).astype(jnp.int32)
    rank = jnp.arange(A, dtype=jnp.int32) - start_sorted[e_sorted]
    dest = start_padded[e_sorted] + rank
    n_blk = -(-A // MOE_BLOCK) + N_EXPERTS
    n_rows = n_blk * MOE_BLOCK
    buf_tok = jnp.full((n_rows,), T, dtype=jnp.int32).at[dest].set(flat_tok[order])
    buf_w = jnp.zeros((n_rows,), jnp.float32).at[dest].set(flat_w[order])
    blk_start = jnp.arange(n_blk, dtype=jnp.int32) * MOE_BLOCK
    blk_expert = jnp.minimum(jnp.sum((ends[None, :] <= blk_start[:, None]).astype(jnp.int32), axis=1), N_EXPERTS - 1)
    t_pad = jnp.concatenate([t, jnp.zeros((1, D_MODEL), t.dtype)], axis=0)
    xin = t_pad[buf_tok].reshape(n_blk, MOE_BLOCK, D_MODEL)

    def run(args):
        xb, e = args
        z = xb @ w_up[e] + b_up[e]
        x_glu = jnp.minimum(z[:, ::2], SWIGLU_LIMIT)
        x_lin = jnp.clip(z[:, 1::2], -SWIGLU_LIMIT, SWIGLU_LIMIT)
        y = x_glu * jax.nn.sigmoid(SWIGLU_ALPHA * x_glu) * (x_lin + 1.0)
        return y @ w_down[e] + b_down[e]

    yb = lax.map(run, (xin, blk_expert)).reshape(n_rows, D_MODEL)
    out = jnp.zeros((T + 1, D_MODEL), jnp.float32).at[buf_tok].add(yb.astype(jnp.float32) * buf_w[:, None])[:T]
    return out.astype(h.dtype).reshape(shp)


def setup_inputs(seed: int = 0) -> dict:
    key = jax.random.key(seed)
    ks = jax.random.split(key, 28)
    f32 = jnp.float32

    def nrm(k, shape, scale):
        return jax.random.normal(k, shape, f32) * scale

    w_buf = min(WINDOW, PAST_LEN)
    a0 = jax.random.uniform(ks[15], (DEPTH, D_RNN), f32, 0.9, 0.999)
    return {
        "x_prompt": nrm(ks[0], (BATCH, SEQ, D_MODEL), 1.0),
        "x_sample": nrm(ks[1], (DEC_BATCH, DEC_SEQ, D_MODEL), 1.0),
        "cache_k": nrm(ks[2], (DEPTH, DEC_BATCH, w_buf, N_KV, HEAD_DIM), 1.0),
        "cache_v": nrm(ks[3], (DEPTH, DEC_BATCH, w_buf, N_KV, HEAD_DIM), 1.0),
        "state_conv": nrm(ks[4], (DEPTH, DEC_BATCH, CONV_W - 1, D_RNN), 1.0),
        "state_h": nrm(ks[5], (DEPTH, DEC_BATCH, D_RNN), 0.5),
        "meta_tokens": nrm(ks[6], (N_META, D_MODEL), 1.0),
        "norm_mix": 1.0 + nrm(ks[7], (DEPTH, D_MODEL), 0.02),
        "w_in": nrm(ks[8], (DEPTH, D_MODEL, D_IN), D_MODEL ** -0.5),
        "conv_w": nrm(ks[9], (DEPTH, CONV_W, D_RNN), CONV_W ** -0.5),
        "conv_b": nrm(ks[10], (DEPTH, D_RNN), 0.01),
        "w_rg": nrm(ks[11], (DEPTH, N_GATE_BLOCKS, GATE_BW, GATE_BW), GATE_BW ** -0.5),
        "b_rg": nrm(ks[12], (DEPTH, D_RNN), 0.01),
        "w_ig": nrm(ks[13], (DEPTH, N_GATE_BLOCKS, GATE_BW, GATE_BW), GATE_BW ** -0.5),
        "b_ig": nrm(ks[14], (DEPTH, D_RNN), 0.01),
        "lru_lambda": jnp.log(a0) - jnp.log1p(-a0),
        "attn_sinks": nrm(ks[16], (DEPTH, N_HEADS), 0.5),
        "w_out": nrm(ks[17], (DEPTH, D_MIX, D_MODEL), D_MIX ** -0.5),
        "norm_ffn": 1.0 + nrm(ks[18], (DEPTH, D_MODEL), 0.02),
        "w_router": nrm(ks[19], (DEPTH, D_MODEL, N_EXPERTS), D_MODEL ** -0.5),
        "b_router": nrm(ks[20], (DEPTH, N_EXPERTS), 0.01),
        "w_up": nrm(ks[21], (DEPTH, N_EXPERTS, D_MODEL, 2 * D_FF), D_MODEL ** -0.5),
        "b_up": nrm(ks[22], (DEPTH, N_EXPERTS, 2 * D_FF), 0.01),
        "w_down": nrm(ks[23], (DEPTH, N_EXPERTS, D_FF, D_MODEL), D_FF ** -0.5),
        "b_down": nrm(ks[24], (DEPTH, N_EXPERTS, D_MODEL), 0.01),
        "norm_final": 1.0 + nrm(ks[25], (D_MODEL,), 0.02),
    }


def reference(x_prompt, x_sample, cache_k, cache_v, state_conv, state_h, meta_tokens, norm_mix, w_in, conv_w, conv_b, w_rg, b_rg, w_ig, b_ig, lru_lambda, attn_sinks, w_out, norm_ffn, w_router, b_router, w_up, b_up, w_down, b_down, norm_final):
    B_p = x_prompt.shape[0]
    B_s, T_s = x_sample.shape[0], x_sample.shape[1]
    L = N_META + x_prompt.shape[1]
    xp = jnp.concatenate([jnp.broadcast_to(meta_tokens.astype(x_prompt.dtype)[None], (B_p, N_META, D_MODEL)), x_prompt], axis=1)
    xs = x_sample

    pad_front = (-N_META) % BLOCK
    pad_end = (-(pad_front + L)) % BLOCK
    Lp = pad_front + L + pad_end
    qpos_p = (jnp.arange(Lp, dtype=jnp.int32) - pad_front).reshape(Lp // BLOCK, BLOCK)
    kpos_p = jnp.concatenate([qpos_p - BLOCK, qpos_p], axis=1)
    first_p = jnp.arange(L, dtype=jnp.int32) == 0
    w_p = min(WINDOW, L)

    w_buf = cache_k.shape[2]
    qpos_s = (PAST_LEN + jnp.arange(T_s, dtype=jnp.int32))[None]
    kpos_s = jnp.concatenate([PAST_LEN - w_buf + jnp.arange(w_buf, dtype=jnp.int32), qpos_s[0]])[None]
    first_s = jnp.zeros((T_s,), dtype=bool)

    nk_p, nv_p, nc_p, nh_p = [], [], [], []
    nk_s, nv_s, nc_s, nh_s = [], [], [], []
    for l in range(DEPTH):
        h = rmsnorm(xp, norm_mix[l])
        xr, q, k, v, g_rnn, g_attn = split_in(h, w_in[l])
        conv_in = jnp.pad(xr, ((0, 0), (CONV_W - 1, 0), (0, 0)))
        y_rnn, h_last = rg_lru(causal_conv(conv_in, conv_w[l], conv_b[l]), first_p, jnp.zeros((B_p, D_RNN), jnp.float32), w_rg[l], b_rg[l], w_ig[l], b_ig[l], lru_lambda[l])
        k4 = k.reshape(B_p, L, N_KV, HEAD_DIM)
        v4 = v.reshape(B_p, L, N_KV, HEAD_DIM)
        qb = to_blocks(q.reshape(B_p, L, N_HEADS, HEAD_DIM), pad_front, pad_end)
        kw = with_prev_block(to_blocks(k4, pad_front, pad_end))
        vw = with_prev_block(to_blocks(v4, pad_front, pad_end))
        y_attn = window_attend(qb, kw, vw, qpos_p, kpos_p, attn_sinks[l]).reshape(B_p, Lp, D_ATTN)[:, pad_front:pad_front + L]
        xp = xp + merge_branches(y_rnn, y_attn, g_rnn, g_attn, w_out[l])
        xp = xp + moe(rmsnorm(xp, norm_ffn[l]), w_router[l], b_router[l], w_up[l], b_up[l], w_down[l], b_down[l])
        nk_p.append(k4[:, L - w_p:])
        nv_p.append(v4[:, L - w_p:])
        nc_p.append(conv_in[:, -(CONV_W - 1):])
        nh_p.append(h_last)

        h = rmsnorm(xs, norm_mix[l])
        xr, q, k, v, g_rnn, g_attn = split_in(h, w_in[l])
        conv_in = jnp.concatenate([state_conv[l].astype(xr.dtype), xr], axis=1)
        y_rnn, h_last = rg_lru(causal_conv(conv_in, conv_w[l], conv_b[l]), first_s, state_h[l], w_rg[l], b_rg[l], w_ig[l], b_ig[l], lru_lambda[l])
        k_all = jnp.concatenate([cache_k[l].astype(k.dtype), k.reshape(B_s, T_s, N_KV, HEAD_DIM)], axis=1)
        v_all = jnp.concatenate([cache_v[l].astype(v.dtype), v.reshape(B_s, T_s, N_KV, HEAD_DIM)], axis=1)
        y_attn = window_attend(q.reshape(B_s, 1, T_s, N_HEADS, HEAD_DIM), k_all[:, None], v_all[:, None], qpos_s, kpos_s, attn_sinks[l]).reshape(B_s, T_s, D_ATTN)
        xs = xs + merge_branches(y_rnn, y_attn, g_rnn, g_attn, w_out[l])
        xs = xs + moe(rmsnorm(xs, norm_ffn[l]), w_router[l], b_router[l], w_up[l], b_up[l], w_down[l], b_down[l])
        nk_s.append(k_all[:, -w_buf:])
        nv_s.append(v_all[:, -w_buf:])
        nc_s.append(conv_in[:, -(CONV_W - 1):])
        nh_s.append(h_last)

    y_prompt = rmsnorm(xp, norm_final)[:, N_META:]
    y_sample = rmsnorm(xs, norm_final)
    return (y_prompt, y_sample, jnp.stack(nk_p), jnp.stack(nv_p), jnp.stack(nc_p), jnp.stack(nh_p), jnp.stack(nk_s), jnp.stack(nv_s), jnp.stack(nc_s), jnp.stack(nh_s))
```

```python
import functools

import jax
import jax.numpy as jnp
from jax import lax
from jax.experimental import pallas as pl
from jax.experimental.pallas import tpu as pltpu

F32 = jnp.float32
BF16 = jnp.bfloat16

D_MODEL = 1024
N_META = 16
D_RNN = 1024
N_GATE_BLOCKS = 16
GATE_BW = D_RNN // N_GATE_BLOCKS
CONV_W = 4
LRU_C = 8.0
N_HEADS = 16
HEAD_DIM = 64
N_KV = 2
GROUP = N_HEADS // N_KV
D_ATTN = N_HEADS * HEAD_DIM
D_KV = N_KV * HEAD_DIM
WINDOW = 128
BLOCK = 128
PAST_LEN = 16384
N_EXPERTS = 32
TOP_K = 4
D_FF = 1024
SWIGLU_LIMIT = 7.0
SWIGLU_ALPHA = 1.702
EPS = 1e-6

GATE_TILE = 256
N_GATE_TILES = D_RNN // GATE_TILE
LANES = 128
NEG_BIG = -1e30

TM_IN = 256
TT_RNN = 64
TM_MERGE = 256
TM_MOE = 256
TM_FIN = 128
VMEM_LIMIT = 56 * 1024 * 1024


def _cparams(n_grid_dims):
    return pltpu.CompilerParams(
        dimension_semantics=("arbitrary",) * n_grid_dims, vmem_limit_bytes=VMEM_LIMIT)


def _full(shape):
    return pl.BlockSpec(shape, lambda *_: (0,) * len(shape))


def _in_proj_kernel(x_ref, g_ref, w_ref, xr_ref, q_ref, kv_ref, gr_ref, ga_ref):
    x = x_ref[...]
    ms = jnp.mean(x * x, axis=-1, keepdims=True)
    h = (x * lax.rsqrt(ms + EPS) * g_ref[...]).astype(BF16)

    def proj(lo, hi):
        return jnp.dot(h, w_ref[:, lo:hi], preferred_element_type=F32)

    o_q = D_RNN
    o_k = o_q + D_ATTN
    o_gr = o_k + 2 * D_KV
    o_ga = o_gr + D_RNN
    xr_ref[...] = proj(0, o_q)
    q_ref[...] = (proj(o_q, o_k) * (HEAD_DIM ** -0.5)).astype(BF16)
    kv_ref[...] = proj(o_k, o_gr)
    gr_ref[...] = proj(o_gr, o_ga).astype(BF16)
    ga_ref[...] = proj(o_ga, o_ga + D_ATTN).astype(BF16)


def _in_proj(x, g, w_in_bf, tm):
    rows = x.shape[0]
    d_in = w_in_bf.shape[1]
    row = lambda w: pl.BlockSpec((tm, w), lambda i: (i, 0))
    return pl.pallas_call(
        _in_proj_kernel,
        grid=(rows // tm,),
        in_specs=[row(D_MODEL), _full((1, D_MODEL)), _full((D_MODEL, d_in))],
        out_specs=[row(D_RNN), row(D_ATTN), row(2 * D_KV), row(D_RNN), row(D_ATTN)],
        out_shape=[
            jax.ShapeDtypeStruct((rows, D_RNN), F32),
            jax.ShapeDtypeStruct((rows, D_ATTN), BF16),
            jax.ShapeDtypeStruct((rows, 2 * D_KV), F32),
            jax.ShapeDtypeStruct((rows, D_RNN), BF16),
            jax.ShapeDtypeStruct((rows, D_ATTN), BF16),
        ],
        compiler_params=_cparams(1),
        name="in_proj",
    )(x, g, w_in_bf)


def _softplus(x):
    return jnp.maximum(x, 0.0) + jnp.log1p(jnp.exp(-jnp.abs(x)))


def _lru_coeffs(u, wg_ref, brg, big, lam):
    ub = u.astype(BF16)
    sp = _softplus(-lam)
    a_parts, i_parts, m_parts = [], [], []
    for g in range(N_GATE_TILES):
        sl = slice(g * GATE_TILE, (g + 1) * GATE_TILE)
        zz = jnp.dot(ub[:, sl], wg_ref[g], preferred_element_type=F32)
        r = jax.nn.sigmoid(zz[:, :GATE_TILE] + brg[:, sl])
        i = jax.nn.sigmoid(zz[:, GATE_TILE:] + big[:, sl])
        log_a = (-LRU_C) * r * sp[:, sl]
        a = jnp.exp(log_a)
        m = jnp.sqrt(1.0 - a * a)
        a_parts.append(a)
        i_parts.append(i)
        m_parts.append(m)
    cat = lambda ps: jnp.concatenate(ps, axis=-1)
    return cat(a_parts), cat(i_parts), cat(m_parts)


def _conv(ext, t, cw, cb):
    out = cb
    for j in range(CONV_W):
        s = CONV_W - 1 - j
        out = out + cw[j:j + 1, :] * ext[8 - s:8 - s + t, :]
    return out


def _rnn_meta_kernel(xr_ref, cw_ref, cb_ref, wg_ref, brg_ref, big_ref, lam_ref, h_ref):
    x = xr_ref[...]
    ext = jnp.concatenate([jnp.zeros((8, D_RNN), F32), x], axis=0)
    u = _conv(ext, N_META, cw_ref[...], cb_ref[...])
    a, i, m = _lru_coeffs(u, wg_ref, brg_ref[...], big_ref[...], lam_ref[...])
    first = lax.broadcasted_iota(jnp.int32, (N_META, 1), 0) == 0
    b = jnp.where(first, 1.0, m) * i * u
    h = jnp.zeros((1, D_RNN), F32)
    for t in range(N_META):
        h = a[t:t + 1, :] * h + b[t:t + 1, :]
    h_ref[...] = h


def _rnn_prompt_kernel(xr_ref, gr_ref, xm_ref, h0_ref, cw_ref, cb_ref, wg_ref, brg_ref, big_ref,
                       lam_ref, m_ref, hl_ref, halo_s, a_s, b_s, y_s, h_s, *, n_b, tt):
    j = pl.program_id(0)

    n_l = D_RNN // LANES
    lane_tile = lambda l: slice(l * LANES, (l + 1) * LANES)

    @pl.when(j == 0)
    def _():
        for l in range(n_l):
            h_s[l] = jnp.broadcast_to(h0_ref[:, lane_tile(l)], (n_b, LANES))
        for b in range(n_b):
            halo_s[b] = xm_ref[N_META - 8:N_META, :]

    cw = cw_ref[...]
    cb = cb_ref[...]

    def per_batch(b, c):
        x = xr_ref[b]
        ext = jnp.concatenate([halo_s[b], x], axis=0)
        halo_s[b] = x[tt - 8:tt, :]
        u = _conv(ext, tt, cw, cb)
        a, i, m = _lru_coeffs(u, wg_ref, brg_ref[...], big_ref[...], lam_ref[...])
        bb = m * i * u
        row0 = pl.multiple_of(b * tt, tt)
        for l in range(n_l):
            a_s[l, pl.ds(row0, tt), :] = a[:, lane_tile(l)]
            b_s[l, pl.ds(row0, tt), :] = bb[:, lane_tile(l)]
        return c

    lax.fori_loop(0, n_b, per_batch, 0)

    def chunk(c, hs):
        hs = list(hs)
        for k in range(8):
            t = c * 8 + k
            for l in range(n_l):
                at = a_s[l, pl.ds(t, n_b, stride=tt), :]
                bt = b_s[l, pl.ds(t, n_b, stride=tt), :]
                hs[l] = at * hs[l] + bt
                y_s[l, pl.ds(t, n_b, stride=tt), :] = hs[l]
        return tuple(hs)

    hs = lax.fori_loop(0, tt // 8, chunk, tuple(h_s[l] for l in range(n_l)))
    for l in range(n_l):
        h_s[l] = hs[l]
        hl_ref[:, lane_tile(l)] = hs[l]

    def gate_out(b, c):
        row0 = pl.multiple_of(b * tt, tt)
        g = gr_ref[b].astype(F32)
        y = jnp.concatenate([y_s[l, pl.ds(row0, tt), :] for l in range(n_l)], axis=-1)
        m_ref[b] = (jax.nn.sigmoid(g) * y).astype(BF16)
        return c

    lax.fori_loop(0, n_b, gate_out, 0)


def _rnn_sample_kernel(xr_ref, gr_ref, sc_ref, h0_ref, cw_ref, cb_ref, wg_ref, brg_ref, big_ref,
                       lam_ref, m_ref, hn_ref):
    cw = cw_ref[...]
    u = cb_ref[...] + cw[CONV_W - 1:CONV_W, :] * xr_ref[...]
    for j in range(CONV_W - 1):
        u = u + cw[j:j + 1, :] * sc_ref[j]
    a, i, m = _lru_coeffs(u, wg_ref, brg_ref[...], big_ref[...], lam_ref[...])
    h = a * h0_ref[...] + m * i * u
    hn_ref[...] = h
    m_ref[...] = (jax.nn.sigmoid(gr_ref[...].astype(F32)) * h).astype(BF16)


def _alibi_slope(h):
    return 2.0 ** (-8.0 * (h + 1) / N_HEADS)


def _attn_prompt_kernel(sink_ref, q_ref, kc_ref, kp_ref, km_ref, ga_ref, o_ref):
    j = pl.program_id(1)
    first = j == 0
    kv_prev = jnp.where(first, km_ref[...], kp_ref[...])
    kv = jnp.concatenate([kv_prev, kc_ref[...]], axis=0)
    q = q_ref[...]
    qi = lax.broadcasted_iota(jnp.int32, (BLOCK, 2 * BLOCK), 0)
    ci = lax.broadcasted_iota(jnp.int32, (BLOCK, 2 * BLOCK), 1)
    dist = qi + BLOCK - ci
    c_min = jnp.where(first, BLOCK - N_META, 0)
    valid = (dist >= 0) & (dist < WINDOW) & (ci >= c_min)
    distf = dist.astype(F32)
    outs = []
    for h in range(N_HEADS):
        g = h // GROUP
        kh = kv[:, g * HEAD_DIM:(g + 1) * HEAD_DIM].astype(BF16)
        vh = kv[:, D_KV + g * HEAD_DIM:D_KV + (g + 1) * HEAD_DIM].astype(BF16)
        qh = q[:, h * HEAD_DIM:(h + 1) * HEAD_DIM]
        s = lax.dot_general(qh, kh, (((1,), (1,)), ((), ())), preferred_element_type=F32)
        s = jnp.where(valid, s - _alibi_slope(h) * distf, NEG_BIG)
        sink = sink_ref[h]
        mx = jnp.maximum(jnp.max(s, axis=-1, keepdims=True), sink)
        p = jnp.exp(s - mx)
        den = jnp.sum(p, axis=-1, keepdims=True) + jnp.exp(sink - mx)
        o = jnp.dot(p.astype(BF16), vh, preferred_element_type=F32)
        outs.append(o / den)
    y = jnp.concatenate(outs, axis=-1)
    o_ref[...] = (jax.nn.sigmoid(ga_ref[...].astype(F32)) * y).astype(BF16)


def _attn_sample_kernel(sink_ref, q_ref, kvn_ref, ck_ref, cv_ref, ga_ref, o_ref):
    w_buf = ck_ref.shape[1]
    q = q_ref[0].astype(F32)
    ck = ck_ref[0]
    cv = cv_ref[0]
    kvn = kvn_ref[0]
    ci = lax.broadcasted_iota(jnp.int32, (GROUP, w_buf), 1)
    dist = w_buf - ci
    valid = dist < WINDOW
    distf = dist.astype(F32)
    hrow = lax.broadcasted_iota(jnp.int32, (GROUP, 1), 0)
    outs = []
    for g in range(N_KV):
        qg = q[g * GROUP:(g + 1) * GROUP, :]
        kg = ck[:, g * HEAD_DIM:(g + 1) * HEAD_DIM]
        vg = cv[:, g * HEAD_DIM:(g + 1) * HEAD_DIM]
        kn = kvn[:, g * HEAD_DIM:(g + 1) * HEAD_DIM]
        vn = kvn[:, D_KV + g * HEAD_DIM:D_KV + (g + 1) * HEAD_DIM]
        slope = jnp.zeros((GROUP, 1), F32)
        sink = jnp.zeros((GROUP, 1), F32)
        for r in range(GROUP):
            slope = jnp.where(hrow == r, _alibi_slope(g * GROUP + r), slope)
            sink = jnp.where(hrow == r, sink_ref[g * GROUP + r], sink)
        s = lax.dot_general(qg.astype(BF16), kg.astype(BF16), (((1,), (1,)), ((), ())),
                            preferred_element_type=F32)
        s = jnp.where(valid, s - slope * distf, NEG_BIG)
        sn = jnp.sum(qg.astype(BF16).astype(F32) * kn.astype(BF16).astype(F32), axis=-1, keepdims=True)
        mx = jnp.maximum(jnp.maximum(jnp.max(s, axis=-1, keepdims=True), sn), sink)
        p = jnp.exp(s - mx)
        pn = jnp.exp(sn - mx)
        den = jnp.sum(p, axis=-1, keepdims=True) + pn + jnp.exp(sink - mx)
        o = jnp.dot(p.astype(BF16), vg.astype(BF16), preferred_element_type=F32)
        o = o + pn.astype(BF16).astype(F32) * vn.astype(BF16).astype(F32)
        outs.append(o / den)
    y = jnp.concatenate(outs, axis=0)
    o_ref[0] = (jax.nn.sigmoid(ga_ref[0].astype(F32)) * y).astype(BF16)


def _merge_kernel(mr_ref, ma_ref, x_ref, wo_ref, nf_ref, wr_ref, br_ref, *rest):
    x1_ref, hn_ref, ti_ref, tw_ref = rest[-4:]
    mm = jnp.dot(mr_ref[...], wo_ref[:D_RNN, :], preferred_element_type=F32)
    mm = mm + jnp.dot(ma_ref[...], wo_ref[D_RNN:, :], preferred_element_type=F32)
    x1 = x_ref[...] + mm
    x1_ref[...] = x1
    ms = jnp.mean(x1 * x1, axis=-1, keepdims=True)
    hn = x1 * lax.rsqrt(ms + EPS) * nf_ref[...]
    hn_ref[...] = hn
    logits = jnp.dot(hn, wr_ref[...], preferred_element_type=F32,
                     precision=lax.Precision.HIGHEST) + br_ref[...]
    lane = lax.broadcasted_iota(jnp.int32, logits.shape, 1)
    ti = jnp.zeros(logits.shape, jnp.int32)
    tv = jnp.zeros(logits.shape, F32)
    l = logits
    v0 = None
    for k in range(TOP_K):
        mx = jnp.max(l, axis=-1, keepdims=True)
        idx = jnp.min(jnp.where(l == mx, lane, LANES), axis=-1, keepdims=True)
        if k == 0:
            v0 = mx
        ti = jnp.where(lane == k, idx, ti)
        tv = jnp.where(lane == k, jnp.exp(mx - v0), tv)
        l = jnp.where(lane == idx, NEG_BIG * 2, l)
    ti_ref[...] = ti
    tw_ref[...] = tv / jnp.sum(tv, axis=-1, keepdims=True)


def _merge(m_rnn, m_attn, x, w_out_bf, nf, wr_pad, br_pad, hn_all, n_all, row_block0, tm):
    rows = x.shape[0]
    row = lambda w: pl.BlockSpec((tm, w), lambda i: (i, 0))
    in_specs = [row(D_RNN), row(D_ATTN), row(D_MODEL), _full((D_RNN + D_ATTN, D_MODEL)),
                _full((1, D_MODEL)), _full((D_MODEL, LANES)), _full((1, LANES))]
    args = [m_rnn, m_attn, x, w_out_bf, nf, wr_pad, br_pad]
    aliases = {}
    if hn_all is not None:
        in_specs.append(pl.BlockSpec(memory_space=pl.ANY))
        args.append(hn_all)
        aliases = {len(args) - 1: 1}
    return pl.pallas_call(
        _merge_kernel,
        grid=(rows // tm,),
        in_specs=in_specs,
        out_specs=[row(D_MODEL),
                   pl.BlockSpec((tm, D_MODEL), lambda i: (i + row_block0, 0)),
                   row(LANES), row(LANES)],
        out_shape=[
            jax.ShapeDtypeStruct((rows, D_MODEL), F32),
            jax.ShapeDtypeStruct((n_all, D_MODEL), F32),
            jax.ShapeDtypeStruct((rows, LANES), jnp.int32),
            jax.ShapeDtypeStruct((rows, LANES), F32),
        ],
        input_output_aliases=aliases,
        compiler_params=_cparams(1),
        name="merge_router",
    )(*args)


def _moe_kernel(be_ref, nu_ref, tok_ref, hn_hbm, wg_ref, wl_ref, bg_ref, bl_ref, wd_ref, bd_ref,
                y_ref, xbuf, sem):
    i = pl.program_id(0)

    def row_copy(r):
        t = tok_ref[0, 0, r]
        return pltpu.make_async_copy(hn_hbm.at[pl.ds(t, 1)], xbuf.at[pl.ds(r, 1)], sem)

    @pl.when(i < nu_ref[0])
    def _():
        def issue(r, c):
            row_copy(r).start()
            return c

        lax.fori_loop(0, TM_MOE, issue, 0)

        def drain(r, c):
            row_copy(r).wait()
            return c

        lax.fori_loop(0, TM_MOE, drain, 0)
        x = xbuf[...].astype(BF16)
        zg = jnp.dot(x, wg_ref[0], preferred_element_type=F32) + bg_ref[0]
        zl = jnp.dot(x, wl_ref[0], preferred_element_type=F32) + bl_ref[0]
        xg = jnp.minimum(zg, SWIGLU_LIMIT)
        xl = jnp.clip(zl, -SWIGLU_LIMIT, SWIGLU_LIMIT)
        act = xg * jax.nn.sigmoid(SWIGLU_ALPHA * xg) * (xl + 1.0)
        y_ref[...] = jnp.dot(act.astype(BF16), wd_ref[0], preferred_element_type=F32) + bd_ref[0]

    @pl.when(i >= nu_ref[0])
    def _():
        y_ref[...] = jnp.zeros_like(y_ref)


def _moe(blk_expert, n_used, buf_tok, hn_all, wg, wl, bg, bl, wd, bd, n_blk):
    ew = lambda r, c: pl.BlockSpec((1, r, c), lambda i, be, nu: (be[i], 0, 0))
    return pl.pallas_call(
        _moe_kernel,
        grid_spec=pltpu.PrefetchScalarGridSpec(
            num_scalar_prefetch=2,
            grid=(n_blk,),
            in_specs=[
                pl.BlockSpec((1, 1, TM_MOE), lambda i, be, nu: (i, 0, 0), memory_space=pltpu.SMEM),
                pl.BlockSpec(memory_space=pl.ANY),
                ew(D_MODEL, D_FF), ew(D_MODEL, D_FF), ew(1, D_FF), ew(1, D_FF),
                ew(D_FF, D_MODEL), ew(1, D_MODEL),
            ],
            out_specs=pl.BlockSpec((TM_MOE, D_MODEL), lambda i, be, nu: (i, 0)),
            scratch_shapes=[pltpu.VMEM((TM_MOE, D_MODEL), F32), pltpu.SemaphoreType.DMA],
        ),
        out_shape=jax.ShapeDtypeStruct((n_blk * TM_MOE, D_MODEL), F32),
        compiler_params=_cparams(1),
        name="moe_experts",
    )(blk_expert, n_used, buf_tok, hn_all, wg, wl, bg, bl, wd, bd)


def _final_kernel(dest_ref, x1_ref, tw_ref, nf_ref, y_hbm, o_ref, ybuf, sem):
    def row_copy(n):
        r = n // TOP_K
        k = n % TOP_K
        d = dest_ref[0, 0, n]
        return pltpu.make_async_copy(y_hbm.at[pl.ds(d, 1)], ybuf.at[k, pl.ds(r, 1)], sem)

    def issue(n, c):
        row_copy(n).start()
        return c

    lax.fori_loop(0, TM_FIN * TOP_K, issue, 0)

    def drain(n, c):
        row_copy(n).wait()
        return c

    lax.fori_loop(0, TM_FIN * TOP_K, drain, 0)
    tw = tw_ref[...]
    x = x1_ref[...]
    for k in range(TOP_K):
        x = x + tw[:, k:k + 1] * ybuf[k]
    ms = jnp.mean(x * x, axis=-1, keepdims=True)
    o_ref[...] = x * lax.rsqrt(ms + EPS) * nf_ref[...]


def _final(dest, x1, tw, nf, ybuf):
    rows = x1.shape[0]
    n_tiles = rows // TM_FIN
    row = lambda w: pl.BlockSpec((TM_FIN, w), lambda i: (i, 0))
    return pl.pallas_call(
        _final_kernel,
        grid=(n_tiles,),
        in_specs=[
            pl.BlockSpec((1, 1, TM_FIN * TOP_K), lambda i: (i, 0, 0), memory_space=pltpu.SMEM),
            row(D_MODEL), row(LANES), _full((1, D_MODEL)),
            pl.BlockSpec(memory_space=pl.ANY),
        ],
        out_specs=row(D_MODEL),
        out_shape=jax.ShapeDtypeStruct((rows, D_MODEL), F32),
        scratch_shapes=[pltpu.VMEM((TOP_K, TM_FIN, D_MODEL), F32), pltpu.SemaphoreType.DMA],
        compiler_params=_cparams(1),
        name="combine_final",
    )(dest.reshape(n_tiles, 1, TM_FIN * TOP_K), x1, tw, nf, ybuf)


def _block_diag_tiles(w):
    per = GATE_TILE // GATE_BW
    w4 = w.reshape(N_GATE_TILES, per, GATE_BW, GATE_BW)
    eye = jnp.eye(per, dtype=w.dtype)
    return jnp.einsum("gacd,ab->gacbd", w4, eye).reshape(N_GATE_TILES, GATE_TILE, GATE_TILE)


def _route(top_i, n_blk):
    a = top_i.size
    flat_e = top_i.reshape(-1)
    onehot = (flat_e[:, None] == jnp.arange(N_EXPERTS, dtype=jnp.int32)[None, :]).astype(jnp.int32)
    csum = jnp.cumsum(onehot, axis=0)
    rank = jnp.sum(csum * onehot, axis=1) - 1
    counts = csum[-1]
    padded = ((counts + (TM_MOE - 1)) // TM_MOE) * TM_MOE
    ends = jnp.cumsum(padded).astype(jnp.int32)
    start_padded = ends - padded
    dest = (start_padded[flat_e] + rank).astype(jnp.int32)
    buf_tok = jnp.zeros((n_blk * TM_MOE,), jnp.int32).at[dest].set(
        jnp.arange(a, dtype=jnp.int32) // TOP_K)
    blk_start = jnp.arange(n_blk, dtype=jnp.int32) * TM_MOE
    blk_expert = jnp.minimum(
        jnp.sum((ends[None, :] <= blk_start[:, None]).astype(jnp.int32), axis=1), N_EXPERTS - 1)
    n_used = (ends[-1:] // TM_MOE).astype(jnp.int32)
    return dest, buf_tok.reshape(n_blk, 1, TM_MOE), blk_expert.astype(jnp.int32), n_used


def kernel(x_prompt, x_sample, cache_k, cache_v, state_conv, state_h, meta_tokens, norm_mix, w_in, conv_w, conv_b, w_rg, b_rg, w_ig, b_ig, lru_lambda, attn_sinks, w_out, norm_ffn, w_router, b_router, w_up, b_up, w_down, b_down, norm_final):
    n_b, seq = x_prompt.shape[0], x_prompt.shape[1]
    n_s = x_sample.shape[0]
    w_buf = cache_k.shape[2]
    n_p = n_b * seq
    n_tok = n_p + n_s
    row1 = lambda v: v.reshape(1, -1)

    w_in_bf = w_in[0].astype(BF16)
    w_out_bf = w_out[0].astype(BF16)
    wg_tiles = jnp.concatenate([_block_diag_tiles(w_rg[0]), _block_diag_tiles(w_ig[0])],
                               axis=-1).astype(BF16)
    wr_pad = jnp.pad(w_router[0], ((0, 0), (0, LANES - N_EXPERTS)))
    br_pad = jnp.pad(b_router[0], (0, LANES - N_EXPERTS), constant_values=NEG_BIG).reshape(1, LANES)
    w_glu = w_up[0][:, :, 0::2].astype(BF16)
    w_lin = w_up[0][:, :, 1::2].astype(BF16)
    b_glu = b_up[0][:, 0::2].reshape(N_EXPERTS, 1, D_FF)
    b_lin = b_up[0][:, 1::2].reshape(N_EXPERTS, 1, D_FF)
    w_dn = w_down[0].astype(BF16)
    b_dn = b_down[0].reshape(N_EXPERTS, 1, D_MODEL)
    nm, nf, nfin = row1(norm_mix[0]), row1(norm_ffn[0]), row1(norm_final)
    cw, cb = conv_w[0], row1(conv_b[0])
    brg, big, lam = row1(b_rg[0]), row1(b_ig[0]), row1(lru_lambda[0])
    sinks = attn_sinks[0]
    rnn_w = (cw, cb, wg_tiles, brg, big, lam)
    rnn_w_specs = [_full((CONV_W, D_RNN)), _full((1, D_RNN)),
                   _full((N_GATE_TILES, GATE_TILE, 2 * GATE_TILE)),
                   _full((1, D_RNN)), _full((1, D_RNN)), _full((1, D_RNN))]

    xp2 = x_prompt.reshape(n_p, D_MODEL)
    xr_p, q_p, kv_p, gr_p, ga_p = _in_proj(xp2, nm, w_in_bf, TM_IN)
    x_sm = jnp.concatenate([x_sample.reshape(n_s, D_MODEL), meta_tokens], axis=0)
    xr_sm, q_sm, kv_sm, gr_sm, ga_sm = _in_proj(x_sm, nm, w_in_bf, n_s + N_META)
    xr_s, q_s, kv_s, gr_s, ga_s = (t[:n_s] for t in (xr_sm, q_sm, kv_sm, gr_sm, ga_sm))
    xr_m, kv_m = xr_sm[n_s:], kv_sm[n_s:]

    h_meta = pl.pallas_call(
        _rnn_meta_kernel,
        in_specs=[_full((N_META, D_RNN))] + rnn_w_specs,
        out_specs=_full((1, D_RNN)),
        out_shape=jax.ShapeDtypeStruct((1, D_RNN), F32),
        grid=(1,),
        compiler_params=_cparams(1),
        name="rnn_meta",
    )(xr_m, *rnn_w)

    tt = TT_RNN
    blk3 = pl.BlockSpec((n_b, tt, D_RNN), lambda j: (0, j, 0))
    m_rnn_p, h_last_p = pl.pallas_call(
        functools.partial(_rnn_prompt_kernel, n_b=n_b, tt=tt),
        grid=(seq // tt,),
        in_specs=[blk3, blk3, _full((N_META, D_RNN)), _full((1, D_RNN))] + rnn_w_specs,
        out_specs=[blk3, _full((n_b, D_RNN))],
        out_shape=[jax.ShapeDtypeStruct((n_b, seq, D_RNN), BF16),
                   jax.ShapeDtypeStruct((n_b, D_RNN), F32)],
        scratch_shapes=[pltpu.VMEM((n_b, 8, D_RNN), F32),
                        pltpu.VMEM((D_RNN // LANES, n_b * tt, LANES), F32),
                        pltpu.VMEM((D_RNN // LANES, n_b * tt, LANES), F32),
                        pltpu.VMEM((D_RNN // LANES, n_b * tt, LANES), F32),
                        pltpu.VMEM((D_RNN // LANES, n_b, LANES), F32)],
        compiler_params=_cparams(1),
        name="rnn_prompt",
    )(xr_p.reshape(n_b, seq, D_RNN), gr_p.reshape(n_b, seq, D_RNN), xr_m, h_meta, *rnn_w)

    sc_t = jnp.swapaxes(state_conv[0], 0, 1)
    m_rnn_s, h_new_s = pl.pallas_call(
        _rnn_sample_kernel,
        grid=(1,),
        in_specs=[_full((n_s, D_RNN)), _full((n_s, D_RNN)), _full((CONV_W - 1, n_s, D_RNN)),
                  _full((n_s, D_RNN))] + rnn_w_specs,
        out_specs=[_full((n_s, D_RNN)), _full((n_s, D_RNN))],
        out_shape=[jax.ShapeDtypeStruct((n_s, D_RNN), BF16),
                   jax.ShapeDtypeStruct((n_s, D_RNN), F32)],
        compiler_params=_cparams(1),
        name="rnn_sample",
    )(xr_s, gr_s, sc_t, state_h[0], *rnn_w)

    n_blk_seq = seq // BLOCK
    kv_meta_blk = jnp.pad(kv_m, ((BLOCK - N_META, 0), (0, 0)))
    smem_spec = pl.BlockSpec(memory_space=pltpu.SMEM)
    rb = lambda w: pl.BlockSpec((BLOCK, w), lambda b, j: (b * n_blk_seq + j, 0))
    m_attn_p = pl.pallas_call(
        _attn_prompt_kernel,
        grid=(n_b, n_blk_seq),
        in_specs=[smem_spec, rb(D_ATTN), rb(2 * D_KV),
                  pl.BlockSpec((BLOCK, 2 * D_KV),
                               lambda b, j: (jnp.maximum(b * n_blk_seq + j - 1, 0), 0)),
                  _full((BLOCK, 2 * D_KV)), rb(D_ATTN)],
        out_specs=rb(D_ATTN),
        out_shape=jax.ShapeDtypeStruct((n_p, D_ATTN), BF16),
        compiler_params=_cparams(2),
        name="attn_prompt",
    )(sinks, q_p, kv_p, kv_p, kv_meta_blk, ga_p)

    ck = cache_k[0].reshape(n_s, w_buf, D_KV)
    cv = cache_v[0].reshape(n_s, w_buf, D_KV)
    per_s = lambda a, b: pl.BlockSpec((1, a, b), lambda i: (i, 0, 0))
    m_attn_s = pl.pallas_call(
        _attn_sample_kernel,
        grid=(n_s,),
        in_specs=[smem_spec, per_s(N_HEADS, HEAD_DIM), per_s(1, 2 * D_KV), per_s(w_buf, D_KV),
                  per_s(w_buf, D_KV), per_s(N_HEADS, HEAD_DIM)],
        out_specs=per_s(N_HEADS, HEAD_DIM),
        out_shape=jax.ShapeDtypeStruct((n_s, N_HEADS, HEAD_DIM), BF16),
        compiler_params=_cparams(1),
        name="attn_sample",
    )(sinks, q_s.reshape(n_s, N_HEADS, HEAD_DIM), kv_s.reshape(n_s, 1, 2 * D_KV), ck, cv,
      ga_s.reshape(n_s, N_HEADS, HEAD_DIM)).reshape(n_s, D_ATTN)

    x1_p, hn_all, ti_p, tw_p = _merge(m_rnn_p.reshape(n_p, D_RNN), m_attn_p, xp2, w_out_bf, nf,
                                      wr_pad, br_pad, jnp.zeros((n_tok, D_MODEL), F32), n_tok, 0,
                                      TM_MERGE)
    x1_s, hn_all, ti_s, tw_s = _merge(m_rnn_s, m_attn_s, x_sample.reshape(n_s, D_MODEL), w_out_bf,
                                      nf, wr_pad, br_pad, hn_all, n_tok, n_p // n_s, n_s)

    top_i = jnp.concatenate([ti_p[:, :TOP_K], ti_s[:, :TOP_K]], axis=0)
    n_assign = n_tok * TOP_K
    n_blk = n_assign // TM_MOE + N_EXPERTS
    dest, buf_tok, blk_expert, n_used = _route(top_i, n_blk)

    ybuf = _moe(blk_expert, n_used, buf_tok, hn_all, w_glu, w_lin, b_glu, b_lin, w_dn, b_dn, n_blk)
    y_p = _final(dest[:n_p * TOP_K], x1_p, tw_p, nfin, ybuf)
    y_s = _final(dest[n_p * TOP_K:], x1_s, tw_s, nfin, ybuf)

    kv_p3 = kv_p.reshape(n_b, seq, 2 * D_KV)
    w_p = min(WINDOW, seq + N_META)
    new_k_p = kv_p3[:, seq - w_p:, :D_KV].reshape(1, n_b, w_p, N_KV, HEAD_DIM)
    new_v_p = kv_p3[:, seq - w_p:, D_KV:].reshape(1, n_b, w_p, N_KV, HEAD_DIM)
    new_conv_p = xr_p.reshape(n_b, seq, D_RNN)[:, seq - (CONV_W - 1):][None]
    k_new = kv_s[:, :D_KV].reshape(n_s, 1, N_KV, HEAD_DIM)
    v_new = kv_s[:, D_KV:].reshape(n_s, 1, N_KV, HEAD_DIM)
    new_k_s = jnp.concatenate([cache_k[0], k_new], axis=1)[:, -w_buf:][None]
    new_v_s = jnp.concatenate([cache_v[0], v_new], axis=1)[:, -w_buf:][None]
    new_conv_s = jnp.concatenate([state_conv[0], xr_s[:, None, :]], axis=1)[:, -(CONV_W - 1):][None]
    return (y_p.reshape(n_b, seq, D_MODEL), y_s.reshape(n_s, 1, D_MODEL), new_k_p, new_v_p,
            new_conv_p, h_last_p[None], new_k_s, new_v_s, new_conv_s, h_new_s[None])
```

```python
import functools

import jax
import jax.numpy as jnp
from jax import lax
from jax.experimental import pallas as pl
from jax.experimental.pallas import tpu as pltpu

F32 = jnp.float32
BF16 = jnp.bfloat16

D_MODEL = 1024
N_META = 16
D_RNN = 1024
N_GATE_BLOCKS = 16
GATE_BW = D_RNN // N_GATE_BLOCKS
CONV_W = 4
LRU_C = 8.0
N_HEADS = 16
HEAD_DIM = 64
N_KV = 2
GROUP = N_HEADS // N_KV
D_ATTN = N_HEADS * HEAD_DIM
D_KV = N_KV * HEAD_DIM
WINDOW = 128
BLOCK = 128
PAST_LEN = 16384
N_EXPERTS = 32
TOP_K = 4
D_FF = 1024
SWIGLU_LIMIT = 7.0
SWIGLU_ALPHA = 1.702
EPS = 1e-6

GATE_TILE = 256
N_GATE_TILES = D_RNN // GATE_TILE
LANES = 128
NEG_BIG = -1e30

TM_IN = 256
TT_RNN = 64
TM_MERGE = 256
TM_MOE = 256
TM_FIN = 128
VMEM_LIMIT = 56 * 1024 * 1024


def _cparams(n_grid_dims):
    return pltpu.CompilerParams(
        dimension_semantics=("arbitrary",) * n_grid_dims, vmem_limit_bytes=VMEM_LIMIT)


def _full(shape):
    return pl.BlockSpec(shape, lambda *_: (0,) * len(shape))


def _in_proj_kernel(x_ref, g_ref, w_ref, xr_ref, q_ref, kv_ref, gr_ref, ga_ref):
    x = x_ref[...]
    ms = jnp.mean(x * x, axis=-1, keepdims=True)
    h = (x * lax.rsqrt(ms + EPS) * g_ref[...]).astype(BF16)

    def proj(lo, hi):
        return jnp.dot(h, w_ref[:, lo:hi], preferred_element_type=F32)

    o_q = D_RNN
    o_k = o_q + D_ATTN
    o_gr = o_k + 2 * D_KV
    o_ga = o_gr + D_RNN
    xr_ref[...] = proj(0, o_q)
    q_ref[...] = (proj(o_q, o_k) * (HEAD_DIM ** -0.5)).astype(BF16)
    kv_ref[...] = proj(o_k, o_gr)
    gr_ref[...] = proj(o_gr, o_ga).astype(BF16)
    ga_ref[...] = proj(o_ga, o_ga + D_ATTN).astype(BF16)


def _in_proj(x, g, w_in_bf, tm):
    rows = x.shape[0]
    d_in = w_in_bf.shape[1]
    row = lambda w: pl.BlockSpec((tm, w), lambda i: (i, 0))
    return pl.pallas_call(
        _in_proj_kernel,
        grid=(rows // tm,),
        in_specs=[row(D_MODEL), _full((1, D_MODEL)), _full((D_MODEL, d_in))],
        out_specs=[row(D_RNN), row(D_ATTN), row(2 * D_KV), row(D_RNN), row(D_ATTN)],
        out_shape=[
            jax.ShapeDtypeStruct((rows, D_RNN), F32),
            jax.ShapeDtypeStruct((rows, D_ATTN), BF16),
            jax.ShapeDtypeStruct((rows, 2 * D_KV), F32),
            jax.ShapeDtypeStruct((rows, D_RNN), BF16),
            jax.ShapeDtypeStruct((rows, D_ATTN), BF16),
        ],
        compiler_params=_cparams(1),
        name="in_proj",
    )(x, g, w_in_bf)


def _softplus(x):
    return jnp.maximum(x, 0.0) + jnp.log1p(jnp.exp(-jnp.abs(x)))


def _lru_coeffs(u, wg_ref, brg, big, lam):
    ub = u.astype(BF16)
    sp = _softplus(-lam)
    a_parts, i_parts, m_parts = [], [], []
    for g in range(N_GATE_TILES):
        sl = slice(g * GATE_TILE, (g + 1) * GATE_TILE)
        zz = jnp.dot(ub[:, sl], wg_ref[g], preferred_element_type=F32)
        r = jax.nn.sigmoid(zz[:, :GATE_TILE] + brg[:, sl])
        i = jax.nn.sigmoid(zz[:, GATE_TILE:] + big[:, sl])
        log_a = (-LRU_C) * r * sp[:, sl]
        a = jnp.exp(log_a)
        m = jnp.sqrt(1.0 - a * a)
        a_parts.append(a)
        i_parts.append(i)
        m_parts.append(m)
    cat = lambda ps: jnp.concatenate(ps, axis=-1)
    return cat(a_parts), cat(i_parts), cat(m_parts)


def _conv(ext, t, cw, cb):
    out = cb
    for j in range(CONV_W):
        s = CONV_W - 1 - j
        out = out + cw[j:j + 1, :] * ext[8 - s:8 - s + t, :]
    return out


def _rnn_meta_kernel(xr_ref, cw_ref, cb_ref, wg_ref, brg_ref, big_ref, lam_ref, h_ref):
    x = xr_ref[...]
    ext = jnp.concatenate([jnp.zeros((8, D_RNN), F32), x], axis=0)
    u = _conv(ext, N_META, cw_ref[...], cb_ref[...])
    a, i, m = _lru_coeffs(u, wg_ref, brg_ref[...], big_ref[...], lam_ref[...])
    first = lax.broadcasted_iota(jnp.int32, (N_META, 1), 0) == 0
    b = jnp.where(first, 1.0, m) * i * u
    h = jnp.zeros((1, D_RNN), F32)
    for t in range(N_META):
        h = a[t:t + 1, :] * h + b[t:t + 1, :]
    h_ref[...] = h


def _rnn_prompt_kernel(xr_ref, gr_ref, xm_ref, h0_ref, cw_ref, cb_ref, wg_ref, brg_ref, big_ref,
                       lam_ref, m_ref, hl_ref, halo_s, a_s, b_s, y_s, h_s, *, n_b, tt):
    j = pl.program_id(0)

    n_l = D_RNN // LANES
    lane_tile = lambda l: slice(l * LANES, (l + 1) * LANES)

    @pl.when(j == 0)
    def _():
        for l in range(n_l):
            h_s[l] = jnp.broadcast_to(h0_ref[:, lane_tile(l)], (n_b, LANES))
        for b in range(n_b):
            halo_s[b] = xm_ref[N_META - 8:N_META, :]

    cw = cw_ref[...]
    cb = cb_ref[...]

    def per_batch(b, c):
        x = xr_ref[b]
        ext = jnp.concatenate([halo_s[b], x], axis=0)
        halo_s[b] = x[tt - 8:tt, :]
        u = _conv(ext, tt, cw, cb)
        a, i, m = _lru_coeffs(u, wg_ref, brg_ref[...], big_ref[...], lam_ref[...])
        bb = m * i * u
        row0 = pl.multiple_of(b * tt, tt)
        for l in range(n_l):
            a_s[l, pl.ds(row0, tt), :] = a[:, lane_tile(l)]
            b_s[l, pl.ds(row0, tt), :] = bb[:, lane_tile(l)]
        return c

    lax.fori_loop(0, n_b, per_batch, 0)

    def chunk(c, hs):
        hs = list(hs)
        for k in range(8):
            t = c * 8 + k
            for l in range(n_l):
                at = a_s[l, pl.ds(t, n_b, stride=tt), :]
                bt = b_s[l, pl.ds(t, n_b, stride=tt), :]
                hs[l] = at * hs[l] + bt
                y_s[l, pl.ds(t, n_b, stride=tt), :] = hs[l]
        return tuple(hs)

    hs = lax.fori_loop(0, tt // 8, chunk, tuple(h_s[l] for l in range(n_l)))
    for l in range(n_l):
        h_s[l] = hs[l]
        hl_ref[:, lane_tile(l)] = hs[l]

    def gate_out(b, c):
        row0 = pl.multiple_of(b * tt, tt)
        g = gr_ref[b].astype(F32)
        y = jnp.concatenate([y_s[l, pl.ds(row0, tt), :] for l in range(n_l)], axis=-1)
        m_ref[b] = (jax.nn.sigmoid(g) * y).astype(BF16)
        return c

    lax.fori_loop(0, n_b, gate_out, 0)


def _rnn_sample_kernel(xr_ref, gr_ref, sc_ref, h0_ref, cw_ref, cb_ref, wg_ref, brg_ref, big_ref,
                       lam_ref, m_ref, hn_ref):
    cw = cw_ref[...]
    u = cb_ref[...] + cw[CONV_W - 1:CONV_W, :] * xr_ref[...]
    for j in range(CONV_W - 1):
        u = u + cw[j:j + 1, :] * sc_ref[j]
    a, i, m = _lru_coeffs(u, wg_ref, brg_ref[...], big_ref[...], lam_ref[...])
    h = a * h0_ref[...] + m * i * u
    hn_ref[...] = h
    m_ref[...] = (jax.nn.sigmoid(gr_ref[...].astype(F32)) * h).astype(BF16)


def _alibi_slope(h):
    return 2.0 ** (-8.0 * (h + 1) / N_HEADS)


def _attn_prompt_kernel(sink_ref, q_ref, kc_ref, kp_ref, km_ref, ga_ref, o_ref):
    j = pl.program_id(1)
    first = j == 0
    kv_prev = jnp.where(first, km_ref[...], kp_ref[...])
    kv = jnp.concatenate([kv_prev, kc_ref[...]], axis=0)
    q = q_ref[...]
    qi = lax.broadcasted_iota(jnp.int32, (BLOCK, 2 * BLOCK), 0)
    ci = lax.broadcasted_iota(jnp.int32, (BLOCK, 2 * BLOCK), 1)
    dist = qi + BLOCK - ci
    c_min = jnp.where(first, BLOCK - N_META, 0)
    valid = (dist >= 0) & (dist < WINDOW) & (ci >= c_min)
    distf = dist.astype(F32)
    outs = []
    for h in range(N_HEADS):
        g = h // GROUP
        kh = kv[:, g * HEAD_DIM:(g + 1) * HEAD_DIM].astype(BF16)
        vh = kv[:, D_KV + g * HEAD_DIM:D_KV + (g + 1) * HEAD_DIM].astype(BF16)
        qh = q[:, h * HEAD_DIM:(h + 1) * HEAD_DIM]
        s = lax.dot_general(qh, kh, (((1,), (1,)), ((), ())), preferred_element_type=F32)
        s = jnp.where(valid, s - _alibi_slope(h) * distf, NEG_BIG)
        sink = sink_ref[h]
        mx = jnp.maximum(jnp.max(s, axis=-1, keepdims=True), sink)
        p = jnp.exp(s - mx)
        den = jnp.sum(p, axis=-1, keepdims=True) + jnp.exp(sink - mx)
        o = jnp.dot(p.astype(BF16), vh, preferred_element_type=F32)
        outs.append(o / den)
    y = jnp.concatenate(outs, axis=-1)
    o_ref[...] = (jax.nn.sigmoid(ga_ref[...].astype(F32)) * y).astype(BF16)


def _attn_sample_kernel(sink_ref, q_ref, kvn_ref, ck_ref, cv_ref, ga_ref, o_ref):
    w_buf = ck_ref.shape[1]
    q = q_ref[0].astype(F32)
    ck = ck_ref[0]
    cv = cv_ref[0]
    kvn = kvn_ref[0]
    ci = lax.broadcasted_iota(jnp.int32, (GROUP, w_buf), 1)
    dist = w_buf - ci
    valid = dist < WINDOW
    distf = dist.astype(F32)
    hrow = lax.broadcasted_iota(jnp.int32, (GROUP, 1), 0)
    outs = []
    for g in range(N_KV):
        qg = q[g * GROUP:(g + 1) * GROUP, :]
        kg = ck[:, g * HEAD_DIM:(g + 1) * HEAD_DIM]
        vg = cv[:, g * HEAD_DIM:(g + 1) * HEAD_DIM]
        kn = kvn[:, g * HEAD_DIM:(g + 1) * HEAD_DIM]
        vn = kvn[:, D_KV + g * HEAD_DIM:D_KV + (g + 1) * HEAD_DIM]
        slope = jnp.zeros((GROUP, 1), F32)
        sink = jnp.zeros((GROUP, 1), F32)
        for r in range(GROUP):
            slope = jnp.where(hrow == r, _alibi_slope(g * GROUP + r), slope)
            sink = jnp.where(hrow == r, sink_ref[g * GROUP + r], sink)
        s = lax.dot_general(qg.astype(BF16), kg.astype(BF16), (((1,), (1,)), ((), ())),
                            preferred_element_type=F32)
        s = jnp.where(valid, s - slope * distf, NEG_BIG)
        sn = jnp.sum(qg.astype(BF16).astype(F32) * kn.astype(BF16).astype(F32), axis=-1, keepdims=True)
        mx = jnp.maximum(jnp.maximum(jnp.max(s, axis=-1, keepdims=True), sn), sink)
        p = jnp.exp(s - mx)
        pn = jnp.exp(sn - mx)
        den = jnp.sum(p, axis=-1, keepdims=True) + pn + jnp.exp(sink - mx)
        o = jnp.dot(p.astype(BF16), vg.astype(BF16), preferred_element_type=F32)
        o = o + pn.astype(BF16).astype(F32) * vn.astype(BF16).astype(F32)
        outs.append(o / den)
    y = jnp.concatenate(outs, axis=0)
    o_ref[0] = (jax.nn.sigmoid(ga_ref[0].astype(F32)) * y).astype(BF16)


def _merge_kernel(mr_ref, ma_ref, x_ref, wo_ref, nf_ref, wr_ref, br_ref, *rest):
    x1_ref, hn_ref, ti_ref, tw_ref = rest[-4:]
    mm = jnp.dot(mr_ref[...], wo_ref[:D_RNN, :], preferred_element_type=F32)
    mm = mm + jnp.dot(ma_ref[...], wo_ref[D_RNN:, :], preferred_element_type=F32)
    x1 = x_ref[...] + mm
    x1_ref[...] = x1
    ms = jnp.mean(x1 * x1, axis=-1, keepdims=True)
    hn = x1 * lax.rsqrt(ms + EPS) * nf_ref[...]
    hn_ref[...] = hn
    logits = jnp.dot(hn, wr_ref[...], preferred_element_type=F32,
                     precision=lax.Precision.HIGHEST) + br_ref[...]
    lane = lax.broadcasted_iota(jnp.int32, logits.shape, 1)
    ti = jnp.zeros(logits.shape, jnp.int32)
    tv = jnp.zeros(logits.shape, F32)
    l = logits
    v0 = None
    for k in range(TOP_K):
        mx = jnp.max(l, axis=-1, keepdims=True)
        idx = jnp.min(jnp.where(l == mx, lane, LANES), axis=-1, keepdims=True)
        if k == 0:
            v0 = mx
        ti = jnp.where(lane == k, idx, ti)
        tv = jnp.where(lane == k, jnp.exp(mx - v0), tv)
        l = jnp.where(lane == idx, NEG_BIG * 2, l)
    ti_ref[...] = ti
    tw_ref[...] = tv / jnp.sum(tv, axis=-1, keepdims=True)


def _merge(m_rnn, m_attn, x, w_out_bf, nf, wr_pad, br_pad, hn_all, n_all, row_block0, tm):
    rows = x.shape[0]
    row = lambda w: pl.BlockSpec((tm, w), lambda i: (i, 0))
    in_specs = [row(D_RNN), row(D_ATTN), row(D_MODEL), _full((D_RNN + D_ATTN, D_MODEL)),
                _full((1, D_MODEL)), _full((D_MODEL, LANES)), _full((1, LANES))]
    args = [m_rnn, m_attn, x, w_out_bf, nf, wr_pad, br_pad]
    aliases = {}
    if hn_all is not None:
        in_specs.append(pl.BlockSpec(memory_space=pl.ANY))
        args.append(hn_all)
        aliases = {len(args) - 1: 1}
    return pl.pallas_call(
        _merge_kernel,
        grid=(rows // tm,),
        in_specs=in_specs,
        out_specs=[row(D_MODEL),
                   pl.BlockSpec((tm, D_MODEL), lambda i: (i + row_block0, 0)),
                   row(LANES), row(LANES)],
        out_shape=[
            jax.ShapeDtypeStruct((rows, D_MODEL), F32),
            jax.ShapeDtypeStruct((n_all, D_MODEL), F32),
            jax.ShapeDtypeStruct((rows, LANES), jnp.int32),
            jax.ShapeDtypeStruct((rows, LANES), F32),
        ],
        input_output_aliases=aliases,
        compiler_params=_cparams(1),
        name="merge_router",
    )(*args)


UP_TILE = 256


def _deinterleave_matrix():
    half = UP_TILE // 2
    r = lax.broadcasted_iota(jnp.int32, (UP_TILE, UP_TILE), 0)
    c = lax.broadcasted_iota(jnp.int32, (UP_TILE, UP_TILE), 1)
    src = jnp.where(c < half, 2 * c, 2 * (c - half) + 1)
    return jnp.where(r == src, 1.0, 0.0).astype(BF16)


def _moe_kernel(be_ref, nu_ref, tokc_ref, tokn_ref, hn_hbm, wu_ref, bu_ref, wd_ref, bd_ref,
                y_ref, xbuf, xb_s, wu_s, wd_s, sem):
    i = pl.program_id(0)
    nu = nu_ref[0]
    half = UP_TILE // 2
    n_up = 2 * D_FF // UP_TILE

    def row_copy(tok_ref, r):
        return pltpu.make_async_copy(hn_hbm.at[pl.ds(tok_ref[0, 0, r], 1)], xbuf.at[pl.ds(r, 1)],
                                     sem)

    @pl.when(i == 0)
    def _():
        def issue(r, c):
            row_copy(tokc_ref, r).start()
            return c

        lax.fori_loop(0, TM_MOE, issue, 0)

    @pl.when(i <= nu)
    def _():
        pltpu.make_async_copy(hn_hbm.at[pl.ds(0, TM_MOE)], xbuf, sem).wait()

    @pl.when(i < nu)
    def _():
        xb_s[...] = xbuf[...].astype(BF16)

        @pl.when((i == 0) | (be_ref[i] != be_ref[jnp.maximum(i - 1, 0)]))
        def _():
            perm = _deinterleave_matrix()
            for c in range(n_up):
                cols = slice(c * UP_TILE, (c + 1) * UP_TILE)
                blk = wu_ref[0, :, cols].astype(BF16)
                wu_s[:, cols] = jnp.dot(blk, perm, preferred_element_type=F32).astype(BF16)
            wd_s[...] = wd_ref[0].astype(BF16)

        for r in range(TM_MOE):
            row_copy(tokn_ref, r).start()

        x = xb_s[...]
        z = jnp.dot(x, wu_s[...], preferred_element_type=F32) + bu_ref[0]
        zg = jnp.concatenate([z[:, c * UP_TILE:c * UP_TILE + half] for c in range(n_up)], axis=-1)
        zl = jnp.concatenate([z[:, c * UP_TILE + half:(c + 1) * UP_TILE] for c in range(n_up)],
                             axis=-1)
        xg = jnp.minimum(zg, SWIGLU_LIMIT)
        xl = jnp.clip(zl, -SWIGLU_LIMIT, SWIGLU_LIMIT)
        act = xg * jax.nn.sigmoid(SWIGLU_ALPHA * xg) * (xl + 1.0)
        y_ref[...] = jnp.dot(act.astype(BF16), wd_s[...], preferred_element_type=F32) + bd_ref[0]

    @pl.when(i >= nu)
    def _():
        y_ref[...] = jnp.zeros_like(y_ref)


def _moe(blk_expert, n_used, buf_tok, hn_all, w_up, b_up_perm, w_down, b_down, n_blk):
    ew = lambda r, c: pl.BlockSpec((1, r, c), lambda i, be, nu: (be[i], 0, 0))
    tok = lambda f: pl.BlockSpec((1, 1, TM_MOE), lambda i, be, nu: (f(i), 0, 0),
                                 memory_space=pltpu.SMEM)
    return pl.pallas_call(
        _moe_kernel,
        grid_spec=pltpu.PrefetchScalarGridSpec(
            num_scalar_prefetch=2,
            grid=(n_blk,),
            in_specs=[
                tok(lambda i: i), tok(lambda i: jnp.minimum(i + 1, n_blk - 1)),
                pl.BlockSpec(memory_space=pl.ANY),
                ew(D_MODEL, 2 * D_FF), ew(1, 2 * D_FF), ew(D_FF, D_MODEL), ew(1, D_MODEL),
            ],
            out_specs=pl.BlockSpec((TM_MOE, D_MODEL), lambda i, be, nu: (i, 0)),
            scratch_shapes=[pltpu.VMEM((TM_MOE, D_MODEL), F32),
                            pltpu.VMEM((TM_MOE, D_MODEL), BF16),
                            pltpu.VMEM((D_MODEL, 2 * D_FF), BF16),
                            pltpu.VMEM((D_FF, D_MODEL), BF16),
                            pltpu.SemaphoreType.DMA],
        ),
        out_shape=jax.ShapeDtypeStruct((n_blk * TM_MOE, D_MODEL), F32),
        compiler_params=_cparams(1),
        name="moe_experts",
    )(blk_expert, n_used, buf_tok, buf_tok, hn_all, w_up, b_up_perm, w_down, b_down)


def _final_kernel(dest_ref, x1_ref, tw_ref, nf_ref, y_hbm, o_ref, ybuf, sem):
    for n in range(TM_FIN * TOP_K):
        r, k = divmod(n, TOP_K)
        pltpu.make_async_copy(y_hbm.at[pl.ds(dest_ref[0, 0, n], 1)],
                              ybuf.at[pl.ds(k * TM_FIN + r, 1)], sem).start()
    pltpu.make_async_copy(y_hbm.at[pl.ds(0, TOP_K * TM_FIN)], ybuf, sem).wait()
    tw = tw_ref[...]
    x = x1_ref[...]
    for k in range(TOP_K):
        x = x + tw[:, k:k + 1] * ybuf[k * TM_FIN:(k + 1) * TM_FIN, :]
    ms = jnp.mean(x * x, axis=-1, keepdims=True)
    o_ref[...] = x * lax.rsqrt(ms + EPS) * nf_ref[...]


def _final(dest, x1, tw, nf, ybuf):
    rows = x1.shape[0]
    n_tiles = rows // TM_FIN
    row = lambda w: pl.BlockSpec((TM_FIN, w), lambda i: (i, 0))
    return pl.pallas_call(
        _final_kernel,
        grid=(n_tiles,),
        in_specs=[
            pl.BlockSpec((1, 1, TM_FIN * TOP_K), lambda i: (i, 0, 0), memory_space=pltpu.SMEM),
            row(D_MODEL), row(LANES), _full((1, D_MODEL)),
            pl.BlockSpec(memory_space=pl.ANY),
        ],
        out_specs=row(D_MODEL),
        out_shape=jax.ShapeDtypeStruct((rows, D_MODEL), F32),
        scratch_shapes=[pltpu.VMEM((TOP_K * TM_FIN, D_MODEL), F32), pltpu.SemaphoreType.DMA],
        compiler_params=_cparams(1),
        name="combine_final",
    )(dest.reshape(n_tiles, 1, TM_FIN * TOP_K), x1, tw, nf, ybuf)


def _block_diag_tiles(w):
    per = GATE_TILE // GATE_BW
    w4 = w.reshape(N_GATE_TILES, per, GATE_BW, GATE_BW)
    eye = jnp.eye(per, dtype=w.dtype)
    return jnp.einsum("gacd,ab->gacbd", w4, eye).reshape(N_GATE_TILES, GATE_TILE, GATE_TILE)


def _route(top_i, n_blk):
    a = top_i.size
    flat_e = top_i.reshape(-1)
    onehot = (flat_e[:, None] == jnp.arange(N_EXPERTS, dtype=jnp.int32)[None, :]).astype(jnp.int32)
    csum = jnp.cumsum(onehot, axis=0)
    rank = jnp.sum(csum * onehot, axis=1) - 1
    counts = csum[-1]
    padded = ((counts + (TM_MOE - 1)) // TM_MOE) * TM_MOE
    ends = jnp.cumsum(padded).astype(jnp.int32)
    start_padded = ends - padded
    dest = (start_padded[flat_e] + rank).astype(jnp.int32)
    buf_tok = jnp.zeros((n_blk * TM_MOE,), jnp.int32).at[dest].set(
        jnp.arange(a, dtype=jnp.int32) // TOP_K)
    blk_start = jnp.arange(n_blk, dtype=jnp.int32) * TM_MOE
    blk_expert = jnp.minimum(
        jnp.sum((ends[None, :] <= blk_start[:, None]).astype(jnp.int32), axis=1), N_EXPERTS - 1)
    n_used = (ends[-1:] // TM_MOE).astype(jnp.int32)
    return dest, buf_tok.reshape(n_blk, 1, TM_MOE), blk_expert.astype(jnp.int32), n_used


def kernel(x_prompt, x_sample, cache_k, cache_v, state_conv, state_h, meta_tokens, norm_mix, w_in, conv_w, conv_b, w_rg, b_rg, w_ig, b_ig, lru_lambda, attn_sinks, w_out, norm_ffn, w_router, b_router, w_up, b_up, w_down, b_down, norm_final):
    n_b, seq = x_prompt.shape[0], x_prompt.shape[1]
    n_s = x_sample.shape[0]
    w_buf = cache_k.shape[2]
    n_p = n_b * seq
    n_tok = n_p + n_s
    row1 = lambda v: v.reshape(1, -1)

    w_in_bf = w_in[0].astype(BF16)
    w_out_bf = w_out[0].astype(BF16)
    wg_tiles = jnp.concatenate([_block_diag_tiles(w_rg[0]), _block_diag_tiles(w_ig[0])],
                               axis=-1).astype(BF16)
    wr_pad = jnp.pad(w_router[0], ((0, 0), (0, LANES - N_EXPERTS)))
    br_pad = jnp.pad(b_router[0], (0, LANES - N_EXPERTS), constant_values=NEG_BIG).reshape(1, LANES)
    b_up_perm = jnp.swapaxes(b_up[0].reshape(N_EXPERTS, 2 * D_FF // UP_TILE, UP_TILE // 2, 2),
                             2, 3).reshape(N_EXPERTS, 1, 2 * D_FF)
    b_dn = b_down[0].reshape(N_EXPERTS, 1, D_MODEL)
    nm, nf, nfin = row1(norm_mix[0]), row1(norm_ffn[0]), row1(norm_final)
    cw, cb = conv_w[0], row1(conv_b[0])
    brg, big, lam = row1(b_rg[0]), row1(b_ig[0]), row1(lru_lambda[0])
    sinks = attn_sinks[0]
    rnn_w = (cw, cb, wg_tiles, brg, big, lam)
    rnn_w_specs = [_full((CONV_W, D_RNN)), _full((1, D_RNN)),
                   _full((N_GATE_TILES, GATE_TILE, 2 * GATE_TILE)),
                   _full((1, D_RNN)), _full((1, D_RNN)), _full((1, D_RNN))]

    xp2 = x_prompt.reshape(n_p, D_MODEL)
    xr_p, q_p, kv_p, gr_p, ga_p = _in_proj(xp2, nm, w_in_bf, TM_IN)
    x_sm = jnp.concatenate([x_sample.reshape(n_s, D_MODEL), meta_tokens], axis=0)
    xr_sm, q_sm, kv_sm, gr_sm, ga_sm = _in_proj(x_sm, nm, w_in_bf, n_s + N_META)
    xr_s, q_s, kv_s, gr_s, ga_s = (t[:n_s] for t in (xr_sm, q_sm, kv_sm, gr_sm, ga_sm))
    xr_m, kv_m = xr_sm[n_s:], kv_sm[n_s:]

    h_meta = pl.pallas_call(
        _rnn_meta_kernel,
        in_specs=[_full((N_META, D_RNN))] + rnn_w_specs,
        out_specs=_full((1, D_RNN)),
        out_shape=jax.ShapeDtypeStruct((1, D_RNN), F32),
        grid=(1,),
        compiler_params=_cparams(1),
        name="rnn_meta",
    )(xr_m, *rnn_w)

    tt = TT_RNN
    blk3 = pl.BlockSpec((n_b, tt, D_RNN), lambda j: (0, j, 0))
    m_rnn_p, h_last_p = pl.pallas_call(
        functools.partial(_rnn_prompt_kernel, n_b=n_b, tt=tt),
        grid=(seq // tt,),
        in_specs=[blk3, blk3, _full((N_META, D_RNN)), _full((1, D_RNN))] + rnn_w_specs,
        out_specs=[blk3, _full((n_b, D_RNN))],
        out_shape=[jax.ShapeDtypeStruct((n_b, seq, D_RNN), BF16),
                   jax.ShapeDtypeStruct((n_b, D_RNN), F32)],
        scratch_shapes=[pltpu.VMEM((n_b, 8, D_RNN), F32),
                        pltpu.VMEM((D_RNN // LANES, n_b * tt, LANES), F32),
                        pltpu.VMEM((D_RNN // LANES, n_b * tt, LANES), F32),
                        pltpu.VMEM((D_RNN // LANES, n_b * tt, LANES), F32),
                        pltpu.VMEM((D_RNN // LANES, n_b, LANES), F32)],
        compiler_params=_cparams(1),
        name="rnn_prompt",
    )(xr_p.reshape(n_b, seq, D_RNN), gr_p.reshape(n_b, seq, D_RNN), xr_m, h_meta, *rnn_w)

    sc_t = jnp.swapaxes(state_conv[0], 0, 1)
    m_rnn_s, h_new_s = pl.pallas_call(
        _rnn_sample_kernel,
        grid=(1,),
        in_specs=[_full((n_s, D_RNN)), _full((n_s, D_RNN)), _full((CONV_W - 1, n_s, D_RNN)),
                  _full((n_s, D_RNN))] + rnn_w_specs,
        out_specs=[_full((n_s, D_RNN)), _full((n_s, D_RNN))],
        out_shape=[jax.ShapeDtypeStruct((n_s, D_RNN), BF16),
                   jax.ShapeDtypeStruct((n_s, D_RNN), F32)],
        compiler_params=_cparams(1),
        name="rnn_sample",
    )(xr_s, gr_s, sc_t, state_h[0], *rnn_w)

    n_blk_seq = seq // BLOCK
    kv_meta_blk = jnp.pad(kv_m, ((BLOCK - N_META, 0), (0, 0)))
    smem_spec = pl.BlockSpec(memory_space=pltpu.SMEM)
    rb = lambda w: pl.BlockSpec((BLOCK, w), lambda b, j: (b * n_blk_seq + j, 0))
    m_attn_p = pl.pallas_call(
        _attn_prompt_kernel,
        grid=(n_b, n_blk_seq),
        in_specs=[smem_spec, rb(D_ATTN), rb(2 * D_KV),
                  pl.BlockSpec((BLOCK, 2 * D_KV),
                               lambda b, j: (jnp.maximum(b * n_blk_seq + j - 1, 0), 0)),
                  _full((BLOCK, 2 * D_KV)), rb(D_ATTN)],
        out_specs=rb(D_ATTN),
        out_shape=jax.ShapeDtypeStruct((n_p, D_ATTN), BF16),
        compiler_params=_cparams(2),
        name="attn_prompt",
    )(sinks, q_p, kv_p, kv_p, kv_meta_blk, ga_p)

    ck = cache_k[0].reshape(n_s, w_buf, D_KV)
    cv = cache_v[0].reshape(n_s, w_buf, D_KV)
    per_s = lambda a, b: pl.BlockSpec((1, a, b), lambda i: (i, 0, 0))
    m_attn_s = pl.pallas_call(
        _attn_sample_kernel,
        grid=(n_s,),
        in_specs=[smem_spec, per_s(N_HEADS, HEAD_DIM), per_s(1, 2 * D_KV), per_s(w_buf, D_KV),
                  per_s(w_buf, D_KV), per_s(N_HEADS, HEAD_DIM)],
        out_specs=per_s(N_HEADS, HEAD_DIM),
        out_shape=jax.ShapeDtypeStruct((n_s, N_HEADS, HEAD_DIM), BF16),
        compiler_params=_cparams(1),
        name="attn_sample",
    )(sinks, q_s.reshape(n_s, N_HEADS, HEAD_DIM), kv_s.reshape(n_s, 1, 2 * D_KV), ck, cv,
      ga_s.reshape(n_s, N_HEADS, HEAD_DIM)).reshape(n_s, D_ATTN)

    x1_p, hn_all, ti_p, tw_p = _merge(m_rnn_p.reshape(n_p, D_RNN), m_attn_p, xp2, w_out_bf, nf,
                                      wr_pad, br_pad, jnp.zeros((n_tok, D_MODEL), F32), n_tok, 0,
                                      TM_MERGE)
    x1_s, hn_all, ti_s, tw_s = _merge(m_rnn_s, m_attn_s, x_sample.reshape(n_s, D_MODEL), w_out_bf,
                                      nf, wr_pad, br_pad, hn_all, n_tok, n_p // n_s, n_s)

    top_i = jnp.concatenate([ti_p[:, :TOP_K], ti_s[:, :TOP_K]], axis=0)
    n_assign = n_tok * TOP_K
    n_blk = n_assign // TM_MOE + N_EXPERTS
    dest, buf_tok, blk_expert, n_used = _route(top_i, n_blk)

    ybuf = _moe(blk_expert, n_used, buf_tok, hn_all, w_up[0], b_up_perm, w_down[0], b_dn, n_blk)
    y_p = _final(dest[:n_p * TOP_K], x1_p, tw_p, nfin, ybuf)
    y_s = _final(dest[n_p * TOP_K:], x1_s, tw_s, nfin, ybuf)

    kv_p3 = kv_p.reshape(n_b, seq, 2 * D_KV)
    w_p = min(WINDOW, seq + N_META)
    new_k_p = kv_p3[:, seq - w_p:, :D_KV].reshape(1, n_b, w_p, N_KV, HEAD_DIM)
    new_v_p = kv_p3[:, seq - w_p:, D_KV:].reshape(1, n_b, w_p, N_KV, HEAD_DIM)
    new_conv_p = xr_p.reshape(n_b, seq, D_RNN)[:, seq - (CONV_W - 1):][None]
    k_new = kv_s[:, :D_KV].reshape(n_s, 1, N_KV, HEAD_DIM)
    v_new = kv_s[:, D_KV:].reshape(n_s, 1, N_KV, HEAD_DIM)
    new_k_s = jnp.concatenate([cache_k[0], k_new], axis=1)[:, -w_buf:][None]
    new_v_s = jnp.concatenate([cache_v[0], v_new], axis=1)[:, -w_buf:][None]
    new_conv_s = jnp.concatenate([state_conv[0], xr_s[:, None, :]], axis=1)[:, -(CONV_W - 1):][None]
    return (y_p.reshape(n_b, seq, D_MODEL), y_s.reshape(n_s, 1, D_MODEL), new_k_p, new_v_p,
            new_conv_p, h_last_p[None], new_k_s, new_v_s, new_conv_s, h_new_s[None])
```

```python
import functools

import jax
import jax.numpy as jnp
from jax import lax
from jax.experimental import pallas as pl
from jax.experimental.pallas import tpu as pltpu

F32 = jnp.float32
BF16 = jnp.bfloat16

D_MODEL = 1024
N_META = 16
D_RNN = 1024
N_GATE_BLOCKS = 16
GATE_BW = D_RNN // N_GATE_BLOCKS
CONV_W = 4
LRU_C = 8.0
N_HEADS = 16
HEAD_DIM = 64
N_KV = 2
GROUP = N_HEADS // N_KV
D_ATTN = N_HEADS * HEAD_DIM
D_KV = N_KV * HEAD_DIM
WINDOW = 128
BLOCK = 128
PAST_LEN = 16384
N_EXPERTS = 32
TOP_K = 4
D_FF = 1024
SWIGLU_LIMIT = 7.0
SWIGLU_ALPHA = 1.702
EPS = 1e-6

GATE_TILE = 256
N_GATE_TILES = D_RNN // GATE_TILE
LANES = 128
NEG_BIG = -1e30

TM_IN = 256
TT_RNN = 64
TM_MERGE = 256
TM_MOE = 256
TM_FIN = 128
VMEM_LIMIT = 56 * 1024 * 1024


def _cparams(n_grid_dims):
    return pltpu.CompilerParams(
        dimension_semantics=("arbitrary",) * n_grid_dims, vmem_limit_bytes=VMEM_LIMIT)


def _full(shape):
    return pl.BlockSpec(shape, lambda *_: (0,) * len(shape))


def _in_proj_kernel(x_ref, g_ref, w_ref, xr_ref, q_ref, kv_ref, gr_ref, ga_ref):
    x = x_ref[...]
    ms = jnp.mean(x * x, axis=-1, keepdims=True)
    h = (x * lax.rsqrt(ms + EPS) * g_ref[...]).astype(BF16)

    def proj(lo, hi):
        return jnp.dot(h, w_ref[:, lo:hi], preferred_element_type=F32)

    o_q = D_RNN
    o_k = o_q + D_ATTN
    o_gr = o_k + 2 * D_KV
    o_ga = o_gr + D_RNN
    xr_ref[...] = proj(0, o_q)
    q_ref[...] = (proj(o_q, o_k) * (HEAD_DIM ** -0.5)).astype(BF16)
    kv_ref[...] = proj(o_k, o_gr)
    gr_ref[...] = proj(o_gr, o_ga).astype(BF16)
    ga_ref[...] = proj(o_ga, o_ga + D_ATTN).astype(BF16)


def _in_proj(x, g, w_in_bf, tm):
    rows = x.shape[0]
    d_in = w_in_bf.shape[1]
    row = lambda w: pl.BlockSpec((tm, w), lambda i: (i, 0))
    return pl.pallas_call(
        _in_proj_kernel,
        grid=(rows // tm,),
        in_specs=[row(D_MODEL), _full((1, D_MODEL)), _full((D_MODEL, d_in))],
        out_specs=[row(D_RNN), row(D_ATTN), row(2 * D_KV), row(D_RNN), row(D_ATTN)],
        out_shape=[
            jax.ShapeDtypeStruct((rows, D_RNN), F32),
            jax.ShapeDtypeStruct((rows, D_ATTN), BF16),
            jax.ShapeDtypeStruct((rows, 2 * D_KV), F32),
            jax.ShapeDtypeStruct((rows, D_RNN), BF16),
            jax.ShapeDtypeStruct((rows, D_ATTN), BF16),
        ],
        compiler_params=_cparams(1),
        name="in_proj",
    )(x, g, w_in_bf)


def _softplus(x):
    return jnp.maximum(x, 0.0) + jnp.log1p(jnp.exp(-jnp.abs(x)))


def _lru_coeffs(u, wg_ref, brg, big, lam):
    ub = u.astype(BF16)
    sp = _softplus(-lam)
    a_parts, i_parts, m_parts = [], [], []
    for g in range(N_GATE_TILES):
        sl = slice(g * GATE_TILE, (g + 1) * GATE_TILE)
        zz = jnp.dot(ub[:, sl], wg_ref[g], preferred_element_type=F32)
        r = jax.nn.sigmoid(zz[:, :GATE_TILE] + brg[:, sl])
        i = jax.nn.sigmoid(zz[:, GATE_TILE:] + big[:, sl])
        log_a = (-LRU_C) * r * sp[:, sl]
        a = jnp.exp(log_a)
        m = jnp.sqrt(1.0 - a * a)
        a_parts.append(a)
        i_parts.append(i)
        m_parts.append(m)
    cat = lambda ps: jnp.concatenate(ps, axis=-1)
    return cat(a_parts), cat(i_parts), cat(m_parts)


def _conv(ext, t, cw, cb):
    out = cb
    for j in range(CONV_W):
        s = CONV_W - 1 - j
        out = out + cw[j:j + 1, :] * ext[8 - s:8 - s + t, :]
    return out


def _rnn_meta_kernel(xr_ref, cw_ref, cb_ref, wg_ref, brg_ref, big_ref, lam_ref, h_ref):
    x = xr_ref[...]
    ext = jnp.concatenate([jnp.zeros((8, D_RNN), F32), x], axis=0)
    u = _conv(ext, N_META, cw_ref[...], cb_ref[...])
    a, i, m = _lru_coeffs(u, wg_ref, brg_ref[...], big_ref[...], lam_ref[...])
    first = lax.broadcasted_iota(jnp.int32, (N_META, 1), 0) == 0
    b = jnp.where(first, 1.0, m) * i * u
    h = jnp.zeros((1, D_RNN), F32)
    for t in range(N_META):
        h = a[t:t + 1, :] * h + b[t:t + 1, :]
    h_ref[...] = h


def _rnn_prompt_kernel(xr_ref, gr_ref, xm_ref, h0_ref, cw_ref, cb_ref, wg_ref, brg_ref, big_ref,
                       lam_ref, m_ref, hl_ref, halo_s, a_s, b_s, y_s, h_s, *, n_b, tt):
    j = pl.program_id(0)

    n_l = D_RNN // LANES
    lane_tile = lambda l: slice(l * LANES, (l + 1) * LANES)

    @pl.when(j == 0)
    def _():
        for l in range(n_l):
            h_s[l] = jnp.broadcast_to(h0_ref[:, lane_tile(l)], (n_b, LANES))
        for b in range(n_b):
            halo_s[b] = xm_ref[N_META - 8:N_META, :]

    cw = cw_ref[...]
    cb = cb_ref[...]

    def per_batch(b, c):
        x = xr_ref[b]
        ext = jnp.concatenate([halo_s[b], x], axis=0)
        halo_s[b] = x[tt - 8:tt, :]
        u = _conv(ext, tt, cw, cb)
        a, i, m = _lru_coeffs(u, wg_ref, brg_ref[...], big_ref[...], lam_ref[...])
        bb = m * i * u
        row0 = pl.multiple_of(b * tt, tt)
        for l in range(n_l):
            a_s[l, pl.ds(row0, tt), :] = a[:, lane_tile(l)]
            b_s[l, pl.ds(row0, tt), :] = bb[:, lane_tile(l)]
        return c

    lax.fori_loop(0, n_b, per_batch, 0)

    def chunk(c, hs):
        hs = list(hs)
        for k in range(8):
            t = c * 8 + k
            for l in range(n_l):
                at = a_s[l, pl.ds(t, n_b, stride=tt), :]
                bt = b_s[l, pl.ds(t, n_b, stride=tt), :]
                hs[l] = at * hs[l] + bt
                y_s[l, pl.ds(t, n_b, stride=tt), :] = hs[l]
        return tuple(hs)

    hs = lax.fori_loop(0, tt // 8, chunk, tuple(h_s[l] for l in range(n_l)))
    for l in range(n_l):
        h_s[l] = hs[l]
        hl_ref[:, lane_tile(l)] = hs[l]

    def gate_out(b, c):
        row0 = pl.multiple_of(b * tt, tt)
        g = gr_ref[b].astype(F32)
        y = jnp.concatenate([y_s[l, pl.ds(row0, tt), :] for l in range(n_l)], axis=-1)
        m_ref[b] = (jax.nn.sigmoid(g) * y).astype(BF16)
        return c

    lax.fori_loop(0, n_b, gate_out, 0)


def _rnn_sample_kernel(xr_ref, gr_ref, sc_ref, h0_ref, cw_ref, cb_ref, wg_ref, brg_ref, big_ref,
                       lam_ref, m_ref, hn_ref):
    cw = cw_ref[...]
    u = cb_ref[...] + cw[CONV_W - 1:CONV_W, :] * xr_ref[...]
    for j in range(CONV_W - 1):
        u = u + cw[j:j + 1, :] * sc_ref[j]
    a, i, m = _lru_coeffs(u, wg_ref, brg_ref[...], big_ref[...], lam_ref[...])
    h = a * h0_ref[...] + m * i * u
    hn_ref[...] = h
    m_ref[...] = (jax.nn.sigmoid(gr_ref[...].astype(F32)) * h).astype(BF16)


def _alibi_slope(h):
    return 2.0 ** (-8.0 * (h + 1) / N_HEADS)


def _attn_prompt_kernel(sink_ref, q_ref, kc_ref, kp_ref, km_ref, ga_ref, o_ref):
    j = pl.program_id(1)
    first = j == 0
    kv_prev = jnp.where(first, km_ref[...], kp_ref[...])
    kv = jnp.concatenate([kv_prev, kc_ref[...]], axis=0)
    q = q_ref[...]
    qi = lax.broadcasted_iota(jnp.int32, (BLOCK, 2 * BLOCK), 0)
    ci = lax.broadcasted_iota(jnp.int32, (BLOCK, 2 * BLOCK), 1)
    dist = qi + BLOCK - ci
    c_min = jnp.where(first, BLOCK - N_META, 0)
    valid = (dist >= 0) & (dist < WINDOW) & (ci >= c_min)
    distf = dist.astype(F32)
    outs = []
    for h in range(N_HEADS):
        g = h // GROUP
        kh = kv[:, g * HEAD_DIM:(g + 1) * HEAD_DIM].astype(BF16)
        vh = kv[:, D_KV + g * HEAD_DIM:D_KV + (g + 1) * HEAD_DIM].astype(BF16)
        qh = q[:, h * HEAD_DIM:(h + 1) * HEAD_DIM]
        s = lax.dot_general(qh, kh, (((1,), (1,)), ((), ())), preferred_element_type=F32)
        s = jnp.where(valid, s - _alibi_slope(h) * distf, NEG_BIG)
        sink = sink_ref[h]
        mx = jnp.maximum(jnp.max(s, axis=-1, keepdims=True), sink)
        p = jnp.exp(s - mx)
        den = jnp.sum(p, axis=-1, keepdims=True) + jnp.exp(sink - mx)
        o = jnp.dot(p.astype(BF16), vh, preferred_element_type=F32)
        outs.append(o / den)
    y = jnp.concatenate(outs, axis=-1)
    o_ref[...] = (jax.nn.sigmoid(ga_ref[...].astype(F32)) * y).astype(BF16)


def _attn_sample_kernel(sink_ref, q_ref, kvn_ref, ck_ref, cv_ref, ga_ref, o_ref):
    w_buf = ck_ref.shape[1]
    q = q_ref[0].astype(F32)
    ck = ck_ref[0]
    cv = cv_ref[0]
    kvn = kvn_ref[0]
    ci = lax.broadcasted_iota(jnp.int32, (GROUP, w_buf), 1)
    dist = w_buf - ci
    valid = dist < WINDOW
    distf = dist.astype(F32)
    hrow = lax.broadcasted_iota(jnp.int32, (GROUP, 1), 0)
    outs = []
    for g in range(N_KV):
        qg = q[g * GROUP:(g + 1) * GROUP, :]
        kg = ck[:, g * HEAD_DIM:(g + 1) * HEAD_DIM]
        vg = cv[:, g * HEAD_DIM:(g + 1) * HEAD_DIM]
        kn = kvn[:, g * HEAD_DIM:(g + 1) * HEAD_DIM]
        vn = kvn[:, D_KV + g * HEAD_DIM:D_KV + (g + 1) * HEAD_DIM]
        slope = jnp.zeros((GROUP, 1), F32)
        sink = jnp.zeros((GROUP, 1), F32)
        for r in range(GROUP):
            slope = jnp.where(hrow == r, _alibi_slope(g * GROUP + r), slope)
            sink = jnp.where(hrow == r, sink_ref[g * GROUP + r], sink)
        s = lax.dot_general(qg.astype(BF16), kg.astype(BF16), (((1,), (1,)), ((), ())),
                            preferred_element_type=F32)
        s = jnp.where(valid, s - slope * distf, NEG_BIG)
        sn = jnp.sum(qg.astype(BF16).astype(F32) * kn.astype(BF16).astype(F32), axis=-1, keepdims=True)
        mx = jnp.maximum(jnp.maximum(jnp.max(s, axis=-1, keepdims=True), sn), sink)
        p = jnp.exp(s - mx)
        pn = jnp.exp(sn - mx)
        den = jnp.sum(p, axis=-1, keepdims=True) + pn + jnp.exp(sink - mx)
        o = jnp.dot(p.astype(BF16), vg.astype(BF16), preferred_element_type=F32)
        o = o + pn.astype(BF16).astype(F32) * vn.astype(BF16).astype(F32)
        outs.append(o / den)
    y = jnp.concatenate(outs, axis=0)
    o_ref[0] = (jax.nn.sigmoid(ga_ref[0].astype(F32)) * y).astype(BF16)


def _merge_kernel(mr_ref, ma_ref, x_ref, wo_ref, nf_ref, wr_ref, br_ref, *rest):
    x1_ref, hn_ref, ti_ref, tw_ref = rest[-4:]
    mm = jnp.dot(mr_ref[...], wo_ref[:D_RNN, :], preferred_element_type=F32)
    mm = mm + jnp.dot(ma_ref[...], wo_ref[D_RNN:, :], preferred_element_type=F32)
    x1 = x_ref[...] + mm
    x1_ref[...] = x1
    ms = jnp.mean(x1 * x1, axis=-1, keepdims=True)
    hn = x1 * lax.rsqrt(ms + EPS) * nf_ref[...]
    hn_ref[...] = hn
    logits = jnp.dot(hn.astype(BF16), wr_ref[...], preferred_element_type=F32) + br_ref[...]
    lane = lax.broadcasted_iota(jnp.int32, logits.shape, 1)
    ti = jnp.zeros(logits.shape, jnp.int32)
    tv = jnp.zeros(logits.shape, F32)
    l = logits
    v0 = None
    for k in range(TOP_K):
        mx = jnp.max(l, axis=-1, keepdims=True)
        idx = jnp.min(jnp.where(l == mx, lane, LANES), axis=-1, keepdims=True)
        if k == 0:
            v0 = mx
        ti = jnp.where(lane == k, idx, ti)
        tv = jnp.where(lane == k, jnp.exp(mx - v0), tv)
        l = jnp.where(lane == idx, NEG_BIG * 2, l)
    ti_ref[...] = ti
    tw_ref[...] = tv / jnp.sum(tv, axis=-1, keepdims=True)


def _merge(m_rnn, m_attn, x, w_out_bf, nf, wr_pad, br_pad, hn_all, n_all, row_block0, tm):
    rows = x.shape[0]
    row = lambda w: pl.BlockSpec((tm, w), lambda i: (i, 0))
    in_specs = [row(D_RNN), row(D_ATTN), row(D_MODEL), _full((D_RNN + D_ATTN, D_MODEL)),
                _full((1, D_MODEL)), _full((D_MODEL, LANES)), _full((1, LANES))]
    args = [m_rnn, m_attn, x, w_out_bf, nf, wr_pad, br_pad]
    aliases = {}
    if hn_all is not None:
        in_specs.append(pl.BlockSpec(memory_space=pl.ANY))
        args.append(hn_all)
        aliases = {len(args) - 1: 1}
    return pl.pallas_call(
        _merge_kernel,
        grid=(rows // tm,),
        in_specs=in_specs,
        out_specs=[row(D_MODEL),
                   pl.BlockSpec((tm, D_MODEL), lambda i: (i + row_block0, 0)),
                   row(LANES), row(LANES)],
        out_shape=[
            jax.ShapeDtypeStruct((rows, D_MODEL), F32),
            jax.ShapeDtypeStruct((n_all, D_MODEL), F32),
            jax.ShapeDtypeStruct((rows, LANES), jnp.int32),
            jax.ShapeDtypeStruct((rows, LANES), F32),
        ],
        input_output_aliases=aliases,
        compiler_params=_cparams(1),
        name="merge_router",
    )(*args)


UP_TILE = 256


def _deinterleave_matrix():
    half = UP_TILE // 2
    r = lax.broadcasted_iota(jnp.int32, (UP_TILE, UP_TILE), 0)
    c = lax.broadcasted_iota(jnp.int32, (UP_TILE, UP_TILE), 1)
    src = jnp.where(c < half, 2 * c, 2 * (c - half) + 1)
    return jnp.where(r == src, 1.0, 0.0).astype(BF16)


GATHER_AHEAD = 2


def _moe_kernel(be_ref, nu_ref, tok0_ref, tok1_ref, tok2_ref, hn_hbm, wu_ref, bu_ref, wd_ref,
                bd_ref, y_ref, xbuf, xb_s, wu_s, wd_s, sems):
    i = pl.program_id(0)
    nu = nu_ref[0]
    half = UP_TILE // 2
    n_up = 2 * D_FF // UP_TILE
    slot = i % GATHER_AHEAD

    def row_copy(tok_ref, r, s):
        return pltpu.make_async_copy(hn_hbm.at[pl.ds(tok_ref[0, 0, r], 1)],
                                     xbuf.at[s, pl.ds(r, 1)], sems.at[s])

    @pl.when(i == 0)
    def _():
        def issue(r, c):
            row_copy(tok0_ref, r, 0).start()
            row_copy(tok1_ref, r, 1).start()
            return c

        lax.fori_loop(0, TM_MOE, issue, 0)

    @pl.when(i < nu + GATHER_AHEAD)
    def _():
        pltpu.make_async_copy(hn_hbm.at[pl.ds(0, TM_MOE)], xbuf.at[slot], sems.at[slot]).wait()

    @pl.when(i < nu)
    def _():
        xb_s[...] = xbuf[slot].astype(BF16)

        @pl.when((i == 0) | (be_ref[i] != be_ref[jnp.maximum(i - 1, 0)]))
        def _():
            perm = _deinterleave_matrix()
            for c in range(n_up):
                cols = slice(c * UP_TILE, (c + 1) * UP_TILE)
                blk = wu_ref[0, :, cols].astype(BF16)
                wu_s[:, cols] = jnp.dot(blk, perm, preferred_element_type=F32).astype(BF16)
            wd_s[...] = wd_ref[0].astype(BF16)

        for r in range(TM_MOE):
            row_copy(tok2_ref, r, slot).start()

        x = xb_s[...]
        z = jnp.dot(x, wu_s[...], preferred_element_type=F32) + bu_ref[0]
        zg = jnp.concatenate([z[:, c * UP_TILE:c * UP_TILE + half] for c in range(n_up)], axis=-1)
        zl = jnp.concatenate([z[:, c * UP_TILE + half:(c + 1) * UP_TILE] for c in range(n_up)],
                             axis=-1)
        xg = jnp.minimum(zg, SWIGLU_LIMIT)
        xl = jnp.clip(zl, -SWIGLU_LIMIT, SWIGLU_LIMIT)
        act = xg * jax.nn.sigmoid(SWIGLU_ALPHA * xg) * (xl + 1.0)
        y_ref[...] = jnp.dot(act.astype(BF16), wd_s[...], preferred_element_type=F32) + bd_ref[0]

    @pl.when(i >= nu)
    def _():
        y_ref[...] = jnp.zeros_like(y_ref)


def _moe(blk_expert, n_used, buf_tok, hn_all, w_up, b_up_perm, w_down, b_down, n_blk):
    ew = lambda r, c: pl.BlockSpec((1, r, c), lambda i, be, nu: (be[i], 0, 0))
    tok = lambda f: pl.BlockSpec((1, 1, TM_MOE), lambda i, be, nu: (f(i), 0, 0),
                                 memory_space=pltpu.SMEM)
    return pl.pallas_call(
        _moe_kernel,
        grid_spec=pltpu.PrefetchScalarGridSpec(
            num_scalar_prefetch=2,
            grid=(n_blk,),
            in_specs=[
                tok(lambda i: i), tok(lambda i: jnp.minimum(i + 1, n_blk - 1)),
                tok(lambda i: jnp.minimum(i + GATHER_AHEAD, n_blk - 1)),
                pl.BlockSpec(memory_space=pl.ANY),
                ew(D_MODEL, 2 * D_FF), ew(1, 2 * D_FF), ew(D_FF, D_MODEL), ew(1, D_MODEL),
            ],
            out_specs=pl.BlockSpec((TM_MOE, D_MODEL), lambda i, be, nu: (i, 0)),
            scratch_shapes=[pltpu.VMEM((GATHER_AHEAD, TM_MOE, D_MODEL), F32),
                            pltpu.VMEM((TM_MOE, D_MODEL), BF16),
                            pltpu.VMEM((D_MODEL, 2 * D_FF), BF16),
                            pltpu.VMEM((D_FF, D_MODEL), BF16),
                            pltpu.SemaphoreType.DMA((GATHER_AHEAD,))],
        ),
        out_shape=jax.ShapeDtypeStruct((n_blk * TM_MOE, D_MODEL), F32),
        compiler_params=_cparams(1),
        name="moe_experts",
    )(blk_expert, n_used, buf_tok, buf_tok, buf_tok, hn_all, w_up, b_up_perm, w_down, b_down)


def _final_kernel(dest_ref, x1_ref, tw_ref, nf_ref, y_hbm, o_ref, ybuf, sem):
    for n in range(TM_FIN * TOP_K):
        r, k = divmod(n, TOP_K)
        pltpu.make_async_copy(y_hbm.at[pl.ds(dest_ref[0, 0, n], 1)],
                              ybuf.at[pl.ds(k * TM_FIN + r, 1)], sem).start()
    pltpu.make_async_copy(y_hbm.at[pl.ds(0, TOP_K * TM_FIN)], ybuf, sem).wait()
    tw = tw_ref[...]
    x = x1_ref[...]
    for k in range(TOP_K):
        x = x + tw[:, k:k + 1] * ybuf[k * TM_FIN:(k + 1) * TM_FIN, :]
    ms = jnp.mean(x * x, axis=-1, keepdims=True)
    o_ref[...] = x * lax.rsqrt(ms + EPS) * nf_ref[...]


def _final(dest, x1, tw, nf, ybuf):
    rows = x1.shape[0]
    n_tiles = rows // TM_FIN
    row = lambda w: pl.BlockSpec((TM_FIN, w), lambda i: (i, 0))
    return pl.pallas_call(
        _final_kernel,
        grid=(n_tiles,),
        in_specs=[
            pl.BlockSpec((1, 1, TM_FIN * TOP_K), lambda i: (i, 0, 0), memory_space=pltpu.SMEM),
            row(D_MODEL), row(LANES), _full((1, D_MODEL)),
            pl.BlockSpec(memory_space=pl.ANY),
        ],
        out_specs=row(D_MODEL),
        out_shape=jax.ShapeDtypeStruct((rows, D_MODEL), F32),
        scratch_shapes=[pltpu.VMEM((TOP_K * TM_FIN, D_MODEL), F32), pltpu.SemaphoreType.DMA],
        compiler_params=_cparams(1),
        name="combine_final",
    )(dest.reshape(n_tiles, 1, TM_FIN * TOP_K), x1, tw, nf, ybuf)


def _block_diag_tiles(w):
    per = GATE_TILE // GATE_BW
    w4 = w.reshape(N_GATE_TILES, per, GATE_BW, GATE_BW)
    eye = jnp.eye(per, dtype=w.dtype)
    return jnp.einsum("gacd,ab->gacbd", w4, eye).reshape(N_GATE_TILES, GATE_TILE, GATE_TILE)


def _route(top_i, n_blk):
    a = top_i.size
    flat_e = top_i.reshape(-1)
    onehot = (flat_e[:, None] == jnp.arange(N_EXPERTS, dtype=jnp.int32)[None, :]).astype(jnp.int32)
    csum = jnp.cumsum(onehot, axis=0)
    rank = jnp.sum(csum * onehot, axis=1) - 1
    counts = csum[-1]
    padded = ((counts + (TM_MOE - 1)) // TM_MOE) * TM_MOE
    ends = jnp.cumsum(padded).astype(jnp.int32)
    start_padded = ends - padded
    dest = (start_padded[flat_e] + rank).astype(jnp.int32)
    buf_tok = jnp.zeros((n_blk * TM_MOE,), jnp.int32).at[dest].set(
        jnp.arange(a, dtype=jnp.int32) // TOP_K)
    blk_start = jnp.arange(n_blk, dtype=jnp.int32) * TM_MOE
    blk_expert = jnp.minimum(
        jnp.sum((ends[None, :] <= blk_start[:, None]).astype(jnp.int32), axis=1), N_EXPERTS - 1)
    n_used = (ends[-1:] // TM_MOE).astype(jnp.int32)
    return dest, buf_tok.reshape(n_blk, 1, TM_MOE), blk_expert.astype(jnp.int32), n_used


def kernel(x_prompt, x_sample, cache_k, cache_v, state_conv, state_h, meta_tokens, norm_mix, w_in, conv_w, conv_b, w_rg, b_rg, w_ig, b_ig, lru_lambda, attn_sinks, w_out, norm_ffn, w_router, b_router, w_up, b_up, w_down, b_down, norm_final):
    n_b, seq = x_prompt.shape[0], x_prompt.shape[1]
    n_s = x_sample.shape[0]
    w_buf = cache_k.shape[2]
    n_p = n_b * seq
    n_tok = n_p + n_s
    row1 = lambda v: v.reshape(1, -1)

    w_in_bf = w_in[0].astype(BF16)
    w_out_bf = w_out[0].astype(BF16)
    wg_tiles = jnp.concatenate([_block_diag_tiles(w_rg[0]), _block_diag_tiles(w_ig[0])],
                               axis=-1).astype(BF16)
    wr_pad = jnp.pad(w_router[0], ((0, 0), (0, LANES - N_EXPERTS))).astype(BF16)
    br_pad = jnp.pad(b_router[0], (0, LANES - N_EXPERTS), constant_values=NEG_BIG).reshape(1, LANES)
    b_up_perm = jnp.swapaxes(b_up[0].reshape(N_EXPERTS, 2 * D_FF // UP_TILE, UP_TILE // 2, 2),
                             2, 3).reshape(N_EXPERTS, 1, 2 * D_FF)
    b_dn = b_down[0].reshape(N_EXPERTS, 1, D_MODEL)
    nm, nf, nfin = row1(norm_mix[0]), row1(norm_ffn[0]), row1(norm_final)
    cw, cb = conv_w[0], row1(conv_b[0])
    brg, big, lam = row1(b_rg[0]), row1(b_ig[0]), row1(lru_lambda[0])
    sinks = attn_sinks[0]
    rnn_w = (cw, cb, wg_tiles, brg, big, lam)
    rnn_w_specs = [_full((CONV_W, D_RNN)), _full((1, D_RNN)),
                   _full((N_GATE_TILES, GATE_TILE, 2 * GATE_TILE)),
                   _full((1, D_RNN)), _full((1, D_RNN)), _full((1, D_RNN))]

    xp2 = x_prompt.reshape(n_p, D_MODEL)
    xr_p, q_p, kv_p, gr_p, ga_p = _in_proj(xp2, nm, w_in_bf, TM_IN)
    x_sm = jnp.concatenate([x_sample.reshape(n_s, D_MODEL), meta_tokens], axis=0)
    xr_sm, q_sm, kv_sm, gr_sm, ga_sm = _in_proj(x_sm, nm, w_in_bf, n_s + N_META)
    xr_s, q_s, kv_s, gr_s, ga_s = (t[:n_s] for t in (xr_sm, q_sm, kv_sm, gr_sm, ga_sm))
    xr_m, kv_m = xr_sm[n_s:], kv_sm[n_s:]

    h_meta = pl.pallas_call(
        _rnn_meta_kernel,
        in_specs=[_full((N_META, D_RNN))] + rnn_w_specs,
        out_specs=_full((1, D_RNN)),
        out_shape=jax.ShapeDtypeStruct((1, D_RNN), F32),
        grid=(1,),
        compiler_params=_cparams(1),
        name="rnn_meta",
    )(xr_m, *rnn_w)

    tt = TT_RNN
    blk3 = pl.BlockSpec((n_b, tt, D_RNN), lambda j: (0, j, 0))
    m_rnn_p, h_last_p = pl.pallas_call(
        functools.partial(_rnn_prompt_kernel, n_b=n_b, tt=tt),
        grid=(seq // tt,),
        in_specs=[blk3, blk3, _full((N_META, D_RNN)), _full((1, D_RNN))] + rnn_w_specs,
        out_specs=[blk3, _full((n_b, D_RNN))],
        out_shape=[jax.ShapeDtypeStruct((n_b, seq, D_RNN), BF16),
                   jax.ShapeDtypeStruct((n_b, D_RNN), F32)],
        scratch_shapes=[pltpu.VMEM((n_b, 8, D_RNN), F32),
                        pltpu.VMEM((D_RNN // LANES, n_b * tt, LANES), F32),
                        pltpu.VMEM((D_RNN // LANES, n_b * tt, LANES), F32),
                        pltpu.VMEM((D_RNN // LANES, n_b * tt, LANES), F32),
                        pltpu.VMEM((D_RNN // LANES, n_b, LANES), F32)],
        compiler_params=_cparams(1),
        name="rnn_prompt",
    )(xr_p.reshape(n_b, seq, D_RNN), gr_p.reshape(n_b, seq, D_RNN), xr_m, h_meta, *rnn_w)

    sc_t = jnp.swapaxes(state_conv[0], 0, 1)
    m_rnn_s, h_new_s = pl.pallas_call(
        _rnn_sample_kernel,
        grid=(1,),
        in_specs=[_full((n_s, D_RNN)), _full((n_s, D_RNN)), _full((CONV_W - 1, n_s, D_RNN)),
                  _full((n_s, D_RNN))] + rnn_w_specs,
        out_specs=[_full((n_s, D_RNN)), _full((n_s, D_RNN))],
        out_shape=[jax.ShapeDtypeStruct((n_s, D_RNN), BF16),
                   jax.ShapeDtypeStruct((n_s, D_RNN), F32)],
        compiler_params=_cparams(1),
        name="rnn_sample",
    )(xr_s, gr_s, sc_t, state_h[0], *rnn_w)

    n_blk_seq = seq // BLOCK
    kv_meta_blk = jnp.pad(kv_m, ((BLOCK - N_META, 0), (0, 0)))
    smem_spec = pl.BlockSpec(memory_space=pltpu.SMEM)
    rb = lambda w: pl.BlockSpec((BLOCK, w), lambda b, j: (b * n_blk_seq + j, 0))
    m_attn_p = pl.pallas_call(
        _attn_prompt_kernel,
        grid=(n_b, n_blk_seq),
        in_specs=[smem_spec, rb(D_ATTN), rb(2 * D_KV),
                  pl.BlockSpec((BLOCK, 2 * D_KV),
                               lambda b, j: (jnp.maximum(b * n_blk_seq + j - 1, 0), 0)),
                  _full((BLOCK, 2 * D_KV)), rb(D_ATTN)],
        out_specs=rb(D_ATTN),
        out_shape=jax.ShapeDtypeStruct((n_p, D_ATTN), BF16),
        compiler_params=_cparams(2),
        name="attn_prompt",
    )(sinks, q_p, kv_p, kv_p, kv_meta_blk, ga_p)

    ck = cache_k[0].reshape(n_s, w_buf, D_KV)
    cv = cache_v[0].reshape(n_s, w_buf, D_KV)
    per_s = lambda a, b: pl.BlockSpec((1, a, b), lambda i: (i, 0, 0))
    m_attn_s = pl.pallas_call(
        _attn_sample_kernel,
        grid=(n_s,),
        in_specs=[smem_spec, per_s(N_HEADS, HEAD_DIM), per_s(1, 2 * D_KV), per_s(w_buf, D_KV),
                  per_s(w_buf, D_KV), per_s(N_HEADS, HEAD_DIM)],
        out_specs=per_s(N_HEADS, HEAD_DIM),
        out_shape=jax.ShapeDtypeStruct((n_s, N_HEADS, HEAD_DIM), BF16),
        compiler_params=_cparams(1),
        name="attn_sample",
    )(sinks, q_s.reshape(n_s, N_HEADS, HEAD_DIM), kv_s.reshape(n_s, 1, 2 * D_KV), ck, cv,
      ga_s.reshape(n_s, N_HEADS, HEAD_DIM)).reshape(n_s, D_ATTN)

    x1_p, hn_all, ti_p, tw_p = _merge(m_rnn_p.reshape(n_p, D_RNN), m_attn_p, xp2, w_out_bf, nf,
                                      wr_pad, br_pad, jnp.zeros((n_tok, D_MODEL), F32), n_tok, 0,
                                      TM_MERGE)
    x1_s, hn_all, ti_s, tw_s = _merge(m_rnn_s, m_attn_s, x_sample.reshape(n_s, D_MODEL), w_out_bf,
                                      nf, wr_pad, br_pad, hn_all, n_tok, n_p // n_s, n_s)

    top_i = jnp.concatenate([ti_p[:, :TOP_K], ti_s[:, :TOP_K]], axis=0)
    n_assign = n_tok * TOP_K
    n_blk = n_assign // TM_MOE + N_EXPERTS - 1 + GATHER_AHEAD
    dest, buf_tok, blk_expert, n_used = _route(top_i, n_blk)

    ybuf = _moe(blk_expert, n_used, buf_tok, hn_all, w_up[0], b_up_perm, w_down[0], b_dn, n_blk)
    y_p = _final(dest[:n_p * TOP_K], x1_p, tw_p, nfin, ybuf)
    y_s = _final(dest[n_p * TOP_K:], x1_s, tw_s, nfin, ybuf)

    kv_p3 = kv_p.reshape(n_b, seq, 2 * D_KV)
    w_p = min(WINDOW, seq + N_META)
    new_k_p = kv_p3[:, seq - w_p:, :D_KV].reshape(1, n_b, w_p, N_KV, HEAD_DIM)
    new_v_p = kv_p3[:, seq - w_p:, D_KV:].reshape(1, n_b, w_p, N_KV, HEAD_DIM)
    new_conv_p = xr_p.reshape(n_b, seq, D_RNN)[:, seq - (CONV_W - 1):][None]
    k_new = kv_s[:, :D_KV].reshape(n_s, 1, N_KV, HEAD_DIM)
    v_new = kv_s[:, D_KV:].reshape(n_s, 1, N_KV, HEAD_DIM)
    new_k_s = jnp.concatenate([cache_k[0], k_new], axis=1)[:, -w_buf:][None]
    new_v_s = jnp.concatenate([cache_v[0], v_new], axis=1)[:, -w_buf:][None]
    new_conv_s = jnp.concatenate([state_conv[0], xr_s[:, None, :]], axis=1)[:, -(CONV_W - 1):][None]
    return (y_p.reshape(n_b, seq, D_MODEL), y_s.reshape(n_s, 1, D_MODEL), new_k_p, new_v_p,
            new_conv_p, h_last_p[None], new_k_s, new_v_s, new_conv_s, h_new_s[None])
```

```python
import functools

import jax
import jax.numpy as jnp
from jax import lax
from jax.experimental import pallas as pl
from jax.experimental.pallas import tpu as pltpu

F32 = jnp.float32
BF16 = jnp.bfloat16

D_MODEL = 1024
N_META = 16
D_RNN = 1024
N_GATE_BLOCKS = 16
GATE_BW = D_RNN // N_GATE_BLOCKS
CONV_W = 4
LRU_C = 8.0
N_HEADS = 16
HEAD_DIM = 64
N_KV = 2
GROUP = N_HEADS // N_KV
D_ATTN = N_HEADS * HEAD_DIM
D_KV = N_KV * HEAD_DIM
WINDOW = 128
BLOCK = 128
PAST_LEN = 16384
N_EXPERTS = 32
TOP_K = 4
D_FF = 1024
SWIGLU_LIMIT = 7.0
SWIGLU_ALPHA = 1.702
EPS = 1e-6

GATE_TILE = 256
N_GATE_TILES = D_RNN // GATE_TILE
LANES = 128
NEG_BIG = -1e30

TM_IN = 256
TT_RNN = 64
TM_MERGE = 256
TM_MOE = 256
TM_FIN = 128
VMEM_LIMIT = 56 * 1024 * 1024


def _cparams(n_grid_dims):
    return pltpu.CompilerParams(
        dimension_semantics=("arbitrary",) * n_grid_dims, vmem_limit_bytes=VMEM_LIMIT)


def _full(shape):
    return pl.BlockSpec(shape, lambda *_: (0,) * len(shape))


def _in_proj_kernel(x_ref, g_ref, w_ref, xr_ref, q_ref, kv_ref, gr_ref, ga_ref):
    x = x_ref[...]
    ms = jnp.mean(x * x, axis=-1, keepdims=True)
    h = (x * lax.rsqrt(ms + EPS) * g_ref[...]).astype(BF16)

    def proj(lo, hi):
        return jnp.dot(h, w_ref[:, lo:hi], preferred_element_type=F32)

    o_q = D_RNN
    o_k = o_q + D_ATTN
    o_gr = o_k + 2 * D_KV
    o_ga = o_gr + D_RNN
    xr_ref[...] = proj(0, o_q)
    q_ref[...] = (proj(o_q, o_k) * (HEAD_DIM ** -0.5)).astype(BF16)
    kv_ref[...] = proj(o_k, o_gr)
    gr_ref[...] = proj(o_gr, o_ga).astype(BF16)
    ga_ref[...] = proj(o_ga, o_ga + D_ATTN).astype(BF16)


def _in_proj(x, g, w_in_bf, tm):
    rows = x.shape[0]
    d_in = w_in_bf.shape[1]
    row = lambda w: pl.BlockSpec((tm, w), lambda i: (i, 0))
    return pl.pallas_call(
        _in_proj_kernel,
        grid=(rows // tm,),
        in_specs=[row(D_MODEL), _full((1, D_MODEL)), _full((D_MODEL, d_in))],
        out_specs=[row(D_RNN), row(D_ATTN), row(2 * D_KV), row(D_RNN), row(D_ATTN)],
        out_shape=[
            jax.ShapeDtypeStruct((rows, D_RNN), F32),
            jax.ShapeDtypeStruct((rows, D_ATTN), BF16),
            jax.ShapeDtypeStruct((rows, 2 * D_KV), F32),
            jax.ShapeDtypeStruct((rows, D_RNN), BF16),
            jax.ShapeDtypeStruct((rows, D_ATTN), BF16),
        ],
        compiler_params=_cparams(1),
        name="in_proj",
    )(x, g, w_in_bf)


def _softplus(x):
    return jnp.maximum(x, 0.0) + jnp.log1p(jnp.exp(-jnp.abs(x)))


def _lru_coeffs(u, wg_ref, brg, big, lam):
    ub = u.astype(BF16)
    sp = _softplus(-lam)
    a_parts, i_parts, m_parts = [], [], []
    for g in range(N_GATE_TILES):
        sl = slice(g * GATE_TILE, (g + 1) * GATE_TILE)
        zz = jnp.dot(ub[:, sl], wg_ref[g], preferred_element_type=F32)
        r = jax.nn.sigmoid(zz[:, :GATE_TILE] + brg[:, sl])
        i = jax.nn.sigmoid(zz[:, GATE_TILE:] + big[:, sl])
        log_a = (-LRU_C) * r * sp[:, sl]
        a = jnp.exp(log_a)
        m = jnp.sqrt(1.0 - a * a)
        a_parts.append(a)
        i_parts.append(i)
        m_parts.append(m)
    cat = lambda ps: jnp.concatenate(ps, axis=-1)
    return cat(a_parts), cat(i_parts), cat(m_parts)


def _conv(ext, t, cw, cb):
    out = cb
    for j in range(CONV_W):
        s = CONV_W - 1 - j
        out = out + cw[j:j + 1, :] * ext[8 - s:8 - s + t, :]
    return out


def _rnn_meta_kernel(xr_ref, cw_ref, cb_ref, wg_ref, brg_ref, big_ref, lam_ref, h_ref):
    x = xr_ref[...]
    ext = jnp.concatenate([jnp.zeros((8, D_RNN), F32), x], axis=0)
    u = _conv(ext, N_META, cw_ref[...], cb_ref[...])
    a, i, m = _lru_coeffs(u, wg_ref, brg_ref[...], big_ref[...], lam_ref[...])
    first = lax.broadcasted_iota(jnp.int32, (N_META, 1), 0) == 0
    b = jnp.where(first, 1.0, m) * i * u
    h = jnp.zeros((1, D_RNN), F32)
    for t in range(N_META):
        h = a[t:t + 1, :] * h + b[t:t + 1, :]
    h_ref[...] = h


SUBLANES = 8


def _linear_scan(a, b, h_prev):
    t, d = a.shape
    g = t // SUBLANES
    a3 = a.reshape(g, SUBLANES, d)
    b3 = b.reshape(g, SUBLANES, d)
    row = lax.broadcasted_iota(jnp.int32, (g, SUBLANES, d), 1)
    step = 1
    while step < SUBLANES:
        keep = row >= step
        a_up = jnp.where(keep, pltpu.roll(a3, step, axis=1), 1.0)
        b_up = jnp.where(keep, pltpu.roll(b3, step, axis=1), 0.0)
        b3 = a3 * b_up + b3
        a3 = a3 * a_up
        step *= 2
    hs = []
    h = h_prev
    for k in range(g):
        hk = a3[k] * h + b3[k]
        hs.append(hk)
        h = hk[SUBLANES - 1:SUBLANES, :]
    return jnp.concatenate(hs, axis=0), h


def _rnn_prompt_kernel(xr_ref, gr_ref, xm_ref, h0_ref, cw_ref, cb_ref, wg_ref, brg_ref, big_ref,
                       lam_ref, m_ref, hl_ref, halo_s, h_s, *, n_b, tt):
    j = pl.program_id(0)

    @pl.when(j == 0)
    def _():
        h_s[...] = jnp.broadcast_to(h0_ref[...], (n_b, D_RNN))
        for b in range(n_b):
            halo_s[b] = xm_ref[N_META - 8:N_META, :]

    cw = cw_ref[...]
    cb = cb_ref[...]

    def per_batch(b, c):
        x = xr_ref[b]
        ext = jnp.concatenate([halo_s[b], x], axis=0)
        halo_s[b] = x[tt - 8:tt, :]
        u = _conv(ext, tt, cw, cb)
        a, i, m = _lru_coeffs(u, wg_ref, brg_ref[...], big_ref[...], lam_ref[...])
        y, h = _linear_scan(a, m * i * u, h_s[pl.ds(b, 1), :])
        h_s[pl.ds(b, 1), :] = h
        m_ref[b] = (jax.nn.sigmoid(gr_ref[b].astype(F32)) * y).astype(BF16)
        return c

    lax.fori_loop(0, n_b, per_batch, 0)
    hl_ref[...] = h_s[...]


def _rnn_sample_kernel(xr_ref, gr_ref, sc_ref, h0_ref, cw_ref, cb_ref, wg_ref, brg_ref, big_ref,
                       lam_ref, m_ref, hn_ref):
    cw = cw_ref[...]
    u = cb_ref[...] + cw[CONV_W - 1:CONV_W, :] * xr_ref[...]
    for j in range(CONV_W - 1):
        u = u + cw[j:j + 1, :] * sc_ref[j]
    a, i, m = _lru_coeffs(u, wg_ref, brg_ref[...], big_ref[...], lam_ref[...])
    h = a * h0_ref[...] + m * i * u
    hn_ref[...] = h
    m_ref[...] = (jax.nn.sigmoid(gr_ref[...].astype(F32)) * h).astype(BF16)


def _alibi_slope(h):
    return 2.0 ** (-8.0 * (h + 1) / N_HEADS)


def _attn_prompt_kernel(sink_ref, q_ref, kc_ref, kp_ref, km_ref, ga_ref, o_ref):
    j = pl.program_id(1)
    first = j == 0
    kv_prev = jnp.where(first, km_ref[...], kp_ref[...])
    kv = jnp.concatenate([kv_prev, kc_ref[...]], axis=0)
    q = q_ref[...]
    qi = lax.broadcasted_iota(jnp.int32, (BLOCK, 2 * BLOCK), 0)
    ci = lax.broadcasted_iota(jnp.int32, (BLOCK, 2 * BLOCK), 1)
    dist = qi + BLOCK - ci
    c_min = jnp.where(first, BLOCK - N_META, 0)
    valid = (dist >= 0) & (dist < WINDOW) & (ci >= c_min)
    distf = dist.astype(F32)
    outs = []
    for h in range(N_HEADS):
        g = h // GROUP
        kh = kv[:, g * HEAD_DIM:(g + 1) * HEAD_DIM].astype(BF16)
        vh = kv[:, D_KV + g * HEAD_DIM:D_KV + (g + 1) * HEAD_DIM].astype(BF16)
        qh = q[:, h * HEAD_DIM:(h + 1) * HEAD_DIM]
        s = lax.dot_general(qh, kh, (((1,), (1,)), ((), ())), preferred_element_type=F32)
        s = jnp.where(valid, s - _alibi_slope(h) * distf, NEG_BIG)
        sink = sink_ref[h]
        mx = jnp.maximum(jnp.max(s, axis=-1, keepdims=True), sink)
        p = jnp.exp(s - mx)
        den = jnp.sum(p, axis=-1, keepdims=True) + jnp.exp(sink - mx)
        o = jnp.dot(p.astype(BF16), vh, preferred_element_type=F32)
        outs.append(o / den)
    y = jnp.concatenate(outs, axis=-1)
    o_ref[...] = (jax.nn.sigmoid(ga_ref[...].astype(F32)) * y).astype(BF16)


SAMPLES_PER_STEP = 8


def _attn_sample_kernel(sink_ref, q_ref, kvn_ref, ck_ref, cv_ref, ga_ref, o_ref):
    w_buf = ck_ref.shape[1]
    ci = lax.broadcasted_iota(jnp.int32, (GROUP, w_buf), 1)
    dist = w_buf - ci
    valid = dist < WINDOW
    distf = dist.astype(F32)
    hrow = lax.broadcasted_iota(jnp.int32, (GROUP, 1), 0)
    bias, sinks = [], []
    for g in range(N_KV):
        slope = jnp.zeros((GROUP, 1), F32)
        sink = jnp.zeros((GROUP, 1), F32)
        for r in range(GROUP):
            slope = jnp.where(hrow == r, _alibi_slope(g * GROUP + r), slope)
            sink = jnp.where(hrow == r, sink_ref[g * GROUP + r], sink)
        bias.append(slope * distf)
        sinks.append(sink)
    for n in range(SAMPLES_PER_STEP):
        q = q_ref[n]
        ck = ck_ref[n].astype(BF16)
        cv = cv_ref[n].astype(BF16)
        kvn = kvn_ref[n].astype(BF16).astype(F32)
        outs = []
        for g in range(N_KV):
            qg = q[g * GROUP:(g + 1) * GROUP, :]
            kg = ck[:, g * HEAD_DIM:(g + 1) * HEAD_DIM]
            vg = cv[:, g * HEAD_DIM:(g + 1) * HEAD_DIM]
            kn = kvn[:, g * HEAD_DIM:(g + 1) * HEAD_DIM]
            vn = kvn[:, D_KV + g * HEAD_DIM:D_KV + (g + 1) * HEAD_DIM]
            s = lax.dot_general(qg, kg, (((1,), (1,)), ((), ())), preferred_element_type=F32)
            s = jnp.where(valid, s - bias[g], NEG_BIG)
            sn = jnp.sum(qg.astype(F32) * kn, axis=-1, keepdims=True)
            mx = jnp.maximum(jnp.maximum(jnp.max(s, axis=-1, keepdims=True), sn), sinks[g])
            p = jnp.exp(s - mx)
            pn = jnp.exp(sn - mx)
            den = jnp.sum(p, axis=-1, keepdims=True) + pn + jnp.exp(sinks[g] - mx)
            o = jnp.dot(p.astype(BF16), vg, preferred_element_type=F32)
            o = o + pn.astype(BF16).astype(F32) * vn
            outs.append(o / den)
        y = jnp.concatenate(outs, axis=0)
        o_ref[n] = (jax.nn.sigmoid(ga_ref[n].astype(F32)) * y).astype(BF16)


def _merge_kernel(mr_ref, ma_ref, x_ref, wo_ref, nf_ref, wr_ref, br_ref, *rest):
    x1_ref, hn_ref, ti_ref, tw_ref = rest[-4:]
    mm = jnp.dot(mr_ref[...], wo_ref[:D_RNN, :], preferred_element_type=F32)
    mm = mm + jnp.dot(ma_ref[...], wo_ref[D_RNN:, :], preferred_element_type=F32)
    x1 = x_ref[...] + mm
    x1_ref[...] = x1
    ms = jnp.mean(x1 * x1, axis=-1, keepdims=True)
    hn = x1 * lax.rsqrt(ms + EPS) * nf_ref[...]
    hn_ref[...] = hn
    logits = jnp.dot(hn.astype(BF16), wr_ref[...], preferred_element_type=F32) + br_ref[...]
    lane = lax.broadcasted_iota(jnp.int32, logits.shape, 1)
    ti = jnp.zeros(logits.shape, jnp.int32)
    tv = jnp.zeros(logits.shape, F32)
    l = logits
    v0 = None
    for k in range(TOP_K):
        mx = jnp.max(l, axis=-1, keepdims=True)
        idx = jnp.min(jnp.where(l == mx, lane, LANES), axis=-1, keepdims=True)
        if k == 0:
            v0 = mx
        ti = jnp.where(lane == k, idx, ti)
        tv = jnp.where(lane == k, jnp.exp(mx - v0), tv)
        l = jnp.where(lane == idx, NEG_BIG * 2, l)
    ti_ref[...] = ti
    tw_ref[...] = tv / jnp.sum(tv, axis=-1, keepdims=True)


def _merge(m_rnn, m_attn, x, w_out_bf, nf, wr_pad, br_pad, hn_all, n_all, row_block0, tm):
    rows = x.shape[0]
    row = lambda w: pl.BlockSpec((tm, w), lambda i: (i, 0))
    in_specs = [row(D_RNN), row(D_ATTN), row(D_MODEL), _full((D_RNN + D_ATTN, D_MODEL)),
                _full((1, D_MODEL)), _full((D_MODEL, LANES)), _full((1, LANES))]
    args = [m_rnn, m_attn, x, w_out_bf, nf, wr_pad, br_pad]
    aliases = {}
    if hn_all is not None:
        in_specs.append(pl.BlockSpec(memory_space=pl.ANY))
        args.append(hn_all)
        aliases = {len(args) - 1: 1}
    return pl.pallas_call(
        _merge_kernel,
        grid=(rows // tm,),
        in_specs=in_specs,
        out_specs=[row(D_MODEL),
                   pl.BlockSpec((tm, D_MODEL), lambda i: (i + row_block0, 0)),
                   row(LANES), row(LANES)],
        out_shape=[
            jax.ShapeDtypeStruct((rows, D_MODEL), F32),
            jax.ShapeDtypeStruct((n_all, D_MODEL), F32),
            jax.ShapeDtypeStruct((rows, LANES), jnp.int32),
            jax.ShapeDtypeStruct((rows, LANES), F32),
        ],
        input_output_aliases=aliases,
        compiler_params=_cparams(1),
        name="merge_router",
    )(*args)


UP_TILE = 256


def _deinterleave_matrix():
    half = UP_TILE // 2
    r = lax.broadcasted_iota(jnp.int32, (UP_TILE, UP_TILE), 0)
    c = lax.broadcasted_iota(jnp.int32, (UP_TILE, UP_TILE), 1)
    src = jnp.where(c < half, 2 * c, 2 * (c - half) + 1)
    return jnp.where(r == src, 1.0, 0.0).astype(BF16)


GATHER_AHEAD = 2
MOE_PHASES = 4


def _moe_kernel(be_ref, nu_ref, tok0_ref, tok1_ref, tok2_ref, hn_hbm, wu_ref, bu_ref, wd_ref,
                bd_ref, y_ref, xbuf, xb_s, act_s, wu_s, wd_s, sems):
    i = pl.program_id(0)
    nu = nu_ref[0]
    half = UP_TILE // 2
    n_up = 2 * D_FF // UP_TILE
    slot = i % GATHER_AHEAD

    def row_copy(tok_ref, r, s):
        return pltpu.make_async_copy(hn_hbm.at[pl.ds(tok_ref[0, 0, r], 1)],
                                     xbuf.at[s, pl.ds(r, 1)], sems.at[s])

    @pl.when(i == 0)
    def _():
        def issue(r, c):
            row_copy(tok0_ref, r, 0).start()
            row_copy(tok1_ref, r, 1).start()
            return c

        lax.fori_loop(0, TM_MOE, issue, 0)

    @pl.when(i < nu + GATHER_AHEAD)
    def _():
        pltpu.make_async_copy(hn_hbm.at[pl.ds(0, TM_MOE)], xbuf.at[slot], sems.at[slot]).wait()

    @pl.when(i < nu)
    def _():
        xb_s[...] = xbuf[slot].astype(BF16)

        @pl.when((i == 0) | (be_ref[i] != be_ref[jnp.maximum(i - 1, 0)]))
        def _():
            perm = _deinterleave_matrix()
            for c in range(n_up):
                cols = slice(c * UP_TILE, (c + 1) * UP_TILE)
                blk = wu_ref[0, :, cols].astype(BF16)
                wu_s[:, cols] = jnp.dot(blk, perm, preferred_element_type=F32).astype(BF16)
            wd_s[...] = wd_ref[0].astype(BF16)

    rows_per_phase = TM_MOE // MOE_PHASES

    def request_rows(phase):
        for r in range(phase * rows_per_phase, (phase + 1) * rows_per_phase):
            row_copy(tok2_ref, r, slot).start(priority=r % 2)

    def up_tiles(tiles):
        x = xb_s[...]
        for c in tiles:
            cols = slice(c * UP_TILE, (c + 1) * UP_TILE)
            z = jnp.dot(x, wu_s[:, cols], preferred_element_type=F32) + bu_ref[0, :, cols]
            xg = jnp.minimum(z[:, :half], SWIGLU_LIMIT)
            xl = jnp.clip(z[:, half:], -SWIGLU_LIMIT, SWIGLU_LIMIT)
            act = xg * jax.nn.sigmoid(SWIGLU_ALPHA * xg) * (xl + 1.0)
            act_s[:, c * half:(c + 1) * half] = act.astype(BF16)

    def down_cols(lo, hi):
        y_ref[:, lo:hi] = (jnp.dot(act_s[...], wd_s[:, lo:hi], preferred_element_type=F32)
                           + bd_ref[0, :, lo:hi])

    phases = [
        lambda: up_tiles(range(0, n_up // 2)),
        lambda: up_tiles(range(n_up // 2, n_up)),
        lambda: down_cols(0, D_MODEL // 2),
        lambda: down_cols(D_MODEL // 2, D_MODEL),
    ]
    for phase, work in enumerate(phases):
        @pl.when(i < nu)
        def _(phase=phase, work=work):
            request_rows(phase)
            work()

    @pl.when(i >= nu)
    def _():
        y_ref[...] = jnp.zeros_like(y_ref)


def _moe(blk_expert, n_used, buf_tok, hn_all, w_up, b_up_perm, w_down, b_down, n_blk):
    ew = lambda r, c: pl.BlockSpec((1, r, c), lambda i, be, nu: (be[i], 0, 0))
    tok = lambda f: pl.BlockSpec((1, 1, TM_MOE), lambda i, be, nu: (f(i), 0, 0),
                                 memory_space=pltpu.SMEM)
    return pl.pallas_call(
        _moe_kernel,
        grid_spec=pltpu.PrefetchScalarGridSpec(
            num_scalar_prefetch=2,
            grid=(n_blk,),
            in_specs=[
                tok(lambda i: i), tok(lambda i: jnp.minimum(i + 1, n_blk - 1)),
                tok(lambda i: jnp.minimum(i + GATHER_AHEAD, n_blk - 1)),
                pl.BlockSpec(memory_space=pl.ANY),
                ew(D_MODEL, 2 * D_FF), ew(1, 2 * D_FF), ew(D_FF, D_MODEL), ew(1, D_MODEL),
            ],
            out_specs=pl.BlockSpec((TM_MOE, D_MODEL), lambda i, be, nu: (i, 0)),
            scratch_shapes=[pltpu.VMEM((GATHER_AHEAD, TM_MOE, D_MODEL), F32),
                            pltpu.VMEM((TM_MOE, D_MODEL), BF16),
                            pltpu.VMEM((TM_MOE, D_FF), BF16),
                            pltpu.VMEM((D_MODEL, 2 * D_FF), BF16),
                            pltpu.VMEM((D_FF, D_MODEL), BF16),
                            pltpu.SemaphoreType.DMA((GATHER_AHEAD,))],
        ),
        out_shape=jax.ShapeDtypeStruct((n_blk * TM_MOE, D_MODEL), F32),
        compiler_params=_cparams(1),
        name="moe_experts",
    )(blk_expert, n_used, buf_tok, buf_tok, buf_tok, hn_all, w_up, b_up_perm, w_down, b_down)


def _final_kernel(dest_ref, x1_ref, tw_ref, nf_ref, y_hbm, o_ref, ybuf, sem):
    for n in range(TM_FIN * TOP_K):
        r, k = divmod(n, TOP_K)
        pltpu.make_async_copy(y_hbm.at[pl.ds(dest_ref[0, 0, n], 1)],
                              ybuf.at[pl.ds(k * TM_FIN + r, 1)], sem).start(priority=n % 2)
    pltpu.make_async_copy(y_hbm.at[pl.ds(0, TOP_K * TM_FIN)], ybuf, sem).wait()
    tw = tw_ref[...]
    x = x1_ref[...]
    for k in range(TOP_K):
        x = x + tw[:, k:k + 1] * ybuf[k * TM_FIN:(k + 1) * TM_FIN, :]
    ms = jnp.mean(x * x, axis=-1, keepdims=True)
    o_ref[...] = x * lax.rsqrt(ms + EPS) * nf_ref[...]


def _final(dest, x1, tw, nf, ybuf):
    rows = x1.shape[0]
    n_tiles = rows // TM_FIN
    row = lambda w: pl.BlockSpec((TM_FIN, w), lambda i: (i, 0))
    return pl.pallas_call(
        _final_kernel,
        grid=(n_tiles,),
        in_specs=[
            pl.BlockSpec((1, 1, TM_FIN * TOP_K), lambda i: (i, 0, 0), memory_space=pltpu.SMEM),
            row(D_MODEL), row(LANES), _full((1, D_MODEL)),
            pl.BlockSpec(memory_space=pl.ANY),
        ],
        out_specs=row(D_MODEL),
        out_shape=jax.ShapeDtypeStruct((rows, D_MODEL), F32),
        scratch_shapes=[pltpu.VMEM((TOP_K * TM_FIN, D_MODEL), F32), pltpu.SemaphoreType.DMA],
        compiler_params=_cparams(1),
        name="combine_final",
    )(dest.reshape(n_tiles, 1, TM_FIN * TOP_K), x1, tw, nf, ybuf)


def _block_diag_tiles(w):
    per = GATE_TILE // GATE_BW
    w4 = w.reshape(N_GATE_TILES, per, GATE_BW, GATE_BW)
    eye = jnp.eye(per, dtype=w.dtype)
    return jnp.einsum("gacd,ab->gacbd", w4, eye).reshape(N_GATE_TILES, GATE_TILE, GATE_TILE)


def _route(top_i, n_blk):
    a = top_i.size
    flat_e = top_i.reshape(-1)
    onehot = (flat_e[:, None] == jnp.arange(N_EXPERTS, dtype=jnp.int32)[None, :]).astype(jnp.int32)
    csum = jnp.cumsum(onehot, axis=0)
    rank = jnp.sum(csum * onehot, axis=1) - 1
    counts = csum[-1]
    padded = ((counts + (TM_MOE - 1)) // TM_MOE) * TM_MOE
    ends = jnp.cumsum(padded).astype(jnp.int32)
    start_padded = ends - padded
    dest = (start_padded[flat_e] + rank).astype(jnp.int32)
    buf_tok = jnp.zeros((n_blk * TM_MOE,), jnp.int32).at[dest].set(
        jnp.arange(a, dtype=jnp.int32) // TOP_K)
    blk_start = jnp.arange(n_blk, dtype=jnp.int32) * TM_MOE
    blk_expert = jnp.minimum(
        jnp.sum((ends[None, :] <= blk_start[:, None]).astype(jnp.int32), axis=1), N_EXPERTS - 1)
    n_used = (ends[-1:] // TM_MOE).astype(jnp.int32)
    return dest, buf_tok.reshape(n_blk, 1, TM_MOE), blk_expert.astype(jnp.int32), n_used


def kernel(x_prompt, x_sample, cache_k, cache_v, state_conv, state_h, meta_tokens, norm_mix, w_in, conv_w, conv_b, w_rg, b_rg, w_ig, b_ig, lru_lambda, attn_sinks, w_out, norm_ffn, w_router, b_router, w_up, b_up, w_down, b_down, norm_final):
    n_b, seq = x_prompt.shape[0], x_prompt.shape[1]
    n_s = x_sample.shape[0]
    w_buf = cache_k.shape[2]
    n_p = n_b * seq
    n_tok = n_p + n_s
    row1 = lambda v: v.reshape(1, -1)

    w_in_bf = w_in[0].astype(BF16)
    w_out_bf = w_out[0].astype(BF16)
    wg_tiles = jnp.concatenate([_block_diag_tiles(w_rg[0]), _block_diag_tiles(w_ig[0])],
                               axis=-1).astype(BF16)
    wr_pad = jnp.pad(w_router[0], ((0, 0), (0, LANES - N_EXPERTS))).astype(BF16)
    br_pad = jnp.pad(b_router[0], (0, LANES - N_EXPERTS), constant_values=NEG_BIG).reshape(1, LANES)
    b_up_perm = jnp.swapaxes(b_up[0].reshape(N_EXPERTS, 2 * D_FF // UP_TILE, UP_TILE // 2, 2),
                             2, 3).reshape(N_EXPERTS, 1, 2 * D_FF)
    b_dn = b_down[0].reshape(N_EXPERTS, 1, D_MODEL)
    nm, nf, nfin = row1(norm_mix[0]), row1(norm_ffn[0]), row1(norm_final)
    cw, cb = conv_w[0], row1(conv_b[0])
    brg, big, lam = row1(b_rg[0]), row1(b_ig[0]), row1(lru_lambda[0])
    sinks = attn_sinks[0]
    rnn_w = (cw, cb, wg_tiles, brg, big, lam)
    rnn_w_specs = [_full((CONV_W, D_RNN)), _full((1, D_RNN)),
                   _full((N_GATE_TILES, GATE_TILE, 2 * GATE_TILE)),
                   _full((1, D_RNN)), _full((1, D_RNN)), _full((1, D_RNN))]

    xp2 = x_prompt.reshape(n_p, D_MODEL)
    xr_p, q_p, kv_p, gr_p, ga_p = _in_proj(xp2, nm, w_in_bf, TM_IN)
    x_sm = jnp.concatenate([x_sample.reshape(n_s, D_MODEL), meta_tokens], axis=0)
    xr_sm, q_sm, kv_sm, gr_sm, ga_sm = _in_proj(x_sm, nm, w_in_bf, n_s + N_META)
    xr_s, q_s, kv_s, gr_s, ga_s = (t[:n_s] for t in (xr_sm, q_sm, kv_sm, gr_sm, ga_sm))
    xr_m, kv_m = xr_sm[n_s:], kv_sm[n_s:]

    h_meta = pl.pallas_call(
        _rnn_meta_kernel,
        in_specs=[_full((N_META, D_RNN))] + rnn_w_specs,
        out_specs=_full((1, D_RNN)),
        out_shape=jax.ShapeDtypeStruct((1, D_RNN), F32),
        grid=(1,),
        compiler_params=_cparams(1),
        name="rnn_meta",
    )(xr_m, *rnn_w)

    tt = TT_RNN
    blk3 = pl.BlockSpec((n_b, tt, D_RNN), lambda j: (0, j, 0))
    m_rnn_p, h_last_p = pl.pallas_call(
        functools.partial(_rnn_prompt_kernel, n_b=n_b, tt=tt),
        grid=(seq // tt,),
        in_specs=[blk3, blk3, _full((N_META, D_RNN)), _full((1, D_RNN))] + rnn_w_specs,
        out_specs=[blk3, _full((n_b, D_RNN))],
        out_shape=[jax.ShapeDtypeStruct((n_b, seq, D_RNN), BF16),
                   jax.ShapeDtypeStruct((n_b, D_RNN), F32)],
        scratch_shapes=[pltpu.VMEM((n_b, 8, D_RNN), F32), pltpu.VMEM((n_b, D_RNN), F32)],
        compiler_params=_cparams(1),
        name="rnn_prompt",
    )(xr_p.reshape(n_b, seq, D_RNN), gr_p.reshape(n_b, seq, D_RNN), xr_m, h_meta, *rnn_w)

    sc_t = jnp.swapaxes(state_conv[0], 0, 1)
    m_rnn_s, h_new_s = pl.pallas_call(
        _rnn_sample_kernel,
        grid=(1,),
        in_specs=[_full((n_s, D_RNN)), _full((n_s, D_RNN)), _full((CONV_W - 1, n_s, D_RNN)),
                  _full((n_s, D_RNN))] + rnn_w_specs,
        out_specs=[_full((n_s, D_RNN)), _full((n_s, D_RNN))],
        out_shape=[jax.ShapeDtypeStruct((n_s, D_RNN), BF16),
                   jax.ShapeDtypeStruct((n_s, D_RNN), F32)],
        compiler_params=_cparams(1),
        name="rnn_sample",
    )(xr_s, gr_s, sc_t, state_h[0], *rnn_w)

    n_blk_seq = seq // BLOCK
    kv_meta_blk = jnp.pad(kv_m, ((BLOCK - N_META, 0), (0, 0)))
    smem_spec = pl.BlockSpec(memory_space=pltpu.SMEM)
    rb = lambda w: pl.BlockSpec((BLOCK, w), lambda b, j: (b * n_blk_seq + j, 0))
    m_attn_p = pl.pallas_call(
        _attn_prompt_kernel,
        grid=(n_b, n_blk_seq),
        in_specs=[smem_spec, rb(D_ATTN), rb(2 * D_KV),
                  pl.BlockSpec((BLOCK, 2 * D_KV),
                               lambda b, j: (jnp.maximum(b * n_blk_seq + j - 1, 0), 0)),
                  _full((BLOCK, 2 * D_KV)), rb(D_ATTN)],
        out_specs=rb(D_ATTN),
        out_shape=jax.ShapeDtypeStruct((n_p, D_ATTN), BF16),
        compiler_params=_cparams(2),
        name="attn_prompt",
    )(sinks, q_p, kv_p, kv_p, kv_meta_blk, ga_p)

    ck = cache_k[0].reshape(n_s, w_buf, D_KV)
    cv = cache_v[0].reshape(n_s, w_buf, D_KV)
    per_s = lambda a, b: pl.BlockSpec((SAMPLES_PER_STEP, a, b), lambda i: (i, 0, 0))
    m_attn_s = pl.pallas_call(
        _attn_sample_kernel,
        grid=(n_s // SAMPLES_PER_STEP,),
        in_specs=[smem_spec, per_s(N_HEADS, HEAD_DIM), per_s(1, 2 * D_KV), per_s(w_buf, D_KV),
                  per_s(w_buf, D_KV), per_s(N_HEADS, HEAD_DIM)],
        out_specs=per_s(N_HEADS, HEAD_DIM),
        out_shape=jax.ShapeDtypeStruct((n_s, N_HEADS, HEAD_DIM), BF16),
        compiler_params=_cparams(1),
        name="attn_sample",
    )(sinks, q_s.reshape(n_s, N_HEADS, HEAD_DIM), kv_s.reshape(n_s, 1, 2 * D_KV), ck, cv,
      ga_s.reshape(n_s, N_HEADS, HEAD_DIM)).reshape(n_s, D_ATTN)

    x1_p, hn_all, ti_p, tw_p = _merge(m_rnn_p.reshape(n_p, D_RNN), m_attn_p, xp2, w_out_bf, nf,
                                      wr_pad, br_pad, jnp.zeros((n_tok, D_MODEL), F32), n_tok, 0,
                                      TM_MERGE)
    x1_s, hn_all, ti_s, tw_s = _merge(m_rnn_s, m_attn_s, x_sample.reshape(n_s, D_MODEL), w_out_bf,
                                      nf, wr_pad, br_pad, hn_all, n_tok, n_p // n_s, n_s)

    top_i = jnp.concatenate([ti_p[:, :TOP_K], ti_s[:, :TOP_K]], axis=0)
    n_assign = n_tok * TOP_K
    n_blk = n_assign // TM_MOE + N_EXPERTS - 1 + GATHER_AHEAD
    dest, buf_tok, blk_expert, n_used = _route(top_i, n_blk)

    ybuf = _moe(blk_expert, n_used, buf_tok, hn_all, w_up[0], b_up_perm, w_down[0], b_dn, n_blk)
    y_p = _final(dest[:n_p * TOP_K], x1_p, tw_p, nfin, ybuf)
    y_s = _final(dest[n_p * TOP_K:], x1_s, tw_s, nfin, ybuf)

    kv_p3 = kv_p.reshape(n_b, seq, 2 * D_KV)
    w_p = min(WINDOW, seq + N_META)
    new_k_p = kv_p3[:, seq - w_p:, :D_KV].reshape(1, n_b, w_p, N_KV, HEAD_DIM)
    new_v_p = kv_p3[:, seq - w_p:, D_KV:].reshape(1, n_b, w_p, N_KV, HEAD_DIM)
    new_conv_p = xr_p.reshape(n_b, seq, D_RNN)[:, seq - (CONV_W - 1):][None]
    k_new = kv_s[:, :D_KV].reshape(n_s, 1, N_KV, HEAD_DIM)
    v_new = kv_s[:, D_KV:].reshape(n_s, 1, N_KV, HEAD_DIM)
    new_k_s = jnp.concatenate([cache_k[0], k_new], axis=1)[:, -w_buf:][None]
    new_v_s = jnp.concatenate([cache_v[0], v_new], axis=1)[:, -w_buf:][None]
    new_conv_s = jnp.concatenate([state_conv[0], xr_s[:, None, :]], axis=1)[:, -(CONV_W - 1):][None]
    return (y_p.reshape(n_b, seq, D_MODEL), y_s.reshape(n_s, 1, D_MODEL), new_k_p, new_v_p,
            new_conv_p, h_last_p[None], new_k_s, new_v_s, new_conv_s, h_new_s[None])
```

```python
import functools

import jax
import jax.numpy as jnp
from jax import lax
from jax.experimental import pallas as pl
from jax.experimental.pallas import tpu as pltpu

F32 = jnp.float32
BF16 = jnp.bfloat16

D_MODEL = 1024
N_META = 16
D_RNN = 1024
N_GATE_BLOCKS = 16
GATE_BW = D_RNN // N_GATE_BLOCKS
CONV_W = 4
LRU_C = 8.0
N_HEADS = 16
HEAD_DIM = 64
N_KV = 2
GROUP = N_HEADS // N_KV
D_ATTN = N_HEADS * HEAD_DIM
D_KV = N_KV * HEAD_DIM
WINDOW = 128
BLOCK = 128
PAST_LEN = 16384
N_EXPERTS = 32
TOP_K = 4
D_FF = 1024
SWIGLU_LIMIT = 7.0
SWIGLU_ALPHA = 1.702
EPS = 1e-6

GATE_TILE = 256
N_GATE_TILES = D_RNN // GATE_TILE
LANES = 128
NEG_BIG = -1e30

TM_IN = 256
TT_RNN = 64
TM_MERGE = 256
TM_MOE = 256
TM_FIN = 128
VMEM_LIMIT = 56 * 1024 * 1024


def _cparams(n_grid_dims):
    return pltpu.CompilerParams(
        dimension_semantics=("arbitrary",) * n_grid_dims, vmem_limit_bytes=VMEM_LIMIT)


def _full(shape):
    return pl.BlockSpec(shape, lambda *_: (0,) * len(shape))


def _in_proj_kernel(x_ref, g_ref, w_ref, xr_ref, q_ref, kv_ref, gr_ref, ga_ref):
    x = x_ref[...]
    ms = jnp.mean(x * x, axis=-1, keepdims=True)
    h = (x * lax.rsqrt(ms + EPS) * g_ref[...]).astype(BF16)

    def proj(lo, hi):
        return jnp.dot(h, w_ref[:, lo:hi], preferred_element_type=F32)

    o_q = D_RNN
    o_k = o_q + D_ATTN
    o_gr = o_k + 2 * D_KV
    o_ga = o_gr + D_RNN
    xr_ref[...] = proj(0, o_q)
    q_ref[...] = (proj(o_q, o_k) * (HEAD_DIM ** -0.5)).astype(BF16)
    kv_ref[...] = proj(o_k, o_gr)
    gr_ref[...] = proj(o_gr, o_ga).astype(BF16)
    ga_ref[...] = proj(o_ga, o_ga + D_ATTN).astype(BF16)


def _in_proj(x, g, w_in_bf, tm):
    rows = x.shape[0]
    d_in = w_in_bf.shape[1]
    row = lambda w: pl.BlockSpec((tm, w), lambda i: (i, 0))
    return pl.pallas_call(
        _in_proj_kernel,
        grid=(rows // tm,),
        in_specs=[row(D_MODEL), _full((1, D_MODEL)), _full((D_MODEL, d_in))],
        out_specs=[row(D_RNN), row(D_ATTN), row(2 * D_KV), row(D_RNN), row(D_ATTN)],
        out_shape=[
            jax.ShapeDtypeStruct((rows, D_RNN), F32),
            jax.ShapeDtypeStruct((rows, D_ATTN), BF16),
            jax.ShapeDtypeStruct((rows, 2 * D_KV), F32),
            jax.ShapeDtypeStruct((rows, D_RNN), BF16),
            jax.ShapeDtypeStruct((rows, D_ATTN), BF16),
        ],
        compiler_params=_cparams(1),
        name="in_proj",
    )(x, g, w_in_bf)


def _softplus(x):
    return jnp.maximum(x, 0.0) + jnp.log1p(jnp.exp(-jnp.abs(x)))


def _lru_coeffs(u, wg_ref, brg, big, lam):
    ub = u.astype(BF16)
    sp = _softplus(-lam)
    a_parts, i_parts, m_parts = [], [], []
    for g in range(N_GATE_TILES):
        sl = slice(g * GATE_TILE, (g + 1) * GATE_TILE)
        zz = jnp.dot(ub[:, sl], wg_ref[g], preferred_element_type=F32)
        r = jax.nn.sigmoid(zz[:, :GATE_TILE] + brg[:, sl])
        i = jax.nn.sigmoid(zz[:, GATE_TILE:] + big[:, sl])
        log_a = (-LRU_C) * r * sp[:, sl]
        a = jnp.exp(log_a)
        m = jnp.sqrt(1.0 - a * a)
        a_parts.append(a)
        i_parts.append(i)
        m_parts.append(m)
    cat = lambda ps: jnp.concatenate(ps, axis=-1)
    return cat(a_parts), cat(i_parts), cat(m_parts)


def _conv(ext, t, cw, cb):
    out = cb
    for j in range(CONV_W):
        s = CONV_W - 1 - j
        out = out + cw[j:j + 1, :] * ext[8 - s:8 - s + t, :]
    return out


def _rnn_meta_kernel(xr_ref, cw_ref, cb_ref, wg_ref, brg_ref, big_ref, lam_ref, h_ref):
    x = xr_ref[...]
    ext = jnp.concatenate([jnp.zeros((8, D_RNN), F32), x], axis=0)
    u = _conv(ext, N_META, cw_ref[...], cb_ref[...])
    a, i, m = _lru_coeffs(u, wg_ref, brg_ref[...], big_ref[...], lam_ref[...])
    first = lax.broadcasted_iota(jnp.int32, (N_META, 1), 0) == 0
    b = jnp.where(first, 1.0, m) * i * u
    h = jnp.zeros((1, D_RNN), F32)
    for t in range(N_META):
        h = a[t:t + 1, :] * h + b[t:t + 1, :]
    h_ref[...] = h


SUBLANES = 8


def _linear_scan(a, b, h_prev):
    t, d = a.shape
    g = t // SUBLANES
    a3 = a.reshape(g, SUBLANES, d)
    b3 = b.reshape(g, SUBLANES, d)
    row = lax.broadcasted_iota(jnp.int32, (g, SUBLANES, d), 1)
    step = 1
    while step < SUBLANES:
        keep = row >= step
        a_up = jnp.where(keep, pltpu.roll(a3, step, axis=1), 1.0)
        b_up = jnp.where(keep, pltpu.roll(b3, step, axis=1), 0.0)
        b3 = a3 * b_up + b3
        a3 = a3 * a_up
        step *= 2
    hs = []
    h = h_prev
    for k in range(g):
        hk = a3[k] * h + b3[k]
        hs.append(hk)
        h = hk[SUBLANES - 1:SUBLANES, :]
    return jnp.concatenate(hs, axis=0), h


def _rnn_prompt_kernel(xr_ref, gr_ref, xm_ref, h0_ref, cw_ref, cb_ref, wg_ref, brg_ref, big_ref,
                       lam_ref, m_ref, hl_ref, halo_s, h_s, *, n_b, tt):
    j = pl.program_id(0)

    @pl.when(j == 0)
    def _():
        h_s[...] = jnp.broadcast_to(h0_ref[...], (n_b, D_RNN))
        for b in range(n_b):
            halo_s[b] = xm_ref[N_META - 8:N_META, :]

    cw = cw_ref[...]
    cb = cb_ref[...]

    def per_batch(b, c):
        x = xr_ref[b]
        ext = jnp.concatenate([halo_s[b], x], axis=0)
        halo_s[b] = x[tt - 8:tt, :]
        u = _conv(ext, tt, cw, cb)
        a, i, m = _lru_coeffs(u, wg_ref, brg_ref[...], big_ref[...], lam_ref[...])
        y, h = _linear_scan(a, m * i * u, h_s[pl.ds(b, 1), :])
        h_s[pl.ds(b, 1), :] = h
        m_ref[b] = (jax.nn.sigmoid(gr_ref[b].astype(F32)) * y).astype(BF16)
        return c

    lax.fori_loop(0, n_b, per_batch, 0)
    hl_ref[...] = h_s[...]


def _rnn_sample_kernel(xr_ref, gr_ref, sc_ref, h0_ref, cw_ref, cb_ref, wg_ref, brg_ref, big_ref,
                       lam_ref, m_ref, hn_ref):
    cw = cw_ref[...]
    u = cb_ref[...] + cw[CONV_W - 1:CONV_W, :] * xr_ref[...]
    for j in range(CONV_W - 1):
        u = u + cw[j:j + 1, :] * sc_ref[j]
    a, i, m = _lru_coeffs(u, wg_ref, brg_ref[...], big_ref[...], lam_ref[...])
    h = a * h0_ref[...] + m * i * u
    hn_ref[...] = h
    m_ref[...] = (jax.nn.sigmoid(gr_ref[...].astype(F32)) * h).astype(BF16)


def _alibi_slope(h):
    return 2.0 ** (-8.0 * (h + 1) / N_HEADS)


def _attn_prompt_kernel(sink_ref, q_ref, kc_ref, kp_ref, km_ref, ga_ref, o_ref):
    j = pl.program_id(1)
    first = j == 0
    kv_prev = jnp.where(first, km_ref[...], kp_ref[...])
    kv = jnp.concatenate([kv_prev, kc_ref[...]], axis=0)
    q = q_ref[...]
    qi = lax.broadcasted_iota(jnp.int32, (BLOCK, 2 * BLOCK), 0)
    ci = lax.broadcasted_iota(jnp.int32, (BLOCK, 2 * BLOCK), 1)
    dist = qi + BLOCK - ci
    c_min = jnp.where(first, BLOCK - N_META, 0)
    valid = (dist >= 0) & (dist < WINDOW) & (ci >= c_min)
    distf = dist.astype(F32)
    outs = []
    for h in range(N_HEADS):
        g = h // GROUP
        kh = kv[:, g * HEAD_DIM:(g + 1) * HEAD_DIM].astype(BF16)
        vh = kv[:, D_KV + g * HEAD_DIM:D_KV + (g + 1) * HEAD_DIM].astype(BF16)
        qh = q[:, h * HEAD_DIM:(h + 1) * HEAD_DIM]
        s = lax.dot_general(qh, kh, (((1,), (1,)), ((), ())), preferred_element_type=F32)
        s = jnp.where(valid, s - _alibi_slope(h) * distf, NEG_BIG)
        sink = sink_ref[h]
        mx = jnp.maximum(jnp.max(s, axis=-1, keepdims=True), sink)
        p = jnp.exp(s - mx)
        den = jnp.sum(p, axis=-1, keepdims=True) + jnp.exp(sink - mx)
        o = jnp.dot(p.astype(BF16), vh, preferred_element_type=F32)
        outs.append(o / den)
    y = jnp.concatenate(outs, axis=-1)
    o_ref[...] = (jax.nn.sigmoid(ga_ref[...].astype(F32)) * y).astype(BF16)


SAMPLES_PER_STEP = 8


def _attn_sample_kernel(sink_ref, q_ref, kvn_ref, ck_ref, cv_ref, ga_ref, o_ref):
    w_buf = ck_ref.shape[1]
    ci = lax.broadcasted_iota(jnp.int32, (GROUP, w_buf), 1)
    dist = w_buf - ci
    valid = dist < WINDOW
    distf = dist.astype(F32)
    hrow = lax.broadcasted_iota(jnp.int32, (GROUP, 1), 0)
    bias, sinks = [], []
    for g in range(N_KV):
        slope = jnp.zeros((GROUP, 1), F32)
        sink = jnp.zeros((GROUP, 1), F32)
        for r in range(GROUP):
            slope = jnp.where(hrow == r, _alibi_slope(g * GROUP + r), slope)
            sink = jnp.where(hrow == r, sink_ref[g * GROUP + r], sink)
        bias.append(slope * distf)
        sinks.append(sink)
    for n in range(SAMPLES_PER_STEP):
        q = q_ref[n]
        ck = ck_ref[n].astype(BF16)
        cv = cv_ref[n].astype(BF16)
        kvn = kvn_ref[n].astype(BF16).astype(F32)
        outs = []
        for g in range(N_KV):
            qg = q[g * GROUP:(g + 1) * GROUP, :]
            kg = ck[:, g * HEAD_DIM:(g + 1) * HEAD_DIM]
            vg = cv[:, g * HEAD_DIM:(g + 1) * HEAD_DIM]
            kn = kvn[:, g * HEAD_DIM:(g + 1) * HEAD_DIM]
            vn = kvn[:, D_KV + g * HEAD_DIM:D_KV + (g + 1) * HEAD_DIM]
            s = lax.dot_general(qg, kg, (((1,), (1,)), ((), ())), preferred_element_type=F32)
            s = jnp.where(valid, s - bias[g], NEG_BIG)
            sn = jnp.sum(qg.astype(F32) * kn, axis=-1, keepdims=True)
            mx = jnp.maximum(jnp.maximum(jnp.max(s, axis=-1, keepdims=True), sn), sinks[g])
            p = jnp.exp(s - mx)
            pn = jnp.exp(sn - mx)
            den = jnp.sum(p, axis=-1, keepdims=True) + pn + jnp.exp(sinks[g] - mx)
            o = jnp.dot(p.astype(BF16), vg, preferred_element_type=F32)
            o = o + pn.astype(BF16).astype(F32) * vn
            outs.append(o / den)
        y = jnp.concatenate(outs, axis=0)
        o_ref[n] = (jax.nn.sigmoid(ga_ref[n].astype(F32)) * y).astype(BF16)


def _merge_kernel(mr_ref, ma_ref, x_ref, wo_ref, nf_ref, wr_ref, br_ref, *rest):
    x1_ref, hn_ref, ti_ref, tw_ref = rest[-4:]
    mm = jnp.dot(mr_ref[...], wo_ref[:D_RNN, :], preferred_element_type=F32)
    mm = mm + jnp.dot(ma_ref[...], wo_ref[D_RNN:, :], preferred_element_type=F32)
    x1 = x_ref[...] + mm
    x1_ref[...] = x1
    ms = jnp.mean(x1 * x1, axis=-1, keepdims=True)
    hn = x1 * lax.rsqrt(ms + EPS) * nf_ref[...]
    hn_ref[...] = hn
    logits = jnp.dot(hn.astype(BF16), wr_ref[...], preferred_element_type=F32) + br_ref[...]
    lane = lax.broadcasted_iota(jnp.int32, logits.shape, 1)
    ti = jnp.zeros(logits.shape, jnp.int32)
    tv = jnp.zeros(logits.shape, F32)
    l = logits
    v0 = None
    for k in range(TOP_K):
        mx = jnp.max(l, axis=-1, keepdims=True)
        idx = jnp.min(jnp.where(l == mx, lane, LANES), axis=-1, keepdims=True)
        if k == 0:
            v0 = mx
        ti = jnp.where(lane == k, idx, ti)
        tv = jnp.where(lane == k, jnp.exp(mx - v0), tv)
        l = jnp.where(lane == idx, NEG_BIG * 2, l)
    ti_ref[...] = ti
    tw_ref[...] = tv / jnp.sum(tv, axis=-1, keepdims=True)


def _merge(m_rnn, m_attn, x, w_out_bf, nf, wr_pad, br_pad, hn_all, n_all, row_block0, tm):
    rows = x.shape[0]
    row = lambda w: pl.BlockSpec((tm, w), lambda i: (i, 0))
    in_specs = [row(D_RNN), row(D_ATTN), row(D_MODEL), _full((D_RNN + D_ATTN, D_MODEL)),
                _full((1, D_MODEL)), _full((D_MODEL, LANES)), _full((1, LANES))]
    args = [m_rnn, m_attn, x, w_out_bf, nf, wr_pad, br_pad]
    aliases = {}
    if hn_all is not None:
        in_specs.append(pl.BlockSpec(memory_space=pl.ANY))
        args.append(hn_all)
        aliases = {len(args) - 1: 1}
    return pl.pallas_call(
        _merge_kernel,
        grid=(rows // tm,),
        in_specs=in_specs,
        out_specs=[row(D_MODEL),
                   pl.BlockSpec((tm, D_MODEL), lambda i: (i + row_block0, 0)),
                   row(LANES), row(LANES)],
        out_shape=[
            jax.ShapeDtypeStruct((rows, D_MODEL), F32),
            jax.ShapeDtypeStruct((n_all, D_MODEL), F32),
            jax.ShapeDtypeStruct((rows, LANES), jnp.int32),
            jax.ShapeDtypeStruct((rows, LANES), F32),
        ],
        input_output_aliases=aliases,
        compiler_params=_cparams(1),
        name="merge_router",
    )(*args)


UP_TILE = 256


def _deinterleave_matrix():
    half = UP_TILE // 2
    r = lax.broadcasted_iota(jnp.int32, (UP_TILE, UP_TILE), 0)
    c = lax.broadcasted_iota(jnp.int32, (UP_TILE, UP_TILE), 1)
    src = jnp.where(c < half, 2 * c, 2 * (c - half) + 1)
    return jnp.where(r == src, 1.0, 0.0).astype(BF16)


GATHER_AHEAD = 2
WEIGHT_DMA_PRIORITY = 1


def _moe_kernel(be_ref, nxt_ref, nu_ref, tok0_ref, tok1_ref, tok2_ref, hn_hbm, wu_hbm, bu_ref,
                wd_hbm, bd_ref, y_ref, xbuf, xb_s, wu_f, wd_f, wu_s, wd_s, sems, wsems):
    i = pl.program_id(0)
    nu = nu_ref[0]
    half = UP_TILE // 2
    n_up = 2 * D_FF // UP_TILE
    slot = i % GATHER_AHEAD

    def row_copy(tok_ref, r, s):
        return pltpu.make_async_copy(hn_hbm.at[pl.ds(tok_ref[0, 0, r], 1)],
                                     xbuf.at[s, pl.ds(r, 1)], sems.at[s])

    def weight_copies(e):
        return (pltpu.make_async_copy(wu_hbm.at[e], wu_f, wsems.at[0]),
                pltpu.make_async_copy(wd_hbm.at[e], wd_f, wsems.at[1]))

    @pl.when(i == 0)
    def _():
        for cp in weight_copies(be_ref[0]):
            cp.start(priority=WEIGHT_DMA_PRIORITY)

        def issue(r, c):
            row_copy(tok0_ref, r, 0).start()
            row_copy(tok1_ref, r, 1).start()
            return c

        lax.fori_loop(0, TM_MOE, issue, 0)

    @pl.when(i < nu + GATHER_AHEAD)
    def _():
        pltpu.make_async_copy(hn_hbm.at[pl.ds(0, TM_MOE)], xbuf.at[slot], sems.at[slot]).wait()

    @pl.when(i < nu)
    def _():
        xb_s[...] = xbuf[slot].astype(BF16)

        @pl.when((i == 0) | (be_ref[i] != be_ref[jnp.maximum(i - 1, 0)]))
        def _():
            for cp in weight_copies(be_ref[i]):
                cp.wait()
            perm = _deinterleave_matrix()
            for c in range(n_up):
                cols = slice(c * UP_TILE, (c + 1) * UP_TILE)
                blk = wu_f[:, cols].astype(BF16)
                wu_s[:, cols] = jnp.dot(blk, perm, preferred_element_type=F32).astype(BF16)
            wd_s[...] = wd_f[...].astype(BF16)

            @pl.when(nxt_ref[i] >= 0)
            def _():
                for cp in weight_copies(nxt_ref[i]):
                    cp.start(priority=WEIGHT_DMA_PRIORITY)

        for n in range(TM_MOE):
            r = (n % (TM_MOE // SUBLANES)) * SUBLANES + n // (TM_MOE // SUBLANES)
            row_copy(tok2_ref, r, slot).start()

        x = xb_s[...]
        z = jnp.dot(x, wu_s[...], preferred_element_type=F32) + bu_ref[0]
        zg = jnp.concatenate([z[:, c * UP_TILE:c * UP_TILE + half] for c in range(n_up)], axis=-1)
        zl = jnp.concatenate([z[:, c * UP_TILE + half:(c + 1) * UP_TILE] for c in range(n_up)],
                             axis=-1)
        xg = jnp.minimum(zg, SWIGLU_LIMIT)
        xl = jnp.clip(zl, -SWIGLU_LIMIT, SWIGLU_LIMIT)
        act = xg * jax.nn.sigmoid(SWIGLU_ALPHA * xg) * (xl + 1.0)
        y_ref[...] = jnp.dot(act.astype(BF16), wd_s[...], preferred_element_type=F32) + bd_ref[0]

    @pl.when(i >= nu)
    def _():
        y_ref[...] = jnp.zeros_like(y_ref)


def _moe(blk_expert, next_expert, n_used, buf_tok, hn_all, w_up, b_up_perm, w_down, b_down, n_blk):
    ew = lambda r, c: pl.BlockSpec((1, r, c), lambda i, be, nxt, nu: (be[i], 0, 0))
    tok = lambda f: pl.BlockSpec((1, 1, TM_MOE), lambda i, be, nxt, nu: (f(i), 0, 0),
                                 memory_space=pltpu.SMEM)
    hbm = pl.BlockSpec(memory_space=pl.ANY)
    return pl.pallas_call(
        _moe_kernel,
        grid_spec=pltpu.PrefetchScalarGridSpec(
            num_scalar_prefetch=3,
            grid=(n_blk,),
            in_specs=[
                tok(lambda i: i), tok(lambda i: jnp.minimum(i + 1, n_blk - 1)),
                tok(lambda i: jnp.minimum(i + GATHER_AHEAD, n_blk - 1)),
                hbm, hbm, ew(1, 2 * D_FF), hbm, ew(1, D_MODEL),
            ],
            out_specs=pl.BlockSpec((TM_MOE, D_MODEL), lambda i, be, nxt, nu: (i, 0)),
            scratch_shapes=[pltpu.VMEM((GATHER_AHEAD, TM_MOE, D_MODEL), F32),
                            pltpu.VMEM((TM_MOE, D_MODEL), BF16),
                            pltpu.VMEM((D_MODEL, 2 * D_FF), F32),
                            pltpu.VMEM((D_FF, D_MODEL), F32),
                            pltpu.VMEM((D_MODEL, 2 * D_FF), BF16),
                            pltpu.VMEM((D_FF, D_MODEL), BF16),
                            pltpu.SemaphoreType.DMA((GATHER_AHEAD,)),
                            pltpu.SemaphoreType.DMA((2,))],
        ),
        out_shape=jax.ShapeDtypeStruct((n_blk * TM_MOE, D_MODEL), F32),
        compiler_params=_cparams(1),
        name="moe_experts",
    )(blk_expert, next_expert, n_used, buf_tok, buf_tok, buf_tok, hn_all, w_up, b_up_perm, w_down,
      b_down)


def _final_kernel(dest_ref, x1_ref, tw_ref, nf_ref, y_hbm, o_ref, ybuf, sem):
    for n in range(TM_FIN * TOP_K):
        r, k = divmod(n, TOP_K)
        pltpu.make_async_copy(y_hbm.at[pl.ds(dest_ref[0, 0, n], 1)],
                              ybuf.at[pl.ds(k * TM_FIN + r, 1)], sem).start(priority=n % 2)
    pltpu.make_async_copy(y_hbm.at[pl.ds(0, TOP_K * TM_FIN)], ybuf, sem).wait()
    tw = tw_ref[...]
    x = x1_ref[...]
    for k in range(TOP_K):
        x = x + tw[:, k:k + 1] * ybuf[k * TM_FIN:(k + 1) * TM_FIN, :]
    ms = jnp.mean(x * x, axis=-1, keepdims=True)
    o_ref[...] = x * lax.rsqrt(ms + EPS) * nf_ref[...]


def _final(dest, x1, tw, nf, ybuf):
    rows = x1.shape[0]
    n_tiles = rows // TM_FIN
    row = lambda w: pl.BlockSpec((TM_FIN, w), lambda i: (i, 0))
    return pl.pallas_call(
        _final_kernel,
        grid=(n_tiles,),
        in_specs=[
            pl.BlockSpec((1, 1, TM_FIN * TOP_K), lambda i: (i, 0, 0), memory_space=pltpu.SMEM),
            row(D_MODEL), row(LANES), _full((1, D_MODEL)),
            pl.BlockSpec(memory_space=pl.ANY),
        ],
        out_specs=row(D_MODEL),
        out_shape=jax.ShapeDtypeStruct((rows, D_MODEL), F32),
        scratch_shapes=[pltpu.VMEM((TOP_K * TM_FIN, D_MODEL), F32), pltpu.SemaphoreType.DMA],
        compiler_params=_cparams(1),
        name="combine_final",
    )(dest.reshape(n_tiles, 1, TM_FIN * TOP_K), x1, tw, nf, ybuf)


def _block_diag_tiles(w):
    per = GATE_TILE // GATE_BW
    w4 = w.reshape(N_GATE_TILES, per, GATE_BW, GATE_BW)
    eye = jnp.eye(per, dtype=w.dtype)
    return jnp.einsum("gacd,ab->gacbd", w4, eye).reshape(N_GATE_TILES, GATE_TILE, GATE_TILE)


def _route(top_i, n_blk):
    a = top_i.size
    flat_e = top_i.reshape(-1)
    onehot = (flat_e[:, None] == jnp.arange(N_EXPERTS, dtype=jnp.int32)[None, :]).astype(jnp.int32)
    csum = jnp.cumsum(onehot, axis=0)
    rank = jnp.sum(csum * onehot, axis=1) - 1
    counts = csum[-1]
    padded = ((counts + (TM_MOE - 1)) // TM_MOE) * TM_MOE
    ends = jnp.cumsum(padded).astype(jnp.int32)
    start_padded = ends - padded
    dest = (start_padded[flat_e] + rank).astype(jnp.int32)
    buf_tok = jnp.zeros((n_blk * TM_MOE,), jnp.int32).at[dest].set(
        jnp.arange(a, dtype=jnp.int32) // TOP_K)
    blk_start = jnp.arange(n_blk, dtype=jnp.int32) * TM_MOE
    blk_expert = jnp.minimum(
        jnp.sum((ends[None, :] <= blk_start[:, None]).astype(jnp.int32), axis=1), N_EXPERTS - 1)
    n_used = (ends[-1:] // TM_MOE).astype(jnp.int32)
    e_ids = jnp.arange(N_EXPERTS, dtype=jnp.int32)
    later_used = (e_ids[None, :] > e_ids[:, None]) & (counts[None, :] > 0)
    next_used = jnp.min(jnp.where(later_used, e_ids[None, :], N_EXPERTS), axis=1)
    next_used = jnp.where(next_used == N_EXPERTS, -1, next_used).astype(jnp.int32)
    return (dest, buf_tok.reshape(n_blk, 1, TM_MOE), blk_expert.astype(jnp.int32),
            next_used[blk_expert], n_used)


def kernel(x_prompt, x_sample, cache_k, cache_v, state_conv, state_h, meta_tokens, norm_mix, w_in, conv_w, conv_b, w_rg, b_rg, w_ig, b_ig, lru_lambda, attn_sinks, w_out, norm_ffn, w_router, b_router, w_up, b_up, w_down, b_down, norm_final):
    n_b, seq = x_prompt.shape[0], x_prompt.shape[1]
    n_s = x_sample.shape[0]
    w_buf = cache_k.shape[2]
    n_p = n_b * seq
    n_tok = n_p + n_s
    row1 = lambda v: v.reshape(1, -1)

    w_in_bf = w_in[0].astype(BF16)
    w_out_bf = w_out[0].astype(BF16)
    wg_tiles = jnp.concatenate([_block_diag_tiles(w_rg[0]), _block_diag_tiles(w_ig[0])],
                               axis=-1).astype(BF16)
    wr_pad = jnp.pad(w_router[0], ((0, 0), (0, LANES - N_EXPERTS))).astype(BF16)
    br_pad = jnp.pad(b_router[0], (0, LANES - N_EXPERTS), constant_values=NEG_BIG).reshape(1, LANES)
    b_up_perm = jnp.swapaxes(b_up[0].reshape(N_EXPERTS, 2 * D_FF // UP_TILE, UP_TILE // 2, 2),
                             2, 3).reshape(N_EXPERTS, 1, 2 * D_FF)
    b_dn = b_down[0].reshape(N_EXPERTS, 1, D_MODEL)
    nm, nf, nfin = row1(norm_mix[0]), row1(norm_ffn[0]), row1(norm_final)
    cw, cb = conv_w[0], row1(conv_b[0])
    brg, big, lam = row1(b_rg[0]), row1(b_ig[0]), row1(lru_lambda[0])
    sinks = attn_sinks[0]
    rnn_w = (cw, cb, wg_tiles, brg, big, lam)
    rnn_w_specs = [_full((CONV_W, D_RNN)), _full((1, D_RNN)),
                   _full((N_GATE_TILES, GATE_TILE, 2 * GATE_TILE)),
                   _full((1, D_RNN)), _full((1, D_RNN)), _full((1, D_RNN))]

    xp2 = x_prompt.reshape(n_p, D_MODEL)
    xr_p, q_p, kv_p, gr_p, ga_p = _in_proj(xp2, nm, w_in_bf, TM_IN)
    x_sm = jnp.concatenate([x_sample.reshape(n_s, D_MODEL), meta_tokens], axis=0)
    xr_sm, q_sm, kv_sm, gr_sm, ga_sm = _in_proj(x_sm, nm, w_in_bf, n_s + N_META)
    xr_s, q_s, kv_s, gr_s, ga_s = (t[:n_s] for t in (xr_sm, q_sm, kv_sm, gr_sm, ga_sm))
    xr_m, kv_m = xr_sm[n_s:], kv_sm[n_s:]

    h_meta = pl.pallas_call(
        _rnn_meta_kernel,
        in_specs=[_full((N_META, D_RNN))] + rnn_w_specs,
        out_specs=_full((1, D_RNN)),
        out_shape=jax.ShapeDtypeStruct((1, D_RNN), F32),
        grid=(1,),
        compiler_params=_cparams(1),
        name="rnn_meta",
    )(xr_m, *rnn_w)

    tt = TT_RNN
    blk3 = pl.BlockSpec((n_b, tt, D_RNN), lambda j: (0, j, 0))
    m_rnn_p, h_last_p = pl.pallas_call(
        functools.partial(_rnn_prompt_kernel, n_b=n_b, tt=tt),
        grid=(seq // tt,),
        in_specs=[blk3, blk3, _full((N_META, D_RNN)), _full((1, D_RNN))] + rnn_w_specs,
        out_specs=[blk3, _full((n_b, D_RNN))],
        out_shape=[jax.ShapeDtypeStruct((n_b, seq, D_RNN), BF16),
                   jax.ShapeDtypeStruct((n_b, D_RNN), F32)],
        scratch_shapes=[pltpu.VMEM((n_b, 8, D_RNN), F32), pltpu.VMEM((n_b, D_RNN), F32)],
        compiler_params=_cparams(1),
        name="rnn_prompt",
    )(xr_p.reshape(n_b, seq, D_RNN), gr_p.reshape(n_b, seq, D_RNN), xr_m, h_meta, *rnn_w)

    sc_t = jnp.swapaxes(state_conv[0], 0, 1)
    m_rnn_s, h_new_s = pl.pallas_call(
        _rnn_sample_kernel,
        grid=(1,),
        in_specs=[_full((n_s, D_RNN)), _full((n_s, D_RNN)), _full((CONV_W - 1, n_s, D_RNN)),
                  _full((n_s, D_RNN))] + rnn_w_specs,
        out_specs=[_full((n_s, D_RNN)), _full((n_s, D_RNN))],
        out_shape=[jax.ShapeDtypeStruct((n_s, D_RNN), BF16),
                   jax.ShapeDtypeStruct((n_s, D_RNN), F32)],
        compiler_params=_cparams(1),
        name="rnn_sample",
    )(xr_s, gr_s, sc_t, state_h[0], *rnn_w)

    n_blk_seq = seq // BLOCK
    kv_meta_blk = jnp.pad(kv_m, ((BLOCK - N_META, 0), (0, 0)))
    smem_spec = pl.BlockSpec(memory_space=pltpu.SMEM)
    rb = lambda w: pl.BlockSpec((BLOCK, w), lambda b, j: (b * n_blk_seq + j, 0))
    m_attn_p = pl.pallas_call(
        _attn_prompt_kernel,
        grid=(n_b, n_blk_seq),
        in_specs=[smem_spec, rb(D_ATTN), rb(2 * D_KV),
                  pl.BlockSpec((BLOCK, 2 * D_KV),
                               lambda b, j: (jnp.maximum(b * n_blk_seq + j - 1, 0), 0)),
                  _full((BLOCK, 2 * D_KV)), rb(D_ATTN)],
        out_specs=rb(D_ATTN),
        out_shape=jax.ShapeDtypeStruct((n_p, D_ATTN), BF16),
        compiler_params=_cparams(2),
        name="attn_prompt",
    )(sinks, q_p, kv_p, kv_p, kv_meta_blk, ga_p)

    ck = cache_k[0].reshape(n_s, w_buf, D_KV)
    cv = cache_v[0].reshape(n_s, w_buf, D_KV)
    per_s = lambda a, b: pl.BlockSpec((SAMPLES_PER_STEP, a, b), lambda i: (i, 0, 0))
    m_attn_s = pl.pallas_call(
        _attn_sample_kernel,
        grid=(n_s // SAMPLES_PER_STEP,),
        in_specs=[smem_spec, per_s(N_HEADS, HEAD_DIM), per_s(1, 2 * D_KV), per_s(w_buf, D_KV),
                  per_s(w_buf, D_KV), per_s(N_HEADS, HEAD_DIM)],
        out_specs=per_s(N_HEADS, HEAD_DIM),
        out_shape=jax.ShapeDtypeStruct((n_s, N_HEADS, HEAD_DIM), BF16),
        compiler_params=_cparams(1),
        name="attn_sample",
    )(sinks, q_s.reshape(n_s, N_HEADS, HEAD_DIM), kv_s.reshape(n_s, 1, 2 * D_KV), ck, cv,
      ga_s.reshape(n_s, N_HEADS, HEAD_DIM)).reshape(n_s, D_ATTN)

    x1_p, hn_all, ti_p, tw_p = _merge(m_rnn_p.reshape(n_p, D_RNN), m_attn_p, xp2, w_out_bf, nf,
                                      wr_pad, br_pad, jnp.zeros((n_tok, D_MODEL), F32), n_tok, 0,
                                      TM_MERGE)
    x1_s, hn_all, ti_s, tw_s = _merge(m_rnn_s, m_attn_s, x_sample.reshape(n_s, D_MODEL), w_out_bf,
                                      nf, wr_pad, br_pad, hn_all, n_tok, n_p // n_s, n_s)

    top_i = jnp.concatenate([ti_p[:, :TOP_K], ti_s[:, :TOP_K]], axis=0)
    n_assign = n_tok * TOP_K
    n_blk = n_assign // TM_MOE + N_EXPERTS - 1 + GATHER_AHEAD
    dest, buf_tok, blk_expert, next_expert, n_used = _route(top_i, n_blk)

    ybuf = _moe(blk_expert, next_expert, n_used, buf_tok, hn_all, w_up[0], b_up_perm, w_down[0],
                b_dn, n_blk)
    y_p = _final(dest[:n_p * TOP_K], x1_p, tw_p, nfin, ybuf)
    y_s = _final(dest[n_p * TOP_K:], x1_s, tw_s, nfin, ybuf)

    kv_p3 = kv_p.reshape(n_b, seq, 2 * D_KV)
    w_p = min(WINDOW, seq + N_META)
    new_k_p = kv_p3[:, seq - w_p:, :D_KV].reshape(1, n_b, w_p, N_KV, HEAD_DIM)
    new_v_p = kv_p3[:, seq - w_p:, D_KV:].reshape(1, n_b, w_p, N_KV, HEAD_DIM)
    new_conv_p = xr_p.reshape(n_b, seq, D_RNN)[:, seq - (CONV_W - 1):][None]
    k_new = kv_s[:, :D_KV].reshape(n_s, 1, N_KV, HEAD_DIM)
    v_new = kv_s[:, D_KV:].reshape(n_s, 1, N_KV, HEAD_DIM)
    new_k_s = jnp.concatenate([cache_k[0], k_new], axis=1)[:, -w_buf:][None]
    new_v_s = jnp.concatenate([cache_v[0], v_new], axis=1)[:, -w_buf:][None]
    new_conv_s = jnp.concatenate([state_conv[0], xr_s[:, None, :]], axis=1)[:, -(CONV_W - 1):][None]
    return (y_p.reshape(n_b, seq, D_MODEL), y_s.reshape(n_s, 1, D_MODEL), new_k_p, new_v_p,
            new_conv_p, h_last_p[None], new_k_s, new_v_s, new_conv_s, h_new_s[None])
```

```python
import functools

import jax
import jax.numpy as jnp
from jax import lax
from jax.experimental import pallas as pl
from jax.experimental.pallas import tpu as pltpu

F32 = jnp.float32
BF16 = jnp.bfloat16

D_MODEL = 1024
N_META = 16
D_RNN = 1024
N_GATE_BLOCKS = 16
GATE_BW = D_RNN // N_GATE_BLOCKS
CONV_W = 4
LRU_C = 8.0
N_HEADS = 16
HEAD_DIM = 64
N_KV = 2
GROUP = N_HEADS // N_KV
D_ATTN = N_HEADS * HEAD_DIM
D_KV = N_KV * HEAD_DIM
WINDOW = 128
BLOCK = 128
PAST_LEN = 16384
N_EXPERTS = 32
TOP_K = 4
D_FF = 1024
SWIGLU_LIMIT = 7.0
SWIGLU_ALPHA = 1.702
EPS = 1e-6

GATE_TILE = 256
N_GATE_TILES = D_RNN // GATE_TILE
LANES = 128
NEG_BIG = -1e30

TM_IN = 256
TT_RNN = 64
TM_MERGE = 256
TM_MOE = 256
TM_FIN = 128
VMEM_LIMIT = 56 * 1024 * 1024


def _cparams(n_grid_dims):
    return pltpu.CompilerParams(
        dimension_semantics=("arbitrary",) * n_grid_dims, vmem_limit_bytes=VMEM_LIMIT)


def _full(shape):
    return pl.BlockSpec(shape, lambda *_: (0,) * len(shape))


def _in_proj_kernel(x_ref, g_ref, w_ref, xr_ref, q_ref, kv_ref, gr_ref, ga_ref):
    x = x_ref[...]
    ms = jnp.mean(x * x, axis=-1, keepdims=True)
    h = (x * lax.rsqrt(ms + EPS) * g_ref[...]).astype(BF16)

    def proj(lo, hi):
        return jnp.dot(h, w_ref[:, lo:hi], preferred_element_type=F32)

    o_q = D_RNN
    o_k = o_q + D_ATTN
    o_gr = o_k + 2 * D_KV
    o_ga = o_gr + D_RNN
    xr_ref[...] = proj(0, o_q)
    q_ref[...] = (proj(o_q, o_k) * (HEAD_DIM ** -0.5)).astype(BF16)
    kv_ref[...] = proj(o_k, o_gr)
    gr_ref[...] = proj(o_gr, o_ga).astype(BF16)
    ga_ref[...] = proj(o_ga, o_ga + D_ATTN).astype(BF16)


def _in_proj(x, g, w_in_bf, tm):
    rows = x.shape[0]
    d_in = w_in_bf.shape[1]
    row = lambda w: pl.BlockSpec((tm, w), lambda i: (i, 0))
    return pl.pallas_call(
        _in_proj_kernel,
        grid=(rows // tm,),
        in_specs=[row(D_MODEL), _full((1, D_MODEL)), _full((D_MODEL, d_in))],
        out_specs=[row(D_RNN), row(D_ATTN), row(2 * D_KV), row(D_RNN), row(D_ATTN)],
        out_shape=[
            jax.ShapeDtypeStruct((rows, D_RNN), F32),
            jax.ShapeDtypeStruct((rows, D_ATTN), BF16),
            jax.ShapeDtypeStruct((rows, 2 * D_KV), F32),
            jax.ShapeDtypeStruct((rows, D_RNN), BF16),
            jax.ShapeDtypeStruct((rows, D_ATTN), BF16),
        ],
        compiler_params=_cparams(1),
        name="in_proj",
    )(x, g, w_in_bf)


def _softplus(x):
    return jnp.maximum(x, 0.0) + jnp.log1p(jnp.exp(-jnp.abs(x)))


def _lru_coeffs(u, wg_ref, brg, big, lam):
    ub = u.astype(BF16)
    sp = _softplus(-lam)
    a_parts, i_parts, m_parts = [], [], []
    for g in range(N_GATE_TILES):
        sl = slice(g * GATE_TILE, (g + 1) * GATE_TILE)
        zz = jnp.dot(ub[:, sl], wg_ref[g], preferred_element_type=F32)
        r = jax.nn.sigmoid(zz[:, :GATE_TILE] + brg[:, sl])
        i = jax.nn.sigmoid(zz[:, GATE_TILE:] + big[:, sl])
        log_a = (-LRU_C) * r * sp[:, sl]
        a = jnp.exp(log_a)
        m = jnp.sqrt(1.0 - a * a)
        a_parts.append(a)
        i_parts.append(i)
        m_parts.append(m)
    cat = lambda ps: jnp.concatenate(ps, axis=-1)
    return cat(a_parts), cat(i_parts), cat(m_parts)


def _conv(ext, t, cw, cb):
    out = cb
    for j in range(CONV_W):
        s = CONV_W - 1 - j
        out = out + cw[j:j + 1, :] * ext[8 - s:8 - s + t, :]
    return out


def _rnn_meta_kernel(xr_ref, cw_ref, cb_ref, wg_ref, brg_ref, big_ref, lam_ref, h_ref):
    x = xr_ref[...]
    ext = jnp.concatenate([jnp.zeros((8, D_RNN), F32), x], axis=0)
    u = _conv(ext, N_META, cw_ref[...], cb_ref[...])
    a, i, m = _lru_coeffs(u, wg_ref, brg_ref[...], big_ref[...], lam_ref[...])
    first = lax.broadcasted_iota(jnp.int32, (N_META, 1), 0) == 0
    b = jnp.where(first, 1.0, m) * i * u
    h = jnp.zeros((1, D_RNN), F32)
    for t in range(N_META):
        h = a[t:t + 1, :] * h + b[t:t + 1, :]
    h_ref[...] = h


SUBLANES = 8


def _linear_scan(a, b, h_prev):
    t, d = a.shape
    g = t // SUBLANES
    a3 = a.reshape(g, SUBLANES, d)
    b3 = b.reshape(g, SUBLANES, d)
    row = lax.broadcasted_iota(jnp.int32, (g, SUBLANES, d), 1)
    step = 1
    while step < SUBLANES:
        keep = row >= step
        a_up = jnp.where(keep, pltpu.roll(a3, step, axis=1), 1.0)
        b_up = jnp.where(keep, pltpu.roll(b3, step, axis=1), 0.0)
        b3 = a3 * b_up + b3
        a3 = a3 * a_up
        step *= 2
    hs = []
    h = h_prev
    for k in range(g):
        hk = a3[k] * h + b3[k]
        hs.append(hk)
        h = hk[SUBLANES - 1:SUBLANES, :]
    return jnp.concatenate(hs, axis=0), h


def _rnn_prompt_kernel(xr_ref, gr_ref, xm_ref, h0_ref, cw_ref, cb_ref, wg_ref, brg_ref, big_ref,
                       lam_ref, m_ref, hl_ref, halo_s, h_s, *, n_b, tt):
    j = pl.program_id(0)

    @pl.when(j == 0)
    def _():
        h_s[...] = jnp.broadcast_to(h0_ref[...], (n_b, D_RNN))
        for b in range(n_b):
            halo_s[b] = xm_ref[N_META - 8:N_META, :]

    cw = cw_ref[...]
    cb = cb_ref[...]

    def per_batch(b, c):
        x = xr_ref[b]
        ext = jnp.concatenate([halo_s[b], x], axis=0)
        halo_s[b] = x[tt - 8:tt, :]
        u = _conv(ext, tt, cw, cb)
        a, i, m = _lru_coeffs(u, wg_ref, brg_ref[...], big_ref[...], lam_ref[...])
        y, h = _linear_scan(a, m * i * u, h_s[pl.ds(b, 1), :])
        h_s[pl.ds(b, 1), :] = h
        m_ref[b] = (jax.nn.sigmoid(gr_ref[b].astype(F32)) * y).astype(BF16)
        return c

    lax.fori_loop(0, n_b, per_batch, 0)
    hl_ref[...] = h_s[...]


def _rnn_sample_kernel(xr_ref, gr_ref, sc_ref, h0_ref, cw_ref, cb_ref, wg_ref, brg_ref, big_ref,
                       lam_ref, m_ref, hn_ref):
    cw = cw_ref[...]
    u = cb_ref[...] + cw[CONV_W - 1:CONV_W, :] * xr_ref[...]
    for j in range(CONV_W - 1):
        u = u + cw[j:j + 1, :] * sc_ref[j]
    a, i, m = _lru_coeffs(u, wg_ref, brg_ref[...], big_ref[...], lam_ref[...])
    h = a * h0_ref[...] + m * i * u
    hn_ref[...] = h
    m_ref[...] = (jax.nn.sigmoid(gr_ref[...].astype(F32)) * h).astype(BF16)


def _alibi_slope(h):
    return 2.0 ** (-8.0 * (h + 1) / N_HEADS)


def _attn_prompt_kernel(sink_ref, q_ref, kc_ref, kp_ref, km_ref, ga_ref, o_ref):
    j = pl.program_id(1)
    first = j == 0
    kv_prev = jnp.where(first, km_ref[...], kp_ref[...])
    kv = jnp.concatenate([kv_prev, kc_ref[...]], axis=0)
    q = q_ref[...]
    qi = lax.broadcasted_iota(jnp.int32, (BLOCK, 2 * BLOCK), 0)
    ci = lax.broadcasted_iota(jnp.int32, (BLOCK, 2 * BLOCK), 1)
    dist = qi + BLOCK - ci
    c_min = jnp.where(first, BLOCK - N_META, 0)
    valid = (dist >= 0) & (dist < WINDOW) & (ci >= c_min)
    distf = dist.astype(F32)
    outs = []
    for h in range(N_HEADS):
        g = h // GROUP
        kh = kv[:, g * HEAD_DIM:(g + 1) * HEAD_DIM].astype(BF16)
        vh = kv[:, D_KV + g * HEAD_DIM:D_KV + (g + 1) * HEAD_DIM].astype(BF16)
        qh = q[:, h * HEAD_DIM:(h + 1) * HEAD_DIM]
        s = lax.dot_general(qh, kh, (((1,), (1,)), ((), ())), preferred_element_type=F32)
        s = jnp.where(valid, s - _alibi_slope(h) * distf, NEG_BIG)
        sink = sink_ref[h]
        mx = jnp.maximum(jnp.max(s, axis=-1, keepdims=True), sink)
        p = jnp.exp(s - mx)
        den = jnp.sum(p, axis=-1, keepdims=True) + jnp.exp(sink - mx)
        o = jnp.dot(p.astype(BF16), vh, preferred_element_type=F32)
        outs.append(o / den)
    y = jnp.concatenate(outs, axis=-1)
    o_ref[...] = (jax.nn.sigmoid(ga_ref[...].astype(F32)) * y).astype(BF16)


SAMPLES_PER_STEP = 8


def _attn_sample_kernel(sink_ref, q_ref, kvn_ref, ck_ref, cv_ref, ga_ref, o_ref):
    w_buf = ck_ref.shape[1]
    ci = lax.broadcasted_iota(jnp.int32, (GROUP, w_buf), 1)
    dist = w_buf - ci
    valid = dist < WINDOW
    distf = dist.astype(F32)
    hrow = lax.broadcasted_iota(jnp.int32, (GROUP, 1), 0)
    bias, sinks = [], []
    for g in range(N_KV):
        slope = jnp.zeros((GROUP, 1), F32)
        sink = jnp.zeros((GROUP, 1), F32)
        for r in range(GROUP):
            slope = jnp.where(hrow == r, _alibi_slope(g * GROUP + r), slope)
            sink = jnp.where(hrow == r, sink_ref[g * GROUP + r], sink)
        bias.append(slope * distf)
        sinks.append(sink)
    for n in range(SAMPLES_PER_STEP):
        q = q_ref[n]
        ck = ck_ref[n].astype(BF16)
        cv = cv_ref[n].astype(BF16)
        kvn = kvn_ref[n].astype(BF16).astype(F32)
        outs = []
        for g in range(N_KV):
            qg = q[g * GROUP:(g + 1) * GROUP, :]
            kg = ck[:, g * HEAD_DIM:(g + 1) * HEAD_DIM]
            vg = cv[:, g * HEAD_DIM:(g + 1) * HEAD_DIM]
            kn = kvn[:, g * HEAD_DIM:(g + 1) * HEAD_DIM]
            vn = kvn[:, D_KV + g * HEAD_DIM:D_KV + (g + 1) * HEAD_DIM]
            s = lax.dot_general(qg, kg, (((1,), (1,)), ((), ())), preferred_element_type=F32)
            s = jnp.where(valid, s - bias[g], NEG_BIG)
            sn = jnp.sum(qg.astype(F32) * kn, axis=-1, keepdims=True)
            mx = jnp.maximum(jnp.maximum(jnp.max(s, axis=-1, keepdims=True), sn), sinks[g])
            p = jnp.exp(s - mx)
            pn = jnp.exp(sn - mx)
            den = jnp.sum(p, axis=-1, keepdims=True) + pn + jnp.exp(sinks[g] - mx)
            o = jnp.dot(p.astype(BF16), vg, preferred_element_type=F32)
            o = o + pn.astype(BF16).astype(F32) * vn
            outs.append(o / den)
        y = jnp.concatenate(outs, axis=0)
        o_ref[n] = (jax.nn.sigmoid(ga_ref[n].astype(F32)) * y).astype(BF16)


def _merge_kernel(mr_ref, ma_ref, x_ref, wo_ref, nf_ref, wr_ref, br_ref, *rest):
    x1_ref, hn_ref, ti_ref, tw_ref = rest[-4:]
    mm = jnp.dot(mr_ref[...], wo_ref[:D_RNN, :], preferred_element_type=F32)
    mm = mm + jnp.dot(ma_ref[...], wo_ref[D_RNN:, :], preferred_element_type=F32)
    x1 = x_ref[...] + mm
    x1_ref[...] = x1
    ms = jnp.mean(x1 * x1, axis=-1, keepdims=True)
    hn = x1 * lax.rsqrt(ms + EPS) * nf_ref[...]
    hn_ref[...] = hn
    logits = jnp.dot(hn.astype(BF16), wr_ref[...], preferred_element_type=F32) + br_ref[...]
    lane = lax.broadcasted_iota(jnp.int32, logits.shape, 1)
    ti = jnp.zeros(logits.shape, jnp.int32)
    tv = jnp.zeros(logits.shape, F32)
    l = logits
    v0 = None
    for k in range(TOP_K):
        mx = jnp.max(l, axis=-1, keepdims=True)
        idx = jnp.min(jnp.where(l == mx, lane, LANES), axis=-1, keepdims=True)
        if k == 0:
            v0 = mx
        ti = jnp.where(lane == k, idx, ti)
        tv = jnp.where(lane == k, jnp.exp(mx - v0), tv)
        l = jnp.where(lane == idx, NEG_BIG * 2, l)
    ti_ref[...] = ti
    tw_ref[...] = tv / jnp.sum(tv, axis=-1, keepdims=True)


def _merge(m_rnn, m_attn, x, w_out_bf, nf, wr_pad, br_pad, hn_all, n_all, row_block0, tm):
    rows = x.shape[0]
    row = lambda w: pl.BlockSpec((tm, w), lambda i: (i, 0))
    in_specs = [row(D_RNN), row(D_ATTN), row(D_MODEL), _full((D_RNN + D_ATTN, D_MODEL)),
                _full((1, D_MODEL)), _full((D_MODEL, LANES)), _full((1, LANES))]
    args = [m_rnn, m_attn, x, w_out_bf, nf, wr_pad, br_pad]
    aliases = {}
    if hn_all is not None:
        in_specs.append(pl.BlockSpec(memory_space=pl.ANY))
        args.append(hn_all)
        aliases = {len(args) - 1: 1}
    return pl.pallas_call(
        _merge_kernel,
        grid=(rows // tm,),
        in_specs=in_specs,
        out_specs=[row(D_MODEL),
                   pl.BlockSpec((tm, D_MODEL), lambda i: (i + row_block0, 0)),
                   row(LANES), row(LANES)],
        out_shape=[
            jax.ShapeDtypeStruct((rows, D_MODEL), F32),
            jax.ShapeDtypeStruct((n_all, D_MODEL), F32),
            jax.ShapeDtypeStruct((rows, LANES), jnp.int32),
            jax.ShapeDtypeStruct((rows, LANES), F32),
        ],
        input_output_aliases=aliases,
        compiler_params=_cparams(1),
        name="merge_router",
    )(*args)


UP_TILE = 256


def _deinterleave_matrix():
    half = UP_TILE // 2
    r = lax.broadcasted_iota(jnp.int32, (UP_TILE, UP_TILE), 0)
    c = lax.broadcasted_iota(jnp.int32, (UP_TILE, UP_TILE), 1)
    src = jnp.where(c < half, 2 * c, 2 * (c - half) + 1)
    return jnp.where(r == src, 1.0, 0.0).astype(BF16)


GATHER_AHEAD = 2
GATHER_DMA_PRIORITY = 0
OTHER_DMA_PRIORITY = 1


def _moe_kernel(be_ref, nxt_ref, nval_ref, nu_ref, tok0_ref, tok1_ref, tok2_ref, asg_ref, hn_hbm,
                wu_hbm, bu_ref, wd_hbm, bd_ref, out_hbm, xbuf, xb_s, ystage, wu_f, wd_f, wu_s, wd_s,
                sems, ssems, wsems):
    i = pl.program_id(0)
    nu = nu_ref[0]
    half = UP_TILE // 2
    n_up = 2 * D_FF // UP_TILE
    slot = i % GATHER_AHEAD
    prev_slot = 1 - slot
    n_prev = nval_ref[i]

    def row_copy(tok_ref, r, s):
        return pltpu.make_async_copy(hn_hbm.at[pl.ds(tok_ref[0, 0, r], 1)],
                                     xbuf.at[s, pl.ds(r, 1)], sems.at[s])

    def weight_copies(e):
        return (pltpu.make_async_copy(wu_hbm.at[e], wu_f, wsems.at[0]),
                pltpu.make_async_copy(wd_hbm.at[e], wd_f, wsems.at[1]))

    def out_copy(r):
        return pltpu.make_async_copy(ystage.at[prev_slot, pl.ds(r, 1)],
                                     out_hbm.at[pl.ds(asg_ref[0, 0, r], 1)], ssems.at[prev_slot])

    @pl.when(i == 0)
    def _():
        for cp in weight_copies(be_ref[0]):
            cp.start(priority=OTHER_DMA_PRIORITY)

        def issue(r, c):
            row_copy(tok0_ref, r, 0).start(priority=GATHER_DMA_PRIORITY)
            row_copy(tok1_ref, r, 1).start(priority=GATHER_DMA_PRIORITY)
            return c

        lax.fori_loop(0, TM_MOE, issue, 0)

    @pl.when(i < nu + GATHER_AHEAD)
    def _():
        pltpu.make_async_copy(hn_hbm.at[pl.ds(0, TM_MOE)], xbuf.at[slot], sems.at[slot]).wait()

    @pl.when((i >= 2) & (i < nu + 2))
    def _():
        n_sent = nval_ref[i - 1]
        n_tiled = pl.multiple_of((n_sent // SUBLANES) * SUBLANES, SUBLANES)

        @pl.when(n_tiled > 0)
        def _():
            pltpu.make_async_copy(ystage.at[slot, pl.ds(0, n_tiled)],
                                  out_hbm.at[pl.ds(0, n_tiled)], ssems.at[slot]).wait()

        def wait_row(r, c):
            pltpu.make_async_copy(ystage.at[slot, pl.ds(0, 1)], out_hbm.at[pl.ds(0, 1)],
                                  ssems.at[slot]).wait()
            return c

        lax.fori_loop(0, n_sent - n_tiled, wait_row, 0)

    @pl.when(i < nu)
    def _():
        xb_s[...] = xbuf[slot].astype(BF16)

        @pl.when((i == 0) | (be_ref[i] != be_ref[jnp.maximum(i - 1, 0)]))
        def _():
            for cp in weight_copies(be_ref[i]):
                cp.wait()
            perm = _deinterleave_matrix()
            for c in range(n_up):
                cols = slice(c * UP_TILE, (c + 1) * UP_TILE)
                blk = wu_f[:, cols].astype(BF16)
                wu_s[:, cols] = jnp.dot(blk, perm, preferred_element_type=F32).astype(BF16)
            wd_s[...] = wd_f[...].astype(BF16)

            @pl.when(nxt_ref[i] >= 0)
            def _():
                for cp in weight_copies(nxt_ref[i]):
                    cp.start(priority=OTHER_DMA_PRIORITY)

    def send_some_rows():
        def send(r, c):
            out_copy(r).start(priority=OTHER_DMA_PRIORITY)
            return c

        lax.fori_loop(0, n_prev, send, 0)

    def run_block(send_all_rows):
        for n in range(TM_MOE):
            r = (n % (TM_MOE // SUBLANES)) * SUBLANES + n // (TM_MOE // SUBLANES)
            row_copy(tok2_ref, r, slot).start(priority=GATHER_DMA_PRIORITY)
            if send_all_rows:
                out_copy(r).start(priority=OTHER_DMA_PRIORITY)
        x = xb_s[...]
        z = jnp.dot(x, wu_s[...], preferred_element_type=F32) + bu_ref[0]
        zg = jnp.concatenate([z[:, c * UP_TILE:c * UP_TILE + half] for c in range(n_up)], axis=-1)
        zl = jnp.concatenate([z[:, c * UP_TILE + half:(c + 1) * UP_TILE] for c in range(n_up)],
                             axis=-1)
        xg = jnp.minimum(zg, SWIGLU_LIMIT)
        xl = jnp.clip(zl, -SWIGLU_LIMIT, SWIGLU_LIMIT)
        act = xg * jax.nn.sigmoid(SWIGLU_ALPHA * xg) * (xl + 1.0)
        ystage[slot] = (jnp.dot(act.astype(BF16), wd_s[...], preferred_element_type=F32)
                        + bd_ref[0])

    @pl.when((i < nu) & (n_prev == TM_MOE))
    def _():
        run_block(send_all_rows=True)

    @pl.when((i < nu) & (n_prev != TM_MOE))
    def _():
        send_some_rows()
        run_block(send_all_rows=False)

    @pl.when(i == nu)
    def _():
        send_some_rows()


def _moe(blk_expert, next_expert, n_valid, n_used, buf_tok, buf_asg, hn_all, w_up, b_up_perm,
         w_down, b_down, n_blk, n_assign):
    ew = lambda r, c: pl.BlockSpec((1, r, c), lambda i, be, nxt, nv, nu: (be[i], 0, 0))
    tok = lambda f: pl.BlockSpec((1, 1, TM_MOE), lambda i, be, nxt, nv, nu: (f(i), 0, 0),
                                 memory_space=pltpu.SMEM)
    hbm = pl.BlockSpec(memory_space=pl.ANY)
    return pl.pallas_call(
        _moe_kernel,
        grid_spec=pltpu.PrefetchScalarGridSpec(
            num_scalar_prefetch=4,
            grid=(n_blk,),
            in_specs=[
                tok(lambda i: i), tok(lambda i: jnp.minimum(i + 1, n_blk - 1)),
                tok(lambda i: jnp.minimum(i + GATHER_AHEAD, n_blk - 1)), tok(lambda i: i),
                hbm, hbm, ew(1, 2 * D_FF), hbm, ew(1, D_MODEL),
            ],
            out_specs=hbm,
            scratch_shapes=[pltpu.VMEM((GATHER_AHEAD, TM_MOE, D_MODEL), F32),
                            pltpu.VMEM((TM_MOE, D_MODEL), BF16),
                            pltpu.VMEM((2, TM_MOE, D_MODEL), F32),
                            pltpu.VMEM((D_MODEL, 2 * D_FF), F32),
                            pltpu.VMEM((D_FF, D_MODEL), F32),
                            pltpu.VMEM((D_MODEL, 2 * D_FF), BF16),
                            pltpu.VMEM((D_FF, D_MODEL), BF16),
                            pltpu.SemaphoreType.DMA((GATHER_AHEAD,)),
                            pltpu.SemaphoreType.DMA((2,)),
                            pltpu.SemaphoreType.DMA((2,))],
        ),
        out_shape=jax.ShapeDtypeStruct((n_assign, D_MODEL), F32),
        compiler_params=_cparams(1),
        name="moe_experts",
    )(blk_expert, next_expert, n_valid, n_used, buf_tok, buf_tok, buf_tok, buf_asg, hn_all, w_up,
      b_up_perm, w_down, b_down)


def _final_kernel(x1_ref, tw_ref, nf_ref, y_ref, o_ref):
    tw = tw_ref[...]
    x = x1_ref[...]
    for k in range(TOP_K):
        x = x + tw[:, k:k + 1] * y_ref[:, k * D_MODEL:(k + 1) * D_MODEL]
    ms = jnp.mean(x * x, axis=-1, keepdims=True)
    o_ref[...] = x * lax.rsqrt(ms + EPS) * nf_ref[...]


def _final(x1, tw, nf, y_tok, row_block0):
    rows = x1.shape[0]
    row = lambda w: pl.BlockSpec((TM_FIN, w), lambda i: (i, 0))
    return pl.pallas_call(
        _final_kernel,
        grid=(rows // TM_FIN,),
        in_specs=[row(D_MODEL), row(LANES), _full((1, D_MODEL)),
                  pl.BlockSpec((TM_FIN, TOP_K * D_MODEL), lambda i: (i + row_block0, 0))],
        out_specs=row(D_MODEL),
        out_shape=jax.ShapeDtypeStruct((rows, D_MODEL), F32),
        compiler_params=_cparams(1),
        name="combine_final",
    )(x1, tw, nf, y_tok)


def _block_diag_tiles(w):
    per = GATE_TILE // GATE_BW
    w4 = w.reshape(N_GATE_TILES, per, GATE_BW, GATE_BW)
    eye = jnp.eye(per, dtype=w.dtype)
    return jnp.einsum("gacd,ab->gacbd", w4, eye).reshape(N_GATE_TILES, GATE_TILE, GATE_TILE)


def _route(top_i, n_blk):
    a = top_i.size
    flat_e = top_i.reshape(-1)
    onehot = (flat_e[:, None] == jnp.arange(N_EXPERTS, dtype=jnp.int32)[None, :]).astype(jnp.int32)
    csum = jnp.cumsum(onehot, axis=0)
    rank = jnp.sum(csum * onehot, axis=1) - 1
    counts = csum[-1]
    padded = ((counts + (TM_MOE - 1)) // TM_MOE) * TM_MOE
    ends = jnp.cumsum(padded).astype(jnp.int32)
    start_padded = ends - padded
    dest = (start_padded[flat_e] + rank).astype(jnp.int32)
    buf_asg = jnp.zeros((n_blk * TM_MOE,), jnp.int32).at[dest].set(jnp.arange(a, dtype=jnp.int32))
    buf_tok = buf_asg // TOP_K
    blk_start = jnp.arange(n_blk, dtype=jnp.int32) * TM_MOE
    blk_expert = jnp.minimum(
        jnp.sum((ends[None, :] <= blk_start[:, None]).astype(jnp.int32), axis=1), N_EXPERTS - 1)
    n_used = (ends[-1:] // TM_MOE).astype(jnp.int32)
    real_end = (start_padded + counts)[blk_expert]
    n_valid = jnp.clip(real_end - blk_start, 0, TM_MOE).astype(jnp.int32)
    n_valid = jnp.concatenate([jnp.zeros((1,), jnp.int32), n_valid])
    buf_asg = jnp.concatenate([jnp.zeros((TM_MOE,), jnp.int32), buf_asg])
    e_ids = jnp.arange(N_EXPERTS, dtype=jnp.int32)
    later_used = (e_ids[None, :] > e_ids[:, None]) & (counts[None, :] > 0)
    next_used = jnp.min(jnp.where(later_used, e_ids[None, :], N_EXPERTS), axis=1)
    next_used = jnp.where(next_used == N_EXPERTS, -1, next_used).astype(jnp.int32)
    return (buf_tok.reshape(n_blk, 1, TM_MOE), buf_asg.reshape(n_blk + 1, 1, TM_MOE),
            blk_expert.astype(jnp.int32), next_used[blk_expert], n_valid, n_used)


def kernel(x_prompt, x_sample, cache_k, cache_v, state_conv, state_h, meta_tokens, norm_mix, w_in, conv_w, conv_b, w_rg, b_rg, w_ig, b_ig, lru_lambda, attn_sinks, w_out, norm_ffn, w_router, b_router, w_up, b_up, w_down, b_down, norm_final):
    n_b, seq = x_prompt.shape[0], x_prompt.shape[1]
    n_s = x_sample.shape[0]
    w_buf = cache_k.shape[2]
    n_p = n_b * seq
    n_tok = n_p + n_s
    row1 = lambda v: v.reshape(1, -1)

    w_in_bf = w_in[0].astype(BF16)
    w_out_bf = w_out[0].astype(BF16)
    wg_tiles = jnp.concatenate([_block_diag_tiles(w_rg[0]), _block_diag_tiles(w_ig[0])],
                               axis=-1).astype(BF16)
    wr_pad = jnp.pad(w_router[0], ((0, 0), (0, LANES - N_EXPERTS))).astype(BF16)
    br_pad = jnp.pad(b_router[0], (0, LANES - N_EXPERTS), constant_values=NEG_BIG).reshape(1, LANES)
    b_up_perm = jnp.swapaxes(b_up[0].reshape(N_EXPERTS, 2 * D_FF // UP_TILE, UP_TILE // 2, 2),
                             2, 3).reshape(N_EXPERTS, 1, 2 * D_FF)
    b_dn = b_down[0].reshape(N_EXPERTS, 1, D_MODEL)
    nm, nf, nfin = row1(norm_mix[0]), row1(norm_ffn[0]), row1(norm_final)
    cw, cb = conv_w[0], row1(conv_b[0])
    brg, big, lam = row1(b_rg[0]), row1(b_ig[0]), row1(lru_lambda[0])
    sinks = attn_sinks[0]
    rnn_w = (cw, cb, wg_tiles, brg, big, lam)
    rnn_w_specs = [_full((CONV_W, D_RNN)), _full((1, D_RNN)),
                   _full((N_GATE_TILES, GATE_TILE, 2 * GATE_TILE)),
                   _full((1, D_RNN)), _full((1, D_RNN)), _full((1, D_RNN))]

    xp2 = x_prompt.reshape(n_p, D_MODEL)
    xr_p, q_p, kv_p, gr_p, ga_p = _in_proj(xp2, nm, w_in_bf, TM_IN)
    x_sm = jnp.concatenate([x_sample.reshape(n_s, D_MODEL), meta_tokens], axis=0)
    xr_sm, q_sm, kv_sm, gr_sm, ga_sm = _in_proj(x_sm, nm, w_in_bf, n_s + N_META)
    xr_s, q_s, kv_s, gr_s, ga_s = (t[:n_s] for t in (xr_sm, q_sm, kv_sm, gr_sm, ga_sm))
    xr_m, kv_m = xr_sm[n_s:], kv_sm[n_s:]

    h_meta = pl.pallas_call(
        _rnn_meta_kernel,
        in_specs=[_full((N_META, D_RNN))] + rnn_w_specs,
        out_specs=_full((1, D_RNN)),
        out_shape=jax.ShapeDtypeStruct((1, D_RNN), F32),
        grid=(1,),
        compiler_params=_cparams(1),
        name="rnn_meta",
    )(xr_m, *rnn_w)

    tt = TT_RNN
    blk3 = pl.BlockSpec((n_b, tt, D_RNN), lambda j: (0, j, 0))
    m_rnn_p, h_last_p = pl.pallas_call(
        functools.partial(_rnn_prompt_kernel, n_b=n_b, tt=tt),
        grid=(seq // tt,),
        in_specs=[blk3, blk3, _full((N_META, D_RNN)), _full((1, D_RNN))] + rnn_w_specs,
        out_specs=[blk3, _full((n_b, D_RNN))],
        out_shape=[jax.ShapeDtypeStruct((n_b, seq, D_RNN), BF16),
                   jax.ShapeDtypeStruct((n_b, D_RNN), F32)],
        scratch_shapes=[pltpu.VMEM((n_b, 8, D_RNN), F32), pltpu.VMEM((n_b, D_RNN), F32)],
        compiler_params=_cparams(1),
        name="rnn_prompt",
    )(xr_p.reshape(n_b, seq, D_RNN), gr_p.reshape(n_b, seq, D_RNN), xr_m, h_meta, *rnn_w)

    sc_t = jnp.swapaxes(state_conv[0], 0, 1)
    m_rnn_s, h_new_s = pl.pallas_call(
        _rnn_sample_kernel,
        grid=(1,),
        in_specs=[_full((n_s, D_RNN)), _full((n_s, D_RNN)), _full((CONV_W - 1, n_s, D_RNN)),
                  _full((n_s, D_RNN))] + rnn_w_specs,
        out_specs=[_full((n_s, D_RNN)), _full((n_s, D_RNN))],
        out_shape=[jax.ShapeDtypeStruct((n_s, D_RNN), BF16),
                   jax.ShapeDtypeStruct((n_s, D_RNN), F32)],
        compiler_params=_cparams(1),
        name="rnn_sample",
    )(xr_s, gr_s, sc_t, state_h[0], *rnn_w)

    n_blk_seq = seq // BLOCK
    kv_meta_blk = jnp.pad(kv_m, ((BLOCK - N_META, 0), (0, 0)))
    smem_spec = pl.BlockSpec(memory_space=pltpu.SMEM)
    rb = lambda w: pl.BlockSpec((BLOCK, w), lambda b, j: (b * n_blk_seq + j, 0))
    m_attn_p = pl.pallas_call(
        _attn_prompt_kernel,
        grid=(n_b, n_blk_seq),
        in_specs=[smem_spec, rb(D_ATTN), rb(2 * D_KV),
                  pl.BlockSpec((BLOCK, 2 * D_KV),
                               lambda b, j: (jnp.maximum(b * n_blk_seq + j - 1, 0), 0)),
                  _full((BLOCK, 2 * D_KV)), rb(D_ATTN)],
        out_specs=rb(D_ATTN),
        out_shape=jax.ShapeDtypeStruct((n_p, D_ATTN), BF16),
        compiler_params=_cparams(2),
        name="attn_prompt",
    )(sinks, q_p, kv_p, kv_p, kv_meta_blk, ga_p)

    ck = cache_k[0].reshape(n_s, w_buf, D_KV)
    cv = cache_v[0].reshape(n_s, w_buf, D_KV)
    per_s = lambda a, b: pl.BlockSpec((SAMPLES_PER_STEP, a, b), lambda i: (i, 0, 0))
    m_attn_s = pl.pallas_call(
        _attn_sample_kernel,
        grid=(n_s // SAMPLES_PER_STEP,),
        in_specs=[smem_spec, per_s(N_HEADS, HEAD_DIM), per_s(1, 2 * D_KV), per_s(w_buf, D_KV),
                  per_s(w_buf, D_KV), per_s(N_HEADS, HEAD_DIM)],
        out_specs=per_s(N_HEADS, HEAD_DIM),
        out_shape=jax.ShapeDtypeStruct((n_s, N_HEADS, HEAD_DIM), BF16),
        compiler_params=_cparams(1),
        name="attn_sample",
    )(sinks, q_s.reshape(n_s, N_HEADS, HEAD_DIM), kv_s.reshape(n_s, 1, 2 * D_KV), ck, cv,
      ga_s.reshape(n_s, N_HEADS, HEAD_DIM)).reshape(n_s, D_ATTN)

    x1_p, hn_all, ti_p, tw_p = _merge(m_rnn_p.reshape(n_p, D_RNN), m_attn_p, xp2, w_out_bf, nf,
                                      wr_pad, br_pad, jnp.zeros((n_tok, D_MODEL), F32), n_tok, 0,
                                      TM_MERGE)
    x1_s, hn_all, ti_s, tw_s = _merge(m_rnn_s, m_attn_s, x_sample.reshape(n_s, D_MODEL), w_out_bf,
                                      nf, wr_pad, br_pad, hn_all, n_tok, n_p // n_s, n_s)

    top_i = jnp.concatenate([ti_p[:, :TOP_K], ti_s[:, :TOP_K]], axis=0)
    n_assign = n_tok * TOP_K
    n_blk = n_assign // TM_MOE + N_EXPERTS - 1 + GATHER_AHEAD
    buf_tok, buf_asg, blk_expert, next_expert, n_valid, n_used = _route(top_i, n_blk)

    y_tok = _moe(blk_expert, next_expert, n_valid, n_used, buf_tok, buf_asg, hn_all, w_up[0],
                 b_up_perm, w_down[0], b_dn, n_blk, n_assign).reshape(n_tok, TOP_K * D_MODEL)
    y_p = _final(x1_p, tw_p, nfin, y_tok, 0)
    y_s = _final(x1_s, tw_s, nfin, y_tok, n_p // TM_FIN)

    kv_p3 = kv_p.reshape(n_b, seq, 2 * D_KV)
    w_p = min(WINDOW, seq + N_META)
    new_k_p = kv_p3[:, seq - w_p:, :D_KV].reshape(1, n_b, w_p, N_KV, HEAD_DIM)
    new_v_p = kv_p3[:, seq - w_p:, D_KV:].reshape(1, n_b, w_p, N_KV, HEAD_DIM)
    new_conv_p = xr_p.reshape(n_b, seq, D_RNN)[:, seq - (CONV_W - 1):][None]
    k_new = kv_s[:, :D_KV].reshape(n_s, 1, N_KV, HEAD_DIM)
    v_new = kv_s[:, D_KV:].reshape(n_s, 1, N_KV, HEAD_DIM)
    new_k_s = jnp.concatenate([cache_k[0], k_new], axis=1)[:, -w_buf:][None]
    new_v_s = jnp.concatenate([cache_v[0], v_new], axis=1)[:, -w_buf:][None]
    new_conv_s = jnp.concatenate([state_conv[0], xr_s[:, None, :]], axis=1)[:, -(CONV_W - 1):][None]
    return (y_p.reshape(n_b, seq, D_MODEL), y_s.reshape(n_s, 1, D_MODEL), new_k_p, new_v_p,
            new_conv_p, h_last_p[None], new_k_s, new_v_s, new_conv_s, h_new_s[None])
```

```python
import functools

import jax
import jax.numpy as jnp
from jax import lax
from jax.experimental import pallas as pl
from jax.experimental.pallas import tpu as pltpu

F32 = jnp.float32
BF16 = jnp.bfloat16

D_MODEL = 1024
N_META = 16
D_RNN = 1024
N_GATE_BLOCKS = 16
GATE_BW = D_RNN // N_GATE_BLOCKS
CONV_W = 4
LRU_C = 8.0
N_HEADS = 16
HEAD_DIM = 64
N_KV = 2
GROUP = N_HEADS // N_KV
D_ATTN = N_HEADS * HEAD_DIM
D_KV = N_KV * HEAD_DIM
WINDOW = 128
BLOCK = 128
PAST_LEN = 16384
N_EXPERTS = 32
TOP_K = 4
D_FF = 1024
SWIGLU_LIMIT = 7.0
SWIGLU_ALPHA = 1.702
EPS = 1e-6

GATE_TILE = 256
N_GATE_TILES = D_RNN // GATE_TILE
LANES = 128
NEG_BIG = -1e30

TM_IN = 256
TT_RNN = 64
TM_MERGE = 256
TM_MOE = 256
TM_FIN = 128
VMEM_LIMIT = 56 * 1024 * 1024


def _cparams(n_grid_dims):
    return pltpu.CompilerParams(
        dimension_semantics=("arbitrary",) * n_grid_dims, vmem_limit_bytes=VMEM_LIMIT)


def _full(shape):
    return pl.BlockSpec(shape, lambda *_: (0,) * len(shape))


SUBLANES = 8
ROW_TILE = (SUBLANES, D_MODEL // SUBLANES)


def _transpose_vreg_group(v):
    sub = lax.broadcasted_iota(jnp.int32, v[0].shape, 0)
    v = list(v)
    d = SUBLANES // 2
    while d >= 1:
        hi = (sub & d) != 0
        nv = list(v)
        for a in range(SUBLANES):
            if a & d:
                continue
            b = a | d
            nv[a] = jnp.where(hi, pltpu.roll(v[b], d, axis=0), v[a])
            nv[b] = jnp.where(hi, v[b], pltpu.roll(v[a], SUBLANES - d, axis=0))
        v = nv
        d //= 2
    return v


def _store_row_tiled(x, dst):
    for g in range(x.shape[0] // SUBLANES):
        rows = slice(g * SUBLANES, (g + 1) * SUBLANES)
        w = _transpose_vreg_group([x[rows, s * LANES:(s + 1) * LANES] for s in range(SUBLANES)])
        for j in range(SUBLANES):
            dst[g * SUBLANES + j] = w[j]


def _load_row_tiled(src, n_rows):
    groups = []
    for g in range(n_rows // SUBLANES):
        v = _transpose_vreg_group([src[g * SUBLANES + j] for j in range(SUBLANES)])
        groups.append(jnp.concatenate(v, axis=-1))
    return jnp.concatenate(groups, axis=0)


def _in_proj_kernel(x_ref, g_ref, w_ref, xr_ref, q_ref, kv_ref, gr_ref, ga_ref):
    x = x_ref[...]
    ms = jnp.mean(x * x, axis=-1, keepdims=True)
    h = (x * lax.rsqrt(ms + EPS) * g_ref[...]).astype(BF16)

    def proj(lo, hi):
        return jnp.dot(h, w_ref[:, lo:hi], preferred_element_type=F32)

    o_q = D_RNN
    o_k = o_q + D_ATTN
    o_gr = o_k + 2 * D_KV
    o_ga = o_gr + D_RNN
    xr_ref[...] = proj(0, o_q)
    q_ref[...] = (proj(o_q, o_k) * (HEAD_DIM ** -0.5)).astype(BF16)
    kv_ref[...] = proj(o_k, o_gr)
    gr_ref[...] = proj(o_gr, o_ga).astype(BF16)
    ga_ref[...] = proj(o_ga, o_ga + D_ATTN).astype(BF16)


def _in_proj(x, g, w_in_bf, tm):
    rows = x.shape[0]
    d_in = w_in_bf.shape[1]
    row = lambda w: pl.BlockSpec((tm, w), lambda i: (i, 0))
    return pl.pallas_call(
        _in_proj_kernel,
        grid=(rows // tm,),
        in_specs=[row(D_MODEL), _full((1, D_MODEL)), _full((D_MODEL, d_in))],
        out_specs=[row(D_RNN), row(D_ATTN), row(2 * D_KV), row(D_RNN), row(D_ATTN)],
        out_shape=[
            jax.ShapeDtypeStruct((rows, D_RNN), F32),
            jax.ShapeDtypeStruct((rows, D_ATTN), BF16),
            jax.ShapeDtypeStruct((rows, 2 * D_KV), F32),
            jax.ShapeDtypeStruct((rows, D_RNN), BF16),
            jax.ShapeDtypeStruct((rows, D_ATTN), BF16),
        ],
        compiler_params=_cparams(1),
        name="in_proj",
    )(x, g, w_in_bf)


def _softplus(x):
    return jnp.maximum(x, 0.0) + jnp.log1p(jnp.exp(-jnp.abs(x)))


def _lru_coeffs(u, wg_ref, brg, big, lam):
    ub = u.astype(BF16)
    sp = _softplus(-lam)
    a_parts, i_parts, m_parts = [], [], []
    for g in range(N_GATE_TILES):
        sl = slice(g * GATE_TILE, (g + 1) * GATE_TILE)
        zz = jnp.dot(ub[:, sl], wg_ref[g], preferred_element_type=F32)
        r = jax.nn.sigmoid(zz[:, :GATE_TILE] + brg[:, sl])
        i = jax.nn.sigmoid(zz[:, GATE_TILE:] + big[:, sl])
        log_a = (-LRU_C) * r * sp[:, sl]
        a = jnp.exp(log_a)
        m = jnp.sqrt(1.0 - a * a)
        a_parts.append(a)
        i_parts.append(i)
        m_parts.append(m)
    cat = lambda ps: jnp.concatenate(ps, axis=-1)
    return cat(a_parts), cat(i_parts), cat(m_parts)


def _conv(ext, t, cw, cb):
    out = cb
    for j in range(CONV_W):
        s = CONV_W - 1 - j
        out = out + cw[j:j + 1, :] * ext[8 - s:8 - s + t, :]
    return out


def _rnn_meta_kernel(xr_ref, cw_ref, cb_ref, wg_ref, brg_ref, big_ref, lam_ref, h_ref):
    x = xr_ref[...]
    ext = jnp.concatenate([jnp.zeros((8, D_RNN), F32), x], axis=0)
    u = _conv(ext, N_META, cw_ref[...], cb_ref[...])
    a, i, m = _lru_coeffs(u, wg_ref, brg_ref[...], big_ref[...], lam_ref[...])
    first = lax.broadcasted_iota(jnp.int32, (N_META, 1), 0) == 0
    b = jnp.where(first, 1.0, m) * i * u
    h = jnp.zeros((1, D_RNN), F32)
    for t in range(N_META):
        h = a[t:t + 1, :] * h + b[t:t + 1, :]
    h_ref[...] = h


def _linear_scan(a, b, h_prev):
    t, d = a.shape
    g = t // SUBLANES
    a3 = a.reshape(g, SUBLANES, d)
    b3 = b.reshape(g, SUBLANES, d)
    row = lax.broadcasted_iota(jnp.int32, (g, SUBLANES, d), 1)
    step = 1
    while step < SUBLANES:
        keep = row >= step
        a_up = jnp.where(keep, pltpu.roll(a3, step, axis=1), 1.0)
        b_up = jnp.where(keep, pltpu.roll(b3, step, axis=1), 0.0)
        b3 = a3 * b_up + b3
        a3 = a3 * a_up
        step *= 2
    hs = []
    h = h_prev
    for k in range(g):
        hk = a3[k] * h + b3[k]
        hs.append(hk)
        h = hk[SUBLANES - 1:SUBLANES, :]
    return jnp.concatenate(hs, axis=0), h


def _rnn_prompt_kernel(xr_ref, gr_ref, xm_ref, h0_ref, cw_ref, cb_ref, wg_ref, brg_ref, big_ref,
                       lam_ref, m_ref, hl_ref, halo_s, h_s, *, n_b, tt):
    j = pl.program_id(0)

    @pl.when(j == 0)
    def _():
        h_s[...] = jnp.broadcast_to(h0_ref[...], (n_b, D_RNN))
        for b in range(n_b):
            halo_s[b] = xm_ref[N_META - 8:N_META, :]

    cw = cw_ref[...]
    cb = cb_ref[...]

    def per_batch(b, c):
        x = xr_ref[b]
        ext = jnp.concatenate([halo_s[b], x], axis=0)
        halo_s[b] = x[tt - 8:tt, :]
        u = _conv(ext, tt, cw, cb)
        a, i, m = _lru_coeffs(u, wg_ref, brg_ref[...], big_ref[...], lam_ref[...])
        y, h = _linear_scan(a, m * i * u, h_s[pl.ds(b, 1), :])
        h_s[pl.ds(b, 1), :] = h
        m_ref[b] = (jax.nn.sigmoid(gr_ref[b].astype(F32)) * y).astype(BF16)
        return c

    lax.fori_loop(0, n_b, per_batch, 0)
    hl_ref[...] = h_s[...]


def _rnn_sample_kernel(xr_ref, gr_ref, sc_ref, h0_ref, cw_ref, cb_ref, wg_ref, brg_ref, big_ref,
                       lam_ref, m_ref, hn_ref):
    cw = cw_ref[...]
    u = cb_ref[...] + cw[CONV_W - 1:CONV_W, :] * xr_ref[...]
    for j in range(CONV_W - 1):
        u = u + cw[j:j + 1, :] * sc_ref[j]
    a, i, m = _lru_coeffs(u, wg_ref, brg_ref[...], big_ref[...], lam_ref[...])
    h = a * h0_ref[...] + m * i * u
    hn_ref[...] = h
    m_ref[...] = (jax.nn.sigmoid(gr_ref[...].astype(F32)) * h).astype(BF16)


def _alibi_slope(h):
    return 2.0 ** (-8.0 * (h + 1) / N_HEADS)


def _attn_prompt_kernel(sink_ref, q_ref, kc_ref, kp_ref, km_ref, ga_ref, o_ref):
    j = pl.program_id(1)
    first = j == 0
    kv_prev = jnp.where(first, km_ref[...], kp_ref[...])
    kv = jnp.concatenate([kv_prev, kc_ref[...]], axis=0)
    q = q_ref[...]
    qi = lax.broadcasted_iota(jnp.int32, (BLOCK, 2 * BLOCK), 0)
    ci = lax.broadcasted_iota(jnp.int32, (BLOCK, 2 * BLOCK), 1)
    dist = qi + BLOCK - ci
    c_min = jnp.where(first, BLOCK - N_META, 0)
    valid = (dist >= 0) & (dist < WINDOW) & (ci >= c_min)
    distf = dist.astype(F32)
    outs = []
    for h in range(N_HEADS):
        g = h // GROUP
        kh = kv[:, g * HEAD_DIM:(g + 1) * HEAD_DIM].astype(BF16)
        vh = kv[:, D_KV + g * HEAD_DIM:D_KV + (g + 1) * HEAD_DIM].astype(BF16)
        qh = q[:, h * HEAD_DIM:(h + 1) * HEAD_DIM]
        s = lax.dot_general(qh, kh, (((1,), (1,)), ((), ())), preferred_element_type=F32)
        s = jnp.where(valid, s - _alibi_slope(h) * distf, NEG_BIG)
        sink = sink_ref[h]
        mx = jnp.maximum(jnp.max(s, axis=-1, keepdims=True), sink)
        p = jnp.exp(s - mx)
        den = jnp.sum(p, axis=-1, keepdims=True) + jnp.exp(sink - mx)
        o = jnp.dot(p.astype(BF16), vh, preferred_element_type=F32)
        outs.append(o / den)
    y = jnp.concatenate(outs, axis=-1)
    o_ref[...] = (jax.nn.sigmoid(ga_ref[...].astype(F32)) * y).astype(BF16)


SAMPLES_PER_STEP = 8


def _attn_sample_kernel(sink_ref, q_ref, kvn_ref, ck_ref, cv_ref, ga_ref, o_ref):
    w_buf = ck_ref.shape[1]
    ci = lax.broadcasted_iota(jnp.int32, (GROUP, w_buf), 1)
    dist = w_buf - ci
    valid = dist < WINDOW
    distf = dist.astype(F32)
    hrow = lax.broadcasted_iota(jnp.int32, (GROUP, 1), 0)
    bias, sinks = [], []
    for g in range(N_KV):
        slope = jnp.zeros((GROUP, 1), F32)
        sink = jnp.zeros((GROUP, 1), F32)
        for r in range(GROUP):
            slope = jnp.where(hrow == r, _alibi_slope(g * GROUP + r), slope)
            sink = jnp.where(hrow == r, sink_ref[g * GROUP + r], sink)
        bias.append(slope * distf)
        sinks.append(sink)
    for n in range(SAMPLES_PER_STEP):
        q = q_ref[n]
        ck = ck_ref[n].astype(BF16)
        cv = cv_ref[n].astype(BF16)
        kvn = kvn_ref[n].astype(BF16).astype(F32)
        outs = []
        for g in range(N_KV):
            qg = q[g * GROUP:(g + 1) * GROUP, :]
            kg = ck[:, g * HEAD_DIM:(g + 1) * HEAD_DIM]
            vg = cv[:, g * HEAD_DIM:(g + 1) * HEAD_DIM]
            kn = kvn[:, g * HEAD_DIM:(g + 1) * HEAD_DIM]
            vn = kvn[:, D_KV + g * HEAD_DIM:D_KV + (g + 1) * HEAD_DIM]
            s = lax.dot_general(qg, kg, (((1,), (1,)), ((), ())), preferred_element_type=F32)
            s = jnp.where(valid, s - bias[g], NEG_BIG)
            sn = jnp.sum(qg.astype(F32) * kn, axis=-1, keepdims=True)
            mx = jnp.maximum(jnp.maximum(jnp.max(s, axis=-1, keepdims=True), sn), sinks[g])
            p = jnp.exp(s - mx)
            pn = jnp.exp(sn - mx)
            den = jnp.sum(p, axis=-1, keepdims=True) + pn + jnp.exp(sinks[g] - mx)
            o = jnp.dot(p.astype(BF16), vg, preferred_element_type=F32)
            o = o + pn.astype(BF16).astype(F32) * vn
            outs.append(o / den)
        y = jnp.concatenate(outs, axis=0)
        o_ref[n] = (jax.nn.sigmoid(ga_ref[n].astype(F32)) * y).astype(BF16)


def _merge_kernel(mr_ref, ma_ref, x_ref, wo_ref, nf_ref, wr_ref, br_ref, *rest):
    x1_ref, hn_ref, ti_ref, tw_ref = rest[-4:]
    mm = jnp.dot(mr_ref[...], wo_ref[:D_RNN, :], preferred_element_type=F32)
    mm = mm + jnp.dot(ma_ref[...], wo_ref[D_RNN:, :], preferred_element_type=F32)
    x1 = x_ref[...] + mm
    x1_ref[...] = x1
    ms = jnp.mean(x1 * x1, axis=-1, keepdims=True)
    hn = x1 * lax.rsqrt(ms + EPS) * nf_ref[...]
    _store_row_tiled(hn, hn_ref)
    logits = jnp.dot(hn.astype(BF16), wr_ref[...], preferred_element_type=F32) + br_ref[...]
    lane = lax.broadcasted_iota(jnp.int32, logits.shape, 1)
    ti = jnp.zeros(logits.shape, jnp.int32)
    tv = jnp.zeros(logits.shape, F32)
    l = logits
    v0 = None
    for k in range(TOP_K):
        mx = jnp.max(l, axis=-1, keepdims=True)
        idx = jnp.min(jnp.where(l == mx, lane, LANES), axis=-1, keepdims=True)
        if k == 0:
            v0 = mx
        ti = jnp.where(lane == k, idx, ti)
        tv = jnp.where(lane == k, jnp.exp(mx - v0), tv)
        l = jnp.where(lane == idx, NEG_BIG * 2, l)
    ti_ref[...] = ti
    tw_ref[...] = tv / jnp.sum(tv, axis=-1, keepdims=True)


def _merge(m_rnn, m_attn, x, w_out_bf, nf, wr_pad, br_pad, hn_all, n_all, row_block0, tm):
    rows = x.shape[0]
    row = lambda w: pl.BlockSpec((tm, w), lambda i: (i, 0))
    in_specs = [row(D_RNN), row(D_ATTN), row(D_MODEL), _full((D_RNN + D_ATTN, D_MODEL)),
                _full((1, D_MODEL)), _full((D_MODEL, LANES)), _full((1, LANES))]
    args = [m_rnn, m_attn, x, w_out_bf, nf, wr_pad, br_pad]
    aliases = {}
    if hn_all is not None:
        in_specs.append(pl.BlockSpec(memory_space=pl.ANY))
        args.append(hn_all)
        aliases = {len(args) - 1: 1}
    return pl.pallas_call(
        _merge_kernel,
        grid=(rows // tm,),
        in_specs=in_specs,
        out_specs=[row(D_MODEL),
                   pl.BlockSpec((tm,) + ROW_TILE, lambda i: (i + row_block0, 0, 0)),
                   row(LANES), row(LANES)],
        out_shape=[
            jax.ShapeDtypeStruct((rows, D_MODEL), F32),
            jax.ShapeDtypeStruct((n_all,) + ROW_TILE, F32),
            jax.ShapeDtypeStruct((rows, LANES), jnp.int32),
            jax.ShapeDtypeStruct((rows, LANES), F32),
        ],
        input_output_aliases=aliases,
        compiler_params=_cparams(1),
        name="merge_router",
    )(*args)


UP_TILE = 256


def _deinterleave_matrix():
    half = UP_TILE // 2
    r = lax.broadcasted_iota(jnp.int32, (UP_TILE, UP_TILE), 0)
    c = lax.broadcasted_iota(jnp.int32, (UP_TILE, UP_TILE), 1)
    src = jnp.where(c < half, 2 * c, 2 * (c - half) + 1)
    return jnp.where(r == src, 1.0, 0.0).astype(BF16)


GATHER_AHEAD = 2
GATHER_DMA_PRIORITY = 0
OTHER_DMA_PRIORITY = 1


def _moe_kernel(be_ref, nxt_ref, nval_ref, nu_ref, tok0_ref, tok1_ref, tok2_ref, asg_ref, hn_hbm,
                wu_hbm, bu_ref, wd_hbm, bd_ref, out_hbm, xbuf, xb_s, ystage, wu_f, wd_f, wu_s, wd_s,
                sems, ssems, wsems):
    i = pl.program_id(0)
    nu = nu_ref[0]
    half = UP_TILE // 2
    n_up = 2 * D_FF // UP_TILE
    slot = i % GATHER_AHEAD
    prev_slot = 1 - slot
    n_prev = nval_ref[i]

    def row_copy(tok_ref, r, s):
        return pltpu.make_async_copy(hn_hbm.at[pl.ds(tok_ref[0, 0, r], 1)],
                                     xbuf.at[s, pl.ds(r, 1)], sems.at[s])

    def weight_copies(e):
        return (pltpu.make_async_copy(wu_hbm.at[e], wu_f, wsems.at[0]),
                pltpu.make_async_copy(wd_hbm.at[e], wd_f, wsems.at[1]))

    def out_copy(r):
        return pltpu.make_async_copy(ystage.at[prev_slot, pl.ds(r, 1)],
                                     out_hbm.at[pl.ds(asg_ref[0, 0, r], 1)], ssems.at[prev_slot])

    @pl.when(i == 0)
    def _():
        for cp in weight_copies(be_ref[0]):
            cp.start(priority=OTHER_DMA_PRIORITY)

        def issue(r, c):
            row_copy(tok0_ref, r, 0).start(priority=GATHER_DMA_PRIORITY)
            row_copy(tok1_ref, r, 1).start(priority=GATHER_DMA_PRIORITY)
            return c

        lax.fori_loop(0, TM_MOE, issue, 0)

    @pl.when(i < nu + GATHER_AHEAD)
    def _():
        pltpu.make_async_copy(hn_hbm.at[pl.ds(0, TM_MOE)], xbuf.at[slot], sems.at[slot]).wait()

    @pl.when((i >= 2) & (i < nu + 2))
    def _():
        n_sent = nval_ref[i - 1]
        n_tiled = pl.multiple_of((n_sent // SUBLANES) * SUBLANES, SUBLANES)

        @pl.when(n_tiled > 0)
        def _():
            pltpu.make_async_copy(ystage.at[slot, pl.ds(0, n_tiled)],
                                  out_hbm.at[pl.ds(0, n_tiled)], ssems.at[slot]).wait()

        def wait_row(r, c):
            pltpu.make_async_copy(ystage.at[slot, pl.ds(0, 1)], out_hbm.at[pl.ds(0, 1)],
                                  ssems.at[slot]).wait()
            return c

        lax.fori_loop(0, n_sent - n_tiled, wait_row, 0)

    @pl.when(i < nu)
    def _():
        xb_s[...] = _load_row_tiled(xbuf.at[slot], TM_MOE).astype(BF16)

        @pl.when((i == 0) | (be_ref[i] != be_ref[jnp.maximum(i - 1, 0)]))
        def _():
            for cp in weight_copies(be_ref[i]):
                cp.wait()
            perm = _deinterleave_matrix()
            for c in range(n_up):
                cols = slice(c * UP_TILE, (c + 1) * UP_TILE)
                blk = wu_f[:, cols].astype(BF16)
                wu_s[:, cols] = jnp.dot(blk, perm, preferred_element_type=F32).astype(BF16)
            wd_s[...] = wd_f[...].astype(BF16)

            @pl.when(nxt_ref[i] >= 0)
            def _():
                for cp in weight_copies(nxt_ref[i]):
                    cp.start(priority=OTHER_DMA_PRIORITY)

    def send_some_rows():
        def send(r, c):
            out_copy(r).start(priority=OTHER_DMA_PRIORITY)
            return c

        lax.fori_loop(0, n_prev, send, 0)

    def run_block(send_all_rows):
        for r in range(TM_MOE):
            row_copy(tok2_ref, r, slot).start(priority=GATHER_DMA_PRIORITY)
            if send_all_rows:
                out_copy(r).start(priority=OTHER_DMA_PRIORITY)
        x = xb_s[...]
        z = jnp.dot(x, wu_s[...], preferred_element_type=F32) + bu_ref[0]
        zg = jnp.concatenate([z[:, c * UP_TILE:c * UP_TILE + half] for c in range(n_up)], axis=-1)
        zl = jnp.concatenate([z[:, c * UP_TILE + half:(c + 1) * UP_TILE] for c in range(n_up)],
                             axis=-1)
        xg = jnp.minimum(zg, SWIGLU_LIMIT)
        xl = jnp.clip(zl, -SWIGLU_LIMIT, SWIGLU_LIMIT)
        act = xg * jax.nn.sigmoid(SWIGLU_ALPHA * xg) * (xl + 1.0)
        y = jnp.dot(act.astype(BF16), wd_s[...], preferred_element_type=F32) + bd_ref[0]
        _store_row_tiled(y, ystage.at[slot])

    @pl.when((i < nu) & (n_prev == TM_MOE))
    def _():
        run_block(send_all_rows=True)

    @pl.when((i < nu) & (n_prev != TM_MOE))
    def _():
        send_some_rows()
        run_block(send_all_rows=False)

    @pl.when(i == nu)
    def _():
        send_some_rows()


def _moe(blk_expert, next_expert, n_valid, n_used, buf_tok, buf_asg, hn_all, w_up, b_up_perm,
         w_down, b_down, n_blk, n_assign):
    ew = lambda r, c: pl.BlockSpec((1, r, c), lambda i, be, nxt, nv, nu: (be[i], 0, 0))
    tok = lambda f: pl.BlockSpec((1, 1, TM_MOE), lambda i, be, nxt, nv, nu: (f(i), 0, 0),
                                 memory_space=pltpu.SMEM)
    hbm = pl.BlockSpec(memory_space=pl.ANY)
    return pl.pallas_call(
        _moe_kernel,
        grid_spec=pltpu.PrefetchScalarGridSpec(
            num_scalar_prefetch=4,
            grid=(n_blk,),
            in_specs=[
                tok(lambda i: i), tok(lambda i: jnp.minimum(i + 1, n_blk - 1)),
                tok(lambda i: jnp.minimum(i + GATHER_AHEAD, n_blk - 1)), tok(lambda i: i),
                hbm, hbm, ew(1, 2 * D_FF), hbm, ew(1, D_MODEL),
            ],
            out_specs=hbm,
            scratch_shapes=[pltpu.VMEM((GATHER_AHEAD, TM_MOE) + ROW_TILE, F32),
                            pltpu.VMEM((TM_MOE, D_MODEL), BF16),
                            pltpu.VMEM((2, TM_MOE) + ROW_TILE, F32),
                            pltpu.VMEM((D_MODEL, 2 * D_FF), F32),
                            pltpu.VMEM((D_FF, D_MODEL), F32),
                            pltpu.VMEM((D_MODEL, 2 * D_FF), BF16),
                            pltpu.VMEM((D_FF, D_MODEL), BF16),
                            pltpu.SemaphoreType.DMA((GATHER_AHEAD,)),
                            pltpu.SemaphoreType.DMA((2,)),
                            pltpu.SemaphoreType.DMA((2,))],
        ),
        out_shape=jax.ShapeDtypeStruct((n_assign,) + ROW_TILE, F32),
        compiler_params=_cparams(1),
        name="moe_experts",
    )(blk_expert, next_expert, n_valid, n_used, buf_tok, buf_tok, buf_tok, buf_asg, hn_all, w_up,
      b_up_perm, w_down, b_down)


def _final_kernel(x1_ref, tw_ref, nf_ref, *rest):
    y_refs, o_ref = rest[:TOP_K], rest[TOP_K]
    tw = tw_ref[...]
    x = x1_ref[...]
    for k in range(TOP_K):
        x = x + tw[:, k:k + 1] * _load_row_tiled(y_refs[k].at[0], TM_FIN)
    ms = jnp.mean(x * x, axis=-1, keepdims=True)
    o_ref[...] = x * lax.rsqrt(ms + EPS) * nf_ref[...]


def _final(x1, tw, nf, y_choice, row_block0):
    rows = x1.shape[0]
    row = lambda w: pl.BlockSpec((TM_FIN, w), lambda i: (i, 0))
    y_spec = lambda k: pl.BlockSpec((1, TM_FIN) + ROW_TILE, lambda i: (k, i + row_block0, 0, 0))
    return pl.pallas_call(
        _final_kernel,
        grid=(rows // TM_FIN,),
        in_specs=[row(D_MODEL), row(LANES), _full((1, D_MODEL))]
        + [y_spec(k) for k in range(TOP_K)],
        out_specs=row(D_MODEL),
        out_shape=jax.ShapeDtypeStruct((rows, D_MODEL), F32),
        compiler_params=_cparams(1),
        name="combine_final",
    )(x1, tw, nf, *([y_choice] * TOP_K))


def _block_diag_tiles(w):
    per = GATE_TILE // GATE_BW
    w4 = w.reshape(N_GATE_TILES, per, GATE_BW, GATE_BW)
    eye = jnp.eye(per, dtype=w.dtype)
    return jnp.einsum("gacd,ab->gacbd", w4, eye).reshape(N_GATE_TILES, GATE_TILE, GATE_TILE)


def _route(top_i, n_blk):
    a = top_i.size
    flat_e = top_i.reshape(-1)
    onehot = (flat_e[:, None] == jnp.arange(N_EXPERTS, dtype=jnp.int32)[None, :]).astype(jnp.int32)
    csum = jnp.cumsum(onehot, axis=0)
    rank = jnp.sum(csum * onehot, axis=1) - 1
    counts = csum[-1]
    padded = ((counts + (TM_MOE - 1)) // TM_MOE) * TM_MOE
    ends = jnp.cumsum(padded).astype(jnp.int32)
    start_padded = ends - padded
    dest = (start_padded[flat_e] + rank).astype(jnp.int32)
    n_tokens = a // TOP_K
    buf_n = jnp.zeros((n_blk * TM_MOE,), jnp.int32).at[dest].set(jnp.arange(a, dtype=jnp.int32))
    buf_tok = buf_n // TOP_K
    buf_asg = (buf_n % TOP_K) * n_tokens + buf_tok
    blk_start = jnp.arange(n_blk, dtype=jnp.int32) * TM_MOE
    blk_expert = jnp.minimum(
        jnp.sum((ends[None, :] <= blk_start[:, None]).astype(jnp.int32), axis=1), N_EXPERTS - 1)
    n_used = (ends[-1:] // TM_MOE).astype(jnp.int32)
    real_end = (start_padded + counts)[blk_expert]
    n_valid = jnp.clip(real_end - blk_start, 0, TM_MOE).astype(jnp.int32)
    n_valid = jnp.concatenate([jnp.zeros((1,), jnp.int32), n_valid])
    buf_asg = jnp.concatenate([jnp.zeros((TM_MOE,), jnp.int32), buf_asg])
    e_ids = jnp.arange(N_EXPERTS, dtype=jnp.int32)
    later_used = (e_ids[None, :] > e_ids[:, None]) & (counts[None, :] > 0)
    next_used = jnp.min(jnp.where(later_used, e_ids[None, :], N_EXPERTS), axis=1)
    next_used = jnp.where(next_used == N_EXPERTS, -1, next_used).astype(jnp.int32)
    return (buf_tok.reshape(n_blk, 1, TM_MOE), buf_asg.reshape(n_blk + 1, 1, TM_MOE),
            blk_expert.astype(jnp.int32), next_used[blk_expert], n_valid, n_used)


def kernel(x_prompt, x_sample, cache_k, cache_v, state_conv, state_h, meta_tokens, norm_mix, w_in, conv_w, conv_b, w_rg, b_rg, w_ig, b_ig, lru_lambda, attn_sinks, w_out, norm_ffn, w_router, b_router, w_up, b_up, w_down, b_down, norm_final):
    n_b, seq = x_prompt.shape[0], x_prompt.shape[1]
    n_s = x_sample.shape[0]
    w_buf = cache_k.shape[2]
    n_p = n_b * seq
    n_tok = n_p + n_s
    row1 = lambda v: v.reshape(1, -1)

    w_in_bf = w_in[0].astype(BF16)
    w_out_bf = w_out[0].astype(BF16)
    wg_tiles = jnp.concatenate([_block_diag_tiles(w_rg[0]), _block_diag_tiles(w_ig[0])],
                               axis=-1).astype(BF16)
    wr_pad = jnp.pad(w_router[0], ((0, 0), (0, LANES - N_EXPERTS))).astype(BF16)
    br_pad = jnp.pad(b_router[0], (0, LANES - N_EXPERTS), constant_values=NEG_BIG).reshape(1, LANES)
    b_up_perm = jnp.swapaxes(b_up[0].reshape(N_EXPERTS, 2 * D_FF // UP_TILE, UP_TILE // 2, 2),
                             2, 3).reshape(N_EXPERTS, 1, 2 * D_FF)
    b_dn = b_down[0].reshape(N_EXPERTS, 1, D_MODEL)
    nm, nf, nfin = row1(norm_mix[0]), row1(norm_ffn[0]), row1(norm_final)
    cw, cb = conv_w[0], row1(conv_b[0])
    brg, big, lam = row1(b_rg[0]), row1(b_ig[0]), row1(lru_lambda[0])
    sinks = attn_sinks[0]
    rnn_w = (cw, cb, wg_tiles, brg, big, lam)
    rnn_w_specs = [_full((CONV_W, D_RNN)), _full((1, D_RNN)),
                   _full((N_GATE_TILES, GATE_TILE, 2 * GATE_TILE)),
                   _full((1, D_RNN)), _full((1, D_RNN)), _full((1, D_RNN))]

    xp2 = x_prompt.reshape(n_p, D_MODEL)
    xr_p, q_p, kv_p, gr_p, ga_p = _in_proj(xp2, nm, w_in_bf, TM_IN)
    x_sm = jnp.concatenate([x_sample.reshape(n_s, D_MODEL), meta_tokens], axis=0)
    xr_sm, q_sm, kv_sm, gr_sm, ga_sm = _in_proj(x_sm, nm, w_in_bf, n_s + N_META)
    xr_s, q_s, kv_s, gr_s, ga_s = (t[:n_s] for t in (xr_sm, q_sm, kv_sm, gr_sm, ga_sm))
    xr_m, kv_m = xr_sm[n_s:], kv_sm[n_s:]

    h_meta = pl.pallas_call(
        _rnn_meta_kernel,
        in_specs=[_full((N_META, D_RNN))] + rnn_w_specs,
        out_specs=_full((1, D_RNN)),
        out_shape=jax.ShapeDtypeStruct((1, D_RNN), F32),
        grid=(1,),
        compiler_params=_cparams(1),
        name="rnn_meta",
    )(xr_m, *rnn_w)

    tt = TT_RNN
    blk3 = pl.BlockSpec((n_b, tt, D_RNN), lambda j: (0, j, 0))
    m_rnn_p, h_last_p = pl.pallas_call(
        functools.partial(_rnn_prompt_kernel, n_b=n_b, tt=tt),
        grid=(seq // tt,),
        in_specs=[blk3, blk3, _full((N_META, D_RNN)), _full((1, D_RNN))] + rnn_w_specs,
        out_specs=[blk3, _full((n_b, D_RNN))],
        out_shape=[jax.ShapeDtypeStruct((n_b, seq, D_RNN), BF16),
                   jax.ShapeDtypeStruct((n_b, D_RNN), F32)],
        scratch_shapes=[pltpu.VMEM((n_b, 8, D_RNN), F32), pltpu.VMEM((n_b, D_RNN), F32)],
        compiler_params=_cparams(1),
        name="rnn_prompt",
    )(xr_p.reshape(n_b, seq, D_RNN), gr_p.reshape(n_b, seq, D_RNN), xr_m, h_meta, *rnn_w)

    sc_t = jnp.swapaxes(state_conv[0], 0, 1)
    m_rnn_s, h_new_s = pl.pallas_call(
        _rnn_sample_kernel,
        grid=(1,),
        in_specs=[_full((n_s, D_RNN)), _full((n_s, D_RNN)), _full((CONV_W - 1, n_s, D_RNN)),
                  _full((n_s, D_RNN))] + rnn_w_specs,
        out_specs=[_full((n_s, D_RNN)), _full((n_s, D_RNN))],
        out_shape=[jax.ShapeDtypeStruct((n_s, D_RNN), BF16),
                   jax.ShapeDtypeStruct((n_s, D_RNN), F32)],
        compiler_params=_cparams(1),
        name="rnn_sample",
    )(xr_s, gr_s, sc_t, state_h[0], *rnn_w)

    n_blk_seq = seq // BLOCK
    kv_meta_blk = jnp.pad(kv_m, ((BLOCK - N_META, 0), (0, 0)))
    smem_spec = pl.BlockSpec(memory_space=pltpu.SMEM)
    rb = lambda w: pl.BlockSpec((BLOCK, w), lambda b, j: (b * n_blk_seq + j, 0))
    m_attn_p = pl.pallas_call(
        _attn_prompt_kernel,
        grid=(n_b, n_blk_seq),
        in_specs=[smem_spec, rb(D_ATTN), rb(2 * D_KV),
                  pl.BlockSpec((BLOCK, 2 * D_KV),
                               lambda b, j: (jnp.maximum(b * n_blk_seq + j - 1, 0), 0)),
                  _full((BLOCK, 2 * D_KV)), rb(D_ATTN)],
        out_specs=rb(D_ATTN),
        out_shape=jax.ShapeDtypeStruct((n_p, D_ATTN), BF16),
        compiler_params=_cparams(2),
        name="attn_prompt",
    )(sinks, q_p, kv_p, kv_p, kv_meta_blk, ga_p)

    ck = cache_k[0].reshape(n_s, w_buf, D_KV)
    cv = cache_v[0].reshape(n_s, w_buf, D_KV)
    per_s = lambda a, b: pl.BlockSpec((SAMPLES_PER_STEP, a, b), lambda i: (i, 0, 0))
    m_attn_s = pl.pallas_call(
        _attn_sample_kernel,
        grid=(n_s // SAMPLES_PER_STEP,),
        in_specs=[smem_spec, per_s(N_HEADS, HEAD_DIM), per_s(1, 2 * D_KV), per_s(w_buf, D_KV),
                  per_s(w_buf, D_KV), per_s(N_HEADS, HEAD_DIM)],
        out_specs=per_s(N_HEADS, HEAD_DIM),
        out_shape=jax.ShapeDtypeStruct((n_s, N_HEADS, HEAD_DIM), BF16),
        compiler_params=_cparams(1),
        name="attn_sample",
    )(sinks, q_s.reshape(n_s, N_HEADS, HEAD_DIM), kv_s.reshape(n_s, 1, 2 * D_KV), ck, cv,
      ga_s.reshape(n_s, N_HEADS, HEAD_DIM)).reshape(n_s, D_ATTN)

    x1_p, hn_all, ti_p, tw_p = _merge(m_rnn_p.reshape(n_p, D_RNN), m_attn_p, xp2, w_out_bf, nf,
                                      wr_pad, br_pad, jnp.zeros((n_tok,) + ROW_TILE, F32), n_tok, 0,
                                      TM_MERGE)
    x1_s, hn_all, ti_s, tw_s = _merge(m_rnn_s, m_attn_s, x_sample.reshape(n_s, D_MODEL), w_out_bf,
                                      nf, wr_pad, br_pad, hn_all, n_tok, n_p // n_s, n_s)

    top_i = jnp.concatenate([ti_p[:, :TOP_K], ti_s[:, :TOP_K]], axis=0)
    n_assign = n_tok * TOP_K
    n_blk = n_assign // TM_MOE + N_EXPERTS - 1 + GATHER_AHEAD
    buf_tok, buf_asg, blk_expert, next_expert, n_valid, n_used = _route(top_i, n_blk)

    y_tok = _moe(blk_expert, next_expert, n_valid, n_used, buf_tok, buf_asg, hn_all, w_up[0],
                 b_up_perm, w_down[0], b_dn, n_blk, n_assign).reshape((TOP_K, n_tok) + ROW_TILE)
    y_p = _final(x1_p, tw_p, nfin, y_tok, 0)
    y_s = _final(x1_s, tw_s, nfin, y_tok, n_p // TM_FIN)

    kv_p3 = kv_p.reshape(n_b, seq, 2 * D_KV)
    w_p = min(WINDOW, seq + N_META)
    new_k_p = kv_p3[:, seq - w_p:, :D_KV].reshape(1, n_b, w_p, N_KV, HEAD_DIM)
    new_v_p = kv_p3[:, seq - w_p:, D_KV:].reshape(1, n_b, w_p, N_KV, HEAD_DIM)
    new_conv_p = xr_p.reshape(n_b, seq, D_RNN)[:, seq - (CONV_W - 1):][None]
    k_new = kv_s[:, :D_KV].reshape(n_s, 1, N_KV, HEAD_DIM)
    v_new = kv_s[:, D_KV:].reshape(n_s, 1, N_KV, HEAD_DIM)
    new_k_s = jnp.concatenate([cache_k[0], k_new], axis=1)[:, -w_buf:][None]
    new_v_s = jnp.concatenate([cache_v[0], v_new], axis=1)[:, -w_buf:][None]
    new_conv_s = jnp.concatenate([state_conv[0], xr_s[:, None, :]], axis=1)[:, -(CONV_W - 1):][None]
    return (y_p.reshape(n_b, seq, D_MODEL), y_s.reshape(n_s, 1, D_MODEL), new_k_p, new_v_p,
            new_conv_p, h_last_p[None], new_k_s, new_v_s, new_conv_s, h_new_s[None])
```

```python
import functools

import jax
import jax.numpy as jnp
from jax import lax
from jax.experimental import pallas as pl
from jax.experimental.pallas import tpu as pltpu

F32 = jnp.float32
BF16 = jnp.bfloat16

D_MODEL = 1024
N_META = 16
D_RNN = 1024
N_GATE_BLOCKS = 16
GATE_BW = D_RNN // N_GATE_BLOCKS
CONV_W = 4
LRU_C = 8.0
N_HEADS = 16
HEAD_DIM = 64
N_KV = 2
GROUP = N_HEADS // N_KV
D_ATTN = N_HEADS * HEAD_DIM
D_KV = N_KV * HEAD_DIM
WINDOW = 128
BLOCK = 128
PAST_LEN = 16384
N_EXPERTS = 32
TOP_K = 4
D_FF = 1024
SWIGLU_LIMIT = 7.0
SWIGLU_ALPHA = 1.702
EPS = 1e-6

GATE_TILE = 256
N_GATE_TILES = D_RNN // GATE_TILE
LANES = 128
NEG_BIG = -1e30

TM_IN = 256
TT_RNN = 64
TM_MERGE = 256
TM_MOE = 256
TM_FIN = 128
VMEM_LIMIT = 56 * 1024 * 1024


def _cparams(n_grid_dims):
    return pltpu.CompilerParams(
        dimension_semantics=("arbitrary",) * n_grid_dims, vmem_limit_bytes=VMEM_LIMIT)


def _full(shape):
    return pl.BlockSpec(shape, lambda *_: (0,) * len(shape))


SUBLANES = 8
ROW_TILE = (SUBLANES, D_MODEL // SUBLANES)


def _transpose_vreg_group(v):
    sub = lax.broadcasted_iota(jnp.int32, v[0].shape, 0)
    v = list(v)
    d = SUBLANES // 2
    while d >= 1:
        hi = (sub & d) != 0
        nv = list(v)
        for a in range(SUBLANES):
            if a & d:
                continue
            b = a | d
            nv[a] = jnp.where(hi, pltpu.roll(v[b], d, axis=0), v[a])
            nv[b] = jnp.where(hi, v[b], pltpu.roll(v[a], SUBLANES - d, axis=0))
        v = nv
        d //= 2
    return v


def _store_row_tiled(x, dst):
    for g in range(x.shape[0] // SUBLANES):
        rows = slice(g * SUBLANES, (g + 1) * SUBLANES)
        w = _transpose_vreg_group([x[rows, s * LANES:(s + 1) * LANES] for s in range(SUBLANES)])
        for j in range(SUBLANES):
            dst[g * SUBLANES + j] = w[j]


def _load_row_tiled(src, n_rows):
    groups = []
    for g in range(n_rows // SUBLANES):
        v = _transpose_vreg_group([src[g * SUBLANES + j] for j in range(SUBLANES)])
        groups.append(jnp.concatenate(v, axis=-1))
    return jnp.concatenate(groups, axis=0)


def _in_proj_kernel(x_ref, g_ref, w_ref, xr_ref, q_ref, kv_ref, gr_ref, ga_ref):
    x = x_ref[...]
    ms = jnp.mean(x * x, axis=-1, keepdims=True)
    h = (x * lax.rsqrt(ms + EPS) * g_ref[...]).astype(BF16)

    def proj(lo, hi):
        return jnp.dot(h, w_ref[:, lo:hi], preferred_element_type=F32)

    o_q = D_RNN
    o_k = o_q + D_ATTN
    o_gr = o_k + 2 * D_KV
    o_ga = o_gr + D_RNN
    xr_ref[...] = proj(0, o_q)
    q_ref[...] = (proj(o_q, o_k) * (HEAD_DIM ** -0.5)).astype(BF16)
    kv_ref[...] = proj(o_k, o_gr)
    gr_ref[...] = proj(o_gr, o_ga).astype(BF16)
    ga_ref[...] = proj(o_ga, o_ga + D_ATTN).astype(BF16)


def _in_proj(x, g, w_in_bf, tm):
    rows = x.shape[0]
    d_in = w_in_bf.shape[1]
    row = lambda w: pl.BlockSpec((tm, w), lambda i: (i, 0))
    return pl.pallas_call(
        _in_proj_kernel,
        grid=(rows // tm,),
        in_specs=[row(D_MODEL), _full((1, D_MODEL)), _full((D_MODEL, d_in))],
        out_specs=[row(D_RNN), row(D_ATTN), row(2 * D_KV), row(D_RNN), row(D_ATTN)],
        out_shape=[
            jax.ShapeDtypeStruct((rows, D_RNN), F32),
            jax.ShapeDtypeStruct((rows, D_ATTN), BF16),
            jax.ShapeDtypeStruct((rows, 2 * D_KV), F32),
            jax.ShapeDtypeStruct((rows, D_RNN), BF16),
            jax.ShapeDtypeStruct((rows, D_ATTN), BF16),
        ],
        compiler_params=_cparams(1),
        name="in_proj",
    )(x, g, w_in_bf)


def _softplus(x):
    return jnp.maximum(x, 0.0) + jnp.log1p(jnp.exp(-jnp.abs(x)))


def _lru_coeffs(u, wg_ref, brg, big, lam):
    ub = u.astype(BF16)
    sp = _softplus(-lam)
    a_parts, i_parts, m_parts = [], [], []
    for g in range(N_GATE_TILES):
        sl = slice(g * GATE_TILE, (g + 1) * GATE_TILE)
        zz = jnp.dot(ub[:, sl], wg_ref[g], preferred_element_type=F32)
        r = jax.nn.sigmoid(zz[:, :GATE_TILE] + brg[:, sl])
        i = jax.nn.sigmoid(zz[:, GATE_TILE:] + big[:, sl])
        log_a = (-LRU_C) * r * sp[:, sl]
        a = jnp.exp(log_a)
        m = jnp.sqrt(1.0 - a * a)
        a_parts.append(a)
        i_parts.append(i)
        m_parts.append(m)
    cat = lambda ps: jnp.concatenate(ps, axis=-1)
    return cat(a_parts), cat(i_parts), cat(m_parts)


def _conv(ext, t, cw, cb):
    out = cb
    for j in range(CONV_W):
        s = CONV_W - 1 - j
        out = out + cw[j:j + 1, :] * ext[8 - s:8 - s + t, :]
    return out


def _rnn_meta_kernel(xr_ref, cw_ref, cb_ref, wg_ref, brg_ref, big_ref, lam_ref, h_ref):
    x = xr_ref[...]
    ext = jnp.concatenate([jnp.zeros((8, D_RNN), F32), x], axis=0)
    u = _conv(ext, N_META, cw_ref[...], cb_ref[...])
    a, i, m = _lru_coeffs(u, wg_ref, brg_ref[...], big_ref[...], lam_ref[...])
    first = lax.broadcasted_iota(jnp.int32, (N_META, 1), 0) == 0
    b = jnp.where(first, 1.0, m) * i * u
    h = jnp.zeros((1, D_RNN), F32)
    for t in range(N_META):
        h = a[t:t + 1, :] * h + b[t:t + 1, :]
    h_ref[...] = h


def _linear_scan(a, b, h_prev):
    t, d = a.shape
    g = t // SUBLANES
    a3 = a.reshape(g, SUBLANES, d)
    b3 = b.reshape(g, SUBLANES, d)
    row = lax.broadcasted_iota(jnp.int32, (g, SUBLANES, d), 1)
    step = 1
    while step < SUBLANES:
        keep = row >= step
        a_up = jnp.where(keep, pltpu.roll(a3, step, axis=1), 1.0)
        b_up = jnp.where(keep, pltpu.roll(b3, step, axis=1), 0.0)
        b3 = a3 * b_up + b3
        a3 = a3 * a_up
        step *= 2
    hs = []
    h = h_prev
    for k in range(g):
        hk = a3[k] * h + b3[k]
        hs.append(hk)
        h = hk[SUBLANES - 1:SUBLANES, :]
    return jnp.concatenate(hs, axis=0), h


def _rnn_prompt_kernel(xr_ref, gr_ref, xm_ref, h0_ref, cw_ref, cb_ref, wg_ref, brg_ref, big_ref,
                       lam_ref, m_ref, hl_ref, halo_s, h_s, *, n_b, tt):
    j = pl.program_id(0)

    @pl.when(j == 0)
    def _():
        h_s[...] = jnp.broadcast_to(h0_ref[...], (n_b, D_RNN))
        for b in range(n_b):
            halo_s[b] = xm_ref[N_META - 8:N_META, :]

    cw = cw_ref[...]
    cb = cb_ref[...]

    def per_batch(b, c):
        x = xr_ref[b]
        ext = jnp.concatenate([halo_s[b], x], axis=0)
        halo_s[b] = x[tt - 8:tt, :]
        u = _conv(ext, tt, cw, cb)
        a, i, m = _lru_coeffs(u, wg_ref, brg_ref[...], big_ref[...], lam_ref[...])
        y, h = _linear_scan(a, m * i * u, h_s[pl.ds(b, 1), :])
        h_s[pl.ds(b, 1), :] = h
        m_ref[b] = (jax.nn.sigmoid(gr_ref[b].astype(F32)) * y).astype(BF16)
        return c

    lax.fori_loop(0, n_b, per_batch, 0)
    hl_ref[...] = h_s[...]


def _rnn_sample_kernel(xr_ref, gr_ref, sc_ref, h0_ref, cw_ref, cb_ref, wg_ref, brg_ref, big_ref,
                       lam_ref, m_ref, hn_ref):
    cw = cw_ref[...]
    u = cb_ref[...] + cw[CONV_W - 1:CONV_W, :] * xr_ref[...]
    for j in range(CONV_W - 1):
        u = u + cw[j:j + 1, :] * sc_ref[j]
    a, i, m = _lru_coeffs(u, wg_ref, brg_ref[...], big_ref[...], lam_ref[...])
    h = a * h0_ref[...] + m * i * u
    hn_ref[...] = h
    m_ref[...] = (jax.nn.sigmoid(gr_ref[...].astype(F32)) * h).astype(BF16)


def _alibi_slope(h):
    return 2.0 ** (-8.0 * (h + 1) / N_HEADS)


def _attn_bias_table():
    assert WINDOW == BLOCK
    qi = jnp.arange(BLOCK, dtype=jnp.int32)[:, None]
    ci = jnp.arange(BLOCK, dtype=jnp.int32)[None, :]
    own = ci <= qi
    dist = jnp.where(own, qi - ci, qi + BLOCK - ci).astype(F32)
    slopes = jnp.asarray([_alibi_slope(h) for h in range(N_HEADS)], F32)[:, None, None]
    ali = -slopes * dist[None]
    tables = [jnp.where((own | (ci >= c_min))[None], ali, NEG_BIG)
              for c_min in (BLOCK - N_META, 0)]
    return jnp.stack(tables)


def _attn_prompt_kernel(sink_ref, q_ref, kc_ref, kp_ref, km_ref, bias_ref, ga_ref, o_ref):
    first = pl.program_id(1) == 0
    kv_prev = jnp.where(first, km_ref[...], kp_ref[...])
    kv = jnp.concatenate([kv_prev, kc_ref[...]], axis=0).astype(BF16)
    q = q_ref[...]
    qi = lax.broadcasted_iota(jnp.int32, (BLOCK, BLOCK), 0)
    ci = lax.broadcasted_iota(jnp.int32, (BLOCK, BLOCK), 1)
    own = ci <= qi
    zero = jnp.zeros((BLOCK, BLOCK), BF16)
    outs = []
    for h in range(N_HEADS):
        g = h // GROUP
        kh = kv[:, g * HEAD_DIM:(g + 1) * HEAD_DIM]
        vh = kv[:, D_KV + g * HEAD_DIM:D_KV + (g + 1) * HEAD_DIM]
        qh = q[:, h * HEAD_DIM:(h + 1) * HEAD_DIM]
        s2 = lax.dot_general(qh, kh, (((1,), (1,)), ((), ())), preferred_element_type=F32)
        s = jnp.where(own, s2[:, BLOCK:], s2[:, :BLOCK]) + bias_ref[0, h]
        sink = sink_ref[h]
        mx = jnp.maximum(jnp.max(s, axis=-1, keepdims=True), sink)
        p = jnp.exp(s - mx)
        den = jnp.sum(p, axis=-1, keepdims=True) + jnp.exp(sink - mx)
        pb = p.astype(BF16)
        p2 = jnp.concatenate([jnp.where(own, zero, pb), jnp.where(own, pb, zero)], axis=1)
        o = jnp.dot(p2, vh, preferred_element_type=F32)
        outs.append(o / den)
    y = jnp.concatenate(outs, axis=-1)
    o_ref[...] = (jax.nn.sigmoid(ga_ref[...].astype(F32)) * y).astype(BF16)


SAMPLES_PER_STEP = 8


def _attn_sample_kernel(sink_ref, q_ref, kvn_ref, ck_ref, cv_ref, ga_ref, o_ref):
    w_buf = ck_ref.shape[1]
    ci = lax.broadcasted_iota(jnp.int32, (GROUP, w_buf), 1)
    dist = w_buf - ci
    valid = dist < WINDOW
    distf = dist.astype(F32)
    hrow = lax.broadcasted_iota(jnp.int32, (GROUP, 1), 0)
    bias, sinks = [], []
    for g in range(N_KV):
        slope = jnp.zeros((GROUP, 1), F32)
        sink = jnp.zeros((GROUP, 1), F32)
        for r in range(GROUP):
            slope = jnp.where(hrow == r, _alibi_slope(g * GROUP + r), slope)
            sink = jnp.where(hrow == r, sink_ref[g * GROUP + r], sink)
        bias.append(slope * distf)
        sinks.append(sink)
    for n in range(SAMPLES_PER_STEP):
        q = q_ref[n]
        ck = ck_ref[n].astype(BF16)
        cv = cv_ref[n].astype(BF16)
        kvn = kvn_ref[n].astype(BF16).astype(F32)
        outs = []
        for g in range(N_KV):
            qg = q[g * GROUP:(g + 1) * GROUP, :]
            kg = ck[:, g * HEAD_DIM:(g + 1) * HEAD_DIM]
            vg = cv[:, g * HEAD_DIM:(g + 1) * HEAD_DIM]
            kn = kvn[:, g * HEAD_DIM:(g + 1) * HEAD_DIM]
            vn = kvn[:, D_KV + g * HEAD_DIM:D_KV + (g + 1) * HEAD_DIM]
            s = lax.dot_general(qg, kg, (((1,), (1,)), ((), ())), preferred_element_type=F32)
            s = jnp.where(valid, s - bias[g], NEG_BIG)
            sn = jnp.sum(qg.astype(F32) * kn, axis=-1, keepdims=True)
            mx = jnp.maximum(jnp.maximum(jnp.max(s, axis=-1, keepdims=True), sn), sinks[g])
            p = jnp.exp(s - mx)
            pn = jnp.exp(sn - mx)
            den = jnp.sum(p, axis=-1, keepdims=True) + pn + jnp.exp(sinks[g] - mx)
            o = jnp.dot(p.astype(BF16), vg, preferred_element_type=F32)
            o = o + pn.astype(BF16).astype(F32) * vn
            outs.append(o / den)
        y = jnp.concatenate(outs, axis=0)
        o_ref[n] = (jax.nn.sigmoid(ga_ref[n].astype(F32)) * y).astype(BF16)


def _merge_kernel(mr_ref, ma_ref, x_ref, wo_ref, nf_ref, wr_ref, br_ref, *rest):
    x1_ref, hn_ref, ti_ref, tw_ref = rest[-4:]
    mm = jnp.dot(mr_ref[...], wo_ref[:D_RNN, :], preferred_element_type=F32)
    mm = mm + jnp.dot(ma_ref[...], wo_ref[D_RNN:, :], preferred_element_type=F32)
    x1 = x_ref[...] + mm
    x1_ref[...] = x1
    ms = jnp.mean(x1 * x1, axis=-1, keepdims=True)
    hn = x1 * lax.rsqrt(ms + EPS) * nf_ref[...]
    _store_row_tiled(hn, hn_ref)
    logits = jnp.dot(hn.astype(BF16), wr_ref[...], preferred_element_type=F32) + br_ref[...]
    lane = lax.broadcasted_iota(jnp.int32, logits.shape, 1)
    ti = jnp.zeros(logits.shape, jnp.int32)
    tv = jnp.zeros(logits.shape, F32)
    l = logits
    v0 = None
    for k in range(TOP_K):
        mx = jnp.max(l, axis=-1, keepdims=True)
        idx = jnp.min(jnp.where(l == mx, lane, LANES), axis=-1, keepdims=True)
        if k == 0:
            v0 = mx
        ti = jnp.where(lane == k, idx, ti)
        tv = jnp.where(lane == k, jnp.exp(mx - v0), tv)
        l = jnp.where(lane == idx, NEG_BIG * 2, l)
    ti_ref[...] = ti
    tw_ref[...] = tv / jnp.sum(tv, axis=-1, keepdims=True)


def _merge(m_rnn, m_attn, x, w_out_bf, nf, wr_pad, br_pad, hn_all, n_all, row_block0, tm):
    rows = x.shape[0]
    row = lambda w: pl.BlockSpec((tm, w), lambda i: (i, 0))
    in_specs = [row(D_RNN), row(D_ATTN), row(D_MODEL), _full((D_RNN + D_ATTN, D_MODEL)),
                _full((1, D_MODEL)), _full((D_MODEL, LANES)), _full((1, LANES))]
    args = [m_rnn, m_attn, x, w_out_bf, nf, wr_pad, br_pad]
    aliases = {}
    if hn_all is not None:
        in_specs.append(pl.BlockSpec(memory_space=pl.ANY))
        args.append(hn_all)
        aliases = {len(args) - 1: 1}
    return pl.pallas_call(
        _merge_kernel,
        grid=(rows // tm,),
        in_specs=in_specs,
        out_specs=[row(D_MODEL),
                   pl.BlockSpec((tm,) + ROW_TILE, lambda i: (i + row_block0, 0, 0)),
                   row(LANES), row(LANES)],
        out_shape=[
            jax.ShapeDtypeStruct((rows, D_MODEL), F32),
            jax.ShapeDtypeStruct((n_all,) + ROW_TILE, F32),
            jax.ShapeDtypeStruct((rows, LANES), jnp.int32),
            jax.ShapeDtypeStruct((rows, LANES), F32),
        ],
        input_output_aliases=aliases,
        compiler_params=_cparams(1),
        name="merge_router",
    )(*args)


UP_TILE = 256


def _deinterleave_matrix():
    half = UP_TILE // 2
    r = lax.broadcasted_iota(jnp.int32, (UP_TILE, UP_TILE), 0)
    c = lax.broadcasted_iota(jnp.int32, (UP_TILE, UP_TILE), 1)
    src = jnp.where(c < half, 2 * c, 2 * (c - half) + 1)
    return jnp.where(r == src, 1.0, 0.0).astype(BF16)


GATHER_AHEAD = 2
GATHER_DMA_PRIORITY = 0
OTHER_DMA_PRIORITY = 1


def _moe_kernel(be_ref, nxt_ref, nval_ref, nu_ref, tok0_ref, tok1_ref, tok2_ref, asg_ref, hn_hbm,
                wu_hbm, bu_ref, wd_hbm, bd_ref, out_hbm, xbuf, xb_s, ystage, wu_f, wd_f, wu_s, wd_s,
                sems, ssems, wsems):
    i = pl.program_id(0)
    nu = nu_ref[0]
    half = UP_TILE // 2
    n_up = 2 * D_FF // UP_TILE
    slot = i % GATHER_AHEAD
    prev_slot = 1 - slot
    n_prev = nval_ref[i]

    def row_copy(tok_ref, r, s):
        return pltpu.make_async_copy(hn_hbm.at[pl.ds(tok_ref[0, 0, r], 1)],
                                     xbuf.at[s, pl.ds(r, 1)], sems.at[s])

    def weight_copies(e):
        return (pltpu.make_async_copy(wu_hbm.at[e], wu_f, wsems.at[0]),
                pltpu.make_async_copy(wd_hbm.at[e], wd_f, wsems.at[1]))

    def out_copy(r):
        return pltpu.make_async_copy(ystage.at[prev_slot, pl.ds(r, 1)],
                                     out_hbm.at[pl.ds(asg_ref[0, 0, r], 1)], ssems.at[prev_slot])

    @pl.when(i == 0)
    def _():
        for cp in weight_copies(be_ref[0]):
            cp.start(priority=OTHER_DMA_PRIORITY)

        def issue(r, c):
            row_copy(tok0_ref, r, 0).start(priority=GATHER_DMA_PRIORITY)
            row_copy(tok1_ref, r, 1).start(priority=GATHER_DMA_PRIORITY)
            return c

        lax.fori_loop(0, TM_MOE, issue, 0)

    @pl.when(i < nu + GATHER_AHEAD)
    def _():
        pltpu.make_async_copy(hn_hbm.at[pl.ds(0, TM_MOE)], xbuf.at[slot], sems.at[slot]).wait()

    @pl.when((i >= 2) & (i < nu + 2))
    def _():
        n_sent = nval_ref[i - 1]
        n_tiled = pl.multiple_of((n_sent // SUBLANES) * SUBLANES, SUBLANES)

        @pl.when(n_tiled > 0)
        def _():
            pltpu.make_async_copy(ystage.at[slot, pl.ds(0, n_tiled)],
                                  out_hbm.at[pl.ds(0, n_tiled)], ssems.at[slot]).wait()

        def wait_row(r, c):
            pltpu.make_async_copy(ystage.at[slot, pl.ds(0, 1)], out_hbm.at[pl.ds(0, 1)],
                                  ssems.at[slot]).wait()
            return c

        lax.fori_loop(0, n_sent - n_tiled, wait_row, 0)

    @pl.when(i < nu)
    def _():
        xb_s[...] = _load_row_tiled(xbuf.at[slot], TM_MOE).astype(BF16)

        @pl.when((i == 0) | (be_ref[i] != be_ref[jnp.maximum(i - 1, 0)]))
        def _():
            for cp in weight_copies(be_ref[i]):
                cp.wait()
            perm = _deinterleave_matrix()
            for c in range(n_up):
                cols = slice(c * UP_TILE, (c + 1) * UP_TILE)
                blk = wu_f[:, cols].astype(BF16)
                wu_s[:, cols] = jnp.dot(blk, perm, preferred_element_type=F32).astype(BF16)
            wd_s[...] = wd_f[...].astype(BF16)

            @pl.when(nxt_ref[i] >= 0)
            def _():
                for cp in weight_copies(nxt_ref[i]):
                    cp.start(priority=OTHER_DMA_PRIORITY)

    def send_some_rows():
        def send(r, c):
            out_copy(r).start(priority=OTHER_DMA_PRIORITY)
            return c

        lax.fori_loop(0, n_prev, send, 0)

    def run_block(send_all_rows):
        for r in range(TM_MOE):
            row_copy(tok2_ref, r, slot).start(priority=GATHER_DMA_PRIORITY)
            if send_all_rows:
                out_copy(r).start(priority=OTHER_DMA_PRIORITY)
        x = xb_s[...]
        z = jnp.dot(x, wu_s[...], preferred_element_type=F32) + bu_ref[0]
        zg = jnp.concatenate([z[:, c * UP_TILE:c * UP_TILE + half] for c in range(n_up)], axis=-1)
        zl = jnp.concatenate([z[:, c * UP_TILE + half:(c + 1) * UP_TILE] for c in range(n_up)],
                             axis=-1)
        xg = jnp.minimum(zg, SWIGLU_LIMIT)
        xl = jnp.clip(zl, -SWIGLU_LIMIT, SWIGLU_LIMIT)
        act = xg * jax.nn.sigmoid(SWIGLU_ALPHA * xg) * (xl + 1.0)
        y = jnp.dot(act.astype(BF16), wd_s[...], preferred_element_type=F32) + bd_ref[0]
        _store_row_tiled(y, ystage.at[slot])

    @pl.when((i < nu) & (n_prev == TM_MOE))
    def _():
        run_block(send_all_rows=True)

    @pl.when((i < nu) & (n_prev != TM_MOE))
    def _():
        send_some_rows()
        run_block(send_all_rows=False)

    @pl.when(i == nu)
    def _():
        send_some_rows()


def _moe(blk_expert, next_expert, n_valid, n_used, buf_tok, buf_asg, hn_all, w_up, b_up_perm,
         w_down, b_down, n_blk, n_assign):
    ew = lambda r, c: pl.BlockSpec((1, r, c), lambda i, be, nxt, nv, nu: (be[i], 0, 0))
    tok = lambda f: pl.BlockSpec((1, 1, TM_MOE), lambda i, be, nxt, nv, nu: (f(i), 0, 0),
                                 memory_space=pltpu.SMEM)
    hbm = pl.BlockSpec(memory_space=pl.ANY)
    return pl.pallas_call(
        _moe_kernel,
        grid_spec=pltpu.PrefetchScalarGridSpec(
            num_scalar_prefetch=4,
            grid=(n_blk,),
            in_specs=[
                tok(lambda i: i), tok(lambda i: jnp.minimum(i + 1, n_blk - 1)),
                tok(lambda i: jnp.minimum(i + GATHER_AHEAD, n_blk - 1)), tok(lambda i: i),
                hbm, hbm, ew(1, 2 * D_FF), hbm, ew(1, D_MODEL),
            ],
            out_specs=hbm,
            scratch_shapes=[pltpu.VMEM((GATHER_AHEAD, TM_MOE) + ROW_TILE, F32),
                            pltpu.VMEM((TM_MOE, D_MODEL), BF16),
                            pltpu.VMEM((2, TM_MOE) + ROW_TILE, F32),
                            pltpu.VMEM((D_MODEL, 2 * D_FF), F32),
                            pltpu.VMEM((D_FF, D_MODEL), F32),
                            pltpu.VMEM((D_MODEL, 2 * D_FF), BF16),
                            pltpu.VMEM((D_FF, D_MODEL), BF16),
                            pltpu.SemaphoreType.DMA((GATHER_AHEAD,)),
                            pltpu.SemaphoreType.DMA((2,)),
                            pltpu.SemaphoreType.DMA((2,))],
        ),
        out_shape=jax.ShapeDtypeStruct((n_assign,) + ROW_TILE, F32),
        compiler_params=_cparams(1),
        name="moe_experts",
    )(blk_expert, next_expert, n_valid, n_used, buf_tok, buf_tok, buf_tok, buf_asg, hn_all, w_up,
      b_up_perm, w_down, b_down)


def _final_kernel(x1_ref, tw_ref, nf_ref, *rest):
    y_refs, o_ref = rest[:TOP_K], rest[TOP_K]
    tw = tw_ref[...]
    x = x1_ref[...]
    for k in range(TOP_K):
        x = x + tw[:, k:k + 1] * _load_row_tiled(y_refs[k].at[0], TM_FIN)
    ms = jnp.mean(x * x, axis=-1, keepdims=True)
    o_ref[...] = x * lax.rsqrt(ms + EPS) * nf_ref[...]


def _final(x1, tw, nf, y_choice, row_block0):
    rows = x1.shape[0]
    row = lambda w: pl.BlockSpec((TM_FIN, w), lambda i: (i, 0))
    y_spec = lambda k: pl.BlockSpec((1, TM_FIN) + ROW_TILE, lambda i: (k, i + row_block0, 0, 0))
    return pl.pallas_call(
        _final_kernel,
        grid=(rows // TM_FIN,),
        in_specs=[row(D_MODEL), row(LANES), _full((1, D_MODEL))]
        + [y_spec(k) for k in range(TOP_K)],
        out_specs=row(D_MODEL),
        out_shape=jax.ShapeDtypeStruct((rows, D_MODEL), F32),
        compiler_params=_cparams(1),
        name="combine_final",
    )(x1, tw, nf, *([y_choice] * TOP_K))


def _block_diag_tiles(w):
    per = GATE_TILE // GATE_BW
    w4 = w.reshape(N_GATE_TILES, per, GATE_BW, GATE_BW)
    eye = jnp.eye(per, dtype=w.dtype)
    return jnp.einsum("gacd,ab->gacbd", w4, eye).reshape(N_GATE_TILES, GATE_TILE, GATE_TILE)


def _route(top_i, n_blk):
    a = top_i.size
    flat_e = top_i.reshape(-1)
    shift = (a - 1).bit_length()
    keys = (flat_e << shift) | jnp.arange(a, dtype=jnp.int32)
    sorted_n = jnp.sort(keys) & ((1 << shift) - 1)
    onehot = (flat_e[:, None] == jnp.arange(N_EXPERTS, dtype=jnp.int32)[None, :]).astype(jnp.int32)
    counts = jnp.sum(onehot, axis=0)
    padded = ((counts + (TM_MOE - 1)) // TM_MOE) * TM_MOE
    ends = jnp.cumsum(padded).astype(jnp.int32)
    start_padded = ends - padded
    start_sorted = (jnp.cumsum(counts) - counts).astype(jnp.int32)
    blk_start = jnp.arange(n_blk, dtype=jnp.int32) * TM_MOE
    blk_expert = jnp.minimum(
        jnp.sum((ends[None, :] <= blk_start[:, None]).astype(jnp.int32), axis=1), N_EXPERTS - 1)
    n_used = (ends[-1:] // TM_MOE).astype(jnp.int32)
    real_end = (start_padded + counts)[blk_expert]
    n_valid = jnp.clip(real_end - blk_start, 0, TM_MOE).astype(jnp.int32)
    n_tokens = a // TOP_K
    row_shift = jnp.repeat((start_padded - start_sorted)[blk_expert], TM_MOE)
    pos = jnp.clip(jnp.arange(n_blk * TM_MOE, dtype=jnp.int32) - row_shift, 0, a - 1)
    buf_n = sorted_n[pos]
    buf_tok = buf_n // TOP_K
    buf_asg = (buf_n % TOP_K) * n_tokens + buf_tok
    n_valid = jnp.concatenate([jnp.zeros((1,), jnp.int32), n_valid])
    buf_asg = jnp.concatenate([jnp.zeros((TM_MOE,), jnp.int32), buf_asg])
    e_ids = jnp.arange(N_EXPERTS, dtype=jnp.int32)
    later_used = (e_ids[None, :] > e_ids[:, None]) & (counts[None, :] > 0)
    next_used = jnp.min(jnp.where(later_used, e_ids[None, :], N_EXPERTS), axis=1)
    next_used = jnp.where(next_used == N_EXPERTS, -1, next_used).astype(jnp.int32)
    return (buf_tok.reshape(n_blk, 1, TM_MOE), buf_asg.reshape(n_blk + 1, 1, TM_MOE),
            blk_expert.astype(jnp.int32), next_used[blk_expert], n_valid, n_used)


def kernel(x_prompt, x_sample, cache_k, cache_v, state_conv, state_h, meta_tokens, norm_mix, w_in, conv_w, conv_b, w_rg, b_rg, w_ig, b_ig, lru_lambda, attn_sinks, w_out, norm_ffn, w_router, b_router, w_up, b_up, w_down, b_down, norm_final):
    n_b, seq = x_prompt.shape[0], x_prompt.shape[1]
    n_s = x_sample.shape[0]
    w_buf = cache_k.shape[2]
    n_p = n_b * seq
    n_tok = n_p + n_s
    row1 = lambda v: v.reshape(1, -1)

    w_in_bf = w_in[0].astype(BF16)
    w_out_bf = w_out[0].astype(BF16)
    wg_tiles = jnp.concatenate([_block_diag_tiles(w_rg[0]), _block_diag_tiles(w_ig[0])],
                               axis=-1).astype(BF16)
    wr_pad = jnp.pad(w_router[0], ((0, 0), (0, LANES - N_EXPERTS))).astype(BF16)
    br_pad = jnp.pad(b_router[0], (0, LANES - N_EXPERTS), constant_values=NEG_BIG).reshape(1, LANES)
    b_up_perm = jnp.swapaxes(b_up[0].reshape(N_EXPERTS, 2 * D_FF // UP_TILE, UP_TILE // 2, 2),
                             2, 3).reshape(N_EXPERTS, 1, 2 * D_FF)
    b_dn = b_down[0].reshape(N_EXPERTS, 1, D_MODEL)
    nm, nf, nfin = row1(norm_mix[0]), row1(norm_ffn[0]), row1(norm_final)
    cw, cb = conv_w[0], row1(conv_b[0])
    brg, big, lam = row1(b_rg[0]), row1(b_ig[0]), row1(lru_lambda[0])
    sinks = attn_sinks[0]
    rnn_w = (cw, cb, wg_tiles, brg, big, lam)
    rnn_w_specs = [_full((CONV_W, D_RNN)), _full((1, D_RNN)),
                   _full((N_GATE_TILES, GATE_TILE, 2 * GATE_TILE)),
                   _full((1, D_RNN)), _full((1, D_RNN)), _full((1, D_RNN))]

    xp2 = x_prompt.reshape(n_p, D_MODEL)
    xr_p, q_p, kv_p, gr_p, ga_p = _in_proj(xp2, nm, w_in_bf, TM_IN)
    x_sm = jnp.concatenate([x_sample.reshape(n_s, D_MODEL), meta_tokens], axis=0)
    xr_sm, q_sm, kv_sm, gr_sm, ga_sm = _in_proj(x_sm, nm, w_in_bf, n_s + N_META)
    xr_s, q_s, kv_s, gr_s, ga_s = (t[:n_s] for t in (xr_sm, q_sm, kv_sm, gr_sm, ga_sm))
    xr_m, kv_m = xr_sm[n_s:], kv_sm[n_s:]

    h_meta = pl.pallas_call(
        _rnn_meta_kernel,
        in_specs=[_full((N_META, D_RNN))] + rnn_w_specs,
        out_specs=_full((1, D_RNN)),
        out_shape=jax.ShapeDtypeStruct((1, D_RNN), F32),
        grid=(1,),
        compiler_params=_cparams(1),
        name="rnn_meta",
    )(xr_m, *rnn_w)

    tt = TT_RNN
    blk3 = pl.BlockSpec((n_b, tt, D_RNN), lambda j: (0, j, 0))
    m_rnn_p, h_last_p = pl.pallas_call(
        functools.partial(_rnn_prompt_kernel, n_b=n_b, tt=tt),
        grid=(seq // tt,),
        in_specs=[blk3, blk3, _full((N_META, D_RNN)), _full((1, D_RNN))] + rnn_w_specs,
        out_specs=[blk3, _full((n_b, D_RNN))],
        out_shape=[jax.ShapeDtypeStruct((n_b, seq, D_RNN), BF16),
                   jax.ShapeDtypeStruct((n_b, D_RNN), F32)],
        scratch_shapes=[pltpu.VMEM((n_b, 8, D_RNN), F32), pltpu.VMEM((n_b, D_RNN), F32)],
        compiler_params=_cparams(1),
        name="rnn_prompt",
    )(xr_p.reshape(n_b, seq, D_RNN), gr_p.reshape(n_b, seq, D_RNN), xr_m, h_meta, *rnn_w)

    sc_t = jnp.swapaxes(state_conv[0], 0, 1)
    m_rnn_s, h_new_s = pl.pallas_call(
        _rnn_sample_kernel,
        grid=(1,),
        in_specs=[_full((n_s, D_RNN)), _full((n_s, D_RNN)), _full((CONV_W - 1, n_s, D_RNN)),
                  _full((n_s, D_RNN))] + rnn_w_specs,
        out_specs=[_full((n_s, D_RNN)), _full((n_s, D_RNN))],
        out_shape=[jax.ShapeDtypeStruct((n_s, D_RNN), BF16),
                   jax.ShapeDtypeStruct((n_s, D_RNN), F32)],
        compiler_params=_cparams(1),
        name="rnn_sample",
    )(xr_s, gr_s, sc_t, state_h[0], *rnn_w)

    n_blk_seq = seq // BLOCK
    kv_meta_blk = jnp.pad(kv_m, ((BLOCK - N_META, 0), (0, 0)))
    smem_spec = pl.BlockSpec(memory_space=pltpu.SMEM)
    rb = lambda w: pl.BlockSpec((BLOCK, w), lambda b, j: (b * n_blk_seq + j, 0))
    m_attn_p = pl.pallas_call(
        _attn_prompt_kernel,
        grid=(n_b, n_blk_seq),
        in_specs=[smem_spec, rb(D_ATTN), rb(2 * D_KV),
                  pl.BlockSpec((BLOCK, 2 * D_KV),
                               lambda b, j: (jnp.maximum(b * n_blk_seq + j - 1, 0), 0)),
                  _full((BLOCK, 2 * D_KV)),
                  pl.BlockSpec((1, N_HEADS, BLOCK, BLOCK),
                               lambda b, j: (jnp.minimum(j, 1), 0, 0, 0)),
                  rb(D_ATTN)],
        out_specs=rb(D_ATTN),
        out_shape=jax.ShapeDtypeStruct((n_p, D_ATTN), BF16),
        compiler_params=_cparams(2),
        name="attn_prompt",
    )(sinks, q_p, kv_p, kv_p, kv_meta_blk, _attn_bias_table(), ga_p)

    ck = cache_k[0].reshape(n_s, w_buf, D_KV)
    cv = cache_v[0].reshape(n_s, w_buf, D_KV)
    per_s = lambda a, b: pl.BlockSpec((SAMPLES_PER_STEP, a, b), lambda i: (i, 0, 0))
    m_attn_s = pl.pallas_call(
        _attn_sample_kernel,
        grid=(n_s // SAMPLES_PER_STEP,),
        in_specs=[smem_spec, per_s(N_HEADS, HEAD_DIM), per_s(1, 2 * D_KV), per_s(w_buf, D_KV),
                  per_s(w_buf, D_KV), per_s(N_HEADS, HEAD_DIM)],
        out_specs=per_s(N_HEADS, HEAD_DIM),
        out_shape=jax.ShapeDtypeStruct((n_s, N_HEADS, HEAD_DIM), BF16),
        compiler_params=_cparams(1),
        name="attn_sample",
    )(sinks, q_s.reshape(n_s, N_HEADS, HEAD_DIM), kv_s.reshape(n_s, 1, 2 * D_KV), ck, cv,
      ga_s.reshape(n_s, N_HEADS, HEAD_DIM)).reshape(n_s, D_ATTN)

    x1_p, hn_all, ti_p, tw_p = _merge(m_rnn_p.reshape(n_p, D_RNN), m_attn_p, xp2, w_out_bf, nf,
                                      wr_pad, br_pad, jnp.zeros((n_tok,) + ROW_TILE, F32), n_tok, 0,
                                      TM_MERGE)
    x1_s, hn_all, ti_s, tw_s = _merge(m_rnn_s, m_attn_s, x_sample.reshape(n_s, D_MODEL), w_out_bf,
                                      nf, wr_pad, br_pad, hn_all, n_tok, n_p // n_s, n_s)

    top_i = jnp.concatenate([ti_p[:, :TOP_K], ti_s[:, :TOP_K]], axis=0)
    n_assign = n_tok * TOP_K
    n_blk = n_assign // TM_MOE + N_EXPERTS - 1 + GATHER_AHEAD
    buf_tok, buf_asg, blk_expert, next_expert, n_valid, n_used = _route(top_i, n_blk)

    y_tok = _moe(blk_expert, next_expert, n_valid, n_used, buf_tok, buf_asg, hn_all, w_up[0],
                 b_up_perm, w_down[0], b_dn, n_blk, n_assign).reshape((TOP_K, n_tok) + ROW_TILE)
    y_p = _final(x1_p, tw_p, nfin, y_tok, 0)
    y_s = _final(x1_s, tw_s, nfin, y_tok, n_p // TM_FIN)

    kv_p3 = kv_p.reshape(n_b, seq, 2 * D_KV)
    w_p = min(WINDOW, seq + N_META)
    new_k_p = kv_p3[:, seq - w_p:, :D_KV].reshape(1, n_b, w_p, N_KV, HEAD_DIM)
    new_v_p = kv_p3[:, seq - w_p:, D_KV:].reshape(1, n_b, w_p, N_KV, HEAD_DIM)
    new_conv_p = xr_p.reshape(n_b, seq, D_RNN)[:, seq - (CONV_W - 1):][None]
    k_new = kv_s[:, :D_KV].reshape(n_s, 1, N_KV, HEAD_DIM)
    v_new = kv_s[:, D_KV:].reshape(n_s, 1, N_KV, HEAD_DIM)
    new_k_s = jnp.concatenate([cache_k[0], k_new], axis=1)[:, -w_buf:][None]
    new_v_s = jnp.concatenate([cache_v[0], v_new], axis=1)[:, -w_buf:][None]
    new_conv_s = jnp.concatenate([state_conv[0], xr_s[:, None, :]], axis=1)[:, -(CONV_W - 1):][None]
    return (y_p.reshape(n_b, seq, D_MODEL), y_s.reshape(n_s, 1, D_MODEL), new_k_p, new_v_p,
            new_conv_p, h_last_p[None], new_k_s, new_v_s, new_conv_s, h_new_s[None])
```

```python
import functools

import jax
import jax.numpy as jnp
import numpy as np
from jax import lax
from jax.experimental import pallas as pl
from jax.experimental.pallas import tpu as pltpu

F32 = jnp.float32
BF16 = jnp.bfloat16

D_MODEL = 1024
N_META = 16
D_RNN = 1024
N_GATE_BLOCKS = 16
GATE_BW = D_RNN // N_GATE_BLOCKS
CONV_W = 4
LRU_C = 8.0
N_HEADS = 16
HEAD_DIM = 64
N_KV = 2
GROUP = N_HEADS // N_KV
D_ATTN = N_HEADS * HEAD_DIM
D_KV = N_KV * HEAD_DIM
WINDOW = 128
BLOCK = 128
PAST_LEN = 16384
N_EXPERTS = 32
TOP_K = 4
D_FF = 1024
SWIGLU_LIMIT = 7.0
SWIGLU_ALPHA = 1.702
EPS = 1e-6

GATE_TILE = 256
N_GATE_TILES = D_RNN // GATE_TILE
LANES = 128
NEG_BIG = -1e30

TM_IN = 512
TT_RNN = 64
TM_MERGE = 256
TM_MOE = 256
TM_FIN = 256
VMEM_LIMIT = 56 * 1024 * 1024


def _cparams(n_grid_dims):
    return pltpu.CompilerParams(
        dimension_semantics=("arbitrary",) * n_grid_dims, vmem_limit_bytes=VMEM_LIMIT)


def _full(shape):
    return pl.BlockSpec(shape, lambda *_: (0,) * len(shape))


SUBLANES = 8
ROW_TILE = (SUBLANES, D_MODEL // SUBLANES)


def _transpose_vreg_group(v):
    sub = lax.broadcasted_iota(jnp.int32, v[0].shape, 0)
    v = list(v)
    d = SUBLANES // 2
    while d >= 1:
        hi = (sub & d) != 0
        nv = list(v)
        for a in range(SUBLANES):
            if a & d:
                continue
            b = a | d
            nv[a] = jnp.where(hi, pltpu.roll(v[b], d, axis=0), v[a])
            nv[b] = jnp.where(hi, v[b], pltpu.roll(v[a], SUBLANES - d, axis=0))
        v = nv
        d //= 2
    return v


def _store_row_tiled(x, dst):
    for g in range(x.shape[0] // SUBLANES):
        rows = slice(g * SUBLANES, (g + 1) * SUBLANES)
        w = _transpose_vreg_group([x[rows, s * LANES:(s + 1) * LANES] for s in range(SUBLANES)])
        for j in range(SUBLANES):
            dst[g * SUBLANES + j] = w[j]


def _load_row_tiled(src, n_rows):
    groups = []
    for g in range(n_rows // SUBLANES):
        v = _transpose_vreg_group([src[g * SUBLANES + j] for j in range(SUBLANES)])
        groups.append(jnp.concatenate(v, axis=-1))
    return jnp.concatenate(groups, axis=0)


def _in_proj_kernel(x_ref, g_ref, w_ref, xr_ref, q_ref, kv_ref, gr_ref, ga_ref):
    x = x_ref[...]
    ms = jnp.mean(x * x, axis=-1, keepdims=True)
    h = (x * lax.rsqrt(ms + EPS) * g_ref[...]).astype(BF16)

    def proj(lo, hi):
        return jnp.dot(h, w_ref[:, lo:hi], preferred_element_type=F32)

    o_q = D_RNN
    o_k = o_q + D_ATTN
    o_gr = o_k + 2 * D_KV
    o_ga = o_gr + D_RNN
    xr_ref[...] = proj(0, o_q)
    q_ref[...] = (proj(o_q, o_k) * (HEAD_DIM ** -0.5)).astype(BF16)
    kv_ref[...] = proj(o_k, o_gr)
    gr_ref[...] = proj(o_gr, o_ga).astype(BF16)
    ga_ref[...] = proj(o_ga, o_ga + D_ATTN).astype(BF16)


def _in_proj(x, g, w_in_bf, tm):
    rows = x.shape[0]
    d_in = w_in_bf.shape[1]
    row = lambda w: pl.BlockSpec((tm, w), lambda i: (i, 0))
    return pl.pallas_call(
        _in_proj_kernel,
        grid=(rows // tm,),
        in_specs=[row(D_MODEL), _full((1, D_MODEL)), _full((D_MODEL, d_in))],
        out_specs=[row(D_RNN), row(D_ATTN), row(2 * D_KV), row(D_RNN), row(D_ATTN)],
        out_shape=[
            jax.ShapeDtypeStruct((rows, D_RNN), F32),
            jax.ShapeDtypeStruct((rows, D_ATTN), BF16),
            jax.ShapeDtypeStruct((rows, 2 * D_KV), F32),
            jax.ShapeDtypeStruct((rows, D_RNN), BF16),
            jax.ShapeDtypeStruct((rows, D_ATTN), BF16),
        ],
        compiler_params=_cparams(1),
        name="in_proj",
    )(x, g, w_in_bf)


def _softplus(x):
    return jnp.maximum(x, 0.0) + jnp.log1p(jnp.exp(-jnp.abs(x)))


def _lru_coeffs(u, wg_ref, brg, big, lam):
    ub = u.astype(BF16)
    sp = _softplus(-lam)
    a_parts, i_parts, m_parts = [], [], []
    for g in range(N_GATE_TILES):
        sl = slice(g * GATE_TILE, (g + 1) * GATE_TILE)
        zz = jnp.dot(ub[:, sl], wg_ref[g], preferred_element_type=F32)
        r = jax.nn.sigmoid(zz[:, :GATE_TILE] + brg[:, sl])
        i = jax.nn.sigmoid(zz[:, GATE_TILE:] + big[:, sl])
        log_a = (-LRU_C) * r * sp[:, sl]
        a = jnp.exp(log_a)
        m = jnp.sqrt(1.0 - a * a)
        a_parts.append(a)
        i_parts.append(i)
        m_parts.append(m)
    cat = lambda ps: jnp.concatenate(ps, axis=-1)
    return cat(a_parts), cat(i_parts), cat(m_parts)


def _conv(ext, t, cw, cb):
    out = cb
    for j in range(CONV_W):
        s = CONV_W - 1 - j
        out = out + cw[j:j + 1, :] * ext[8 - s:8 - s + t, :]
    return out


def _rnn_meta_kernel(xr_ref, cw_ref, cb_ref, wg_ref, brg_ref, big_ref, lam_ref, h_ref):
    x = xr_ref[...]
    ext = jnp.concatenate([jnp.zeros((8, D_RNN), F32), x], axis=0)
    u = _conv(ext, N_META, cw_ref[...], cb_ref[...])
    a, i, m = _lru_coeffs(u, wg_ref, brg_ref[...], big_ref[...], lam_ref[...])
    first = lax.broadcasted_iota(jnp.int32, (N_META, 1), 0) == 0
    b = jnp.where(first, 1.0, m) * i * u
    h = jnp.zeros((1, D_RNN), F32)
    for t in range(N_META):
        h = a[t:t + 1, :] * h + b[t:t + 1, :]
    h_ref[...] = h


def _linear_scan(a, b, h_prev):
    t, d = a.shape
    g = t // SUBLANES
    a3 = a.reshape(g, SUBLANES, d)
    b3 = b.reshape(g, SUBLANES, d)
    row = lax.broadcasted_iota(jnp.int32, (g, SUBLANES, d), 1)
    step = 1
    while step < SUBLANES:
        keep = row >= step
        a_up = jnp.where(keep, pltpu.roll(a3, step, axis=1), 1.0)
        b_up = jnp.where(keep, pltpu.roll(b3, step, axis=1), 0.0)
        b3 = a3 * b_up + b3
        a3 = a3 * a_up
        step *= 2
    hs = []
    h = h_prev
    for k in range(g):
        hk = a3[k] * h + b3[k]
        hs.append(hk)
        h = hk[SUBLANES - 1:SUBLANES, :]
    return jnp.concatenate(hs, axis=0), h


def _rnn_prompt_kernel(xr_ref, gr_ref, xm_ref, h0_ref, cw_ref, cb_ref, wg_ref, brg_ref, big_ref,
                       lam_ref, m_ref, hl_ref, halo_s, h_s, *, n_b, tt):
    j = pl.program_id(0)

    @pl.when(j == 0)
    def _():
        h_s[...] = jnp.broadcast_to(h0_ref[...], (n_b, D_RNN))
        for b in range(n_b):
            halo_s[b] = xm_ref[N_META - 8:N_META, :]

    cw = cw_ref[...]
    cb = cb_ref[...]

    def per_batch(b, c):
        x = xr_ref[b]
        ext = jnp.concatenate([halo_s[b], x], axis=0)
        halo_s[b] = x[tt - 8:tt, :]
        u = _conv(ext, tt, cw, cb)
        a, i, m = _lru_coeffs(u, wg_ref, brg_ref[...], big_ref[...], lam_ref[...])
        y, h = _linear_scan(a, m * i * u, h_s[pl.ds(b, 1), :])
        h_s[pl.ds(b, 1), :] = h
        m_ref[b] = (jax.nn.sigmoid(gr_ref[b].astype(F32)) * y).astype(BF16)
        return c

    lax.fori_loop(0, n_b, per_batch, 0)
    hl_ref[...] = h_s[...]


def _rnn_sample_kernel(xr_ref, gr_ref, sc_ref, h0_ref, cw_ref, cb_ref, wg_ref, brg_ref, big_ref,
                       lam_ref, m_ref, hn_ref):
    cw = cw_ref[...]
    u = cb_ref[...] + cw[CONV_W - 1:CONV_W, :] * xr_ref[...]
    for j in range(CONV_W - 1):
        u = u + cw[j:j + 1, :] * sc_ref[j]
    a, i, m = _lru_coeffs(u, wg_ref, brg_ref[...], big_ref[...], lam_ref[...])
    h = a * h0_ref[...] + m * i * u
    hn_ref[...] = h
    m_ref[...] = (jax.nn.sigmoid(gr_ref[...].astype(F32)) * h).astype(BF16)


def _alibi_slope(h):
    return 2.0 ** (-8.0 * (h + 1) / N_HEADS)


def _attn_bias_table():
    assert WINDOW == BLOCK
    qi = np.arange(BLOCK, dtype=np.int32)[:, None]
    ci = np.arange(BLOCK, dtype=np.int32)[None, :]
    own = ci <= qi
    dist = np.where(own, qi - ci, qi + BLOCK - ci).astype(np.float32)
    slopes = np.asarray([_alibi_slope(h) for h in range(N_HEADS)], np.float32)[:, None, None]
    ali = -slopes * dist[None]
    tables = [np.where((own | (ci >= c_min))[None], ali, np.float32(NEG_BIG))
              for c_min in (BLOCK - N_META, 0)]
    return jnp.asarray(np.stack(tables).astype(np.float32))


def _attn_prompt_kernel(sink_ref, q_ref, kc_ref, kp_ref, km_ref, bias_ref, ga_ref, o_ref):
    first = pl.program_id(1) == 0
    kv_prev = jnp.where(first, km_ref[...], kp_ref[...])
    kv = jnp.concatenate([kv_prev, kc_ref[...]], axis=0).astype(BF16)
    q = q_ref[...]
    qi = lax.broadcasted_iota(jnp.int32, (BLOCK, BLOCK), 0)
    ci = lax.broadcasted_iota(jnp.int32, (BLOCK, BLOCK), 1)
    own = ci <= qi
    zero = jnp.zeros((BLOCK, BLOCK), BF16)
    outs = []
    for h in range(N_HEADS):
        g = h // GROUP
        kh = kv[:, g * HEAD_DIM:(g + 1) * HEAD_DIM]
        vh = kv[:, D_KV + g * HEAD_DIM:D_KV + (g + 1) * HEAD_DIM]
        qh = q[:, h * HEAD_DIM:(h + 1) * HEAD_DIM]
        s2 = lax.dot_general(qh, kh, (((1,), (1,)), ((), ())), preferred_element_type=F32)
        s = jnp.where(own, s2[:, BLOCK:], s2[:, :BLOCK]) + bias_ref[0, h]
        sink = sink_ref[h]
        mx = jnp.maximum(jnp.max(s, axis=-1, keepdims=True), sink)
        p = jnp.exp(s - mx)
        den = jnp.sum(p, axis=-1, keepdims=True) + jnp.exp(sink - mx)
        pb = p.astype(BF16)
        p2 = jnp.concatenate([jnp.where(own, zero, pb), jnp.where(own, pb, zero)], axis=1)
        o = jnp.dot(p2, vh, preferred_element_type=F32)
        outs.append(o / den)
    y = jnp.concatenate(outs, axis=-1)
    o_ref[...] = (jax.nn.sigmoid(ga_ref[...].astype(F32)) * y).astype(BF16)


SAMPLES_PER_STEP = 8


def _attn_sample_kernel(sink_ref, q_ref, kvn_ref, ck_ref, cv_ref, ga_ref, o_ref):
    w_buf = ck_ref.shape[1]
    ci = lax.broadcasted_iota(jnp.int32, (GROUP, w_buf), 1)
    dist = w_buf - ci
    valid = dist < WINDOW
    distf = dist.astype(F32)
    hrow = lax.broadcasted_iota(jnp.int32, (GROUP, 1), 0)
    bias, sinks = [], []
    for g in range(N_KV):
        slope = jnp.zeros((GROUP, 1), F32)
        sink = jnp.zeros((GROUP, 1), F32)
        for r in range(GROUP):
            slope = jnp.where(hrow == r, _alibi_slope(g * GROUP + r), slope)
            sink = jnp.where(hrow == r, sink_ref[g * GROUP + r], sink)
        bias.append(slope * distf)
        sinks.append(sink)
    for n in range(SAMPLES_PER_STEP):
        q = q_ref[n]
        ck = ck_ref[n].astype(BF16)
        cv = cv_ref[n].astype(BF16)
        kvn = kvn_ref[n].astype(BF16).astype(F32)
        outs = []
        for g in range(N_KV):
            qg = q[g * GROUP:(g + 1) * GROUP, :]
            kg = ck[:, g * HEAD_DIM:(g + 1) * HEAD_DIM]
            vg = cv[:, g * HEAD_DIM:(g + 1) * HEAD_DIM]
            kn = kvn[:, g * HEAD_DIM:(g + 1) * HEAD_DIM]
            vn = kvn[:, D_KV + g * HEAD_DIM:D_KV + (g + 1) * HEAD_DIM]
            s = lax.dot_general(qg, kg, (((1,), (1,)), ((), ())), preferred_element_type=F32)
            s = jnp.where(valid, s - bias[g], NEG_BIG)
            sn = jnp.sum(qg.astype(F32) * kn, axis=-1, keepdims=True)
            mx = jnp.maximum(jnp.maximum(jnp.max(s, axis=-1, keepdims=True), sn), sinks[g])
            p = jnp.exp(s - mx)
            pn = jnp.exp(sn - mx)
            den = jnp.sum(p, axis=-1, keepdims=True) + pn + jnp.exp(sinks[g] - mx)
            o = jnp.dot(p.astype(BF16), vg, preferred_element_type=F32)
            o = o + pn.astype(BF16).astype(F32) * vn
            outs.append(o / den)
        y = jnp.concatenate(outs, axis=0)
        o_ref[n] = (jax.nn.sigmoid(ga_ref[n].astype(F32)) * y).astype(BF16)


def _merge_kernel(mr_ref, ma_ref, x_ref, wo_ref, nf_ref, wr_ref, br_ref, *rest, n_steps):
    x1_ref, hn_ref, ti_ref, tw_ref = rest[-4:]

    @pl.when(pl.program_id(0) >= n_steps)
    def _():
        hn_ref[...] = jnp.zeros_like(hn_ref)

    @pl.when(pl.program_id(0) < n_steps)
    def _():
        _merge_tile(mr_ref, ma_ref, x_ref, wo_ref, nf_ref, wr_ref, br_ref, x1_ref, hn_ref, ti_ref,
                    tw_ref)


def _merge_tile(mr_ref, ma_ref, x_ref, wo_ref, nf_ref, wr_ref, br_ref, x1_ref, hn_ref, ti_ref,
                tw_ref):
    mm = jnp.dot(mr_ref[...], wo_ref[:D_RNN, :], preferred_element_type=F32)
    mm = mm + jnp.dot(ma_ref[...], wo_ref[D_RNN:, :], preferred_element_type=F32)
    x1 = x_ref[...] + mm
    x1_ref[...] = x1
    ms = jnp.mean(x1 * x1, axis=-1, keepdims=True)
    hn = x1 * lax.rsqrt(ms + EPS) * nf_ref[...]
    _store_row_tiled(hn, hn_ref)
    logits = jnp.dot(hn.astype(BF16), wr_ref[...], preferred_element_type=F32) + br_ref[...]
    lane = lax.broadcasted_iota(jnp.int32, logits.shape, 1)
    ti = jnp.zeros(logits.shape, jnp.int32)
    tv = jnp.zeros(logits.shape, F32)
    l = logits
    v0 = None
    for k in range(TOP_K):
        mx = jnp.max(l, axis=-1, keepdims=True)
        idx = jnp.min(jnp.where(l == mx, lane, LANES), axis=-1, keepdims=True)
        if k == 0:
            v0 = mx
        ti = jnp.where(lane == k, idx, ti)
        tv = jnp.where(lane == k, jnp.exp(mx - v0), tv)
        l = jnp.where(lane == idx, NEG_BIG * 2, l)
    ti_ref[...] = ti
    tw_ref[...] = tv / jnp.sum(tv, axis=-1, keepdims=True)


def _merge(m_rnn, m_attn, x, w_out_bf, nf, wr_pad, br_pad, hn_all, n_all, row_block0, tm):
    rows = x.shape[0]
    n_steps = rows // tm
    n_zero_steps = 0 if hn_all is not None else n_all // tm - n_steps
    row = lambda w: pl.BlockSpec((tm, w), lambda i: (jnp.minimum(i, n_steps - 1), 0))
    in_specs = [row(D_RNN), row(D_ATTN), row(D_MODEL), _full((D_RNN + D_ATTN, D_MODEL)),
                _full((1, D_MODEL)), _full((D_MODEL, LANES)), _full((1, LANES))]
    args = [m_rnn, m_attn, x, w_out_bf, nf, wr_pad, br_pad]
    aliases = {}
    if hn_all is not None:
        in_specs.append(pl.BlockSpec(memory_space=pl.ANY))
        args.append(hn_all)
        aliases = {len(args) - 1: 1}
    return pl.pallas_call(
        functools.partial(_merge_kernel, n_steps=n_steps),
        grid=(n_steps + n_zero_steps,),
        in_specs=in_specs,
        out_specs=[row(D_MODEL),
                   pl.BlockSpec((tm,) + ROW_TILE, lambda i: (i + row_block0, 0, 0)),
                   row(LANES), row(LANES)],
        out_shape=[
            jax.ShapeDtypeStruct((rows, D_MODEL), F32),
            jax.ShapeDtypeStruct((n_all,) + ROW_TILE, F32),
            jax.ShapeDtypeStruct((rows, LANES), jnp.int32),
            jax.ShapeDtypeStruct((rows, LANES), F32),
        ],
        input_output_aliases=aliases,
        compiler_params=_cparams(1),
        name="merge_router",
    )(*args)


UP_TILE = 256


def _deinterleave_matrix():
    half = UP_TILE // 2
    r = lax.broadcasted_iota(jnp.int32, (UP_TILE, UP_TILE), 0)
    c = lax.broadcasted_iota(jnp.int32, (UP_TILE, UP_TILE), 1)
    src = jnp.where(c < half, 2 * c, 2 * (c - half) + 1)
    return jnp.where(r == src, 1.0, 0.0).astype(BF16)


GATHER_AHEAD = 2
GATHER_DMA_PRIORITY = 0
OTHER_DMA_PRIORITY = 1


def _moe_kernel(be_ref, nxt_ref, nval_ref, nu_ref, tok0_ref, tok1_ref, tok2_ref, asg_ref, hn_hbm,
                wu_hbm, bu_ref, wd_hbm, bd_ref, out_hbm, xbuf, xb_s, ystage, wu_f, wd_f, wu_s, wd_s,
                sems, ssems, wsems):
    i = pl.program_id(0)
    nu = nu_ref[0]
    half = UP_TILE // 2
    n_up = 2 * D_FF // UP_TILE
    slot = i % GATHER_AHEAD
    prev_slot = 1 - slot
    n_prev = nval_ref[i]

    def row_copy(tok_ref, r, s):
        return pltpu.make_async_copy(hn_hbm.at[pl.ds(tok_ref[0, 0, r], 1)],
                                     xbuf.at[s, pl.ds(r, 1)], sems.at[s])

    def weight_copies(e):
        return (pltpu.make_async_copy(wu_hbm.at[e], wu_f, wsems.at[0]),
                pltpu.make_async_copy(wd_hbm.at[e], wd_f, wsems.at[1]))

    def out_copy(r):
        return pltpu.make_async_copy(ystage.at[prev_slot, pl.ds(r, 1)],
                                     out_hbm.at[pl.ds(asg_ref[0, 0, r], 1)], ssems.at[prev_slot])

    @pl.when(i == 0)
    def _():
        for cp in weight_copies(be_ref[0]):
            cp.start(priority=OTHER_DMA_PRIORITY)

        def issue(r, c):
            row_copy(tok0_ref, r, 0).start(priority=GATHER_DMA_PRIORITY)
            row_copy(tok1_ref, r, 1).start(priority=GATHER_DMA_PRIORITY)
            return c

        lax.fori_loop(0, TM_MOE, issue, 0)

    @pl.when(i < nu + GATHER_AHEAD)
    def _():
        pltpu.make_async_copy(hn_hbm.at[pl.ds(0, TM_MOE)], xbuf.at[slot], sems.at[slot]).wait()

    @pl.when((i >= 2) & (i < nu + 2))
    def _():
        n_sent = nval_ref[i - 1]
        n_tiled = pl.multiple_of((n_sent // SUBLANES) * SUBLANES, SUBLANES)

        @pl.when(n_tiled > 0)
        def _():
            pltpu.make_async_copy(ystage.at[slot, pl.ds(0, n_tiled)],
                                  out_hbm.at[pl.ds(0, n_tiled)], ssems.at[slot]).wait()

        def wait_row(r, c):
            pltpu.make_async_copy(ystage.at[slot, pl.ds(0, 1)], out_hbm.at[pl.ds(0, 1)],
                                  ssems.at[slot]).wait()
            return c

        lax.fori_loop(0, n_sent - n_tiled, wait_row, 0)

    @pl.when(i < nu)
    def _():
        xb_s[...] = _load_row_tiled(xbuf.at[slot], TM_MOE).astype(BF16)

        @pl.when((i == 0) | (be_ref[i] != be_ref[jnp.maximum(i - 1, 0)]))
        def _():
            for cp in weight_copies(be_ref[i]):
                cp.wait()
            perm = _deinterleave_matrix()
            for c in range(n_up):
                cols = slice(c * UP_TILE, (c + 1) * UP_TILE)
                blk = wu_f[:, cols].astype(BF16)
                wu_s[:, cols] = jnp.dot(blk, perm, preferred_element_type=F32).astype(BF16)
            wd_s[...] = wd_f[...].astype(BF16)

            @pl.when(nxt_ref[i] >= 0)
            def _():
                for cp in weight_copies(nxt_ref[i]):
                    cp.start(priority=OTHER_DMA_PRIORITY)

    def send_some_rows():
        def send(r, c):
            out_copy(r).start(priority=OTHER_DMA_PRIORITY)
            return c

        lax.fori_loop(0, n_prev, send, 0)

    def run_block(send_all_rows):
        for r in range(TM_MOE):
            row_copy(tok2_ref, r, slot).start(priority=GATHER_DMA_PRIORITY)
            if send_all_rows:
                out_copy(r).start(priority=OTHER_DMA_PRIORITY)
        x = xb_s[...]
        z = jnp.dot(x, wu_s[...], preferred_element_type=F32) + bu_ref[0]
        zg = jnp.concatenate([z[:, c * UP_TILE:c * UP_TILE + half] for c in range(n_up)], axis=-1)
        zl = jnp.concatenate([z[:, c * UP_TILE + half:(c + 1) * UP_TILE] for c in range(n_up)],
                             axis=-1)
        xg = jnp.minimum(zg, SWIGLU_LIMIT)
        xl = jnp.clip(zl, -SWIGLU_LIMIT, SWIGLU_LIMIT)
        act = xg * jax.nn.sigmoid(SWIGLU_ALPHA * xg) * (xl + 1.0)
        y = jnp.dot(act.astype(BF16), wd_s[...], preferred_element_type=F32) + bd_ref[0]
        _store_row_tiled(y, ystage.at[slot])

    @pl.when((i < nu) & (n_prev == TM_MOE))
    def _():
        run_block(send_all_rows=True)

    @pl.when((i < nu) & (n_prev != TM_MOE))
    def _():
        send_some_rows()
        run_block(send_all_rows=False)

    @pl.when(i == nu)
    def _():
        send_some_rows()


def _moe(blk_expert, next_expert, n_valid, n_used, buf_tok, buf_asg, hn_all, w_up, b_up_perm,
         w_down, b_down, n_blk, n_assign):
    ew = lambda r, c: pl.BlockSpec((1, r, c), lambda i, be, nxt, nv, nu: (be[i], 0, 0))
    tok = lambda f: pl.BlockSpec((1, 1, TM_MOE), lambda i, be, nxt, nv, nu: (f(i), 0, 0),
                                 memory_space=pltpu.SMEM)
    hbm = pl.BlockSpec(memory_space=pl.ANY)
    return pl.pallas_call(
        _moe_kernel,
        grid_spec=pltpu.PrefetchScalarGridSpec(
            num_scalar_prefetch=4,
            grid=(n_blk,),
            in_specs=[
                tok(lambda i: i), tok(lambda i: jnp.minimum(i + 1, n_blk - 1)),
                tok(lambda i: jnp.minimum(i + GATHER_AHEAD, n_blk - 1)), tok(lambda i: i),
                hbm, hbm, ew(1, 2 * D_FF), hbm, ew(1, D_MODEL),
            ],
            out_specs=hbm,
            scratch_shapes=[pltpu.VMEM((GATHER_AHEAD, TM_MOE) + ROW_TILE, F32),
                            pltpu.VMEM((TM_MOE, D_MODEL), BF16),
                            pltpu.VMEM((2, TM_MOE) + ROW_TILE, F32),
                            pltpu.VMEM((D_MODEL, 2 * D_FF), F32),
                            pltpu.VMEM((D_FF, D_MODEL), F32),
                            pltpu.VMEM((D_MODEL, 2 * D_FF), BF16),
                            pltpu.VMEM((D_FF, D_MODEL), BF16),
                            pltpu.SemaphoreType.DMA((GATHER_AHEAD,)),
                            pltpu.SemaphoreType.DMA((2,)),
                            pltpu.SemaphoreType.DMA((2,))],
        ),
        out_shape=jax.ShapeDtypeStruct((n_assign,) + ROW_TILE, F32),
        compiler_params=_cparams(1),
        name="moe_experts",
    )(blk_expert, next_expert, n_valid, n_used, buf_tok, buf_tok, buf_tok, buf_asg, hn_all, w_up,
      b_up_perm, w_down, b_down)


def _final_kernel(x1_ref, tw_ref, nf_ref, *rest):
    y_refs, o_ref = rest[:TOP_K], rest[TOP_K]
    tw = tw_ref[...]
    x = x1_ref[...]
    for k in range(TOP_K):
        x = x + tw[:, k:k + 1] * _load_row_tiled(y_refs[k].at[0], x.shape[0])
    ms = jnp.mean(x * x, axis=-1, keepdims=True)
    o_ref[...] = x * lax.rsqrt(ms + EPS) * nf_ref[...]


def _final(x1, tw, nf, y_choice, row0):
    rows = x1.shape[0]
    tm = min(TM_FIN, rows)
    row_block0 = row0 // tm
    row = lambda w: pl.BlockSpec((tm, w), lambda i: (i, 0))
    y_spec = lambda k: pl.BlockSpec((1, tm) + ROW_TILE, lambda i: (k, i + row_block0, 0, 0))
    return pl.pallas_call(
        _final_kernel,
        grid=(rows // tm,),
        in_specs=[row(D_MODEL), row(LANES), _full((1, D_MODEL))]
        + [y_spec(k) for k in range(TOP_K)],
        out_specs=row(D_MODEL),
        out_shape=jax.ShapeDtypeStruct((rows, D_MODEL), F32),
        compiler_params=_cparams(1),
        name="combine_final",
    )(x1, tw, nf, *([y_choice] * TOP_K))


def _block_diag_tiles(w):
    per = GATE_TILE // GATE_BW
    w4 = w.reshape(N_GATE_TILES, per, GATE_BW, GATE_BW)
    eye = jnp.eye(per, dtype=w.dtype)
    return jnp.einsum("gacd,ab->gacbd", w4, eye).reshape(N_GATE_TILES, GATE_TILE, GATE_TILE)


def _route(top_i, n_blk):
    a = top_i.size
    flat_e = top_i.reshape(-1)
    shift = (a - 1).bit_length()
    keys = (flat_e << shift) | jnp.arange(a, dtype=jnp.int32)
    sorted_n = jnp.sort(keys) & ((1 << shift) - 1)
    onehot = (flat_e[:, None] == jnp.arange(N_EXPERTS, dtype=jnp.int32)[None, :]).astype(jnp.int32)
    counts = jnp.sum(onehot, axis=0)
    padded = ((counts + (TM_MOE - 1)) // TM_MOE) * TM_MOE
    ends = jnp.cumsum(padded).astype(jnp.int32)
    start_padded = ends - padded
    start_sorted = (jnp.cumsum(counts) - counts).astype(jnp.int32)
    blk_start = jnp.arange(n_blk, dtype=jnp.int32) * TM_MOE
    blk_expert = jnp.minimum(
        jnp.sum((ends[None, :] <= blk_start[:, None]).astype(jnp.int32), axis=1), N_EXPERTS - 1)
    n_used = (ends[-1:] // TM_MOE).astype(jnp.int32)
    real_end = (start_padded + counts)[blk_expert]
    n_valid = jnp.clip(real_end - blk_start, 0, TM_MOE).astype(jnp.int32)
    n_tokens = a // TOP_K
    row_shift = jnp.repeat((start_padded - start_sorted)[blk_expert], TM_MOE)
    pos = jnp.clip(jnp.arange(n_blk * TM_MOE, dtype=jnp.int32) - row_shift, 0, a - 1)
    buf_n = sorted_n[pos]
    buf_tok = buf_n // TOP_K
    buf_asg = (buf_n % TOP_K) * n_tokens + buf_tok
    n_valid = jnp.concatenate([jnp.zeros((1,), jnp.int32), n_valid])
    buf_asg = jnp.concatenate([jnp.zeros((TM_MOE,), jnp.int32), buf_asg])
    e_ids = jnp.arange(N_EXPERTS, dtype=jnp.int32)
    later_used = (e_ids[None, :] > e_ids[:, None]) & (counts[None, :] > 0)
    next_used = jnp.min(jnp.where(later_used, e_ids[None, :], N_EXPERTS), axis=1)
    next_used = jnp.where(next_used == N_EXPERTS, -1, next_used).astype(jnp.int32)
    return (buf_tok.reshape(n_blk, 1, TM_MOE), buf_asg.reshape(n_blk + 1, 1, TM_MOE),
            blk_expert.astype(jnp.int32), next_used[blk_expert], n_valid, n_used)


def kernel(x_prompt, x_sample, cache_k, cache_v, state_conv, state_h, meta_tokens, norm_mix, w_in, conv_w, conv_b, w_rg, b_rg, w_ig, b_ig, lru_lambda, attn_sinks, w_out, norm_ffn, w_router, b_router, w_up, b_up, w_down, b_down, norm_final):
    n_b, seq = x_prompt.shape[0], x_prompt.shape[1]
    n_s = x_sample.shape[0]
    w_buf = cache_k.shape[2]
    n_p = n_b * seq
    n_tok = n_p + n_s
    row1 = lambda v: v.reshape(1, -1)

    w_in_bf = w_in[0].astype(BF16)
    w_out_bf = w_out[0].astype(BF16)
    wg_tiles = jnp.concatenate([_block_diag_tiles(w_rg[0]), _block_diag_tiles(w_ig[0])],
                               axis=-1).astype(BF16)
    wr_pad = jnp.pad(w_router[0], ((0, 0), (0, LANES - N_EXPERTS))).astype(BF16)
    br_pad = jnp.pad(b_router[0], (0, LANES - N_EXPERTS), constant_values=NEG_BIG).reshape(1, LANES)
    b_up_perm = jnp.swapaxes(b_up[0].reshape(N_EXPERTS, 2 * D_FF // UP_TILE, UP_TILE // 2, 2),
                             2, 3).reshape(N_EXPERTS, 1, 2 * D_FF)
    b_dn = b_down[0].reshape(N_EXPERTS, 1, D_MODEL)
    nm, nf, nfin = row1(norm_mix[0]), row1(norm_ffn[0]), row1(norm_final)
    cw, cb = conv_w[0], row1(conv_b[0])
    brg, big, lam = row1(b_rg[0]), row1(b_ig[0]), row1(lru_lambda[0])
    sinks = attn_sinks[0]
    rnn_w = (cw, cb, wg_tiles, brg, big, lam)
    rnn_w_specs = [_full((CONV_W, D_RNN)), _full((1, D_RNN)),
                   _full((N_GATE_TILES, GATE_TILE, 2 * GATE_TILE)),
                   _full((1, D_RNN)), _full((1, D_RNN)), _full((1, D_RNN))]

    xp2 = x_prompt.reshape(n_p, D_MODEL)
    xr_p, q_p, kv_p, gr_p, ga_p = _in_proj(xp2, nm, w_in_bf, TM_IN)
    x_sm = jnp.concatenate([x_sample.reshape(n_s, D_MODEL), meta_tokens], axis=0)
    xr_sm, q_sm, kv_sm, gr_sm, ga_sm = _in_proj(x_sm, nm, w_in_bf, n_s + N_META)
    xr_s, q_s, kv_s, gr_s, ga_s = (t[:n_s] for t in (xr_sm, q_sm, kv_sm, gr_sm, ga_sm))
    xr_m, kv_m = xr_sm[n_s:], kv_sm[n_s:]

    h_meta = pl.pallas_call(
        _rnn_meta_kernel,
        in_specs=[_full((N_META, D_RNN))] + rnn_w_specs,
        out_specs=_full((1, D_RNN)),
        out_shape=jax.ShapeDtypeStruct((1, D_RNN), F32),
        grid=(1,),
        compiler_params=_cparams(1),
        name="rnn_meta",
    )(xr_m, *rnn_w)

    tt = TT_RNN
    blk3 = pl.BlockSpec((n_b, tt, D_RNN), lambda j: (0, j, 0))
    m_rnn_p, h_last_p = pl.pallas_call(
        functools.partial(_rnn_prompt_kernel, n_b=n_b, tt=tt),
        grid=(seq // tt,),
        in_specs=[blk3, blk3, _full((N_META, D_RNN)), _full((1, D_RNN))] + rnn_w_specs,
        out_specs=[blk3, _full((n_b, D_RNN))],
        out_shape=[jax.ShapeDtypeStruct((n_b, seq, D_RNN), BF16),
                   jax.ShapeDtypeStruct((n_b, D_RNN), F32)],
        scratch_shapes=[pltpu.VMEM((n_b, 8, D_RNN), F32), pltpu.VMEM((n_b, D_RNN), F32)],
        compiler_params=_cparams(1),
        name="rnn_prompt",
    )(xr_p.reshape(n_b, seq, D_RNN), gr_p.reshape(n_b, seq, D_RNN), xr_m, h_meta, *rnn_w)

    sc_t = jnp.swapaxes(state_conv[0], 0, 1)
    m_rnn_s, h_new_s = pl.pallas_call(
        _rnn_sample_kernel,
        grid=(1,),
        in_specs=[_full((n_s, D_RNN)), _full((n_s, D_RNN)), _full((CONV_W - 1, n_s, D_RNN)),
                  _full((n_s, D_RNN))] + rnn_w_specs,
        out_specs=[_full((n_s, D_RNN)), _full((n_s, D_RNN))],
        out_shape=[jax.ShapeDtypeStruct((n_s, D_RNN), BF16),
                   jax.ShapeDtypeStruct((n_s, D_RNN), F32)],
        compiler_params=_cparams(1),
        name="rnn_sample",
    )(xr_s, gr_s, sc_t, state_h[0], *rnn_w)

    n_blk_seq = seq // BLOCK
    kv_meta_blk = jnp.pad(kv_m, ((BLOCK - N_META, 0), (0, 0)))
    smem_spec = pl.BlockSpec(memory_space=pltpu.SMEM)
    rb = lambda w: pl.BlockSpec((BLOCK, w), lambda b, j: (b * n_blk_seq + j, 0))
    m_attn_p = pl.pallas_call(
        _attn_prompt_kernel,
        grid=(n_b, n_blk_seq),
        in_specs=[smem_spec, rb(D_ATTN), rb(2 * D_KV),
                  pl.BlockSpec((BLOCK, 2 * D_KV),
                               lambda b, j: (jnp.maximum(b * n_blk_seq + j - 1, 0), 0)),
                  _full((BLOCK, 2 * D_KV)),
                  pl.BlockSpec((1, N_HEADS, BLOCK, BLOCK),
                               lambda b, j: (jnp.minimum(j, 1), 0, 0, 0)),
                  rb(D_ATTN)],
        out_specs=rb(D_ATTN),
        out_shape=jax.ShapeDtypeStruct((n_p, D_ATTN), BF16),
        compiler_params=_cparams(2),
        name="attn_prompt",
    )(sinks, q_p, kv_p, kv_p, kv_meta_blk, _attn_bias_table(), ga_p)

    ck = cache_k[0].reshape(n_s, w_buf, D_KV)
    cv = cache_v[0].reshape(n_s, w_buf, D_KV)
    per_s = lambda a, b: pl.BlockSpec((SAMPLES_PER_STEP, a, b), lambda i: (i, 0, 0))
    m_attn_s = pl.pallas_call(
        _attn_sample_kernel,
        grid=(n_s // SAMPLES_PER_STEP,),
        in_specs=[smem_spec, per_s(N_HEADS, HEAD_DIM), per_s(1, 2 * D_KV), per_s(w_buf, D_KV),
                  per_s(w_buf, D_KV), per_s(N_HEADS, HEAD_DIM)],
        out_specs=per_s(N_HEADS, HEAD_DIM),
        out_shape=jax.ShapeDtypeStruct((n_s, N_HEADS, HEAD_DIM), BF16),
        compiler_params=_cparams(1),
        name="attn_sample",
    )(sinks, q_s.reshape(n_s, N_HEADS, HEAD_DIM), kv_s.reshape(n_s, 1, 2 * D_KV), ck, cv,
      ga_s.reshape(n_s, N_HEADS, HEAD_DIM)).reshape(n_s, D_ATTN)

    n_hn = n_p + -(-n_s // TM_MERGE) * TM_MERGE
    x1_p, hn_all, ti_p, tw_p = _merge(m_rnn_p.reshape(n_p, D_RNN), m_attn_p, xp2, w_out_bf, nf,
                                      wr_pad, br_pad, None, n_hn, 0, TM_MERGE)
    x1_s, hn_all, ti_s, tw_s = _merge(m_rnn_s, m_attn_s, x_sample.reshape(n_s, D_MODEL), w_out_bf,
                                      nf, wr_pad, br_pad, hn_all, n_hn, n_p // n_s, n_s)

    top_i = jnp.concatenate([ti_p[:, :TOP_K], ti_s[:, :TOP_K]], axis=0)
    n_assign = n_tok * TOP_K
    n_blk = n_assign // TM_MOE + N_EXPERTS - 1 + GATHER_AHEAD
    buf_tok, buf_asg, blk_expert, next_expert, n_valid, n_used = _route(top_i, n_blk)

    y_tok = _moe(blk_expert, next_expert, n_valid, n_used, buf_tok, buf_asg, hn_all, w_up[0],
                 b_up_perm, w_down[0], b_dn, n_blk, n_assign).reshape((TOP_K, n_tok) + ROW_TILE)
    y_p = _final(x1_p, tw_p, nfin, y_tok, 0)
    y_s = _final(x1_s, tw_s, nfin, y_tok, n_p)

    kv_p3 = kv_p.reshape(n_b, seq, 2 * D_KV)
    w_p = min(WINDOW, seq + N_META)
    new_k_p = kv_p3[:, seq - w_p:, :D_KV].reshape(1, n_b, w_p, N_KV, HEAD_DIM)
    new_v_p = kv_p3[:, seq - w_p:, D_KV:].reshape(1, n_b, w_p, N_KV, HEAD_DIM)
    new_conv_p = xr_p.reshape(n_b, seq, D_RNN)[:, seq - (CONV_W - 1):][None]
    k_new = kv_s[:, :D_KV].reshape(n_s, 1, N_KV, HEAD_DIM)
    v_new = kv_s[:, D_KV:].reshape(n_s, 1, N_KV, HEAD_DIM)
    new_k_s = jnp.concatenate([cache_k[0], k_new], axis=1)[:, -w_buf:][None]
    new_v_s = jnp.concatenate([cache_v[0], v_new], axis=1)[:, -w_buf:][None]
    new_conv_s = jnp.concatenate([state_conv[0], xr_s[:, None, :]], axis=1)[:, -(CONV_W - 1):][None]
    return (y_p.reshape(n_b, seq, D_MODEL), y_s.reshape(n_s, 1, D_MODEL), new_k_p, new_v_p,
            new_conv_p, h_last_p[None], new_k_s, new_v_s, new_conv_s, h_new_s[None])
```

```python
import functools

import jax
import jax.numpy as jnp
import numpy as np
from jax import lax
from jax.experimental import pallas as pl
from jax.experimental.pallas import tpu as pltpu

F32 = jnp.float32
BF16 = jnp.bfloat16

D_MODEL = 1024
N_META = 16
D_RNN = 1024
N_GATE_BLOCKS = 16
GATE_BW = D_RNN // N_GATE_BLOCKS
CONV_W = 4
LRU_C = 8.0
N_HEADS = 16
HEAD_DIM = 64
N_KV = 2
GROUP = N_HEADS // N_KV
D_ATTN = N_HEADS * HEAD_DIM
D_KV = N_KV * HEAD_DIM
WINDOW = 128
BLOCK = 128
PAST_LEN = 16384
N_EXPERTS = 32
TOP_K = 4
D_FF = 1024
SWIGLU_LIMIT = 7.0
SWIGLU_ALPHA = 1.702
EPS = 1e-6

GATE_TILE = 256
N_GATE_TILES = D_RNN // GATE_TILE
LANES = 128
NEG_BIG = -1e30

TM_IN = 512
TT_RNN = 64
TM_MERGE = 256
TM_MOE = 256
TM_FIN = 256
VMEM_LIMIT = 56 * 1024 * 1024


def _cparams(n_grid_dims):
    return pltpu.CompilerParams(
        dimension_semantics=("arbitrary",) * n_grid_dims, vmem_limit_bytes=VMEM_LIMIT)


def _full(shape):
    return pl.BlockSpec(shape, lambda *_: (0,) * len(shape))


SUBLANES = 8
ROW_TILE = (SUBLANES, D_MODEL // SUBLANES)


def _transpose_vreg_group(v):
    sub = lax.broadcasted_iota(jnp.int32, v[0].shape, 0)
    v = list(v)
    d = SUBLANES // 2
    while d >= 1:
        hi = (sub & d) != 0
        nv = list(v)
        for a in range(SUBLANES):
            if a & d:
                continue
            b = a | d
            nv[a] = jnp.where(hi, pltpu.roll(v[b], d, axis=0), v[a])
            nv[b] = jnp.where(hi, v[b], pltpu.roll(v[a], SUBLANES - d, axis=0))
        v = nv
        d //= 2
    return v


def _store_row_tiled(x, dst):
    for g in range(x.shape[0] // SUBLANES):
        rows = slice(g * SUBLANES, (g + 1) * SUBLANES)
        w = _transpose_vreg_group([x[rows, s * LANES:(s + 1) * LANES] for s in range(SUBLANES)])
        for j in range(SUBLANES):
            dst[g * SUBLANES + j] = w[j]


def _load_row_tiled(src, n_rows):
    groups = []
    for g in range(n_rows // SUBLANES):
        v = _transpose_vreg_group([src[g * SUBLANES + j] for j in range(SUBLANES)])
        groups.append(jnp.concatenate(v, axis=-1))
    return jnp.concatenate(groups, axis=0)


def _in_proj_kernel(x_ref, g_ref, w_ref, xr_ref, q_ref, kv_ref, gr_ref, ga_ref):
    x = x_ref[...]
    ms = jnp.mean(x * x, axis=-1, keepdims=True)
    h = (x * lax.rsqrt(ms + EPS) * g_ref[...]).astype(BF16)

    def proj(lo, hi):
        return jnp.dot(h, w_ref[:, lo:hi], preferred_element_type=F32)

    o_q = D_RNN
    o_k = o_q + D_ATTN
    o_gr = o_k + 2 * D_KV
    o_ga = o_gr + D_RNN
    xr_ref[...] = proj(0, o_q)
    q_ref[...] = (proj(o_q, o_k) * (HEAD_DIM ** -0.5)).astype(BF16)
    kv_ref[...] = proj(o_k, o_gr)
    gr_ref[...] = proj(o_gr, o_ga).astype(gr_ref.dtype)
    ga_ref[...] = proj(o_ga, o_ga + D_ATTN).astype(ga_ref.dtype)


def _in_proj(x, g, w_in_bf, tm, gate_dtype):
    rows = x.shape[0]
    d_in = w_in_bf.shape[1]
    row = lambda w: pl.BlockSpec((tm, w), lambda i: (i, 0))
    return pl.pallas_call(
        _in_proj_kernel,
        grid=(rows // tm,),
        in_specs=[row(D_MODEL), _full((1, D_MODEL)), _full((D_MODEL, d_in))],
        out_specs=[row(D_RNN), row(D_ATTN), row(2 * D_KV), row(D_RNN), row(D_ATTN)],
        out_shape=[
            jax.ShapeDtypeStruct((rows, D_RNN), F32),
            jax.ShapeDtypeStruct((rows, D_ATTN), BF16),
            jax.ShapeDtypeStruct((rows, 2 * D_KV), F32),
            jax.ShapeDtypeStruct((rows, D_RNN), gate_dtype),
            jax.ShapeDtypeStruct((rows, D_ATTN), gate_dtype),
        ],
        compiler_params=_cparams(1),
        name="in_proj",
    )(x, g, w_in_bf)


def _softplus(x):
    return jnp.maximum(x, 0.0) + jnp.log1p(jnp.exp(-jnp.abs(x)))


def _lru_coeffs(u, wg_ref, brg, big, lam):
    ub = u.astype(BF16)
    sp = _softplus(-lam)
    a_parts, i_parts, m_parts = [], [], []
    for g in range(N_GATE_TILES):
        sl = slice(g * GATE_TILE, (g + 1) * GATE_TILE)
        zz = jnp.dot(ub[:, sl], wg_ref[g], preferred_element_type=F32)
        r = jax.nn.sigmoid(zz[:, :GATE_TILE] + brg[:, sl])
        i = jax.nn.sigmoid(zz[:, GATE_TILE:] + big[:, sl])
        log_a = (-LRU_C) * r * sp[:, sl]
        a = jnp.exp(log_a)
        m = jnp.sqrt(1.0 - a * a)
        a_parts.append(a)
        i_parts.append(i)
        m_parts.append(m)
    cat = lambda ps: jnp.concatenate(ps, axis=-1)
    return cat(a_parts), cat(i_parts), cat(m_parts)


def _conv(ext, t, cw, cb):
    out = cb
    for j in range(CONV_W):
        s = CONV_W - 1 - j
        out = out + cw[j:j + 1, :] * ext[8 - s:8 - s + t, :]
    return out


def _rnn_meta_kernel(xr_ref, cw_ref, cb_ref, wg_ref, brg_ref, big_ref, lam_ref, h_ref):
    x = xr_ref[...]
    ext = jnp.concatenate([jnp.zeros((8, D_RNN), F32), x], axis=0)
    u = _conv(ext, N_META, cw_ref[...], cb_ref[...])
    a, i, m = _lru_coeffs(u, wg_ref, brg_ref[...], big_ref[...], lam_ref[...])
    first = lax.broadcasted_iota(jnp.int32, (N_META, 1), 0) == 0
    b = jnp.where(first, 1.0, m) * i * u
    h = jnp.zeros((1, D_RNN), F32)
    for t in range(N_META):
        h = a[t:t + 1, :] * h + b[t:t + 1, :]
    h_ref[...] = h


def _linear_scan(a, b, h_prev):
    t, d = a.shape
    g = t // SUBLANES
    a3 = a.reshape(g, SUBLANES, d)
    b3 = b.reshape(g, SUBLANES, d)
    row = lax.broadcasted_iota(jnp.int32, (g, SUBLANES, d), 1)
    step = 1
    while step < SUBLANES:
        keep = row >= step
        a_up = jnp.where(keep, pltpu.roll(a3, step, axis=1), 1.0)
        b_up = jnp.where(keep, pltpu.roll(b3, step, axis=1), 0.0)
        b3 = a3 * b_up + b3
        a3 = a3 * a_up
        step *= 2
    hs = []
    h = h_prev
    for k in range(g):
        hk = a3[k] * h + b3[k]
        hs.append(hk)
        h = hk[SUBLANES - 1:SUBLANES, :]
    return jnp.concatenate(hs, axis=0), h


def _rnn_prompt_kernel(xr_ref, gr_ref, xm_ref, h0_ref, cw_ref, cb_ref, wg_ref, brg_ref, big_ref,
                       lam_ref, m_ref, hl_ref, halo_s, h_s, *, n_b, tt):
    j = pl.program_id(0)

    @pl.when(j == 0)
    def _():
        h_s[...] = jnp.broadcast_to(h0_ref[...], (n_b, D_RNN))
        for b in range(n_b):
            halo_s[b] = xm_ref[N_META - 8:N_META, :]

    cw = cw_ref[...]
    cb = cb_ref[...]

    def per_batch(b, c):
        x = xr_ref[b]
        ext = jnp.concatenate([halo_s[b], x], axis=0)
        halo_s[b] = x[tt - 8:tt, :]
        u = _conv(ext, tt, cw, cb)
        a, i, m = _lru_coeffs(u, wg_ref, brg_ref[...], big_ref[...], lam_ref[...])
        y, h = _linear_scan(a, m * i * u, h_s[pl.ds(b, 1), :])
        h_s[pl.ds(b, 1), :] = h
        m_ref[b] = (jax.nn.sigmoid(gr_ref[b].astype(F32)) * y).astype(BF16)
        return c

    lax.fori_loop(0, n_b, per_batch, 0)
    hl_ref[...] = h_s[...]


def _rnn_sample_kernel(xr_ref, gr_ref, sc_ref, h0_ref, cw_ref, cb_ref, wg_ref, brg_ref, big_ref,
                       lam_ref, m_ref, hn_ref):
    cw = cw_ref[...]
    u = cb_ref[...] + cw[CONV_W - 1:CONV_W, :] * xr_ref[...]
    for j in range(CONV_W - 1):
        u = u + cw[j:j + 1, :] * sc_ref[j]
    a, i, m = _lru_coeffs(u, wg_ref, brg_ref[...], big_ref[...], lam_ref[...])
    h = a * h0_ref[...] + m * i * u
    hn_ref[...] = h
    m_ref[...] = (jax.nn.sigmoid(gr_ref[...].astype(F32)) * h).astype(BF16)


def _alibi_slope(h):
    return 2.0 ** (-8.0 * (h + 1) / N_HEADS)


def _attn_bias_table():
    assert WINDOW == BLOCK
    qi = np.arange(BLOCK, dtype=np.int32)[:, None]
    ci = np.arange(BLOCK, dtype=np.int32)[None, :]
    own = ci <= qi
    dist = np.where(own, qi - ci, qi + BLOCK - ci).astype(np.float32)
    slopes = np.asarray([_alibi_slope(h) for h in range(N_HEADS)], np.float32)[:, None, None]
    ali = -slopes * dist[None]
    tables = [np.where((own | (ci >= c_min))[None], ali, np.float32(NEG_BIG))
              for c_min in (BLOCK - N_META, 0)]
    return jnp.asarray(np.stack(tables).astype(np.float32))


def _attn_prompt_kernel(sink_ref, q_ref, kc_ref, kp_ref, km_ref, bias_ref, ga_ref, o_ref):
    first = pl.program_id(1) == 0
    kv_prev = jnp.where(first, km_ref[...], kp_ref[...])
    kv = jnp.concatenate([kv_prev, kc_ref[...]], axis=0).astype(BF16)
    q = q_ref[...]
    qi = lax.broadcasted_iota(jnp.int32, (BLOCK, BLOCK), 0)
    ci = lax.broadcasted_iota(jnp.int32, (BLOCK, BLOCK), 1)
    own = ci <= qi
    zero = jnp.zeros((BLOCK, BLOCK), BF16)
    outs = []
    for h in range(N_HEADS):
        g = h // GROUP
        kh = kv[:, g * HEAD_DIM:(g + 1) * HEAD_DIM]
        vh = kv[:, D_KV + g * HEAD_DIM:D_KV + (g + 1) * HEAD_DIM]
        qh = q[:, h * HEAD_DIM:(h + 1) * HEAD_DIM]
        s2 = lax.dot_general(qh, kh, (((1,), (1,)), ((), ())), preferred_element_type=F32)
        s = jnp.where(own, s2[:, BLOCK:], s2[:, :BLOCK]) + bias_ref[0, h]
        sink = sink_ref[h]
        mx = jnp.maximum(jnp.max(s, axis=-1, keepdims=True), sink)
        p = jnp.exp(s - mx)
        den = jnp.sum(p, axis=-1, keepdims=True) + jnp.exp(sink - mx)
        pb = p.astype(BF16)
        p2 = jnp.concatenate([jnp.where(own, zero, pb), jnp.where(own, pb, zero)], axis=1)
        o = jnp.dot(p2, vh, preferred_element_type=F32)
        outs.append(o / den)
    y = jnp.concatenate(outs, axis=-1)
    o_ref[...] = (jax.nn.sigmoid(ga_ref[...].astype(F32)) * y).astype(BF16)


SAMPLES_PER_STEP = 8


def _attn_sample_kernel(sink_ref, q_ref, kvn_ref, ck_ref, cv_ref, ga_ref, o_ref):
    w_buf = ck_ref.shape[1]
    n_rows = 2 * N_HEADS
    n_blocks = 2 * N_KV
    ci = lax.broadcasted_iota(jnp.int32, (n_rows, w_buf), 1)
    dist = w_buf - ci
    valid = dist < WINDOW
    row = lax.broadcasted_iota(jnp.int32, (n_rows, 1), 0)
    slope = jnp.zeros((n_rows, 1), F32)
    sink = jnp.zeros((n_rows, 1), F32)
    for h in range(N_HEADS):
        slope = jnp.where(row % N_HEADS == h, _alibi_slope(h), slope)
        sink = jnp.where(row % N_HEADS == h, sink_ref[h], sink)
    bias = jnp.where(valid, -slope * dist.astype(F32), NEG_BIG)
    row_block = lax.broadcasted_iota(jnp.int32, (n_rows, n_blocks * HEAD_DIM), 0) // GROUP
    lane_block = lax.broadcasted_iota(jnp.int32, (n_rows, n_blocks * HEAD_DIM), 1) // HEAD_DIM
    own_block = row_block == lane_block
    for n in range(0, SAMPLES_PER_STEP, 2):
        q2 = jnp.concatenate([q_ref[n], q_ref[n + 1]], axis=0)
        qd = jnp.where(own_block, jnp.concatenate([q2] * n_blocks, axis=1), jnp.zeros((), BF16))
        kk = jnp.concatenate([ck_ref[n], ck_ref[n + 1]], axis=1).astype(BF16)
        vv = jnp.concatenate([cv_ref[n], cv_ref[n + 1]], axis=1).astype(BF16)
        kvn = [kvn_ref[n + t].astype(BF16).astype(F32) for t in range(2)]
        k_new = jnp.concatenate([kvn[0][:, :D_KV], kvn[1][:, :D_KV]], axis=1)
        v_new = jnp.concatenate([kvn[0][:, D_KV:], kvn[1][:, D_KV:]], axis=1)
        s = lax.dot_general(qd, kk, (((1,), (1,)), ((), ())), preferred_element_type=F32) + bias
        sn = jnp.sum(qd.astype(F32) * k_new, axis=-1, keepdims=True)
        mx = jnp.maximum(jnp.maximum(jnp.max(s, axis=-1, keepdims=True), sn), sink)
        p = jnp.exp(s - mx)
        pn = jnp.exp(sn - mx)
        den = jnp.sum(p, axis=-1, keepdims=True) + pn + jnp.exp(sink - mx)
        o = jnp.dot(p.astype(BF16), vv, preferred_element_type=F32)
        o = o + pn.astype(BF16).astype(F32) * v_new
        y = jnp.concatenate([o[j * GROUP:(j + 1) * GROUP, j * HEAD_DIM:(j + 1) * HEAD_DIM]
                             for j in range(n_blocks)], axis=0) / den
        for t in range(2):
            rows = slice(t * N_HEADS, (t + 1) * N_HEADS)
            o_ref[n + t] = (jax.nn.sigmoid(ga_ref[n + t].astype(F32)) * y[rows]).astype(BF16)


def _merge_kernel(mr_ref, ma_ref, x_ref, wo_ref, nf_ref, wr_ref, br_ref, *rest, n_steps):
    x1_ref, hn_ref, ti_ref, tw_ref = rest[-4:]

    @pl.when(pl.program_id(0) >= n_steps)
    def _():
        hn_ref[...] = jnp.zeros_like(hn_ref)

    @pl.when(pl.program_id(0) < n_steps)
    def _():
        _merge_tile(mr_ref, ma_ref, x_ref, wo_ref, nf_ref, wr_ref, br_ref, x1_ref, hn_ref, ti_ref,
                    tw_ref)


def _merge_tile(mr_ref, ma_ref, x_ref, wo_ref, nf_ref, wr_ref, br_ref, x1_ref, hn_ref, ti_ref,
                tw_ref):
    mm = jnp.dot(mr_ref[...], wo_ref[:D_RNN, :], preferred_element_type=F32)
    mm = mm + jnp.dot(ma_ref[...], wo_ref[D_RNN:, :], preferred_element_type=F32)
    x1 = x_ref[...] + mm
    x1_ref[...] = x1
    ms = jnp.mean(x1 * x1, axis=-1, keepdims=True)
    hn = x1 * lax.rsqrt(ms + EPS) * nf_ref[...]
    _store_row_tiled(hn, hn_ref)
    logits = jnp.dot(hn.astype(BF16), wr_ref[...], preferred_element_type=F32) + br_ref[...]
    lane = lax.broadcasted_iota(jnp.int32, logits.shape, 1)
    ti = jnp.zeros(logits.shape, jnp.int32)
    tv = jnp.zeros(logits.shape, F32)
    l = logits
    v0 = None
    for k in range(TOP_K):
        mx = jnp.max(l, axis=-1, keepdims=True)
        idx = jnp.min(jnp.where(l == mx, lane, LANES), axis=-1, keepdims=True)
        if k == 0:
            v0 = mx
        ti = jnp.where(lane == k, idx, ti)
        tv = jnp.where(lane == k, jnp.exp(mx - v0), tv)
        l = jnp.where(lane == idx, NEG_BIG * 2, l)
    ti_ref[...] = ti
    tw_ref[...] = tv / jnp.sum(tv, axis=-1, keepdims=True)


def _merge(m_rnn, m_attn, x, w_out_bf, nf, wr_pad, br_pad, hn_all, n_all, row_block0, tm):
    rows = x.shape[0]
    n_steps = rows // tm
    n_zero_steps = 0 if hn_all is not None else n_all // tm - n_steps
    row = lambda w: pl.BlockSpec((tm, w), lambda i: (jnp.minimum(i, n_steps - 1), 0))
    in_specs = [row(D_RNN), row(D_ATTN), row(D_MODEL), _full((D_RNN + D_ATTN, D_MODEL)),
                _full((1, D_MODEL)), _full((D_MODEL, LANES)), _full((1, LANES))]
    args = [m_rnn, m_attn, x, w_out_bf, nf, wr_pad, br_pad]
    aliases = {}
    if hn_all is not None:
        in_specs.append(pl.BlockSpec(memory_space=pl.ANY))
        args.append(hn_all)
        aliases = {len(args) - 1: 1}
    return pl.pallas_call(
        functools.partial(_merge_kernel, n_steps=n_steps),
        grid=(n_steps + n_zero_steps,),
        in_specs=in_specs,
        out_specs=[row(D_MODEL),
                   pl.BlockSpec((tm,) + ROW_TILE, lambda i: (i + row_block0, 0, 0)),
                   row(LANES), row(LANES)],
        out_shape=[
            jax.ShapeDtypeStruct((rows, D_MODEL), F32),
            jax.ShapeDtypeStruct((n_all,) + ROW_TILE, F32),
            jax.ShapeDtypeStruct((rows, LANES), jnp.int32),
            jax.ShapeDtypeStruct((rows, LANES), F32),
        ],
        input_output_aliases=aliases,
        compiler_params=_cparams(1),
        name="merge_router",
    )(*args)


UP_TILE = 256


def _deinterleave_matrix():
    half = UP_TILE // 2
    r = lax.broadcasted_iota(jnp.int32, (UP_TILE, UP_TILE), 0)
    c = lax.broadcasted_iota(jnp.int32, (UP_TILE, UP_TILE), 1)
    src = jnp.where(c < half, 2 * c, 2 * (c - half) + 1)
    return jnp.where(r == src, 1.0, 0.0).astype(BF16)


GATHER_AHEAD = 2
GATHER_DMA_PRIORITY = 0
OTHER_DMA_PRIORITY = 1


def _moe_kernel(be_ref, nxt_ref, nval_ref, nu_ref, tok0_ref, tok1_ref, tok2_ref, asg_ref, hn_hbm,
                wu_hbm, bu_ref, wd_hbm, bd_ref, out_hbm, xbuf, xb_s, ystage, wu_f, wd_f, wu_s, wd_s,
                sems, ssems, wsems):
    i = pl.program_id(0)
    nu = nu_ref[0]
    half = UP_TILE // 2
    n_up = 2 * D_FF // UP_TILE
    slot = i % GATHER_AHEAD
    prev_slot = 1 - slot
    n_prev = nval_ref[i]

    def row_copy(tok_ref, r, s):
        return pltpu.make_async_copy(hn_hbm.at[pl.ds(tok_ref[0, 0, r], 1)],
                                     xbuf.at[s, pl.ds(r, 1)], sems.at[s])

    def weight_copies(e):
        return (pltpu.make_async_copy(wu_hbm.at[e], wu_f, wsems.at[0]),
                pltpu.make_async_copy(wd_hbm.at[e], wd_f, wsems.at[1]))

    def out_copy(r):
        return pltpu.make_async_copy(ystage.at[prev_slot, pl.ds(r, 1)],
                                     out_hbm.at[pl.ds(asg_ref[0, 0, r], 1)], ssems.at[prev_slot])

    @pl.when(i == 0)
    def _():
        for cp in weight_copies(be_ref[0]):
            cp.start(priority=OTHER_DMA_PRIORITY)

        def issue(r, c):
            row_copy(tok0_ref, r, 0).start(priority=GATHER_DMA_PRIORITY)
            row_copy(tok1_ref, r, 1).start(priority=GATHER_DMA_PRIORITY)
            return c

        lax.fori_loop(0, TM_MOE, issue, 0)

    @pl.when(i < nu + GATHER_AHEAD)
    def _():
        pltpu.make_async_copy(hn_hbm.at[pl.ds(0, TM_MOE)], xbuf.at[slot], sems.at[slot]).wait()

    @pl.when((i >= 2) & (i < nu + 2))
    def _():
        n_sent = nval_ref[i - 1]
        n_tiled = pl.multiple_of((n_sent // SUBLANES) * SUBLANES, SUBLANES)

        @pl.when(n_tiled > 0)
        def _():
            pltpu.make_async_copy(ystage.at[slot, pl.ds(0, n_tiled)],
                                  out_hbm.at[pl.ds(0, n_tiled)], ssems.at[slot]).wait()

        def wait_row(r, c):
            pltpu.make_async_copy(ystage.at[slot, pl.ds(0, 1)], out_hbm.at[pl.ds(0, 1)],
                                  ssems.at[slot]).wait()
            return c

        lax.fori_loop(0, n_sent - n_tiled, wait_row, 0)

    @pl.when(i < nu)
    def _():
        xb_s[...] = _load_row_tiled(xbuf.at[slot], TM_MOE).astype(BF16)

        @pl.when((i == 0) | (be_ref[i] != be_ref[jnp.maximum(i - 1, 0)]))
        def _():
            for cp in weight_copies(be_ref[i]):
                cp.wait()
            perm = _deinterleave_matrix()
            for c in range(n_up):
                cols = slice(c * UP_TILE, (c + 1) * UP_TILE)
                blk = wu_f[:, cols].astype(BF16)
                wu_s[:, cols] = jnp.dot(blk, perm, preferred_element_type=F32).astype(BF16)
            wd_s[...] = wd_f[...].astype(BF16)

            @pl.when(nxt_ref[i] >= 0)
            def _():
                for cp in weight_copies(nxt_ref[i]):
                    cp.start(priority=OTHER_DMA_PRIORITY)

    def send_some_rows():
        def send(r, c):
            out_copy(r).start(priority=OTHER_DMA_PRIORITY)
            return c

        lax.fori_loop(0, n_prev, send, 0)

    def run_block(send_all_rows):
        for r in range(TM_MOE):
            row_copy(tok2_ref, r, slot).start(priority=GATHER_DMA_PRIORITY)
            if send_all_rows:
                out_copy(r).start(priority=OTHER_DMA_PRIORITY)
        x = xb_s[...]
        z = jnp.dot(x, wu_s[...], preferred_element_type=F32) + bu_ref[0]
        zg = jnp.concatenate([z[:, c * UP_TILE:c * UP_TILE + half] for c in range(n_up)], axis=-1)
        zl = jnp.concatenate([z[:, c * UP_TILE + half:(c + 1) * UP_TILE] for c in range(n_up)],
                             axis=-1)
        xg = jnp.minimum(zg, SWIGLU_LIMIT)
        xl = jnp.clip(zl, -SWIGLU_LIMIT, SWIGLU_LIMIT)
        act = xg * jax.nn.sigmoid(SWIGLU_ALPHA * xg) * (xl + 1.0)
        y = jnp.dot(act.astype(BF16), wd_s[...], preferred_element_type=F32) + bd_ref[0]
        _store_row_tiled(y, ystage.at[slot])

    @pl.when((i < nu) & (n_prev == TM_MOE))
    def _():
        run_block(send_all_rows=True)

    @pl.when((i < nu) & (n_prev != TM_MOE))
    def _():
        send_some_rows()
        run_block(send_all_rows=False)

    @pl.when(i == nu)
    def _():
        send_some_rows()


def _moe(blk_expert, next_expert, n_valid, n_used, buf_tok, buf_asg, hn_all, w_up, b_up_perm,
         w_down, b_down, n_blk, n_assign):
    ew = lambda r, c: pl.BlockSpec((1, r, c), lambda i, be, nxt, nv, nu: (be[i], 0, 0))
    tok = lambda f: pl.BlockSpec((1, 1, TM_MOE), lambda i, be, nxt, nv, nu: (f(i), 0, 0),
                                 memory_space=pltpu.SMEM)
    hbm = pl.BlockSpec(memory_space=pl.ANY)
    return pl.pallas_call(
        _moe_kernel,
        grid_spec=pltpu.PrefetchScalarGridSpec(
            num_scalar_prefetch=4,
            grid=(n_blk,),
            in_specs=[
                tok(lambda i: i), tok(lambda i: jnp.minimum(i + 1, n_blk - 1)),
                tok(lambda i: jnp.minimum(i + GATHER_AHEAD, n_blk - 1)), tok(lambda i: i),
                hbm, hbm, ew(1, 2 * D_FF), hbm, ew(1, D_MODEL),
            ],
            out_specs=hbm,
            scratch_shapes=[pltpu.VMEM((GATHER_AHEAD, TM_MOE) + ROW_TILE, F32),
                            pltpu.VMEM((TM_MOE, D_MODEL), BF16),
                            pltpu.VMEM((2, TM_MOE) + ROW_TILE, F32),
                            pltpu.VMEM((D_MODEL, 2 * D_FF), F32),
                            pltpu.VMEM((D_FF, D_MODEL), F32),
                            pltpu.VMEM((D_MODEL, 2 * D_FF), BF16),
                            pltpu.VMEM((D_FF, D_MODEL), BF16),
                            pltpu.SemaphoreType.DMA((GATHER_AHEAD,)),
                            pltpu.SemaphoreType.DMA((2,)),
                            pltpu.SemaphoreType.DMA((2,))],
        ),
        out_shape=jax.ShapeDtypeStruct((n_assign,) + ROW_TILE, F32),
        compiler_params=_cparams(1),
        name="moe_experts",
    )(blk_expert, next_expert, n_valid, n_used, buf_tok, buf_tok, buf_tok, buf_asg, hn_all, w_up,
      b_up_perm, w_down, b_down)


def _final_kernel(x1_ref, tw_ref, nf_ref, *rest):
    y_refs, o_ref = rest[:TOP_K], rest[TOP_K]
    tw = tw_ref[...]
    x = x1_ref[...]
    for k in range(TOP_K):
        x = x + tw[:, k:k + 1] * _load_row_tiled(y_refs[k].at[0], x.shape[0])
    ms = jnp.mean(x * x, axis=-1, keepdims=True)
    o_ref[...] = x * lax.rsqrt(ms + EPS) * nf_ref[...]


def _final(x1, tw, nf, y_choice, row0):
    rows = x1.shape[0]
    tm = min(TM_FIN, rows)
    row_block0 = row0 // tm
    row = lambda w: pl.BlockSpec((tm, w), lambda i: (i, 0))
    y_spec = lambda k: pl.BlockSpec((1, tm) + ROW_TILE, lambda i: (k, i + row_block0, 0, 0))
    return pl.pallas_call(
        _final_kernel,
        grid=(rows // tm,),
        in_specs=[row(D_MODEL), row(LANES), _full((1, D_MODEL))]
        + [y_spec(k) for k in range(TOP_K)],
        out_specs=row(D_MODEL),
        out_shape=jax.ShapeDtypeStruct((rows, D_MODEL), F32),
        compiler_params=_cparams(1),
        name="combine_final",
    )(x1, tw, nf, *([y_choice] * TOP_K))


def _block_diag_tiles(w):
    per = GATE_TILE // GATE_BW
    w4 = w.reshape(N_GATE_TILES, per, GATE_BW, GATE_BW)
    eye = jnp.eye(per, dtype=w.dtype)
    return jnp.einsum("gacd,ab->gacbd", w4, eye).reshape(N_GATE_TILES, GATE_TILE, GATE_TILE)


def _route(top_i, n_blk):
    a = top_i.size
    flat_e = top_i.reshape(-1)
    shift = (a - 1).bit_length()
    keys = (flat_e << shift) | jnp.arange(a, dtype=jnp.int32)
    sorted_n = jnp.sort(keys) & ((1 << shift) - 1)
    onehot = (flat_e[:, None] == jnp.arange(N_EXPERTS, dtype=jnp.int32)[None, :]).astype(jnp.int32)
    counts = jnp.sum(onehot, axis=0)
    padded = ((counts + (TM_MOE - 1)) // TM_MOE) * TM_MOE
    ends = jnp.cumsum(padded).astype(jnp.int32)
    start_padded = ends - padded
    start_sorted = (jnp.cumsum(counts) - counts).astype(jnp.int32)
    blk_start = jnp.arange(n_blk, dtype=jnp.int32) * TM_MOE
    blk_expert = jnp.minimum(
        jnp.sum((ends[None, :] <= blk_start[:, None]).astype(jnp.int32), axis=1), N_EXPERTS - 1)
    n_used = (ends[-1:] // TM_MOE).astype(jnp.int32)
    real_end = (start_padded + counts)[blk_expert]
    n_valid = jnp.clip(real_end - blk_start, 0, TM_MOE).astype(jnp.int32)
    n_tokens = a // TOP_K
    row_shift = jnp.repeat((start_padded - start_sorted)[blk_expert], TM_MOE)
    pos = jnp.clip(jnp.arange(n_blk * TM_MOE, dtype=jnp.int32) - row_shift, 0, a - 1)
    buf_n = sorted_n[pos]
    buf_tok = buf_n // TOP_K
    buf_asg = (buf_n % TOP_K) * n_tokens + buf_tok
    n_valid = jnp.concatenate([jnp.zeros((1,), jnp.int32), n_valid])
    buf_asg = jnp.concatenate([jnp.zeros((TM_MOE,), jnp.int32), buf_asg])
    e_ids = jnp.arange(N_EXPERTS, dtype=jnp.int32)
    later_used = (e_ids[None, :] > e_ids[:, None]) & (counts[None, :] > 0)
    next_used = jnp.min(jnp.where(later_used, e_ids[None, :], N_EXPERTS), axis=1)
    next_used = jnp.where(next_used == N_EXPERTS, -1, next_used).astype(jnp.int32)
    return (buf_tok.reshape(n_blk, 1, TM_MOE), buf_asg.reshape(n_blk + 1, 1, TM_MOE),
            blk_expert.astype(jnp.int32), next_used[blk_expert], n_valid, n_used)


def kernel(x_prompt, x_sample, cache_k, cache_v, state_conv, state_h, meta_tokens, norm_mix, w_in, conv_w, conv_b, w_rg, b_rg, w_ig, b_ig, lru_lambda, attn_sinks, w_out, norm_ffn, w_router, b_router, w_up, b_up, w_down, b_down, norm_final):
    n_b, seq = x_prompt.shape[0], x_prompt.shape[1]
    n_s = x_sample.shape[0]
    w_buf = cache_k.shape[2]
    n_p = n_b * seq
    n_tok = n_p + n_s
    row1 = lambda v: v.reshape(1, -1)

    w_in_bf = w_in[0].astype(BF16)
    w_out_bf = w_out[0].astype(BF16)
    wg_tiles = jnp.concatenate([_block_diag_tiles(w_rg[0]), _block_diag_tiles(w_ig[0])],
                               axis=-1).astype(BF16)
    wr_pad = jnp.pad(w_router[0], ((0, 0), (0, LANES - N_EXPERTS))).astype(BF16)
    br_pad = jnp.pad(b_router[0], (0, LANES - N_EXPERTS), constant_values=NEG_BIG).reshape(1, LANES)
    b_up_perm = jnp.swapaxes(b_up[0].reshape(N_EXPERTS, 2 * D_FF // UP_TILE, UP_TILE // 2, 2),
                             2, 3).reshape(N_EXPERTS, 1, 2 * D_FF)
    b_dn = b_down[0].reshape(N_EXPERTS, 1, D_MODEL)
    nm, nf, nfin = row1(norm_mix[0]), row1(norm_ffn[0]), row1(norm_final)
    cw, cb = conv_w[0], row1(conv_b[0])
    brg, big, lam = row1(b_rg[0]), row1(b_ig[0]), row1(lru_lambda[0])
    sinks = attn_sinks[0]
    rnn_w = (cw, cb, wg_tiles, brg, big, lam)
    rnn_w_specs = [_full((CONV_W, D_RNN)), _full((1, D_RNN)),
                   _full((N_GATE_TILES, GATE_TILE, 2 * GATE_TILE)),
                   _full((1, D_RNN)), _full((1, D_RNN)), _full((1, D_RNN))]

    xp2 = x_prompt.reshape(n_p, D_MODEL)
    xr_p, q_p, kv_p, gr_p, ga_p = _in_proj(xp2, nm, w_in_bf, TM_IN, BF16)
    x_sm = jnp.concatenate([x_sample.reshape(n_s, D_MODEL), meta_tokens], axis=0)
    xr_sm, q_sm, kv_sm, gr_sm, ga_sm = _in_proj(x_sm, nm, w_in_bf, n_s + N_META, F32)
    xr_s, q_s, kv_s, gr_s, ga_s = (t[:n_s] for t in (xr_sm, q_sm, kv_sm, gr_sm, ga_sm))
    xr_m, kv_m = xr_sm[n_s:], kv_sm[n_s:]

    h_meta = pl.pallas_call(
        _rnn_meta_kernel,
        in_specs=[_full((N_META, D_RNN))] + rnn_w_specs,
        out_specs=_full((1, D_RNN)),
        out_shape=jax.ShapeDtypeStruct((1, D_RNN), F32),
        grid=(1,),
        compiler_params=_cparams(1),
        name="rnn_meta",
    )(xr_m, *rnn_w)

    tt = TT_RNN
    blk3 = pl.BlockSpec((n_b, tt, D_RNN), lambda j: (0, j, 0))
    m_rnn_p, h_last_p = pl.pallas_call(
        functools.partial(_rnn_prompt_kernel, n_b=n_b, tt=tt),
        grid=(seq // tt,),
        in_specs=[blk3, blk3, _full((N_META, D_RNN)), _full((1, D_RNN))] + rnn_w_specs,
        out_specs=[blk3, _full((n_b, D_RNN))],
        out_shape=[jax.ShapeDtypeStruct((n_b, seq, D_RNN), BF16),
                   jax.ShapeDtypeStruct((n_b, D_RNN), F32)],
        scratch_shapes=[pltpu.VMEM((n_b, 8, D_RNN), F32), pltpu.VMEM((n_b, D_RNN), F32)],
        compiler_params=_cparams(1),
        name="rnn_prompt",
    )(xr_p.reshape(n_b, seq, D_RNN), gr_p.reshape(n_b, seq, D_RNN), xr_m, h_meta, *rnn_w)

    sc_t = jnp.swapaxes(state_conv[0], 0, 1)
    m_rnn_s, h_new_s = pl.pallas_call(
        _rnn_sample_kernel,
        grid=(1,),
        in_specs=[_full((n_s, D_RNN)), _full((n_s, D_RNN)), _full((CONV_W - 1, n_s, D_RNN)),
                  _full((n_s, D_RNN))] + rnn_w_specs,
        out_specs=[_full((n_s, D_RNN)), _full((n_s, D_RNN))],
        out_shape=[jax.ShapeDtypeStruct((n_s, D_RNN), BF16),
                   jax.ShapeDtypeStruct((n_s, D_RNN), F32)],
        compiler_params=_cparams(1),
        name="rnn_sample",
    )(xr_s, gr_s, sc_t, state_h[0], *rnn_w)

    n_blk_seq = seq // BLOCK
    kv_meta_blk = jnp.pad(kv_m, ((BLOCK - N_META, 0), (0, 0)))
    smem_spec = pl.BlockSpec(memory_space=pltpu.SMEM)
    rb = lambda w: pl.BlockSpec((BLOCK, w), lambda b, j: (b * n_blk_seq + j, 0))
    m_attn_p = pl.pallas_call(
        _attn_prompt_kernel,
        grid=(n_b, n_blk_seq),
        in_specs=[smem_spec, rb(D_ATTN), rb(2 * D_KV),
                  pl.BlockSpec((BLOCK, 2 * D_KV),
                               lambda b, j: (jnp.maximum(b * n_blk_seq + j - 1, 0), 0)),
                  _full((BLOCK, 2 * D_KV)),
                  pl.BlockSpec((1, N_HEADS, BLOCK, BLOCK),
                               lambda b, j: (jnp.minimum(j, 1), 0, 0, 0)),
                  rb(D_ATTN)],
        out_specs=rb(D_ATTN),
        out_shape=jax.ShapeDtypeStruct((n_p, D_ATTN), BF16),
        compiler_params=_cparams(2),
        name="attn_prompt",
    )(sinks, q_p, kv_p, kv_p, kv_meta_blk, _attn_bias_table(), ga_p)

    ck = cache_k[0].reshape(n_s, w_buf, D_KV)
    cv = cache_v[0].reshape(n_s, w_buf, D_KV)
    per_s = lambda a, b: pl.BlockSpec((SAMPLES_PER_STEP, a, b), lambda i: (i, 0, 0))
    m_attn_s = pl.pallas_call(
        _attn_sample_kernel,
        grid=(n_s // SAMPLES_PER_STEP,),
        in_specs=[smem_spec, per_s(N_HEADS, HEAD_DIM), per_s(1, 2 * D_KV), per_s(w_buf, D_KV),
                  per_s(w_buf, D_KV), per_s(N_HEADS, HEAD_DIM)],
        out_specs=per_s(N_HEADS, HEAD_DIM),
        out_shape=jax.ShapeDtypeStruct((n_s, N_HEADS, HEAD_DIM), BF16),
        compiler_params=_cparams(1),
        name="attn_sample",
    )(sinks, q_s.reshape(n_s, N_HEADS, HEAD_DIM), kv_s.reshape(n_s, 1, 2 * D_KV), ck, cv,
      ga_s.reshape(n_s, N_HEADS, HEAD_DIM)).reshape(n_s, D_ATTN)

    n_hn = n_p + -(-n_s // TM_MERGE) * TM_MERGE
    x1_p, hn_all, ti_p, tw_p = _merge(m_rnn_p.reshape(n_p, D_RNN), m_attn_p, xp2, w_out_bf, nf,
                                      wr_pad, br_pad, None, n_hn, 0, TM_MERGE)
    x1_s, hn_all, ti_s, tw_s = _merge(m_rnn_s, m_attn_s, x_sample.reshape(n_s, D_MODEL), w_out_bf,
                                      nf, wr_pad, br_pad, hn_all, n_hn, n_p // n_s, n_s)

    top_i = jnp.concatenate([ti_p[:, :TOP_K], ti_s[:, :TOP_K]], axis=0)
    n_assign = n_tok * TOP_K
    n_blk = n_assign // TM_MOE + N_EXPERTS - 1 + GATHER_AHEAD
    buf_tok, buf_asg, blk_expert, next_expert, n_valid, n_used = _route(top_i, n_blk)

    y_tok = _moe(blk_expert, next_expert, n_valid, n_used, buf_tok, buf_asg, hn_all, w_up[0],
                 b_up_perm, w_down[0], b_dn, n_blk, n_assign).reshape((TOP_K, n_tok) + ROW_TILE)
    y_p = _final(x1_p, tw_p, nfin, y_tok, 0)
    y_s = _final(x1_s, tw_s, nfin, y_tok, n_p)

    kv_p3 = kv_p.reshape(n_b, seq, 2 * D_KV)
    w_p = min(WINDOW, seq + N_META)
    new_k_p = kv_p3[:, seq - w_p:, :D_KV].reshape(1, n_b, w_p, N_KV, HEAD_DIM)
    new_v_p = kv_p3[:, seq - w_p:, D_KV:].reshape(1, n_b, w_p, N_KV, HEAD_DIM)
    new_conv_p = xr_p.reshape(n_b, seq, D_RNN)[:, seq - (CONV_W - 1):][None]
    k_new = kv_s[:, :D_KV].reshape(n_s, 1, N_KV, HEAD_DIM)
    v_new = kv_s[:, D_KV:].reshape(n_s, 1, N_KV, HEAD_DIM)
    new_k_s = jnp.concatenate([cache_k[0], k_new], axis=1)[:, -w_buf:][None]
    new_v_s = jnp.concatenate([cache_v[0], v_new], axis=1)[:, -w_buf:][None]
    new_conv_s = jnp.concatenate([state_conv[0], xr_s[:, None, :]], axis=1)[:, -(CONV_W - 1):][None]
    return (y_p.reshape(n_b, seq, D_MODEL), y_s.reshape(n_s, 1, D_MODEL), new_k_p, new_v_p,
            new_conv_p, h_last_p[None], new_k_s, new_v_s, new_conv_s, h_new_s[None])
```

```python
import functools

import jax
import jax.numpy as jnp
import numpy as np
from jax import lax
from jax.experimental import pallas as pl
from jax.experimental.pallas import tpu as pltpu

F32 = jnp.float32
BF16 = jnp.bfloat16

D_MODEL = 1024
N_META = 16
D_RNN = 1024
N_GATE_BLOCKS = 16
GATE_BW = D_RNN // N_GATE_BLOCKS
CONV_W = 4
LRU_C = 8.0
N_HEADS = 16
HEAD_DIM = 64
N_KV = 2
GROUP = N_HEADS // N_KV
D_ATTN = N_HEADS * HEAD_DIM
D_KV = N_KV * HEAD_DIM
WINDOW = 128
BLOCK = 128
PAST_LEN = 16384
N_EXPERTS = 32
TOP_K = 4
D_FF = 1024
SWIGLU_LIMIT = 7.0
SWIGLU_ALPHA = 1.702
EPS = 1e-6

GATE_TILE = 256
N_GATE_TILES = D_RNN // GATE_TILE
LANES = 128
NEG_BIG = -1e30

TM_IN = 512
TT_RNN = 64
TM_MERGE = 256
TM_MOE = 256
TM_FIN = 256
VMEM_LIMIT = 56 * 1024 * 1024


def _cparams(n_grid_dims):
    return pltpu.CompilerParams(
        dimension_semantics=("arbitrary",) * n_grid_dims, vmem_limit_bytes=VMEM_LIMIT)


def _full(shape):
    return pl.BlockSpec(shape, lambda *_: (0,) * len(shape))


SUBLANES = 8
ROW_TILE = (SUBLANES, D_MODEL // SUBLANES)


def _transpose_vreg_group(v):
    sub = lax.broadcasted_iota(jnp.int32, v[0].shape, 0)
    v = list(v)
    d = SUBLANES // 2
    while d >= 1:
        hi = (sub & d) != 0
        nv = list(v)
        for a in range(SUBLANES):
            if a & d:
                continue
            b = a | d
            nv[a] = jnp.where(hi, pltpu.roll(v[b], d, axis=0), v[a])
            nv[b] = jnp.where(hi, v[b], pltpu.roll(v[a], SUBLANES - d, axis=0))
        v = nv
        d //= 2
    return v


def _store_row_tiled(x, dst):
    for g in range(x.shape[0] // SUBLANES):
        rows = slice(g * SUBLANES, (g + 1) * SUBLANES)
        w = _transpose_vreg_group([x[rows, s * LANES:(s + 1) * LANES] for s in range(SUBLANES)])
        for j in range(SUBLANES):
            dst[g * SUBLANES + j] = w[j]


def _load_row_tiled(src, n_rows):
    groups = []
    for g in range(n_rows // SUBLANES):
        v = _transpose_vreg_group([src[g * SUBLANES + j] for j in range(SUBLANES)])
        groups.append(jnp.concatenate(v, axis=-1))
    return jnp.concatenate(groups, axis=0)


def _in_proj_kernel(x_ref, g_ref, w_ref, xr_ref, q_ref, kv_ref, gr_ref, ga_ref):
    x = x_ref[...]
    ms = jnp.mean(x * x, axis=-1, keepdims=True)
    h = (x * lax.rsqrt(ms + EPS) * g_ref[...]).astype(BF16)

    def proj(lo, hi):
        return jnp.dot(h, w_ref[:, lo:hi], preferred_element_type=F32)

    o_q = D_RNN
    o_k = o_q + D_ATTN
    o_gr = o_k + 2 * D_KV
    o_ga = o_gr + D_RNN
    xr_ref[...] = proj(0, o_q)
    q_ref[...] = (proj(o_q, o_k) * (HEAD_DIM ** -0.5)).astype(BF16)
    kv_ref[...] = proj(o_k, o_gr)
    gr_ref[...] = proj(o_gr, o_ga).astype(gr_ref.dtype)
    ga_ref[...] = proj(o_ga, o_ga + D_ATTN).astype(ga_ref.dtype)


def _in_proj(x, g, w_in_bf, tm, gate_dtype):
    rows = x.shape[0]
    d_in = w_in_bf.shape[1]
    row = lambda w: pl.BlockSpec((tm, w), lambda i: (i, 0))
    return pl.pallas_call(
        _in_proj_kernel,
        grid=(rows // tm,),
        in_specs=[row(D_MODEL), _full((1, D_MODEL)), _full((D_MODEL, d_in))],
        out_specs=[row(D_RNN), row(D_ATTN), row(2 * D_KV), row(D_RNN), row(D_ATTN)],
        out_shape=[
            jax.ShapeDtypeStruct((rows, D_RNN), F32),
            jax.ShapeDtypeStruct((rows, D_ATTN), BF16),
            jax.ShapeDtypeStruct((rows, 2 * D_KV), F32),
            jax.ShapeDtypeStruct((rows, D_RNN), gate_dtype),
            jax.ShapeDtypeStruct((rows, D_ATTN), gate_dtype),
        ],
        compiler_params=_cparams(1),
        name="in_proj",
    )(x, g, w_in_bf)


def _softplus(x):
    return jnp.maximum(x, 0.0) + jnp.log1p(jnp.exp(-jnp.abs(x)))


def _lru_coeffs(u, wg_ref, brg, big, lam):
    ub = u.astype(BF16)
    sp = _softplus(-lam)
    a_parts, i_parts, m_parts = [], [], []
    for g in range(N_GATE_TILES):
        sl = slice(g * GATE_TILE, (g + 1) * GATE_TILE)
        zz = jnp.dot(ub[:, sl], wg_ref[g], preferred_element_type=F32)
        r = jax.nn.sigmoid(zz[:, :GATE_TILE] + brg[:, sl])
        i = jax.nn.sigmoid(zz[:, GATE_TILE:] + big[:, sl])
        log_a = (-LRU_C) * r * sp[:, sl]
        a = jnp.exp(log_a)
        m = jnp.sqrt(1.0 - a * a)
        a_parts.append(a)
        i_parts.append(i)
        m_parts.append(m)
    cat = lambda ps: jnp.concatenate(ps, axis=-1)
    return cat(a_parts), cat(i_parts), cat(m_parts)


def _conv(ext, t, cw, cb):
    out = cb
    for j in range(CONV_W):
        s = CONV_W - 1 - j
        out = out + cw[j:j + 1, :] * ext[8 - s:8 - s + t, :]
    return out


def _rnn_meta_kernel(xr_ref, cw_ref, cb_ref, wg_ref, brg_ref, big_ref, lam_ref, h_ref):
    x = xr_ref[...]
    ext = jnp.concatenate([jnp.zeros((8, D_RNN), F32), x], axis=0)
    u = _conv(ext, N_META, cw_ref[...], cb_ref[...])
    a, i, m = _lru_coeffs(u, wg_ref, brg_ref[...], big_ref[...], lam_ref[...])
    first = lax.broadcasted_iota(jnp.int32, (N_META, 1), 0) == 0
    b = jnp.where(first, 1.0, m) * i * u
    h = jnp.zeros((1, D_RNN), F32)
    for t in range(N_META):
        h = a[t:t + 1, :] * h + b[t:t + 1, :]
    h_ref[...] = h


def _linear_scan(a, b, h_prev):
    t, d = a.shape
    g = t // SUBLANES
    a3 = a.reshape(g, SUBLANES, d)
    b3 = b.reshape(g, SUBLANES, d)
    row = lax.broadcasted_iota(jnp.int32, (g, SUBLANES, d), 1)
    step = 1
    while step < SUBLANES:
        keep = row >= step
        a_up = jnp.where(keep, pltpu.roll(a3, step, axis=1), 1.0)
        b_up = jnp.where(keep, pltpu.roll(b3, step, axis=1), 0.0)
        b3 = a3 * b_up + b3
        a3 = a3 * a_up
        step *= 2
    hs = []
    h = h_prev
    for k in range(g):
        hk = a3[k] * h + b3[k]
        hs.append(hk)
        h = hk[SUBLANES - 1:SUBLANES, :]
    return jnp.concatenate(hs, axis=0), h


def _rnn_prompt_kernel(xr_ref, gr_ref, xm_ref, h0_ref, cw_ref, cb_ref, wg_ref, brg_ref, big_ref,
                       lam_ref, m_ref, hl_ref, halo_s, h_s, *, n_b, tt):
    j = pl.program_id(0)

    @pl.when(j == 0)
    def _():
        h_s[...] = jnp.broadcast_to(h0_ref[...], (n_b, D_RNN))
        for b in range(n_b):
            halo_s[b] = xm_ref[N_META - 8:N_META, :]

    cw = cw_ref[...]
    cb = cb_ref[...]

    def per_batch(b, c):
        x = xr_ref[b]
        ext = jnp.concatenate([halo_s[b], x], axis=0)
        halo_s[b] = x[tt - 8:tt, :]
        u = _conv(ext, tt, cw, cb)
        a, i, m = _lru_coeffs(u, wg_ref, brg_ref[...], big_ref[...], lam_ref[...])
        y, h = _linear_scan(a, m * i * u, h_s[pl.ds(b, 1), :])
        h_s[pl.ds(b, 1), :] = h
        m_ref[b] = (jax.nn.sigmoid(gr_ref[b].astype(F32)) * y).astype(BF16)
        return c

    lax.fori_loop(0, n_b, per_batch, 0)
    hl_ref[...] = h_s[...]


def _rnn_sample_kernel(xr_ref, gr_ref, sc_ref, h0_ref, cw_ref, cb_ref, wg_ref, brg_ref, big_ref,
                       lam_ref, m_ref, hn_ref):
    cw = cw_ref[...]
    u = cb_ref[...] + cw[CONV_W - 1:CONV_W, :] * xr_ref[...]
    for j in range(CONV_W - 1):
        u = u + cw[j:j + 1, :] * sc_ref[j]
    a, i, m = _lru_coeffs(u, wg_ref, brg_ref[...], big_ref[...], lam_ref[...])
    h = a * h0_ref[...] + m * i * u
    hn_ref[...] = h
    m_ref[...] = (jax.nn.sigmoid(gr_ref[...].astype(F32)) * h).astype(BF16)


def _alibi_slope(h):
    return 2.0 ** (-8.0 * (h + 1) / N_HEADS)


def _attn_bias_table():
    assert WINDOW == BLOCK
    qi = np.arange(BLOCK, dtype=np.int32)[:, None]
    ci = np.arange(BLOCK, dtype=np.int32)[None, :]
    own = ci <= qi
    dist = np.where(own, qi - ci, qi + BLOCK - ci).astype(np.float32)
    slopes = np.asarray([_alibi_slope(h) for h in range(N_HEADS)], np.float32)[:, None, None]
    ali = -slopes * dist[None]
    tables = [np.where((own | (ci >= c_min))[None], ali, np.float32(NEG_BIG))
              for c_min in (BLOCK - N_META, 0)]
    return jnp.asarray(np.stack(tables).astype(np.float32))


def _attn_prompt_kernel(sink_ref, q_ref, kc_ref, kp_ref, km_ref, bias_ref, ga_ref, o_ref):
    first = pl.program_id(1) == 0
    kv_prev = jnp.where(first, km_ref[...], kp_ref[...])
    kv = jnp.concatenate([kv_prev, kc_ref[...]], axis=0).astype(BF16)
    q = q_ref[...]
    qi = lax.broadcasted_iota(jnp.int32, (BLOCK, BLOCK), 0)
    ci = lax.broadcasted_iota(jnp.int32, (BLOCK, BLOCK), 1)
    own = ci <= qi
    zero = jnp.zeros((BLOCK, BLOCK), BF16)
    outs = []
    for h in range(N_HEADS):
        g = h // GROUP
        kh = kv[:, g * HEAD_DIM:(g + 1) * HEAD_DIM]
        vh = kv[:, D_KV + g * HEAD_DIM:D_KV + (g + 1) * HEAD_DIM]
        qh = q[:, h * HEAD_DIM:(h + 1) * HEAD_DIM]
        s2 = lax.dot_general(qh, kh, (((1,), (1,)), ((), ())), preferred_element_type=F32)
        s = jnp.where(own, s2[:, BLOCK:], s2[:, :BLOCK]) + bias_ref[0, h]
        sink = sink_ref[h]
        mx = jnp.maximum(jnp.max(s, axis=-1, keepdims=True), sink)
        p = jnp.exp(s - mx)
        den = jnp.sum(p, axis=-1, keepdims=True) + jnp.exp(sink - mx)
        pb = p.astype(BF16)
        p2 = jnp.concatenate([jnp.where(own, zero, pb), jnp.where(own, pb, zero)], axis=1)
        o = jnp.dot(p2, vh, preferred_element_type=F32)
        outs.append(o / den)
    y = jnp.concatenate(outs, axis=-1)
    o_ref[...] = (jax.nn.sigmoid(ga_ref[...].astype(F32)) * y).astype(BF16)


SAMPLES_PER_STEP = 8


def _attn_sample_kernel(sink_ref, q_ref, kvn_ref, ck_ref, cv_ref, ga_ref, o_ref):
    w_buf = ck_ref.shape[1]
    n_rows = 2 * N_HEADS
    n_blocks = 2 * N_KV
    ci = lax.broadcasted_iota(jnp.int32, (n_rows, w_buf), 1)
    dist = w_buf - ci
    valid = dist < WINDOW
    row = lax.broadcasted_iota(jnp.int32, (n_rows, 1), 0)
    slope = jnp.zeros((n_rows, 1), F32)
    sink = jnp.zeros((n_rows, 1), F32)
    for h in range(N_HEADS):
        slope = jnp.where(row % N_HEADS == h, _alibi_slope(h), slope)
        sink = jnp.where(row % N_HEADS == h, sink_ref[h], sink)
    bias = jnp.where(valid, -slope * dist.astype(F32), NEG_BIG)
    row_block = lax.broadcasted_iota(jnp.int32, (n_rows, n_blocks * HEAD_DIM), 0) // GROUP
    lane_block = lax.broadcasted_iota(jnp.int32, (n_rows, n_blocks * HEAD_DIM), 1) // HEAD_DIM
    own_block = row_block == lane_block
    for n in range(0, SAMPLES_PER_STEP, 2):
        q2 = jnp.concatenate([q_ref[n], q_ref[n + 1]], axis=0)
        qd = jnp.where(own_block, jnp.concatenate([q2] * n_blocks, axis=1), jnp.zeros((), BF16))
        kk = jnp.concatenate([ck_ref[n], ck_ref[n + 1]], axis=1).astype(BF16)
        vv = jnp.concatenate([cv_ref[n], cv_ref[n + 1]], axis=1).astype(BF16)
        kvn = [kvn_ref[n + t].astype(BF16).astype(F32) for t in range(2)]
        k_new = jnp.concatenate([kvn[0][:, :D_KV], kvn[1][:, :D_KV]], axis=1)
        v_new = jnp.concatenate([kvn[0][:, D_KV:], kvn[1][:, D_KV:]], axis=1)
        s = lax.dot_general(qd, kk, (((1,), (1,)), ((), ())), preferred_element_type=F32) + bias
        sn = jnp.sum(qd.astype(F32) * k_new, axis=-1, keepdims=True)
        mx = jnp.maximum(jnp.maximum(jnp.max(s, axis=-1, keepdims=True), sn), sink)
        p = jnp.exp(s - mx)
        pn = jnp.exp(sn - mx)
        den = jnp.sum(p, axis=-1, keepdims=True) + pn + jnp.exp(sink - mx)
        o = jnp.dot(p.astype(BF16), vv, preferred_element_type=F32)
        o = o + pn.astype(BF16).astype(F32) * v_new
        y = jnp.concatenate([o[j * GROUP:(j + 1) * GROUP, j * HEAD_DIM:(j + 1) * HEAD_DIM]
                             for j in range(n_blocks)], axis=0) / den
        for t in range(2):
            rows = slice(t * N_HEADS, (t + 1) * N_HEADS)
            o_ref[n + t] = (jax.nn.sigmoid(ga_ref[n + t].astype(F32)) * y[rows]).astype(BF16)


def _merge_kernel(mr_ref, ma_ref, x_ref, wo_ref, nf_ref, wr_ref, br_ref, *rest, n_steps):
    x1_ref, hn_ref, ti_ref, tw_ref = rest[-4:]

    @pl.when(pl.program_id(0) >= n_steps)
    def _():
        hn_ref[...] = jnp.zeros_like(hn_ref)

    @pl.when(pl.program_id(0) < n_steps)
    def _():
        _merge_tile(mr_ref, ma_ref, x_ref, wo_ref, nf_ref, wr_ref, br_ref, x1_ref, hn_ref, ti_ref,
                    tw_ref)


def _merge_tile(mr_ref, ma_ref, x_ref, wo_ref, nf_ref, wr_ref, br_ref, x1_ref, hn_ref, ti_ref,
                tw_ref):
    mm = jnp.dot(mr_ref[...], wo_ref[:D_RNN, :], preferred_element_type=F32)
    mm = mm + jnp.dot(ma_ref[...], wo_ref[D_RNN:, :], preferred_element_type=F32)
    x1 = x_ref[...] + mm
    x1_ref[...] = x1
    ms = jnp.mean(x1 * x1, axis=-1, keepdims=True)
    hn = x1 * lax.rsqrt(ms + EPS) * nf_ref[...]
    _store_row_tiled(hn, hn_ref)
    tm = hn.shape[0]
    l = lax.dot_general(wr_ref[...], hn.astype(BF16), (((1,), (1,)), ((), ())),
                        preferred_element_type=F32) + br_ref[...]
    eid = lax.broadcasted_iota(jnp.int32, l.shape, 0)
    vals, idxs = [], []
    for k in range(TOP_K):
        mx = jnp.max(l, axis=0, keepdims=True)
        idx = jnp.min(jnp.where(l == mx, eid, N_EXPERTS), axis=0, keepdims=True)
        vals.append(mx)
        idxs.append(idx)
        l = jnp.where(eid == idx, NEG_BIG, l)
    es = [jnp.exp(v - vals[0]) for v in vals]
    tot = es[0] + es[1] + es[2] + es[3]
    pad = SUBLANES - TOP_K
    ti_ref[...] = jnp.concatenate(idxs + [jnp.zeros((pad, tm), jnp.int32)], axis=0)
    tw_t = jnp.concatenate([e / tot for e in es] + [jnp.zeros((pad, tm), F32)], axis=0)
    eye = (lax.broadcasted_iota(jnp.int32, (tm, tm), 0)
           == lax.broadcasted_iota(jnp.int32, (tm, tm), 1)).astype(F32)
    tw_rows = lax.dot_general(eye, tw_t, (((1,), (1,)), ((), ())), preferred_element_type=F32,
                              precision=lax.Precision.HIGHEST)
    tw_ref[...] = jnp.concatenate([tw_rows, jnp.zeros((tm, LANES - SUBLANES), F32)], axis=1)


def _merge(m_rnn, m_attn, x, w_out_bf, nf, wr_pad, br_pad, hn_all, n_all, row_block0, tm):
    rows = x.shape[0]
    n_steps = rows // tm
    n_zero_steps = 0 if hn_all is not None else n_all // tm - n_steps
    row = lambda w: pl.BlockSpec((tm, w), lambda i: (jnp.minimum(i, n_steps - 1), 0))
    in_specs = [row(D_RNN), row(D_ATTN), row(D_MODEL), _full((D_RNN + D_ATTN, D_MODEL)),
                _full((1, D_MODEL)), _full((N_EXPERTS, D_MODEL)), _full((N_EXPERTS, 1))]
    args = [m_rnn, m_attn, x, w_out_bf, nf, wr_pad, br_pad]
    aliases = {}
    if hn_all is not None:
        in_specs.append(pl.BlockSpec(memory_space=pl.ANY))
        args.append(hn_all)
        aliases = {len(args) - 1: 1}
    return pl.pallas_call(
        functools.partial(_merge_kernel, n_steps=n_steps),
        grid=(n_steps + n_zero_steps,),
        in_specs=in_specs,
        out_specs=[row(D_MODEL),
                   pl.BlockSpec((tm,) + ROW_TILE, lambda i: (i + row_block0, 0, 0)),
                   pl.BlockSpec((SUBLANES, tm), lambda i: (0, jnp.minimum(i, n_steps - 1))),
                   row(LANES)],
        out_shape=[
            jax.ShapeDtypeStruct((rows, D_MODEL), F32),
            jax.ShapeDtypeStruct((n_all,) + ROW_TILE, F32),
            jax.ShapeDtypeStruct((SUBLANES, rows), jnp.int32),
            jax.ShapeDtypeStruct((rows, LANES), F32),
        ],
        input_output_aliases=aliases,
        compiler_params=_cparams(1),
        name="merge_router",
    )(*args)


UP_TILE = 256


def _deinterleave_matrix():
    half = UP_TILE // 2
    r = lax.broadcasted_iota(jnp.int32, (UP_TILE, UP_TILE), 0)
    c = lax.broadcasted_iota(jnp.int32, (UP_TILE, UP_TILE), 1)
    src = jnp.where(c < half, 2 * c, 2 * (c - half) + 1)
    return jnp.where(r == src, 1.0, 0.0).astype(BF16)


GATHER_AHEAD = 2
GATHER_DMA_PRIORITY = 0
OTHER_DMA_PRIORITY = 1


def _moe_kernel(be_ref, nxt_ref, nval_ref, nu_ref, tok0_ref, tok1_ref, tok2_ref, asg_ref, hn_hbm,
                wu_hbm, bu_ref, wd_hbm, bd_ref, out_hbm, xbuf, xb_s, ystage, wu_f, wd_f, wu_s, wd_s,
                sems, ssems, wsems):
    i = pl.program_id(0)
    nu = nu_ref[0]
    half = UP_TILE // 2
    n_up = 2 * D_FF // UP_TILE
    slot = i % GATHER_AHEAD
    prev_slot = 1 - slot
    n_prev = nval_ref[i]

    def row_copy(tok_ref, r, s):
        return pltpu.make_async_copy(hn_hbm.at[pl.ds(tok_ref[0, 0, r], 1)],
                                     xbuf.at[s, pl.ds(r, 1)], sems.at[s])

    def weight_copies(e):
        return (pltpu.make_async_copy(wu_hbm.at[e], wu_f, wsems.at[0]),
                pltpu.make_async_copy(wd_hbm.at[e], wd_f, wsems.at[1]))

    def out_copy(r):
        return pltpu.make_async_copy(ystage.at[prev_slot, pl.ds(r, 1)],
                                     out_hbm.at[pl.ds(asg_ref[0, 0, r], 1)], ssems.at[prev_slot])

    @pl.when(i == 0)
    def _():
        for cp in weight_copies(be_ref[0]):
            cp.start(priority=OTHER_DMA_PRIORITY)

        def issue(r, c):
            row_copy(tok0_ref, r, 0).start(priority=GATHER_DMA_PRIORITY)
            row_copy(tok1_ref, r, 1).start(priority=GATHER_DMA_PRIORITY)
            return c

        lax.fori_loop(0, TM_MOE, issue, 0)

    @pl.when(i < nu + GATHER_AHEAD)
    def _():
        pltpu.make_async_copy(hn_hbm.at[pl.ds(0, TM_MOE)], xbuf.at[slot], sems.at[slot]).wait()

    @pl.when((i >= 2) & (i < nu + 2))
    def _():
        n_sent = nval_ref[i - 1]
        n_tiled = pl.multiple_of((n_sent // SUBLANES) * SUBLANES, SUBLANES)

        @pl.when(n_tiled > 0)
        def _():
            pltpu.make_async_copy(ystage.at[slot, pl.ds(0, n_tiled)],
                                  out_hbm.at[pl.ds(0, n_tiled)], ssems.at[slot]).wait()

        def wait_row(r, c):
            pltpu.make_async_copy(ystage.at[slot, pl.ds(0, 1)], out_hbm.at[pl.ds(0, 1)],
                                  ssems.at[slot]).wait()
            return c

        lax.fori_loop(0, n_sent - n_tiled, wait_row, 0)

    @pl.when(i < nu)
    def _():
        xb_s[...] = _load_row_tiled(xbuf.at[slot], TM_MOE).astype(BF16)

        @pl.when((i == 0) | (be_ref[i] != be_ref[jnp.maximum(i - 1, 0)]))
        def _():
            for cp in weight_copies(be_ref[i]):
                cp.wait()
            perm = _deinterleave_matrix()
            for c in range(n_up):
                cols = slice(c * UP_TILE, (c + 1) * UP_TILE)
                blk = wu_f[:, cols].astype(BF16)
                wu_s[:, cols] = jnp.dot(blk, perm, preferred_element_type=F32).astype(BF16)
            wd_s[...] = wd_f[...].astype(BF16)

            @pl.when(nxt_ref[i] >= 0)
            def _():
                for cp in weight_copies(nxt_ref[i]):
                    cp.start(priority=OTHER_DMA_PRIORITY)

    def send_some_rows():
        def send(r, c):
            out_copy(r).start(priority=OTHER_DMA_PRIORITY)
            return c

        lax.fori_loop(0, n_prev, send, 0)

    def run_block(send_all_rows):
        for r in range(TM_MOE):
            row_copy(tok2_ref, r, slot).start(priority=GATHER_DMA_PRIORITY)
            if send_all_rows:
                out_copy(r).start(priority=OTHER_DMA_PRIORITY)
        x = xb_s[...]
        z = jnp.dot(x, wu_s[...], preferred_element_type=F32) + bu_ref[0]
        zg = jnp.concatenate([z[:, c * UP_TILE:c * UP_TILE + half] for c in range(n_up)], axis=-1)
        zl = jnp.concatenate([z[:, c * UP_TILE + half:(c + 1) * UP_TILE] for c in range(n_up)],
                             axis=-1)
        xg = jnp.minimum(zg, SWIGLU_LIMIT)
        xl = jnp.clip(zl, -SWIGLU_LIMIT, SWIGLU_LIMIT)
        act = xg * jax.nn.sigmoid(SWIGLU_ALPHA * xg) * (xl + 1.0)
        y = jnp.dot(act.astype(BF16), wd_s[...], preferred_element_type=F32) + bd_ref[0]
        _store_row_tiled(y, ystage.at[slot])

    @pl.when((i < nu) & (n_prev == TM_MOE))
    def _():
        run_block(send_all_rows=True)

    @pl.when((i < nu) & (n_prev != TM_MOE))
    def _():
        send_some_rows()
        run_block(send_all_rows=False)

    @pl.when(i == nu)
    def _():
        send_some_rows()


def _moe(blk_expert, next_expert, n_valid, n_used, buf_tok, buf_asg, hn_all, w_up, b_up_perm,
         w_down, b_down, n_blk, n_assign):
    ew = lambda r, c: pl.BlockSpec((1, r, c), lambda i, be, nxt, nv, nu: (be[i], 0, 0))
    tok = lambda f: pl.BlockSpec((1, 1, TM_MOE), lambda i, be, nxt, nv, nu: (f(i), 0, 0),
                                 memory_space=pltpu.SMEM)
    hbm = pl.BlockSpec(memory_space=pl.ANY)
    return pl.pallas_call(
        _moe_kernel,
        grid_spec=pltpu.PrefetchScalarGridSpec(
            num_scalar_prefetch=4,
            grid=(n_blk,),
            in_specs=[
                tok(lambda i: i), tok(lambda i: jnp.minimum(i + 1, n_blk - 1)),
                tok(lambda i: jnp.minimum(i + GATHER_AHEAD, n_blk - 1)), tok(lambda i: i),
                hbm, hbm, ew(1, 2 * D_FF), hbm, ew(1, D_MODEL),
            ],
            out_specs=hbm,
            scratch_shapes=[pltpu.VMEM((GATHER_AHEAD, TM_MOE) + ROW_TILE, F32),
                            pltpu.VMEM((TM_MOE, D_MODEL), BF16),
                            pltpu.VMEM((2, TM_MOE) + ROW_TILE, F32),
                            pltpu.VMEM((D_MODEL, 2 * D_FF), F32),
                            pltpu.VMEM((D_FF, D_MODEL), F32),
                            pltpu.VMEM((D_MODEL, 2 * D_FF), BF16),
                            pltpu.VMEM((D_FF, D_MODEL), BF16),
                            pltpu.SemaphoreType.DMA((GATHER_AHEAD,)),
                            pltpu.SemaphoreType.DMA((2,)),
                            pltpu.SemaphoreType.DMA((2,))],
        ),
        out_shape=jax.ShapeDtypeStruct((n_assign,) + ROW_TILE, F32),
        compiler_params=_cparams(1),
        name="moe_experts",
    )(blk_expert, next_expert, n_valid, n_used, buf_tok, buf_tok, buf_tok, buf_asg, hn_all, w_up,
      b_up_perm, w_down, b_down)


def _final_kernel(x1_ref, tw_ref, nf_ref, *rest):
    y_refs, o_ref = rest[:TOP_K], rest[TOP_K]
    tw = tw_ref[...]
    x = x1_ref[...]
    for k in range(TOP_K):
        x = x + tw[:, k:k + 1] * _load_row_tiled(y_refs[k].at[0], x.shape[0])
    ms = jnp.mean(x * x, axis=-1, keepdims=True)
    o_ref[...] = x * lax.rsqrt(ms + EPS) * nf_ref[...]


def _final(x1, tw, nf, y_choice, row0):
    rows = x1.shape[0]
    tm = min(TM_FIN, rows)
    row_block0 = row0 // tm
    row = lambda w: pl.BlockSpec((tm, w), lambda i: (i, 0))
    y_spec = lambda k: pl.BlockSpec((1, tm) + ROW_TILE, lambda i: (k, i + row_block0, 0, 0))
    return pl.pallas_call(
        _final_kernel,
        grid=(rows // tm,),
        in_specs=[row(D_MODEL), row(LANES), _full((1, D_MODEL))]
        + [y_spec(k) for k in range(TOP_K)],
        out_specs=row(D_MODEL),
        out_shape=jax.ShapeDtypeStruct((rows, D_MODEL), F32),
        compiler_params=_cparams(1),
        name="combine_final",
    )(x1, tw, nf, *([y_choice] * TOP_K))


def _block_diag_tiles(w):
    per = GATE_TILE // GATE_BW
    w4 = w.reshape(N_GATE_TILES, per, GATE_BW, GATE_BW)
    eye = jnp.eye(per, dtype=w.dtype)
    return jnp.einsum("gacd,ab->gacbd", w4, eye).reshape(N_GATE_TILES, GATE_TILE, GATE_TILE)


def _route(top_i, n_blk):
    a = top_i.size
    flat_e = top_i.reshape(-1)
    shift = (a - 1).bit_length()
    keys = (flat_e << shift) | jnp.arange(a, dtype=jnp.int32)
    sorted_n = jnp.sort(keys) & ((1 << shift) - 1)
    onehot = (flat_e[:, None] == jnp.arange(N_EXPERTS, dtype=jnp.int32)[None, :]).astype(jnp.int32)
    counts = jnp.sum(onehot, axis=0)
    padded = ((counts + (TM_MOE - 1)) // TM_MOE) * TM_MOE
    ends = jnp.cumsum(padded).astype(jnp.int32)
    start_padded = ends - padded
    start_sorted = (jnp.cumsum(counts) - counts).astype(jnp.int32)
    blk_start = jnp.arange(n_blk, dtype=jnp.int32) * TM_MOE
    blk_expert = jnp.minimum(
        jnp.sum((ends[None, :] <= blk_start[:, None]).astype(jnp.int32), axis=1), N_EXPERTS - 1)
    n_used = (ends[-1:] // TM_MOE).astype(jnp.int32)
    e_ids = jnp.arange(N_EXPERTS, dtype=jnp.int32)
    blk_onehot = blk_expert[:, None] == e_ids[None, :]

    def per_block(table):
        return jnp.sum(jnp.where(blk_onehot, table[None, :], 0), axis=1).astype(jnp.int32)

    n_valid = jnp.clip(per_block(start_padded + counts) - blk_start, 0, TM_MOE).astype(jnp.int32)
    n_tokens = top_i.shape[1]
    row_shift = jnp.repeat(per_block(start_padded - start_sorted), TM_MOE)
    pos = jnp.clip(jnp.arange(n_blk * TM_MOE, dtype=jnp.int32) - row_shift, 0, a - 1)
    buf_asg = sorted_n[pos]
    buf_tok = buf_asg % n_tokens
    n_valid = jnp.concatenate([jnp.zeros((1,), jnp.int32), n_valid])
    buf_asg = jnp.concatenate([jnp.zeros((TM_MOE,), jnp.int32), buf_asg])
    later_used = (e_ids[None, :] > e_ids[:, None]) & (counts[None, :] > 0)
    next_used = jnp.min(jnp.where(later_used, e_ids[None, :], N_EXPERTS), axis=1)
    next_used = jnp.where(next_used == N_EXPERTS, -1, next_used).astype(jnp.int32)
    return (buf_tok.reshape(n_blk, 1, TM_MOE), buf_asg.reshape(n_blk + 1, 1, TM_MOE),
            blk_expert.astype(jnp.int32), per_block(next_used), n_valid, n_used)


def kernel(x_prompt, x_sample, cache_k, cache_v, state_conv, state_h, meta_tokens, norm_mix, w_in, conv_w, conv_b, w_rg, b_rg, w_ig, b_ig, lru_lambda, attn_sinks, w_out, norm_ffn, w_router, b_router, w_up, b_up, w_down, b_down, norm_final):
    n_b, seq = x_prompt.shape[0], x_prompt.shape[1]
    n_s = x_sample.shape[0]
    w_buf = cache_k.shape[2]
    n_p = n_b * seq
    n_tok = n_p + n_s
    row1 = lambda v: v.reshape(1, -1)

    w_in_bf = w_in[0].astype(BF16)
    w_out_bf = w_out[0].astype(BF16)
    wg_tiles = jnp.concatenate([_block_diag_tiles(w_rg[0]), _block_diag_tiles(w_ig[0])],
                               axis=-1).astype(BF16)
    wr_pad = w_router[0].T.astype(BF16)
    br_pad = b_router[0].reshape(N_EXPERTS, 1)
    b_up_perm = jnp.swapaxes(b_up[0].reshape(N_EXPERTS, 2 * D_FF // UP_TILE, UP_TILE // 2, 2),
                             2, 3).reshape(N_EXPERTS, 1, 2 * D_FF)
    b_dn = b_down[0].reshape(N_EXPERTS, 1, D_MODEL)
    nm, nf, nfin = row1(norm_mix[0]), row1(norm_ffn[0]), row1(norm_final)
    cw, cb = conv_w[0], row1(conv_b[0])
    brg, big, lam = row1(b_rg[0]), row1(b_ig[0]), row1(lru_lambda[0])
    sinks = attn_sinks[0]
    rnn_w = (cw, cb, wg_tiles, brg, big, lam)
    rnn_w_specs = [_full((CONV_W, D_RNN)), _full((1, D_RNN)),
                   _full((N_GATE_TILES, GATE_TILE, 2 * GATE_TILE)),
                   _full((1, D_RNN)), _full((1, D_RNN)), _full((1, D_RNN))]

    xp2 = x_prompt.reshape(n_p, D_MODEL)
    xr_p, q_p, kv_p, gr_p, ga_p = _in_proj(xp2, nm, w_in_bf, TM_IN, BF16)
    x_sm = jnp.concatenate([x_sample.reshape(n_s, D_MODEL), meta_tokens], axis=0)
    xr_sm, q_sm, kv_sm, gr_sm, ga_sm = _in_proj(x_sm, nm, w_in_bf, n_s + N_META, F32)
    xr_s, q_s, kv_s, gr_s, ga_s = (t[:n_s] for t in (xr_sm, q_sm, kv_sm, gr_sm, ga_sm))
    xr_m, kv_m = xr_sm[n_s:], kv_sm[n_s:]

    h_meta = pl.pallas_call(
        _rnn_meta_kernel,
        in_specs=[_full((N_META, D_RNN))] + rnn_w_specs,
        out_specs=_full((1, D_RNN)),
        out_shape=jax.ShapeDtypeStruct((1, D_RNN), F32),
        grid=(1,),
        compiler_params=_cparams(1),
        name="rnn_meta",
    )(xr_m, *rnn_w)

    tt = TT_RNN
    blk3 = pl.BlockSpec((n_b, tt, D_RNN), lambda j: (0, j, 0))
    m_rnn_p, h_last_p = pl.pallas_call(
        functools.partial(_rnn_prompt_kernel, n_b=n_b, tt=tt),
        grid=(seq // tt,),
        in_specs=[blk3, blk3, _full((N_META, D_RNN)), _full((1, D_RNN))] + rnn_w_specs,
        out_specs=[blk3, _full((n_b, D_RNN))],
        out_shape=[jax.ShapeDtypeStruct((n_b, seq, D_RNN), BF16),
                   jax.ShapeDtypeStruct((n_b, D_RNN), F32)],
        scratch_shapes=[pltpu.VMEM((n_b, 8, D_RNN), F32), pltpu.VMEM((n_b, D_RNN), F32)],
        compiler_params=_cparams(1),
        name="rnn_prompt",
    )(xr_p.reshape(n_b, seq, D_RNN), gr_p.reshape(n_b, seq, D_RNN), xr_m, h_meta, *rnn_w)

    sc_t = jnp.swapaxes(state_conv[0], 0, 1)
    m_rnn_s, h_new_s = pl.pallas_call(
        _rnn_sample_kernel,
        grid=(1,),
        in_specs=[_full((n_s, D_RNN)), _full((n_s, D_RNN)), _full((CONV_W - 1, n_s, D_RNN)),
                  _full((n_s, D_RNN))] + rnn_w_specs,
        out_specs=[_full((n_s, D_RNN)), _full((n_s, D_RNN))],
        out_shape=[jax.ShapeDtypeStruct((n_s, D_RNN), BF16),
                   jax.ShapeDtypeStruct((n_s, D_RNN), F32)],
        compiler_params=_cparams(1),
        name="rnn_sample",
    )(xr_s, gr_s, sc_t, state_h[0], *rnn_w)

    n_blk_seq = seq // BLOCK
    kv_meta_blk = jnp.pad(kv_m, ((BLOCK - N_META, 0), (0, 0)))
    smem_spec = pl.BlockSpec(memory_space=pltpu.SMEM)
    rb = lambda w: pl.BlockSpec((BLOCK, w), lambda b, j: (b * n_blk_seq + j, 0))
    m_attn_p = pl.pallas_call(
        _attn_prompt_kernel,
        grid=(n_b, n_blk_seq),
        in_specs=[smem_spec, rb(D_ATTN), rb(2 * D_KV),
                  pl.BlockSpec((BLOCK, 2 * D_KV),
                               lambda b, j: (jnp.maximum(b * n_blk_seq + j - 1, 0), 0)),
                  _full((BLOCK, 2 * D_KV)),
                  pl.BlockSpec((1, N_HEADS, BLOCK, BLOCK),
                               lambda b, j: (jnp.minimum(j, 1), 0, 0, 0)),
                  rb(D_ATTN)],
        out_specs=rb(D_ATTN),
        out_shape=jax.ShapeDtypeStruct((n_p, D_ATTN), BF16),
        compiler_params=_cparams(2),
        name="attn_prompt",
    )(sinks, q_p, kv_p, kv_p, kv_meta_blk, _attn_bias_table(), ga_p)

    ck = cache_k[0].reshape(n_s, w_buf, D_KV)
    cv = cache_v[0].reshape(n_s, w_buf, D_KV)
    per_s = lambda a, b: pl.BlockSpec((SAMPLES_PER_STEP, a, b), lambda i: (i, 0, 0))
    m_attn_s = pl.pallas_call(
        _attn_sample_kernel,
        grid=(n_s // SAMPLES_PER_STEP,),
        in_specs=[smem_spec, per_s(N_HEADS, HEAD_DIM), per_s(1, 2 * D_KV), per_s(w_buf, D_KV),
                  per_s(w_buf, D_KV), per_s(N_HEADS, HEAD_DIM)],
        out_specs=per_s(N_HEADS, HEAD_DIM),
        out_shape=jax.ShapeDtypeStruct((n_s, N_HEADS, HEAD_DIM), BF16),
        compiler_params=_cparams(1),
        name="attn_sample",
    )(sinks, q_s.reshape(n_s, N_HEADS, HEAD_DIM), kv_s.reshape(n_s, 1, 2 * D_KV), ck, cv,
      ga_s.reshape(n_s, N_HEADS, HEAD_DIM)).reshape(n_s, D_ATTN)

    n_hn = n_p + -(-n_s // TM_MERGE) * TM_MERGE
    x1_p, hn_all, ti_p, tw_p = _merge(m_rnn_p.reshape(n_p, D_RNN), m_attn_p, xp2, w_out_bf, nf,
                                      wr_pad, br_pad, None, n_hn, 0, TM_MERGE)
    x1_s, hn_all, ti_s, tw_s = _merge(m_rnn_s, m_attn_s, x_sample.reshape(n_s, D_MODEL), w_out_bf,
                                      nf, wr_pad, br_pad, hn_all, n_hn, n_p // n_s, n_s)

    top_i = jnp.concatenate([ti_p[:TOP_K], ti_s[:TOP_K]], axis=1)
    n_assign = n_tok * TOP_K
    n_blk = n_assign // TM_MOE + N_EXPERTS - 1 + GATHER_AHEAD
    buf_tok, buf_asg, blk_expert, next_expert, n_valid, n_used = _route(top_i, n_blk)

    y_tok = _moe(blk_expert, next_expert, n_valid, n_used, buf_tok, buf_asg, hn_all, w_up[0],
                 b_up_perm, w_down[0], b_dn, n_blk, n_assign).reshape((TOP_K, n_tok) + ROW_TILE)
    y_p = _final(x1_p, tw_p, nfin, y_tok, 0)
    y_s = _final(x1_s, tw_s, nfin, y_tok, n_p)

    kv_p3 = kv_p.reshape(n_b, seq, 2 * D_KV)
    w_p = min(WINDOW, seq + N_META)
    new_k_p = kv_p3[:, seq - w_p:, :D_KV].reshape(1, n_b, w_p, N_KV, HEAD_DIM)
    new_v_p = kv_p3[:, seq - w_p:, D_KV:].reshape(1, n_b, w_p, N_KV, HEAD_DIM)
    new_conv_p = xr_p.reshape(n_b, seq, D_RNN)[:, seq - (CONV_W - 1):][None]
    k_new = kv_s[:, :D_KV].reshape(n_s, 1, N_KV, HEAD_DIM)
    v_new = kv_s[:, D_KV:].reshape(n_s, 1, N_KV, HEAD_DIM)
    new_k_s = jnp.concatenate([cache_k[0], k_new], axis=1)[:, -w_buf:][None]
    new_v_s = jnp.concatenate([cache_v[0], v_new], axis=1)[:, -w_buf:][None]
    new_conv_s = jnp.concatenate([state_conv[0], xr_s[:, None, :]], axis=1)[:, -(CONV_W - 1):][None]
    return (y_p.reshape(n_b, seq, D_MODEL), y_s.reshape(n_s, 1, D_MODEL), new_k_p, new_v_p,
            new_conv_p, h_last_p[None], new_k_s, new_v_s, new_conv_s, h_new_s[None])
```

```python
import functools

import jax
import jax.numpy as jnp
import numpy as np
from jax import lax
from jax.experimental import pallas as pl
from jax.experimental.pallas import tpu as pltpu

F32 = jnp.float32
BF16 = jnp.bfloat16

D_MODEL = 1024
N_META = 16
D_RNN = 1024
N_GATE_BLOCKS = 16
GATE_BW = D_RNN // N_GATE_BLOCKS
CONV_W = 4
LRU_C = 8.0
N_HEADS = 16
HEAD_DIM = 64
N_KV = 2
GROUP = N_HEADS // N_KV
D_ATTN = N_HEADS * HEAD_DIM
D_KV = N_KV * HEAD_DIM
WINDOW = 128
BLOCK = 128
PAST_LEN = 16384
N_EXPERTS = 32
TOP_K = 4
D_FF = 1024
SWIGLU_LIMIT = 7.0
SWIGLU_ALPHA = 1.702
EPS = 1e-6

GATE_TILE = 256
N_GATE_TILES = D_RNN // GATE_TILE
LANES = 128
NEG_BIG = -1e30

TM_IN = 512
TT_RNN = 128
TM_MERGE = 256
TM_MOE = 256
TM_FIN = 256
VMEM_LIMIT = 56 * 1024 * 1024


def _cparams(n_grid_dims):
    return pltpu.CompilerParams(
        dimension_semantics=("arbitrary",) * n_grid_dims, vmem_limit_bytes=VMEM_LIMIT)


def _full(shape):
    return pl.BlockSpec(shape, lambda *_: (0,) * len(shape))


SUBLANES = 8
ROW_TILE = (SUBLANES, D_MODEL // SUBLANES)


def _transpose_vreg_group(v):
    sub = lax.broadcasted_iota(jnp.int32, v[0].shape, 0)
    v = list(v)
    d = SUBLANES // 2
    while d >= 1:
        hi = (sub & d) != 0
        nv = list(v)
        for a in range(SUBLANES):
            if a & d:
                continue
            b = a | d
            nv[a] = jnp.where(hi, pltpu.roll(v[b], d, axis=0), v[a])
            nv[b] = jnp.where(hi, v[b], pltpu.roll(v[a], SUBLANES - d, axis=0))
        v = nv
        d //= 2
    return v


def _store_row_tiled(x, dst):
    for g in range(x.shape[0] // SUBLANES):
        rows = slice(g * SUBLANES, (g + 1) * SUBLANES)
        w = _transpose_vreg_group([x[rows, s * LANES:(s + 1) * LANES] for s in range(SUBLANES)])
        for j in range(SUBLANES):
            dst[g * SUBLANES + j] = w[j]


def _load_row_tiled(src, n_rows):
    groups = []
    for g in range(n_rows // SUBLANES):
        v = _transpose_vreg_group([src[g * SUBLANES + j] for j in range(SUBLANES)])
        groups.append(jnp.concatenate(v, axis=-1))
    return jnp.concatenate(groups, axis=0)


def _in_proj_kernel(x_ref, g_ref, w_ref, xr_ref, q_ref, kv_ref, gr_ref, ga_ref):
    x = x_ref[...]
    ms = jnp.mean(x * x, axis=-1, keepdims=True)
    h = (x * lax.rsqrt(ms + EPS) * g_ref[...]).astype(BF16)

    def proj(lo, hi):
        return jnp.dot(h, w_ref[:, lo:hi], preferred_element_type=F32)

    o_q = D_RNN
    o_k = o_q + D_ATTN
    o_gr = o_k + 2 * D_KV
    o_ga = o_gr + D_RNN
    xr_ref[...] = proj(0, o_q)
    q_ref[...] = (proj(o_q, o_k) * (HEAD_DIM ** -0.5)).astype(BF16)
    kv_ref[...] = proj(o_k, o_gr)
    gr_ref[...] = proj(o_gr, o_ga).astype(gr_ref.dtype)
    ga_ref[...] = proj(o_ga, o_ga + D_ATTN).astype(ga_ref.dtype)


def _in_proj(x, g, w_in_bf, tm, gate_dtype):
    rows = x.shape[0]
    d_in = w_in_bf.shape[1]
    row = lambda w: pl.BlockSpec((tm, w), lambda i: (i, 0))
    return pl.pallas_call(
        _in_proj_kernel,
        grid=(rows // tm,),
        in_specs=[row(D_MODEL), _full((1, D_MODEL)), _full((D_MODEL, d_in))],
        out_specs=[row(D_RNN), row(D_ATTN), row(2 * D_KV), row(D_RNN), row(D_ATTN)],
        out_shape=[
            jax.ShapeDtypeStruct((rows, D_RNN), F32),
            jax.ShapeDtypeStruct((rows, D_ATTN), BF16),
            jax.ShapeDtypeStruct((rows, 2 * D_KV), F32),
            jax.ShapeDtypeStruct((rows, D_RNN), gate_dtype),
            jax.ShapeDtypeStruct((rows, D_ATTN), gate_dtype),
        ],
        compiler_params=_cparams(1),
        name="in_proj",
    )(x, g, w_in_bf)


def _softplus(x):
    return jnp.maximum(x, 0.0) + jnp.log1p(jnp.exp(-jnp.abs(x)))


def _lru_coeffs(u, wg_ref, brg, big, lam):
    ub = u.astype(BF16)
    sp = _softplus(-lam)
    a_parts, i_parts, m_parts = [], [], []
    for g in range(N_GATE_TILES):
        sl = slice(g * GATE_TILE, (g + 1) * GATE_TILE)
        zz = jnp.dot(ub[:, sl], wg_ref[g], preferred_element_type=F32)
        r = jax.nn.sigmoid(zz[:, :GATE_TILE] + brg[:, sl])
        i = jax.nn.sigmoid(zz[:, GATE_TILE:] + big[:, sl])
        log_a = (-LRU_C) * r * sp[:, sl]
        a = jnp.exp(log_a)
        m = jnp.sqrt(1.0 - a * a)
        a_parts.append(a)
        i_parts.append(i)
        m_parts.append(m)
    cat = lambda ps: jnp.concatenate(ps, axis=-1)
    return cat(a_parts), cat(i_parts), cat(m_parts)


def _conv(ext, t, cw, cb):
    out = cb
    for j in range(CONV_W):
        s = CONV_W - 1 - j
        out = out + cw[j:j + 1, :] * ext[8 - s:8 - s + t, :]
    return out


def _rnn_meta_kernel(xr_ref, cw_ref, cb_ref, wg_ref, brg_ref, big_ref, lam_ref, h_ref):
    x = xr_ref[...]
    ext = jnp.concatenate([jnp.zeros((8, D_RNN), F32), x], axis=0)
    u = _conv(ext, N_META, cw_ref[...], cb_ref[...])
    a, i, m = _lru_coeffs(u, wg_ref, brg_ref[...], big_ref[...], lam_ref[...])
    first = lax.broadcasted_iota(jnp.int32, (N_META, 1), 0) == 0
    b = jnp.where(first, 1.0, m) * i * u
    h = jnp.zeros((1, D_RNN), F32)
    for t in range(N_META):
        h = a[t:t + 1, :] * h + b[t:t + 1, :]
    h_ref[...] = h


def _linear_scan(a, b, h_prev):
    t, d = a.shape
    g = t // SUBLANES
    a3 = a.reshape(g, SUBLANES, d)
    b3 = b.reshape(g, SUBLANES, d)
    row = lax.broadcasted_iota(jnp.int32, (g, SUBLANES, d), 1)
    step = 1
    while step < SUBLANES:
        keep = row >= step
        a_up = jnp.where(keep, pltpu.roll(a3, step, axis=1), 1.0)
        b_up = jnp.where(keep, pltpu.roll(b3, step, axis=1), 0.0)
        b3 = a3 * b_up + b3
        a3 = a3 * a_up
        step *= 2
    hs = []
    h = h_prev
    for k in range(g):
        hk = a3[k] * h + b3[k]
        hs.append(hk)
        h = hk[SUBLANES - 1:SUBLANES, :]
    return jnp.concatenate(hs, axis=0), h


def _rnn_prompt_kernel(xr_ref, gr_ref, xm_ref, h0_ref, cw_ref, cb_ref, wg_ref, brg_ref, big_ref,
                       lam_ref, m_ref, hl_ref, halo_s, h_s, *, n_b, tt):
    j = pl.program_id(0)

    @pl.when(j == 0)
    def _():
        h_s[...] = jnp.broadcast_to(h0_ref[...], (n_b, D_RNN))
        for b in range(n_b):
            halo_s[b] = xm_ref[N_META - 8:N_META, :]

    cw = cw_ref[...]
    cb = cb_ref[...]

    def per_batch(b, c):
        x = xr_ref[b]
        ext = jnp.concatenate([halo_s[b], x], axis=0)
        halo_s[b] = x[tt - 8:tt, :]
        u = _conv(ext, tt, cw, cb)
        a, i, m = _lru_coeffs(u, wg_ref, brg_ref[...], big_ref[...], lam_ref[...])
        y, h = _linear_scan(a, m * i * u, h_s[pl.ds(b, 1), :])
        h_s[pl.ds(b, 1), :] = h
        m_ref[b] = (jax.nn.sigmoid(gr_ref[b].astype(F32)) * y).astype(BF16)
        return c

    lax.fori_loop(0, n_b, per_batch, 0)
    hl_ref[...] = h_s[...]


def _rnn_sample_kernel(xr_ref, gr_ref, sc_ref, h0_ref, cw_ref, cb_ref, wg_ref, brg_ref, big_ref,
                       lam_ref, m_ref, hn_ref):
    cw = cw_ref[...]
    u = cb_ref[...] + cw[CONV_W - 1:CONV_W, :] * xr_ref[...]
    for j in range(CONV_W - 1):
        u = u + cw[j:j + 1, :] * sc_ref[j]
    a, i, m = _lru_coeffs(u, wg_ref, brg_ref[...], big_ref[...], lam_ref[...])
    h = a * h0_ref[...] + m * i * u
    hn_ref[...] = h
    m_ref[...] = (jax.nn.sigmoid(gr_ref[...].astype(F32)) * h).astype(BF16)


def _alibi_slope(h):
    return 2.0 ** (-8.0 * (h + 1) / N_HEADS)


def _attn_bias_table():
    assert WINDOW == BLOCK
    qi = np.arange(BLOCK, dtype=np.int32)[:, None]
    ci = np.arange(BLOCK, dtype=np.int32)[None, :]
    own = ci <= qi
    dist = np.where(own, qi - ci, qi + BLOCK - ci).astype(np.float32)
    slopes = np.asarray([_alibi_slope(h) for h in range(N_HEADS)], np.float32)[:, None, None]
    ali = -slopes * dist[None]
    tables = [np.where((own | (ci >= c_min))[None], ali, np.float32(NEG_BIG))
              for c_min in (BLOCK - N_META, 0)]
    return jnp.asarray(np.stack(tables).astype(np.float32))


def _attn_prompt_kernel(sink_ref, q_ref, kc_ref, kp_ref, km_ref, bias_ref, ga_ref, o_ref):
    first = pl.program_id(1) == 0
    kv_prev = jnp.where(first, km_ref[...], kp_ref[...])
    kv = jnp.concatenate([kv_prev, kc_ref[...]], axis=0).astype(BF16)
    q = q_ref[...]
    qi = lax.broadcasted_iota(jnp.int32, (BLOCK, BLOCK), 0)
    ci = lax.broadcasted_iota(jnp.int32, (BLOCK, BLOCK), 1)
    own = ci <= qi
    zero = jnp.zeros((BLOCK, BLOCK), BF16)
    outs = []
    for h in range(N_HEADS):
        g = h // GROUP
        kh = kv[:, g * HEAD_DIM:(g + 1) * HEAD_DIM]
        vh = kv[:, D_KV + g * HEAD_DIM:D_KV + (g + 1) * HEAD_DIM]
        qh = q[:, h * HEAD_DIM:(h + 1) * HEAD_DIM]
        s2 = lax.dot_general(qh, kh, (((1,), (1,)), ((), ())), preferred_element_type=F32)
        s = jnp.where(own, s2[:, BLOCK:], s2[:, :BLOCK]) + bias_ref[0, h]
        sink = sink_ref[h]
        mx = jnp.maximum(jnp.max(s, axis=-1, keepdims=True), sink)
        p = jnp.exp(s - mx)
        den = jnp.sum(p, axis=-1, keepdims=True) + jnp.exp(sink - mx)
        pb = p.astype(BF16)
        p2 = jnp.concatenate([jnp.where(own, zero, pb), jnp.where(own, pb, zero)], axis=1)
        o = jnp.dot(p2, vh, preferred_element_type=F32)
        outs.append(o / den)
    y = jnp.concatenate(outs, axis=-1)
    o_ref[...] = (jax.nn.sigmoid(ga_ref[...].astype(F32)) * y).astype(BF16)


SAMPLES_PER_STEP = 8


def _attn_sample_kernel(sink_ref, q_ref, kvn_ref, ck_ref, cv_ref, ga_ref, o_ref):
    w_buf = ck_ref.shape[1]
    n_rows = 2 * N_HEADS
    n_blocks = 2 * N_KV
    ci = lax.broadcasted_iota(jnp.int32, (n_rows, w_buf), 1)
    dist = w_buf - ci
    valid = dist < WINDOW
    row = lax.broadcasted_iota(jnp.int32, (n_rows, 1), 0)
    slope = jnp.zeros((n_rows, 1), F32)
    sink = jnp.zeros((n_rows, 1), F32)
    for h in range(N_HEADS):
        slope = jnp.where(row % N_HEADS == h, _alibi_slope(h), slope)
        sink = jnp.where(row % N_HEADS == h, sink_ref[h], sink)
    bias = jnp.where(valid, -slope * dist.astype(F32), NEG_BIG)
    row_block = lax.broadcasted_iota(jnp.int32, (n_rows, n_blocks * HEAD_DIM), 0) // GROUP
    lane_block = lax.broadcasted_iota(jnp.int32, (n_rows, n_blocks * HEAD_DIM), 1) // HEAD_DIM
    own_block = row_block == lane_block
    for n in range(0, SAMPLES_PER_STEP, 2):
        q2 = jnp.concatenate([q_ref[n], q_ref[n + 1]], axis=0)
        qd = jnp.where(own_block, jnp.concatenate([q2] * n_blocks, axis=1), jnp.zeros((), BF16))
        kk = jnp.concatenate([ck_ref[n], ck_ref[n + 1]], axis=1).astype(BF16)
        vv = jnp.concatenate([cv_ref[n], cv_ref[n + 1]], axis=1).astype(BF16)
        kvn = [kvn_ref[n + t].astype(BF16).astype(F32) for t in range(2)]
        k_new = jnp.concatenate([kvn[0][:, :D_KV], kvn[1][:, :D_KV]], axis=1)
        v_new = jnp.concatenate([kvn[0][:, D_KV:], kvn[1][:, D_KV:]], axis=1)
        s = lax.dot_general(qd, kk, (((1,), (1,)), ((), ())), preferred_element_type=F32) + bias
        sn = jnp.sum(qd.astype(F32) * k_new, axis=-1, keepdims=True)
        mx = jnp.maximum(jnp.maximum(jnp.max(s, axis=-1, keepdims=True), sn), sink)
        p = jnp.exp(s - mx)
        pn = jnp.exp(sn - mx)
        den = jnp.sum(p, axis=-1, keepdims=True) + pn + jnp.exp(sink - mx)
        o = jnp.dot(p.astype(BF16), vv, preferred_element_type=F32)
        o = o + pn.astype(BF16).astype(F32) * v_new
        y = jnp.concatenate([o[j * GROUP:(j + 1) * GROUP, j * HEAD_DIM:(j + 1) * HEAD_DIM]
                             for j in range(n_blocks)], axis=0) / den
        for t in range(2):
            rows = slice(t * N_HEADS, (t + 1) * N_HEADS)
            o_ref[n + t] = (jax.nn.sigmoid(ga_ref[n + t].astype(F32)) * y[rows]).astype(BF16)


def _merge_kernel(mr_ref, ma_ref, x_ref, wo_ref, nf_ref, wr_ref, br_ref, *rest, n_steps):
    x1_ref, hn_ref, ti_ref, tw_ref = rest[-4:]

    @pl.when(pl.program_id(0) >= n_steps)
    def _():
        hn_ref[...] = jnp.zeros_like(hn_ref)

    @pl.when(pl.program_id(0) < n_steps)
    def _():
        _merge_tile(mr_ref, ma_ref, x_ref, wo_ref, nf_ref, wr_ref, br_ref, x1_ref, hn_ref, ti_ref,
                    tw_ref)


def _merge_tile(mr_ref, ma_ref, x_ref, wo_ref, nf_ref, wr_ref, br_ref, x1_ref, hn_ref, ti_ref,
                tw_ref):
    mm = jnp.dot(mr_ref[...], wo_ref[:D_RNN, :], preferred_element_type=F32)
    mm = mm + jnp.dot(ma_ref[...], wo_ref[D_RNN:, :], preferred_element_type=F32)
    x1 = x_ref[...] + mm
    x1_ref[...] = x1
    ms = jnp.mean(x1 * x1, axis=-1, keepdims=True)
    hn = x1 * lax.rsqrt(ms + EPS) * nf_ref[...]
    _store_row_tiled(hn, hn_ref)
    tm = hn.shape[0]
    l = lax.dot_general(wr_ref[...], hn.astype(BF16), (((1,), (1,)), ((), ())),
                        preferred_element_type=F32) + br_ref[...]
    eid = lax.broadcasted_iota(jnp.int32, l.shape, 0)
    vals, idxs = [], []
    for k in range(TOP_K):
        mx = jnp.max(l, axis=0, keepdims=True)
        idx = jnp.min(jnp.where(l == mx, eid, N_EXPERTS), axis=0, keepdims=True)
        vals.append(mx)
        idxs.append(idx)
        l = jnp.where(eid == idx, NEG_BIG, l)
    es = [jnp.exp(v - vals[0]) for v in vals]
    tot = es[0] + es[1] + es[2] + es[3]
    pad = SUBLANES - TOP_K
    ti_ref[...] = jnp.concatenate(idxs + [jnp.zeros((pad, tm), jnp.int32)], axis=0)
    tw_t = jnp.concatenate([e / tot for e in es] + [jnp.zeros((pad, tm), F32)], axis=0)
    eye = (lax.broadcasted_iota(jnp.int32, (tm, tm), 0)
           == lax.broadcasted_iota(jnp.int32, (tm, tm), 1)).astype(F32)
    tw_rows = lax.dot_general(eye, tw_t, (((1,), (1,)), ((), ())), preferred_element_type=F32,
                              precision=lax.Precision.HIGHEST)
    tw_ref[...] = jnp.concatenate([tw_rows, jnp.zeros((tm, LANES - SUBLANES), F32)], axis=1)


def _merge(m_rnn, m_attn, x, w_out_bf, nf, wr_pad, br_pad, hn_all, n_all, row_block0, tm):
    rows = x.shape[0]
    n_steps = rows // tm
    n_zero_steps = 0 if hn_all is not None else n_all // tm - n_steps
    row = lambda w: pl.BlockSpec((tm, w), lambda i: (jnp.minimum(i, n_steps - 1), 0))
    in_specs = [row(D_RNN), row(D_ATTN), row(D_MODEL), _full((D_RNN + D_ATTN, D_MODEL)),
                _full((1, D_MODEL)), _full((N_EXPERTS, D_MODEL)), _full((N_EXPERTS, 1))]
    args = [m_rnn, m_attn, x, w_out_bf, nf, wr_pad, br_pad]
    aliases = {}
    if hn_all is not None:
        in_specs.append(pl.BlockSpec(memory_space=pl.ANY))
        args.append(hn_all)
        aliases = {len(args) - 1: 1}
    return pl.pallas_call(
        functools.partial(_merge_kernel, n_steps=n_steps),
        grid=(n_steps + n_zero_steps,),
        in_specs=in_specs,
        out_specs=[row(D_MODEL),
                   pl.BlockSpec((tm,) + ROW_TILE, lambda i: (i + row_block0, 0, 0)),
                   pl.BlockSpec((SUBLANES, tm), lambda i: (0, jnp.minimum(i, n_steps - 1))),
                   row(LANES)],
        out_shape=[
            jax.ShapeDtypeStruct((rows, D_MODEL), F32),
            jax.ShapeDtypeStruct((n_all,) + ROW_TILE, F32),
            jax.ShapeDtypeStruct((SUBLANES, rows), jnp.int32),
            jax.ShapeDtypeStruct((rows, LANES), F32),
        ],
        input_output_aliases=aliases,
        compiler_params=_cparams(1),
        name="merge_router",
    )(*args)


UP_TILE = 256


def _deinterleave_matrix():
    half = UP_TILE // 2
    r = lax.broadcasted_iota(jnp.int32, (UP_TILE, UP_TILE), 0)
    c = lax.broadcasted_iota(jnp.int32, (UP_TILE, UP_TILE), 1)
    src = jnp.where(c < half, 2 * c, 2 * (c - half) + 1)
    return jnp.where(r == src, 1.0, 0.0).astype(BF16)


GATHER_AHEAD = 2
GATHER_DMA_PRIORITY = 0
OTHER_DMA_PRIORITY = 1


def _moe_kernel(be_ref, nxt_ref, nval_ref, nu_ref, tok0_ref, tok1_ref, tok2_ref, asg_ref, hn_hbm,
                wu_hbm, bu_ref, wd_hbm, bd_ref, out_hbm, xbuf, xb_s, ystage, wu_f, wd_f, wu_s, wd_s,
                sems, ssems, wsems):
    i = pl.program_id(0)
    nu = nu_ref[0]
    half = UP_TILE // 2
    n_up = 2 * D_FF // UP_TILE
    slot = i % GATHER_AHEAD
    prev_slot = 1 - slot
    n_prev = nval_ref[i]

    def row_copy(tok_ref, r, s):
        return pltpu.make_async_copy(hn_hbm.at[pl.ds(tok_ref[0, 0, r], 1)],
                                     xbuf.at[s, pl.ds(r, 1)], sems.at[s])

    def weight_copies(e):
        return (pltpu.make_async_copy(wu_hbm.at[e], wu_f, wsems.at[0]),
                pltpu.make_async_copy(wd_hbm.at[e], wd_f, wsems.at[1]))

    def out_copy(r):
        return pltpu.make_async_copy(ystage.at[prev_slot, pl.ds(r, 1)],
                                     out_hbm.at[pl.ds(asg_ref[0, 0, r], 1)], ssems.at[prev_slot])

    @pl.when(i == 0)
    def _():
        for cp in weight_copies(be_ref[0]):
            cp.start(priority=OTHER_DMA_PRIORITY)

        def issue(r, c):
            row_copy(tok0_ref, r, 0).start(priority=GATHER_DMA_PRIORITY)
            row_copy(tok1_ref, r, 1).start(priority=GATHER_DMA_PRIORITY)
            return c

        lax.fori_loop(0, TM_MOE, issue, 0)

    @pl.when(i < nu + GATHER_AHEAD)
    def _():
        pltpu.make_async_copy(hn_hbm.at[pl.ds(0, TM_MOE)], xbuf.at[slot], sems.at[slot]).wait()

    @pl.when((i >= 2) & (i < nu + 2))
    def _():
        n_sent = nval_ref[i - 1]
        n_tiled = pl.multiple_of((n_sent // SUBLANES) * SUBLANES, SUBLANES)

        @pl.when(n_tiled > 0)
        def _():
            pltpu.make_async_copy(ystage.at[slot, pl.ds(0, n_tiled)],
                                  out_hbm.at[pl.ds(0, n_tiled)], ssems.at[slot]).wait()

        def wait_row(r, c):
            pltpu.make_async_copy(ystage.at[slot, pl.ds(0, 1)], out_hbm.at[pl.ds(0, 1)],
                                  ssems.at[slot]).wait()
            return c

        lax.fori_loop(0, n_sent - n_tiled, wait_row, 0)

    @pl.when(i < nu)
    def _():
        xb_s[...] = _load_row_tiled(xbuf.at[slot], TM_MOE).astype(BF16)

        @pl.when((i == 0) | (be_ref[i] != be_ref[jnp.maximum(i - 1, 0)]))
        def _():
            for cp in weight_copies(be_ref[i]):
                cp.wait()
            perm = _deinterleave_matrix()
            for c in range(n_up):
                cols = slice(c * UP_TILE, (c + 1) * UP_TILE)
                blk = wu_f[:, cols].astype(BF16)
                wu_s[:, cols] = jnp.dot(blk, perm, preferred_element_type=F32).astype(BF16)
            wd_s[...] = wd_f[...].astype(BF16)

            @pl.when(nxt_ref[i] >= 0)
            def _():
                for cp in weight_copies(nxt_ref[i]):
                    cp.start(priority=OTHER_DMA_PRIORITY)

    def send_some_rows():
        def send(r, c):
            out_copy(r).start(priority=OTHER_DMA_PRIORITY)
            return c

        lax.fori_loop(0, n_prev, send, 0)

    def run_block(send_all_rows):
        for r in range(TM_MOE):
            row_copy(tok2_ref, r, slot).start(priority=GATHER_DMA_PRIORITY)
            if send_all_rows:
                out_copy(r).start(priority=OTHER_DMA_PRIORITY)
        x = xb_s[...]
        z = jnp.dot(x, wu_s[...], preferred_element_type=F32) + bu_ref[0]
        zg = jnp.concatenate([z[:, c * UP_TILE:c * UP_TILE + half] for c in range(n_up)], axis=-1)
        zl = jnp.concatenate([z[:, c * UP_TILE + half:(c + 1) * UP_TILE] for c in range(n_up)],
                             axis=-1)
        xg = jnp.minimum(zg, SWIGLU_LIMIT)
        xl = jnp.clip(zl, -SWIGLU_LIMIT, SWIGLU_LIMIT)
        act = xg * jax.nn.sigmoid(SWIGLU_ALPHA * xg) * (xl + 1.0)
        y = jnp.dot(act.astype(BF16), wd_s[...], preferred_element_type=F32) + bd_ref[0]
        _store_row_tiled(y, ystage.at[slot])

    @pl.when((i < nu) & (n_prev == TM_MOE))
    def _():
        run_block(send_all_rows=True)

    @pl.when((i < nu) & (n_prev != TM_MOE))
    def _():
        send_some_rows()
        run_block(send_all_rows=False)

    @pl.when(i == nu)
    def _():
        send_some_rows()


def _moe(blk_expert, next_expert, n_valid, n_used, buf_tok, buf_asg, hn_all, w_up, b_up_perm,
         w_down, b_down, n_blk, n_assign):
    ew = lambda r, c: pl.BlockSpec((1, r, c), lambda i, be, nxt, nv, nu: (be[i], 0, 0))
    tok = lambda f: pl.BlockSpec((1, 1, TM_MOE), lambda i, be, nxt, nv, nu: (f(i), 0, 0),
                                 memory_space=pltpu.SMEM)
    hbm = pl.BlockSpec(memory_space=pl.ANY)
    return pl.pallas_call(
        _moe_kernel,
        grid_spec=pltpu.PrefetchScalarGridSpec(
            num_scalar_prefetch=4,
            grid=(n_blk,),
            in_specs=[
                tok(lambda i: i), tok(lambda i: jnp.minimum(i + 1, n_blk - 1)),
                tok(lambda i: jnp.minimum(i + GATHER_AHEAD, n_blk - 1)), tok(lambda i: i),
                hbm, hbm, ew(1, 2 * D_FF), hbm, ew(1, D_MODEL),
            ],
            out_specs=hbm,
            scratch_shapes=[pltpu.VMEM((GATHER_AHEAD, TM_MOE) + ROW_TILE, F32),
                            pltpu.VMEM((TM_MOE, D_MODEL), BF16),
                            pltpu.VMEM((2, TM_MOE) + ROW_TILE, F32),
                            pltpu.VMEM((D_MODEL, 2 * D_FF), F32),
                            pltpu.VMEM((D_FF, D_MODEL), F32),
                            pltpu.VMEM((D_MODEL, 2 * D_FF), BF16),
                            pltpu.VMEM((D_FF, D_MODEL), BF16),
                            pltpu.SemaphoreType.DMA((GATHER_AHEAD,)),
                            pltpu.SemaphoreType.DMA((2,)),
                            pltpu.SemaphoreType.DMA((2,))],
        ),
        out_shape=jax.ShapeDtypeStruct((n_assign,) + ROW_TILE, F32),
        compiler_params=_cparams(1),
        name="moe_experts",
    )(blk_expert, next_expert, n_valid, n_used, buf_tok, buf_tok, buf_tok, buf_asg, hn_all, w_up,
      b_up_perm, w_down, b_down)


def _final_kernel(x1_ref, tw_ref, nf_ref, *rest):
    y_refs, o_ref = rest[:TOP_K], rest[TOP_K]
    tw = tw_ref[...]
    x = x1_ref[...]
    for k in range(TOP_K):
        x = x + tw[:, k:k + 1] * _load_row_tiled(y_refs[k].at[0], x.shape[0])
    ms = jnp.mean(x * x, axis=-1, keepdims=True)
    o_ref[...] = x * lax.rsqrt(ms + EPS) * nf_ref[...]


def _final(x1, tw, nf, y_choice, row0):
    rows = x1.shape[0]
    tm = min(TM_FIN, rows)
    row_block0 = row0 // tm
    row = lambda w: pl.BlockSpec((tm, w), lambda i: (i, 0))
    y_spec = lambda k: pl.BlockSpec((1, tm) + ROW_TILE, lambda i: (k, i + row_block0, 0, 0))
    return pl.pallas_call(
        _final_kernel,
        grid=(rows // tm,),
        in_specs=[row(D_MODEL), row(LANES), _full((1, D_MODEL))]
        + [y_spec(k) for k in range(TOP_K)],
        out_specs=row(D_MODEL),
        out_shape=jax.ShapeDtypeStruct((rows, D_MODEL), F32),
        compiler_params=_cparams(1),
        name="combine_final",
    )(x1, tw, nf, *([y_choice] * TOP_K))


def _block_diag_tiles(w):
    per = GATE_TILE // GATE_BW
    w4 = w.reshape(N_GATE_TILES, per, GATE_BW, GATE_BW)
    eye = jnp.eye(per, dtype=w.dtype)
    return jnp.einsum("gacd,ab->gacbd", w4, eye).reshape(N_GATE_TILES, GATE_TILE, GATE_TILE)


def _route(top_i_groups, n_blk):
    e_ids = jnp.arange(N_EXPERTS, dtype=jnp.int32)
    n_tokens = sum(t.shape[1] for t in top_i_groups)
    a = n_tokens * TOP_K
    tables, cnts, bases = [], [], []
    tok0 = 0
    for ti in top_i_groups:
        n_g = ti.shape[1]
        bits = max(n_g - 1, 1).bit_length()
        keys = (ti << bits) | jnp.arange(n_g, dtype=jnp.int32)[None, :]
        tok = (jnp.sort(keys, axis=1) & ((1 << bits) - 1)) + tok0
        tables.append((tok + jnp.arange(TOP_K, dtype=jnp.int32)[:, None] * n_tokens).reshape(-1))
        cnts.append(jnp.sum((ti[:, :, None] == e_ids[None, None, :]).astype(jnp.int32), axis=1))
        bases += [tok0 * TOP_K + k * n_g for k in range(TOP_K)]
        tok0 += n_g
    table = jnp.concatenate(tables)
    cnt = jnp.concatenate(cnts, axis=0)
    col_off = jnp.cumsum(cnt, axis=1) - cnt
    col_cum = jnp.cumsum(cnt, axis=0) - cnt
    counts = jnp.sum(cnt, axis=0)
    col_shift = jnp.asarray(bases, jnp.int32)[:, None] + col_off - col_cum
    col_step = col_shift - jnp.concatenate([jnp.zeros((1, N_EXPERTS), jnp.int32), col_shift[:-1]])
    padded = ((counts + (TM_MOE - 1)) // TM_MOE) * TM_MOE
    ends = jnp.cumsum(padded).astype(jnp.int32)
    start_padded = ends - padded
    blk_start = jnp.arange(n_blk, dtype=jnp.int32) * TM_MOE
    blk_expert = jnp.minimum(
        jnp.sum((ends[None, :] <= blk_start[:, None]).astype(jnp.int32), axis=1), N_EXPERTS - 1)
    n_used = (ends[-1:] // TM_MOE).astype(jnp.int32)
    blk_onehot = blk_expert[:, None] == e_ids[None, :]

    def per_block(tab):
        return jnp.sum(jnp.where(blk_onehot, tab[..., None, :], 0), axis=-1).astype(jnp.int32)

    n_valid = jnp.clip(per_block(start_padded + counts) - blk_start, 0, TM_MOE).astype(jnp.int32)
    per_row = lambda v: jnp.repeat(v, TM_MOE, axis=-1)
    j = jnp.arange(n_blk * TM_MOE, dtype=jnp.int32) - per_row(per_block(start_padded))
    pos = j + jnp.sum(jnp.where(j[None, :] >= per_row(per_block(col_cum)),
                                per_row(per_block(col_step)), 0), axis=0)
    buf_asg = table[jnp.clip(pos, 0, a - 1)]
    buf_tok = buf_asg % n_tokens
    n_valid = jnp.concatenate([jnp.zeros((1,), jnp.int32), n_valid])
    buf_asg = jnp.concatenate([jnp.zeros((TM_MOE,), jnp.int32), buf_asg])
    later_used = (e_ids[None, :] > e_ids[:, None]) & (counts[None, :] > 0)
    next_used = jnp.min(jnp.where(later_used, e_ids[None, :], N_EXPERTS), axis=1)
    next_used = jnp.where(next_used == N_EXPERTS, -1, next_used).astype(jnp.int32)
    return (buf_tok.reshape(n_blk, 1, TM_MOE), buf_asg.reshape(n_blk + 1, 1, TM_MOE),
            blk_expert.astype(jnp.int32), per_block(next_used), n_valid, n_used)


def kernel(x_prompt, x_sample, cache_k, cache_v, state_conv, state_h, meta_tokens, norm_mix, w_in, conv_w, conv_b, w_rg, b_rg, w_ig, b_ig, lru_lambda, attn_sinks, w_out, norm_ffn, w_router, b_router, w_up, b_up, w_down, b_down, norm_final):
    n_b, seq = x_prompt.shape[0], x_prompt.shape[1]
    n_s = x_sample.shape[0]
    w_buf = cache_k.shape[2]
    n_p = n_b * seq
    n_tok = n_p + n_s
    row1 = lambda v: v.reshape(1, -1)

    w_in_bf = w_in[0].astype(BF16)
    w_out_bf = w_out[0].astype(BF16)
    wg_tiles = jnp.concatenate([_block_diag_tiles(w_rg[0]), _block_diag_tiles(w_ig[0])],
                               axis=-1).astype(BF16)
    wr_pad = w_router[0].T.astype(BF16)
    br_pad = b_router[0].reshape(N_EXPERTS, 1)
    b_up_perm = jnp.swapaxes(b_up[0].reshape(N_EXPERTS, 2 * D_FF // UP_TILE, UP_TILE // 2, 2),
                             2, 3).reshape(N_EXPERTS, 1, 2 * D_FF)
    b_dn = b_down[0].reshape(N_EXPERTS, 1, D_MODEL)
    nm, nf, nfin = row1(norm_mix[0]), row1(norm_ffn[0]), row1(norm_final)
    cw, cb = conv_w[0], row1(conv_b[0])
    brg, big, lam = row1(b_rg[0]), row1(b_ig[0]), row1(lru_lambda[0])
    sinks = attn_sinks[0]
    rnn_w = (cw, cb, wg_tiles, brg, big, lam)
    rnn_w_specs = [_full((CONV_W, D_RNN)), _full((1, D_RNN)),
                   _full((N_GATE_TILES, GATE_TILE, 2 * GATE_TILE)),
                   _full((1, D_RNN)), _full((1, D_RNN)), _full((1, D_RNN))]

    xp2 = x_prompt.reshape(n_p, D_MODEL)
    xr_p, q_p, kv_p, gr_p, ga_p = _in_proj(xp2, nm, w_in_bf, TM_IN, BF16)
    x_sm = jnp.concatenate([x_sample.reshape(n_s, D_MODEL), meta_tokens], axis=0)
    xr_sm, q_sm, kv_sm, gr_sm, ga_sm = _in_proj(x_sm, nm, w_in_bf, n_s + N_META, F32)
    xr_s, q_s, kv_s, gr_s, ga_s = (t[:n_s] for t in (xr_sm, q_sm, kv_sm, gr_sm, ga_sm))
    xr_m, kv_m = xr_sm[n_s:], kv_sm[n_s:]

    h_meta = pl.pallas_call(
        _rnn_meta_kernel,
        in_specs=[_full((N_META, D_RNN))] + rnn_w_specs,
        out_specs=_full((1, D_RNN)),
        out_shape=jax.ShapeDtypeStruct((1, D_RNN), F32),
        grid=(1,),
        compiler_params=_cparams(1),
        name="rnn_meta",
    )(xr_m, *rnn_w)

    tt = TT_RNN
    blk3 = pl.BlockSpec((n_b, tt, D_RNN), lambda j: (0, j, 0))
    m_rnn_p, h_last_p = pl.pallas_call(
        functools.partial(_rnn_prompt_kernel, n_b=n_b, tt=tt),
        grid=(seq // tt,),
        in_specs=[blk3, blk3, _full((N_META, D_RNN)), _full((1, D_RNN))] + rnn_w_specs,
        out_specs=[blk3, _full((n_b, D_RNN))],
        out_shape=[jax.ShapeDtypeStruct((n_b, seq, D_RNN), BF16),
                   jax.ShapeDtypeStruct((n_b, D_RNN), F32)],
        scratch_shapes=[pltpu.VMEM((n_b, 8, D_RNN), F32), pltpu.VMEM((n_b, D_RNN), F32)],
        compiler_params=_cparams(1),
        name="rnn_prompt",
    )(xr_p.reshape(n_b, seq, D_RNN), gr_p.reshape(n_b, seq, D_RNN), xr_m, h_meta, *rnn_w)

    sc_t = jnp.swapaxes(state_conv[0], 0, 1)
    m_rnn_s, h_new_s = pl.pallas_call(
        _rnn_sample_kernel,
        grid=(1,),
        in_specs=[_full((n_s, D_RNN)), _full((n_s, D_RNN)), _full((CONV_W - 1, n_s, D_RNN)),
                  _full((n_s, D_RNN))] + rnn_w_specs,
        out_specs=[_full((n_s, D_RNN)), _full((n_s, D_RNN))],
        out_shape=[jax.ShapeDtypeStruct((n_s, D_RNN), BF16),
                   jax.ShapeDtypeStruct((n_s, D_RNN), F32)],
        compiler_params=_cparams(1),
        name="rnn_sample",
    )(xr_s, gr_s, sc_t, state_h[0], *rnn_w)

    n_blk_seq = seq // BLOCK
    kv_meta_blk = jnp.pad(kv_m, ((BLOCK - N_META, 0), (0, 0)))
    smem_spec = pl.BlockSpec(memory_space=pltpu.SMEM)
    rb = lambda w: pl.BlockSpec((BLOCK, w), lambda b, j: (b * n_blk_seq + j, 0))
    m_attn_p = pl.pallas_call(
        _attn_prompt_kernel,
        grid=(n_b, n_blk_seq),
        in_specs=[smem_spec, rb(D_ATTN), rb(2 * D_KV),
                  pl.BlockSpec((BLOCK, 2 * D_KV),
                               lambda b, j: (jnp.maximum(b * n_blk_seq + j - 1, 0), 0)),
                  _full((BLOCK, 2 * D_KV)),
                  pl.BlockSpec((1, N_HEADS, BLOCK, BLOCK),
                               lambda b, j: (jnp.minimum(j, 1), 0, 0, 0)),
                  rb(D_ATTN)],
        out_specs=rb(D_ATTN),
        out_shape=jax.ShapeDtypeStruct((n_p, D_ATTN), BF16),
        compiler_params=_cparams(2),
        name="attn_prompt",
    )(sinks, q_p, kv_p, kv_p, kv_meta_blk, _attn_bias_table(), ga_p)

    ck = cache_k[0].reshape(n_s, w_buf, D_KV)
    cv = cache_v[0].reshape(n_s, w_buf, D_KV)
    per_s = lambda a, b: pl.BlockSpec((SAMPLES_PER_STEP, a, b), lambda i: (i, 0, 0))
    m_attn_s = pl.pallas_call(
        _attn_sample_kernel,
        grid=(n_s // SAMPLES_PER_STEP,),
        in_specs=[smem_spec, per_s(N_HEADS, HEAD_DIM), per_s(1, 2 * D_KV), per_s(w_buf, D_KV),
                  per_s(w_buf, D_KV), per_s(N_HEADS, HEAD_DIM)],
        out_specs=per_s(N_HEADS, HEAD_DIM),
        out_shape=jax.ShapeDtypeStruct((n_s, N_HEADS, HEAD_DIM), BF16),
        compiler_params=_cparams(1),
        name="attn_sample",
    )(sinks, q_s.reshape(n_s, N_HEADS, HEAD_DIM), kv_s.reshape(n_s, 1, 2 * D_KV), ck, cv,
      ga_s.reshape(n_s, N_HEADS, HEAD_DIM)).reshape(n_s, D_ATTN)

    n_hn = n_p + -(-n_s // TM_MERGE) * TM_MERGE
    x1_p, hn_all, ti_p, tw_p = _merge(m_rnn_p.reshape(n_p, D_RNN), m_attn_p, xp2, w_out_bf, nf,
                                      wr_pad, br_pad, None, n_hn, 0, TM_MERGE)
    x1_s, hn_all, ti_s, tw_s = _merge(m_rnn_s, m_attn_s, x_sample.reshape(n_s, D_MODEL), w_out_bf,
                                      nf, wr_pad, br_pad, hn_all, n_hn, n_p // n_s, n_s)

    n_assign = n_tok * TOP_K
    n_blk = n_assign // TM_MOE + N_EXPERTS - 1 + GATHER_AHEAD
    buf_tok, buf_asg, blk_expert, next_expert, n_valid, n_used = _route(
        [ti_p[:TOP_K], ti_s[:TOP_K]], n_blk)

    y_tok = _moe(blk_expert, next_expert, n_valid, n_used, buf_tok, buf_asg, hn_all, w_up[0],
                 b_up_perm, w_down[0], b_dn, n_blk, n_assign).reshape((TOP_K, n_tok) + ROW_TILE)
    y_p = _final(x1_p, tw_p, nfin, y_tok, 0)
    y_s = _final(x1_s, tw_s, nfin, y_tok, n_p)

    kv_p3 = kv_p.reshape(n_b, seq, 2 * D_KV)
    w_p = min(WINDOW, seq + N_META)
    new_k_p = kv_p3[:, seq - w_p:, :D_KV].reshape(1, n_b, w_p, N_KV, HEAD_DIM)
    new_v_p = kv_p3[:, seq - w_p:, D_KV:].reshape(1, n_b, w_p, N_KV, HEAD_DIM)
    new_conv_p = xr_p.reshape(n_b, seq, D_RNN)[:, seq - (CONV_W - 1):][None]
    k_new = kv_s[:, :D_KV].reshape(n_s, 1, N_KV, HEAD_DIM)
    v_new = kv_s[:, D_KV:].reshape(n_s, 1, N_KV, HEAD_DIM)
    new_k_s = jnp.concatenate([cache_k[0], k_new], axis=1)[:, -w_buf:][None]
    new_v_s = jnp.concatenate([cache_v[0], v_new], axis=1)[:, -w_buf:][None]
    new_conv_s = jnp.concatenate([state_conv[0], xr_s[:, None, :]], axis=1)[:, -(CONV_W - 1):][None]
    return (y_p.reshape(n_b, seq, D_MODEL), y_s.reshape(n_s, 1, D_MODEL), new_k_p, new_v_p,
            new_conv_p, h_last_p[None], new_k_s, new_v_s, new_conv_s, h_new_s[None])
```

```python
import functools

import jax
import jax.numpy as jnp
import numpy as np
from jax import lax
from jax.experimental import pallas as pl
from jax.experimental.pallas import tpu as pltpu

F32 = jnp.float32
BF16 = jnp.bfloat16

D_MODEL = 1024
N_META = 16
D_RNN = 1024
N_GATE_BLOCKS = 16
GATE_BW = D_RNN // N_GATE_BLOCKS
CONV_W = 4
LRU_C = 8.0
N_HEADS = 16
HEAD_DIM = 64
N_KV = 2
GROUP = N_HEADS // N_KV
D_ATTN = N_HEADS * HEAD_DIM
D_KV = N_KV * HEAD_DIM
WINDOW = 128
BLOCK = 128
PAST_LEN = 16384
N_EXPERTS = 32
TOP_K = 4
D_FF = 1024
SWIGLU_LIMIT = 7.0
SWIGLU_ALPHA = 1.702
EPS = 1e-6

GATE_TILE = 256
N_GATE_TILES = D_RNN // GATE_TILE
LANES = 128
NEG_BIG = -1e30

TM_IN = 512
TT_RNN = 128
TM_MERGE = 256
TM_MOE = 256
TM_FIN = 256
VMEM_LIMIT = 56 * 1024 * 1024


def _cparams(n_grid_dims):
    return pltpu.CompilerParams(
        dimension_semantics=("arbitrary",) * n_grid_dims, vmem_limit_bytes=VMEM_LIMIT)


def _full(shape):
    return pl.BlockSpec(shape, lambda *_: (0,) * len(shape))


SUBLANES = 8
ROW_TILE = (SUBLANES, D_MODEL // SUBLANES)


def _transpose_vreg_group(v):
    sub = lax.broadcasted_iota(jnp.int32, v[0].shape, 0)
    v = list(v)
    d = SUBLANES // 2
    while d >= 1:
        hi = (sub & d) != 0
        nv = list(v)
        for a in range(SUBLANES):
            if a & d:
                continue
            b = a | d
            nv[a] = jnp.where(hi, pltpu.roll(v[b], d, axis=0), v[a])
            nv[b] = jnp.where(hi, v[b], pltpu.roll(v[a], SUBLANES - d, axis=0))
        v = nv
        d //= 2
    return v


def _store_row_tiled(x, dst):
    for g in range(x.shape[0] // SUBLANES):
        rows = slice(g * SUBLANES, (g + 1) * SUBLANES)
        w = _transpose_vreg_group([x[rows, s * LANES:(s + 1) * LANES] for s in range(SUBLANES)])
        for j in range(SUBLANES):
            dst[g * SUBLANES + j] = w[j]


def _load_row_tiled(src, n_rows):
    groups = []
    for g in range(n_rows // SUBLANES):
        v = _transpose_vreg_group([src[g * SUBLANES + j] for j in range(SUBLANES)])
        groups.append(jnp.concatenate(v, axis=-1))
    return jnp.concatenate(groups, axis=0)


def _in_proj_kernel(x_ref, g_ref, w_ref, xr_ref, q_ref, kv_ref, gr_ref, ga_ref):
    x = x_ref[...]
    ms = jnp.mean(x * x, axis=-1, keepdims=True)
    h = (x * lax.rsqrt(ms + EPS) * g_ref[...]).astype(BF16)

    def proj(lo, hi):
        return jnp.dot(h, w_ref[:, lo:hi], preferred_element_type=F32)

    o_q = D_RNN
    o_k = o_q + D_ATTN
    o_gr = o_k + 2 * D_KV
    o_ga = o_gr + D_RNN
    xr_ref[...] = proj(0, o_q)
    q_ref[...] = (proj(o_q, o_k) * (HEAD_DIM ** -0.5)).astype(BF16)
    kv_ref[...] = proj(o_k, o_gr)
    gr_ref[...] = proj(o_gr, o_ga).astype(gr_ref.dtype)
    ga_ref[...] = proj(o_ga, o_ga + D_ATTN).astype(ga_ref.dtype)


def _in_proj(x, g, w_in_bf, tm, gate_dtype):
    rows = x.shape[0]
    d_in = w_in_bf.shape[1]
    row = lambda w: pl.BlockSpec((tm, w), lambda i: (i, 0))
    return pl.pallas_call(
        _in_proj_kernel,
        grid=(rows // tm,),
        in_specs=[row(D_MODEL), _full((1, D_MODEL)), _full((D_MODEL, d_in))],
        out_specs=[row(D_RNN), row(D_ATTN), row(2 * D_KV), row(D_RNN), row(D_ATTN)],
        out_shape=[
            jax.ShapeDtypeStruct((rows, D_RNN), F32),
            jax.ShapeDtypeStruct((rows, D_ATTN), BF16),
            jax.ShapeDtypeStruct((rows, 2 * D_KV), F32),
            jax.ShapeDtypeStruct((rows, D_RNN), gate_dtype),
            jax.ShapeDtypeStruct((rows, D_ATTN), gate_dtype),
        ],
        compiler_params=_cparams(1),
        name="in_proj",
    )(x, g, w_in_bf)


def _softplus(x):
    return jnp.maximum(x, 0.0) + jnp.log1p(jnp.exp(-jnp.abs(x)))


def _lru_coeffs(u, wg_ref, brg, big, lam):
    ub = u.astype(BF16)
    sp = _softplus(-lam)
    a_parts, i_parts, m_parts = [], [], []
    for g in range(N_GATE_TILES):
        sl = slice(g * GATE_TILE, (g + 1) * GATE_TILE)
        zz = jnp.dot(ub[:, sl], wg_ref[g], preferred_element_type=F32)
        r = jax.nn.sigmoid(zz[:, :GATE_TILE] + brg[:, sl])
        i = jax.nn.sigmoid(zz[:, GATE_TILE:] + big[:, sl])
        log_a = (-LRU_C) * r * sp[:, sl]
        a = jnp.exp(log_a)
        m = jnp.sqrt(1.0 - a * a)
        a_parts.append(a)
        i_parts.append(i)
        m_parts.append(m)
    cat = lambda ps: jnp.concatenate(ps, axis=-1)
    return cat(a_parts), cat(i_parts), cat(m_parts)


def _conv(ext, t, cw, cb):
    out = cb
    for j in range(CONV_W):
        s = CONV_W - 1 - j
        out = out + cw[j:j + 1, :] * ext[8 - s:8 - s + t, :]
    return out


def _rnn_meta_kernel(xr_ref, cw_ref, cb_ref, wg_ref, brg_ref, big_ref, lam_ref, h_ref):
    x = xr_ref[...]
    ext = jnp.concatenate([jnp.zeros((8, D_RNN), F32), x], axis=0)
    u = _conv(ext, N_META, cw_ref[...], cb_ref[...])
    a, i, m = _lru_coeffs(u, wg_ref, brg_ref[...], big_ref[...], lam_ref[...])
    first = lax.broadcasted_iota(jnp.int32, (N_META, 1), 0) == 0
    b = jnp.where(first, 1.0, m) * i * u
    h = jnp.zeros((1, D_RNN), F32)
    for t in range(N_META):
        h = a[t:t + 1, :] * h + b[t:t + 1, :]
    h_ref[...] = h


def _linear_scan(a, b, h_prev):
    t, d = a.shape
    g = t // SUBLANES
    a3 = a.reshape(g, SUBLANES, d)
    b3 = b.reshape(g, SUBLANES, d)
    row = lax.broadcasted_iota(jnp.int32, (g, SUBLANES, d), 1)
    step = 1
    while step < SUBLANES:
        keep = row >= step
        a_up = jnp.where(keep, pltpu.roll(a3, step, axis=1), 1.0)
        b_up = jnp.where(keep, pltpu.roll(b3, step, axis=1), 0.0)
        b3 = a3 * b_up + b3
        a3 = a3 * a_up
        step *= 2
    hs = []
    h = h_prev
    for k in range(g):
        hk = a3[k] * h + b3[k]
        hs.append(hk)
        h = hk[SUBLANES - 1:SUBLANES, :]
    return jnp.concatenate(hs, axis=0), h


def _rnn_prompt_kernel(xr_ref, gr_ref, xm_ref, h0_ref, cw_ref, cb_ref, wg_ref, brg_ref, big_ref,
                       lam_ref, m_ref, hl_ref, halo_s, h_s, *, n_b, tt):
    j = pl.program_id(0)

    @pl.when(j == 0)
    def _():
        h_s[...] = jnp.broadcast_to(h0_ref[...], (n_b, D_RNN))
        for b in range(n_b):
            halo_s[b] = xm_ref[N_META - 8:N_META, :]

    cw = cw_ref[...]
    cb = cb_ref[...]

    def per_batch(b, c):
        x = xr_ref[b]
        ext = jnp.concatenate([halo_s[b], x], axis=0)
        halo_s[b] = x[tt - 8:tt, :]
        u = _conv(ext, tt, cw, cb)
        a, i, m = _lru_coeffs(u, wg_ref, brg_ref[...], big_ref[...], lam_ref[...])
        y, h = _linear_scan(a, m * i * u, h_s[pl.ds(b, 1), :])
        h_s[pl.ds(b, 1), :] = h
        m_ref[b] = (jax.nn.sigmoid(gr_ref[b].astype(F32)) * y).astype(BF16)
        return c

    lax.fori_loop(0, n_b, per_batch, 0)
    hl_ref[...] = h_s[...]


def _rnn_sample_kernel(xr_ref, gr_ref, sc_ref, h0_ref, cw_ref, cb_ref, wg_ref, brg_ref, big_ref,
                       lam_ref, m_ref, hn_ref):
    cw = cw_ref[...]
    u = cb_ref[...] + cw[CONV_W - 1:CONV_W, :] * xr_ref[...]
    for j in range(CONV_W - 1):
        u = u + cw[j:j + 1, :] * sc_ref[j]
    a, i, m = _lru_coeffs(u, wg_ref, brg_ref[...], big_ref[...], lam_ref[...])
    h = a * h0_ref[...] + m * i * u
    hn_ref[...] = h
    m_ref[...] = (jax.nn.sigmoid(gr_ref[...].astype(F32)) * h).astype(BF16)


def _alibi_slope(h):
    return 2.0 ** (-8.0 * (h + 1) / N_HEADS)


def _attn_bias_table():
    assert WINDOW == BLOCK
    qi = np.arange(BLOCK, dtype=np.int32)[:, None]
    ci = np.arange(BLOCK, dtype=np.int32)[None, :]
    own = ci <= qi
    dist = np.where(own, qi - ci, qi + BLOCK - ci).astype(np.float32)
    slopes = np.asarray([_alibi_slope(h) for h in range(N_HEADS)], np.float32)[:, None, None]
    ali = -slopes * dist[None]
    tables = [np.where((own | (ci >= c_min))[None], ali, np.float32(NEG_BIG))
              for c_min in (BLOCK - N_META, 0)]
    return jnp.asarray(np.stack(tables).astype(np.float32))


def _attn_prompt_kernel(sink_ref, q_ref, kc_ref, kp_ref, km_ref, bias_ref, ga_ref, o_ref):
    first = pl.program_id(1) == 0
    kv_prev = jnp.where(first, km_ref[...], kp_ref[...])
    kv = jnp.concatenate([kv_prev, kc_ref[...]], axis=0).astype(BF16)
    q = q_ref[...]
    qi = lax.broadcasted_iota(jnp.int32, (BLOCK, BLOCK), 0)
    ci = lax.broadcasted_iota(jnp.int32, (BLOCK, BLOCK), 1)
    own = ci <= qi
    zero = jnp.zeros((BLOCK, BLOCK), BF16)
    outs = []
    for h in range(N_HEADS):
        g = h // GROUP
        kh = kv[:, g * HEAD_DIM:(g + 1) * HEAD_DIM]
        vh = kv[:, D_KV + g * HEAD_DIM:D_KV + (g + 1) * HEAD_DIM]
        qh = q[:, h * HEAD_DIM:(h + 1) * HEAD_DIM]
        s2 = lax.dot_general(qh, kh, (((1,), (1,)), ((), ())), preferred_element_type=F32)
        s = jnp.where(own, s2[:, BLOCK:], s2[:, :BLOCK]) + bias_ref[0, h]
        sink = sink_ref[h]
        mx = jnp.maximum(jnp.max(s, axis=-1, keepdims=True), sink)
        p = jnp.exp(s - mx)
        den = jnp.sum(p, axis=-1, keepdims=True) + jnp.exp(sink - mx)
        pb = p.astype(BF16)
        p2 = jnp.concatenate([jnp.where(own, zero, pb), jnp.where(own, pb, zero)], axis=1)
        o = jnp.dot(p2, vh, preferred_element_type=F32)
        outs.append(o / den)
    y = jnp.concatenate(outs, axis=-1)
    o_ref[...] = (jax.nn.sigmoid(ga_ref[...].astype(F32)) * y).astype(BF16)


SAMPLES_PER_STEP = 8


def _attn_sample_kernel(sink_ref, q_ref, kvn_ref, ck_ref, cv_ref, ga_ref, o_ref):
    w_buf = ck_ref.shape[1]
    n_rows = 2 * N_HEADS
    n_blocks = 2 * N_KV
    ci = lax.broadcasted_iota(jnp.int32, (n_rows, w_buf), 1)
    dist = w_buf - ci
    valid = dist < WINDOW
    row = lax.broadcasted_iota(jnp.int32, (n_rows, 1), 0)
    slope = jnp.zeros((n_rows, 1), F32)
    sink = jnp.zeros((n_rows, 1), F32)
    for h in range(N_HEADS):
        slope = jnp.where(row % N_HEADS == h, _alibi_slope(h), slope)
        sink = jnp.where(row % N_HEADS == h, sink_ref[h], sink)
    bias = jnp.where(valid, -slope * dist.astype(F32), NEG_BIG)
    row_block = lax.broadcasted_iota(jnp.int32, (n_rows, n_blocks * HEAD_DIM), 0) // GROUP
    lane_block = lax.broadcasted_iota(jnp.int32, (n_rows, n_blocks * HEAD_DIM), 1) // HEAD_DIM
    own_block = row_block == lane_block
    for n in range(0, SAMPLES_PER_STEP, 2):
        q2 = jnp.concatenate([q_ref[n], q_ref[n + 1]], axis=0)
        qd = jnp.where(own_block, jnp.concatenate([q2] * n_blocks, axis=1), jnp.zeros((), BF16))
        kk = jnp.concatenate([ck_ref[n], ck_ref[n + 1]], axis=1).astype(BF16)
        vv = jnp.concatenate([cv_ref[n], cv_ref[n + 1]], axis=1).astype(BF16)
        kvn = [kvn_ref[n + t].astype(BF16).astype(F32) for t in range(2)]
        k_new = jnp.concatenate([kvn[0][:, :D_KV], kvn[1][:, :D_KV]], axis=1)
        v_new = jnp.concatenate([kvn[0][:, D_KV:], kvn[1][:, D_KV:]], axis=1)
        s = lax.dot_general(qd, kk, (((1,), (1,)), ((), ())), preferred_element_type=F32) + bias
        sn = jnp.sum(qd.astype(F32) * k_new, axis=-1, keepdims=True)
        mx = jnp.maximum(jnp.maximum(jnp.max(s, axis=-1, keepdims=True), sn), sink)
        p = jnp.exp(s - mx)
        pn = jnp.exp(sn - mx)
        den = jnp.sum(p, axis=-1, keepdims=True) + pn + jnp.exp(sink - mx)
        o = jnp.dot(p.astype(BF16), vv, preferred_element_type=F32)
        o = o + pn.astype(BF16).astype(F32) * v_new
        y = jnp.concatenate([o[j * GROUP:(j + 1) * GROUP, j * HEAD_DIM:(j + 1) * HEAD_DIM]
                             for j in range(n_blocks)], axis=0) / den
        for t in range(2):
            rows = slice(t * N_HEADS, (t + 1) * N_HEADS)
            o_ref[n + t] = (jax.nn.sigmoid(ga_ref[n + t].astype(F32)) * y[rows]).astype(BF16)


def _merge_kernel(mr_ref, ma_ref, x_ref, wo_ref, nf_ref, wr_ref, br_ref, *rest, n_steps):
    x1_ref, hn_ref, ti_ref, tw_ref = rest[-4:]

    @pl.when(pl.program_id(0) >= n_steps)
    def _():
        hn_ref[...] = jnp.zeros_like(hn_ref)

    @pl.when(pl.program_id(0) < n_steps)
    def _():
        _merge_tile(mr_ref, ma_ref, x_ref, wo_ref, nf_ref, wr_ref, br_ref, x1_ref, hn_ref, ti_ref,
                    tw_ref)


def _merge_tile(mr_ref, ma_ref, x_ref, wo_ref, nf_ref, wr_ref, br_ref, x1_ref, hn_ref, ti_ref,
                tw_ref):
    mm = jnp.dot(mr_ref[...], wo_ref[:D_RNN, :], preferred_element_type=F32)
    mm = mm + jnp.dot(ma_ref[...], wo_ref[D_RNN:, :], preferred_element_type=F32)
    x1 = x_ref[...] + mm
    x1_ref[...] = x1
    ms = jnp.mean(x1 * x1, axis=-1, keepdims=True)
    hn = x1 * lax.rsqrt(ms + EPS) * nf_ref[...]
    _store_row_tiled(hn, hn_ref)
    tm = hn.shape[0]
    l = lax.dot_general(wr_ref[...], hn.astype(BF16), (((1,), (1,)), ((), ())),
                        preferred_element_type=F32) + br_ref[...]
    eid = lax.broadcasted_iota(jnp.int32, l.shape, 0)
    vals, idxs = [], []
    for k in range(TOP_K):
        mx = jnp.max(l, axis=0, keepdims=True)
        idx = jnp.min(jnp.where(l == mx, eid, N_EXPERTS), axis=0, keepdims=True)
        vals.append(mx)
        idxs.append(idx)
        l = jnp.where(eid == idx, NEG_BIG, l)
    es = [jnp.exp(v - vals[0]) for v in vals]
    tot = es[0] + es[1] + es[2] + es[3]
    pad = SUBLANES - TOP_K
    ti_ref[...] = jnp.concatenate(idxs + [jnp.zeros((pad, tm), jnp.int32)], axis=0)
    tw_t = jnp.concatenate([e / tot for e in es] + [jnp.zeros((pad, tm), F32)], axis=0)
    eye = (lax.broadcasted_iota(jnp.int32, (tm, tm), 0)
           == lax.broadcasted_iota(jnp.int32, (tm, tm), 1)).astype(F32)
    tw_rows = lax.dot_general(eye, tw_t, (((1,), (1,)), ((), ())), preferred_element_type=F32,
                              precision=lax.Precision.HIGHEST)
    tw_ref[...] = jnp.concatenate([tw_rows, jnp.zeros((tm, LANES - SUBLANES), F32)], axis=1)


def _merge(m_rnn, m_attn, x, w_out_bf, nf, wr_pad, br_pad, hn_all, n_all, row_block0, tm):
    rows = x.shape[0]
    n_steps = rows // tm
    n_zero_steps = 0 if hn_all is not None else n_all // tm - n_steps
    row = lambda w: pl.BlockSpec((tm, w), lambda i: (jnp.minimum(i, n_steps - 1), 0))
    in_specs = [row(D_RNN), row(D_ATTN), row(D_MODEL), _full((D_RNN + D_ATTN, D_MODEL)),
                _full((1, D_MODEL)), _full((N_EXPERTS, D_MODEL)), _full((N_EXPERTS, 1))]
    args = [m_rnn, m_attn, x, w_out_bf, nf, wr_pad, br_pad]
    aliases = {}
    if hn_all is not None:
        in_specs.append(pl.BlockSpec(memory_space=pl.ANY))
        args.append(hn_all)
        aliases = {len(args) - 1: 1}
    return pl.pallas_call(
        functools.partial(_merge_kernel, n_steps=n_steps),
        grid=(n_steps + n_zero_steps,),
        in_specs=in_specs,
        out_specs=[row(D_MODEL),
                   pl.BlockSpec((tm,) + ROW_TILE, lambda i: (i + row_block0, 0, 0)),
                   pl.BlockSpec((SUBLANES, tm), lambda i: (0, jnp.minimum(i, n_steps - 1))),
                   row(LANES)],
        out_shape=[
            jax.ShapeDtypeStruct((rows, D_MODEL), F32),
            jax.ShapeDtypeStruct((n_all,) + ROW_TILE, F32),
            jax.ShapeDtypeStruct((SUBLANES, rows), jnp.int32),
            jax.ShapeDtypeStruct((rows, LANES), F32),
        ],
        input_output_aliases=aliases,
        compiler_params=_cparams(1),
        name="merge_router",
    )(*args)


UP_TILE = 256


def _deinterleave_matrix():
    half = UP_TILE // 2
    r = lax.broadcasted_iota(jnp.int32, (UP_TILE, UP_TILE), 0)
    c = lax.broadcasted_iota(jnp.int32, (UP_TILE, UP_TILE), 1)
    src = jnp.where(c < half, 2 * c, 2 * (c - half) + 1)
    return jnp.where(r == src, 1.0, 0.0).astype(BF16)


GATHER_AHEAD = 2
GATHER_DMA_PRIORITY = 0
OTHER_DMA_PRIORITY = 1


def _moe_kernel(be_ref, nxt_ref, nval_ref, nu_ref, tok0_ref, tok1_ref, tok2_ref, asg_ref, hn_hbm,
                wu_hbm, bu_ref, wd_hbm, bd_ref, out_hbm, xbuf, xb_s, ystage, wu_f, wd_f, wu_s, wd_s,
                sems, ssems, wsems):
    i = pl.program_id(0)
    nu = nu_ref[0]
    half = UP_TILE // 2
    n_up = 2 * D_FF // UP_TILE
    slot = i % GATHER_AHEAD
    prev_slot = 1 - slot
    n_prev = nval_ref[i]

    def row_copy(tok_ref, r, s):
        return pltpu.make_async_copy(hn_hbm.at[pl.ds(tok_ref[0, 0, r], 1)],
                                     xbuf.at[s, pl.ds(r, 1)], sems.at[s])

    def weight_copies(e):
        return (pltpu.make_async_copy(wu_hbm.at[e], wu_f, wsems.at[0]),
                pltpu.make_async_copy(wd_hbm.at[e], wd_f, wsems.at[1]))

    def out_copy(r):
        return pltpu.make_async_copy(ystage.at[prev_slot, pl.ds(r, 1)],
                                     out_hbm.at[pl.ds(asg_ref[0, 0, r], 1)], ssems.at[prev_slot])

    @pl.when(i == 0)
    def _():
        for cp in weight_copies(be_ref[0]):
            cp.start(priority=OTHER_DMA_PRIORITY)

        def issue(r, c):
            row_copy(tok0_ref, r, 0).start(priority=GATHER_DMA_PRIORITY)
            row_copy(tok1_ref, r, 1).start(priority=GATHER_DMA_PRIORITY)
            return c

        lax.fori_loop(0, TM_MOE, issue, 0)

    @pl.when(i < nu + GATHER_AHEAD)
    def _():
        pltpu.make_async_copy(hn_hbm.at[pl.ds(0, TM_MOE)], xbuf.at[slot], sems.at[slot]).wait()

    @pl.when((i >= 2) & (i < nu + 2))
    def _():
        n_sent = nval_ref[i - 1]
        n_tiled = pl.multiple_of((n_sent // SUBLANES) * SUBLANES, SUBLANES)

        @pl.when(n_tiled > 0)
        def _():
            pltpu.make_async_copy(ystage.at[slot, pl.ds(0, n_tiled)],
                                  out_hbm.at[pl.ds(0, n_tiled)], ssems.at[slot]).wait()

        def wait_row(r, c):
            pltpu.make_async_copy(ystage.at[slot, pl.ds(0, 1)], out_hbm.at[pl.ds(0, 1)],
                                  ssems.at[slot]).wait()
            return c

        lax.fori_loop(0, n_sent - n_tiled, wait_row, 0)

    @pl.when(i < nu)
    def _():
        xb_s[...] = _load_row_tiled(xbuf.at[slot], TM_MOE).astype(BF16)

        @pl.when((i == 0) | (be_ref[i] != be_ref[jnp.maximum(i - 1, 0)]))
        def _():
            for cp in weight_copies(be_ref[i]):
                cp.wait()
            perm = _deinterleave_matrix()
            for c in range(n_up):
                cols = slice(c * UP_TILE, (c + 1) * UP_TILE)
                blk = wu_f[:, cols].astype(BF16)
                wu_s[:, cols] = jnp.dot(blk, perm, preferred_element_type=F32).astype(BF16)
            wd_s[...] = wd_f[...].astype(BF16)

            @pl.when(nxt_ref[i] >= 0)
            def _():
                for cp in weight_copies(nxt_ref[i]):
                    cp.start(priority=OTHER_DMA_PRIORITY)

    def send_some_rows():
        def send(r, c):
            out_copy(r).start(priority=OTHER_DMA_PRIORITY)
            return c

        lax.fori_loop(0, n_prev, send, 0)

    def run_block(send_all_rows):
        for r in range(TM_MOE):
            row_copy(tok2_ref, r, slot).start(priority=GATHER_DMA_PRIORITY)
            if send_all_rows:
                out_copy(r).start(priority=OTHER_DMA_PRIORITY)
        x = xb_s[...]
        z = jnp.dot(x, wu_s[...], preferred_element_type=F32) + bu_ref[0]
        zg = jnp.concatenate([z[:, c * UP_TILE:c * UP_TILE + half] for c in range(n_up)], axis=-1)
        zl = jnp.concatenate([z[:, c * UP_TILE + half:(c + 1) * UP_TILE] for c in range(n_up)],
                             axis=-1)
        xg = jnp.minimum(zg, SWIGLU_LIMIT)
        xl = jnp.clip(zl, -SWIGLU_LIMIT, SWIGLU_LIMIT)
        act = xg * jax.nn.sigmoid(SWIGLU_ALPHA * xg) * (xl + 1.0)
        y = jnp.dot(act.astype(BF16), wd_s[...], preferred_element_type=F32) + bd_ref[0]
        _store_row_tiled(y, ystage.at[slot])

    @pl.when((i < nu) & (n_prev == TM_MOE))
    def _():
        run_block(send_all_rows=True)

    @pl.when((i < nu) & (n_prev != TM_MOE))
    def _():
        send_some_rows()
        run_block(send_all_rows=False)

    @pl.when(i == nu)
    def _():
        send_some_rows()


def _moe(blk_expert, next_expert, n_valid, n_used, buf_tok, buf_asg, hn_all, w_up, b_up_perm,
         w_down, b_down, n_blk, n_assign):
    ew = lambda r, c: pl.BlockSpec((1, r, c), lambda i, be, nxt, nv, nu: (be[i], 0, 0))
    tok = lambda f: pl.BlockSpec((1, 1, TM_MOE), lambda i, be, nxt, nv, nu: (f(i), 0, 0),
                                 memory_space=pltpu.SMEM)
    hbm = pl.BlockSpec(memory_space=pl.ANY)
    return pl.pallas_call(
        _moe_kernel,
        grid_spec=pltpu.PrefetchScalarGridSpec(
            num_scalar_prefetch=4,
            grid=(n_blk,),
            in_specs=[
                tok(lambda i: i), tok(lambda i: jnp.minimum(i + 1, n_blk - 1)),
                tok(lambda i: jnp.minimum(i + GATHER_AHEAD, n_blk - 1)), tok(lambda i: i),
                hbm, hbm, ew(1, 2 * D_FF), hbm, ew(1, D_MODEL),
            ],
            out_specs=hbm,
            scratch_shapes=[pltpu.VMEM((GATHER_AHEAD, TM_MOE) + ROW_TILE, F32),
                            pltpu.VMEM((TM_MOE, D_MODEL), BF16),
                            pltpu.VMEM((2, TM_MOE) + ROW_TILE, F32),
                            pltpu.VMEM((D_MODEL, 2 * D_FF), F32),
                            pltpu.VMEM((D_FF, D_MODEL), F32),
                            pltpu.VMEM((D_MODEL, 2 * D_FF), BF16),
                            pltpu.VMEM((D_FF, D_MODEL), BF16),
                            pltpu.SemaphoreType.DMA((GATHER_AHEAD,)),
                            pltpu.SemaphoreType.DMA((2,)),
                            pltpu.SemaphoreType.DMA((2,))],
        ),
        out_shape=jax.ShapeDtypeStruct((n_assign,) + ROW_TILE, F32),
        compiler_params=_cparams(1),
        name="moe_experts",
    )(blk_expert, next_expert, n_valid, n_used, buf_tok, buf_tok, buf_tok, buf_asg, hn_all, w_up,
      b_up_perm, w_down, b_down)


def _final_kernel(x1_ref, tw_ref, nf_ref, *rest):
    y_refs, o_ref = rest[:TOP_K], rest[TOP_K]
    tw = tw_ref[...]
    x = x1_ref[...]
    for k in range(TOP_K):
        x = x + tw[:, k:k + 1] * _load_row_tiled(y_refs[k].at[0], x.shape[0])
    ms = jnp.mean(x * x, axis=-1, keepdims=True)
    o_ref[...] = x * lax.rsqrt(ms + EPS) * nf_ref[...]


def _final(x1, tw, nf, y_choice, row0):
    rows = x1.shape[0]
    tm = min(TM_FIN, rows)
    row_block0 = row0 // tm
    row = lambda w: pl.BlockSpec((tm, w), lambda i: (i, 0))
    y_spec = lambda k: pl.BlockSpec((1, tm) + ROW_TILE, lambda i: (k, i + row_block0, 0, 0))
    return pl.pallas_call(
        _final_kernel,
        grid=(rows // tm,),
        in_specs=[row(D_MODEL), row(LANES), _full((1, D_MODEL))]
        + [y_spec(k) for k in range(TOP_K)],
        out_specs=row(D_MODEL),
        out_shape=jax.ShapeDtypeStruct((rows, D_MODEL), F32),
        compiler_params=_cparams(1),
        name="combine_final",
    )(x1, tw, nf, *([y_choice] * TOP_K))


def _block_diag_tiles(w):
    per = GATE_TILE // GATE_BW
    w4 = w.reshape(N_GATE_TILES, per, GATE_BW, GATE_BW)
    eye = jnp.eye(per, dtype=w.dtype)
    return jnp.einsum("gacd,ab->gacbd", w4, eye).reshape(N_GATE_TILES, GATE_TILE, GATE_TILE)


def _route(top_i_groups, n_blk):
    e_ids = jnp.arange(N_EXPERTS, dtype=jnp.int32)
    n_tokens = sum(t.shape[1] for t in top_i_groups)
    a = n_tokens * TOP_K
    tables, cnts, bases = [], [], []
    tok0 = 0
    for ti in top_i_groups:
        n_g = ti.shape[1]
        bits = max(n_g - 1, 1).bit_length()
        keys = (ti << bits) | jnp.arange(n_g, dtype=jnp.int32)[None, :]
        tok = (jnp.sort(keys, axis=1) & ((1 << bits) - 1)) + tok0
        tables.append((tok + jnp.arange(TOP_K, dtype=jnp.int32)[:, None] * n_tokens).reshape(-1))
        cnts.append(jnp.sum((ti[:, :, None] == e_ids[None, None, :]).astype(jnp.int32), axis=1))
        bases += [tok0 * TOP_K + k * n_g for k in range(TOP_K)]
        tok0 += n_g
    table = jnp.concatenate(tables)
    cnt = jnp.concatenate(cnts, axis=0)
    col_off = jnp.cumsum(cnt, axis=1) - cnt
    col_cum = jnp.cumsum(cnt, axis=0) - cnt
    counts = jnp.sum(cnt, axis=0)
    col_shift = jnp.asarray(bases, jnp.int32)[:, None] + col_off - col_cum
    col_step = col_shift - jnp.concatenate([jnp.zeros((1, N_EXPERTS), jnp.int32), col_shift[:-1]])
    padded = ((counts + (TM_MOE - 1)) // TM_MOE) * TM_MOE
    ends = jnp.cumsum(padded).astype(jnp.int32)
    start_padded = ends - padded
    blk_start = jnp.arange(n_blk, dtype=jnp.int32) * TM_MOE
    blk_expert = jnp.minimum(
        jnp.sum((ends[None, :] <= blk_start[:, None]).astype(jnp.int32), axis=1), N_EXPERTS - 1)
    n_used = (ends[-1:] // TM_MOE).astype(jnp.int32)
    blk_onehot = blk_expert[:, None] == e_ids[None, :]

    def per_block(tab):
        return jnp.sum(jnp.where(blk_onehot, tab[..., None, :], 0), axis=-1).astype(jnp.int32)

    n_valid = jnp.clip(per_block(start_padded + counts) - blk_start, 0, TM_MOE).astype(jnp.int32)
    per_row = lambda v: jnp.repeat(v, TM_MOE, axis=-1)
    row_id = jnp.arange(n_blk * TM_MOE, dtype=jnp.int32)
    j = row_id - per_row(per_block(start_padded))
    pos = j + jnp.sum(jnp.where(j[None, :] >= per_row(per_block(col_cum)),
                                per_row(per_block(col_step)), 0), axis=0)
    pos = jnp.where(j < per_row(per_block(counts)), pos, row_id % a)
    buf_asg = table[pos]
    buf_tok = buf_asg % n_tokens
    n_valid = jnp.concatenate([jnp.zeros((1,), jnp.int32), n_valid])
    buf_asg = jnp.concatenate([jnp.zeros((TM_MOE,), jnp.int32), buf_asg])
    later_used = (e_ids[None, :] > e_ids[:, None]) & (counts[None, :] > 0)
    next_used = jnp.min(jnp.where(later_used, e_ids[None, :], N_EXPERTS), axis=1)
    next_used = jnp.where(next_used == N_EXPERTS, -1, next_used).astype(jnp.int32)
    return (buf_tok.reshape(n_blk, 1, TM_MOE), buf_asg.reshape(n_blk + 1, 1, TM_MOE),
            blk_expert.astype(jnp.int32), per_block(next_used), n_valid, n_used)


def kernel(x_prompt, x_sample, cache_k, cache_v, state_conv, state_h, meta_tokens, norm_mix, w_in, conv_w, conv_b, w_rg, b_rg, w_ig, b_ig, lru_lambda, attn_sinks, w_out, norm_ffn, w_router, b_router, w_up, b_up, w_down, b_down, norm_final):
    n_b, seq = x_prompt.shape[0], x_prompt.shape[1]
    n_s = x_sample.shape[0]
    w_buf = cache_k.shape[2]
    n_p = n_b * seq
    n_tok = n_p + n_s
    row1 = lambda v: v.reshape(1, -1)

    w_in_bf = w_in[0].astype(BF16)
    w_out_bf = w_out[0].astype(BF16)
    wg_tiles = jnp.concatenate([_block_diag_tiles(w_rg[0]), _block_diag_tiles(w_ig[0])],
                               axis=-1).astype(BF16)
    wr_pad = w_router[0].T.astype(BF16)
    br_pad = b_router[0].reshape(N_EXPERTS, 1)
    b_up_perm = jnp.swapaxes(b_up[0].reshape(N_EXPERTS, 2 * D_FF // UP_TILE, UP_TILE // 2, 2),
                             2, 3).reshape(N_EXPERTS, 1, 2 * D_FF)
    b_dn = b_down[0].reshape(N_EXPERTS, 1, D_MODEL)
    nm, nf, nfin = row1(norm_mix[0]), row1(norm_ffn[0]), row1(norm_final)
    cw, cb = conv_w[0], row1(conv_b[0])
    brg, big, lam = row1(b_rg[0]), row1(b_ig[0]), row1(lru_lambda[0])
    sinks = attn_sinks[0]
    rnn_w = (cw, cb, wg_tiles, brg, big, lam)
    rnn_w_specs = [_full((CONV_W, D_RNN)), _full((1, D_RNN)),
                   _full((N_GATE_TILES, GATE_TILE, 2 * GATE_TILE)),
                   _full((1, D_RNN)), _full((1, D_RNN)), _full((1, D_RNN))]

    xp2 = x_prompt.reshape(n_p, D_MODEL)
    xr_p, q_p, kv_p, gr_p, ga_p = _in_proj(xp2, nm, w_in_bf, TM_IN, BF16)
    x_sm = jnp.concatenate([x_sample.reshape(n_s, D_MODEL), meta_tokens], axis=0)
    xr_sm, q_sm, kv_sm, gr_sm, ga_sm = _in_proj(x_sm, nm, w_in_bf, n_s + N_META, F32)
    xr_s, q_s, kv_s, gr_s, ga_s = (t[:n_s] for t in (xr_sm, q_sm, kv_sm, gr_sm, ga_sm))
    xr_m, kv_m = xr_sm[n_s:], kv_sm[n_s:]

    h_meta = pl.pallas_call(
        _rnn_meta_kernel,
        in_specs=[_full((N_META, D_RNN))] + rnn_w_specs,
        out_specs=_full((1, D_RNN)),
        out_shape=jax.ShapeDtypeStruct((1, D_RNN), F32),
        grid=(1,),
        compiler_params=_cparams(1),
        name="rnn_meta",
    )(xr_m, *rnn_w)

    tt = TT_RNN
    blk3 = pl.BlockSpec((n_b, tt, D_RNN), lambda j: (0, j, 0))
    m_rnn_p, h_last_p = pl.pallas_call(
        functools.partial(_rnn_prompt_kernel, n_b=n_b, tt=tt),
        grid=(seq // tt,),
        in_specs=[blk3, blk3, _full((N_META, D_RNN)), _full((1, D_RNN))] + rnn_w_specs,
        out_specs=[blk3, _full((n_b, D_RNN))],
        out_shape=[jax.ShapeDtypeStruct((n_b, seq, D_RNN), BF16),
                   jax.ShapeDtypeStruct((n_b, D_RNN), F32)],
        scratch_shapes=[pltpu.VMEM((n_b, 8, D_RNN), F32), pltpu.VMEM((n_b, D_RNN), F32)],
        compiler_params=_cparams(1),
        name="rnn_prompt",
    )(xr_p.reshape(n_b, seq, D_RNN), gr_p.reshape(n_b, seq, D_RNN), xr_m, h_meta, *rnn_w)

    sc_t = jnp.swapaxes(state_conv[0], 0, 1)
    m_rnn_s, h_new_s = pl.pallas_call(
        _rnn_sample_kernel,
        grid=(1,),
        in_specs=[_full((n_s, D_RNN)), _full((n_s, D_RNN)), _full((CONV_W - 1, n_s, D_RNN)),
                  _full((n_s, D_RNN))] + rnn_w_specs,
        out_specs=[_full((n_s, D_RNN)), _full((n_s, D_RNN))],
        out_shape=[jax.ShapeDtypeStruct((n_s, D_RNN), BF16),
                   jax.ShapeDtypeStruct((n_s, D_RNN), F32)],
        compiler_params=_cparams(1),
        name="rnn_sample",
    )(xr_s, gr_s, sc_t, state_h[0], *rnn_w)

    n_blk_seq = seq // BLOCK
    kv_meta_blk = jnp.pad(kv_m, ((BLOCK - N_META, 0), (0, 0)))
    smem_spec = pl.BlockSpec(memory_space=pltpu.SMEM)
    rb = lambda w: pl.BlockSpec((BLOCK, w), lambda b, j: (b * n_blk_seq + j, 0))
    m_attn_p = pl.pallas_call(
        _attn_prompt_kernel,
        grid=(n_b, n_blk_seq),
        in_specs=[smem_spec, rb(D_ATTN), rb(2 * D_KV),
                  pl.BlockSpec((BLOCK, 2 * D_KV),
                               lambda b, j: (jnp.maximum(b * n_blk_seq + j - 1, 0), 0)),
                  _full((BLOCK, 2 * D_KV)),
                  pl.BlockSpec((1, N_HEADS, BLOCK, BLOCK),
                               lambda b, j: (jnp.minimum(j, 1), 0, 0, 0)),
                  rb(D_ATTN)],
        out_specs=rb(D_ATTN),
        out_shape=jax.ShapeDtypeStruct((n_p, D_ATTN), BF16),
        compiler_params=_cparams(2),
        name="attn_prompt",
    )(sinks, q_p, kv_p, kv_p, kv_meta_blk, _attn_bias_table(), ga_p)

    ck = cache_k[0].reshape(n_s, w_buf, D_KV)
    cv = cache_v[0].reshape(n_s, w_buf, D_KV)
    per_s = lambda a, b: pl.BlockSpec((SAMPLES_PER_STEP, a, b), lambda i: (i, 0, 0))
    m_attn_s = pl.pallas_call(
        _attn_sample_kernel,
        grid=(n_s // SAMPLES_PER_STEP,),
        in_specs=[smem_spec, per_s(N_HEADS, HEAD_DIM), per_s(1, 2 * D_KV), per_s(w_buf, D_KV),
                  per_s(w_buf, D_KV), per_s(N_HEADS, HEAD_DIM)],
        out_specs=per_s(N_HEADS, HEAD_DIM),
        out_shape=jax.ShapeDtypeStruct((n_s, N_HEADS, HEAD_DIM), BF16),
        compiler_params=_cparams(1),
        name="attn_sample",
    )(sinks, q_s.reshape(n_s, N_HEADS, HEAD_DIM), kv_s.reshape(n_s, 1, 2 * D_KV), ck, cv,
      ga_s.reshape(n_s, N_HEADS, HEAD_DIM)).reshape(n_s, D_ATTN)

    n_hn = n_p + -(-n_s // TM_MERGE) * TM_MERGE
    x1_p, hn_all, ti_p, tw_p = _merge(m_rnn_p.reshape(n_p, D_RNN), m_attn_p, xp2, w_out_bf, nf,
                                      wr_pad, br_pad, None, n_hn, 0, TM_MERGE)
    x1_s, hn_all, ti_s, tw_s = _merge(m_rnn_s, m_attn_s, x_sample.reshape(n_s, D_MODEL), w_out_bf,
                                      nf, wr_pad, br_pad, hn_all, n_hn, n_p // n_s, n_s)

    n_assign = n_tok * TOP_K
    n_blk = n_assign // TM_MOE + N_EXPERTS - 1 + GATHER_AHEAD
    buf_tok, buf_asg, blk_expert, next_expert, n_valid, n_used = _route(
        [ti_p[:TOP_K], ti_s[:TOP_K]], n_blk)

    y_tok = _moe(blk_expert, next_expert, n_valid, n_used, buf_tok, buf_asg, hn_all, w_up[0],
                 b_up_perm, w_down[0], b_dn, n_blk, n_assign).reshape((TOP_K, n_tok) + ROW_TILE)
    y_p = _final(x1_p, tw_p, nfin, y_tok, 0)
    y_s = _final(x1_s, tw_s, nfin, y_tok, n_p)

    kv_p3 = kv_p.reshape(n_b, seq, 2 * D_KV)
    w_p = min(WINDOW, seq + N_META)
    new_k_p = kv_p3[:, seq - w_p:, :D_KV].reshape(1, n_b, w_p, N_KV, HEAD_DIM)
    new_v_p = kv_p3[:, seq - w_p:, D_KV:].reshape(1, n_b, w_p, N_KV, HEAD_DIM)
    new_conv_p = xr_p.reshape(n_b, seq, D_RNN)[:, seq - (CONV_W - 1):][None]
    k_new = kv_s[:, :D_KV].reshape(n_s, 1, N_KV, HEAD_DIM)
    v_new = kv_s[:, D_KV:].reshape(n_s, 1, N_KV, HEAD_DIM)
    new_k_s = jnp.concatenate([cache_k[0], k_new], axis=1)[:, -w_buf:][None]
    new_v_s = jnp.concatenate([cache_v[0], v_new], axis=1)[:, -w_buf:][None]
    new_conv_s = jnp.concatenate([state_conv[0], xr_s[:, None, :]], axis=1)[:, -(CONV_W - 1):][None]
    return (y_p.reshape(n_b, seq, D_MODEL), y_s.reshape(n_s, 1, D_MODEL), new_k_p, new_v_p,
            new_conv_p, h_last_p[None], new_k_s, new_v_s, new_conv_s, h_new_s[None])
```

```python
import functools

import jax
import jax.numpy as jnp
import numpy as np
from jax import lax
from jax.experimental import pallas as pl
from jax.experimental.pallas import tpu as pltpu

F32 = jnp.float32
BF16 = jnp.bfloat16

D_MODEL = 1024
N_META = 16
D_RNN = 1024
N_GATE_BLOCKS = 16
GATE_BW = D_RNN // N_GATE_BLOCKS
CONV_W = 4
LRU_C = 8.0
N_HEADS = 16
HEAD_DIM = 64
N_KV = 2
GROUP = N_HEADS // N_KV
D_ATTN = N_HEADS * HEAD_DIM
D_KV = N_KV * HEAD_DIM
WINDOW = 128
BLOCK = 128
PAST_LEN = 16384
N_EXPERTS = 32
TOP_K = 4
D_FF = 1024
SWIGLU_LIMIT = 7.0
SWIGLU_ALPHA = 1.702
EPS = 1e-6

GATE_TILE = 256
N_GATE_TILES = D_RNN // GATE_TILE
LANES = 128
NEG_BIG = -1e30

TM_IN = 512
TT_RNN = 128
TM_MERGE = 256
TM_MOE = 256
TM_FIN = 256
VMEM_LIMIT = 56 * 1024 * 1024


def _cparams(n_grid_dims):
    return pltpu.CompilerParams(
        dimension_semantics=("arbitrary",) * n_grid_dims, vmem_limit_bytes=VMEM_LIMIT)


def _full(shape):
    return pl.BlockSpec(shape, lambda *_: (0,) * len(shape))


SUBLANES = 8
ROW_TILE = (SUBLANES, D_MODEL // SUBLANES)


def _transpose_vreg_group(v):
    sub = lax.broadcasted_iota(jnp.int32, v[0].shape, 0)
    v = list(v)
    d = SUBLANES // 2
    while d >= 1:
        hi = (sub & d) != 0
        nv = list(v)
        for a in range(SUBLANES):
            if a & d:
                continue
            b = a | d
            nv[a] = jnp.where(hi, pltpu.roll(v[b], d, axis=0), v[a])
            nv[b] = jnp.where(hi, v[b], pltpu.roll(v[a], SUBLANES - d, axis=0))
        v = nv
        d //= 2
    return v


def _store_row_tiled(x, dst):
    for g in range(x.shape[0] // SUBLANES):
        rows = slice(g * SUBLANES, (g + 1) * SUBLANES)
        w = _transpose_vreg_group([x[rows, s * LANES:(s + 1) * LANES] for s in range(SUBLANES)])
        for j in range(SUBLANES):
            dst[g * SUBLANES + j] = w[j]


def _load_row_tiled(src, n_rows):
    groups = []
    for g in range(n_rows // SUBLANES):
        v = _transpose_vreg_group([src[g * SUBLANES + j] for j in range(SUBLANES)])
        groups.append(jnp.concatenate(v, axis=-1))
    return jnp.concatenate(groups, axis=0)


def _in_proj_kernel(x_ref, g_ref, w_ref, xr_ref, q_ref, kv_ref, gr_ref, ga_ref):
    x = x_ref[...]
    ms = jnp.mean(x * x, axis=-1, keepdims=True)
    h = (x * lax.rsqrt(ms + EPS) * g_ref[...]).astype(BF16)

    def proj(lo, hi):
        return jnp.dot(h, w_ref[:, lo:hi], preferred_element_type=F32)

    o_q = D_RNN
    o_k = o_q + D_ATTN
    o_gr = o_k + 2 * D_KV
    o_ga = o_gr + D_RNN
    xr_ref[...] = proj(0, o_q)
    q_ref[...] = (proj(o_q, o_k) * (HEAD_DIM ** -0.5)).astype(BF16)
    kv_ref[...] = proj(o_k, o_gr)
    gr_ref[...] = proj(o_gr, o_ga).astype(gr_ref.dtype)
    ga_ref[...] = proj(o_ga, o_ga + D_ATTN).astype(ga_ref.dtype)


def _in_proj(x, g, w_in_bf, tm, gate_dtype):
    rows = x.shape[0]
    d_in = w_in_bf.shape[1]
    row = lambda w: pl.BlockSpec((tm, w), lambda i: (i, 0))
    return pl.pallas_call(
        _in_proj_kernel,
        grid=(rows // tm,),
        in_specs=[row(D_MODEL), _full((1, D_MODEL)), _full((D_MODEL, d_in))],
        out_specs=[row(D_RNN), row(D_ATTN), row(2 * D_KV), row(D_RNN), row(D_ATTN)],
        out_shape=[
            jax.ShapeDtypeStruct((rows, D_RNN), F32),
            jax.ShapeDtypeStruct((rows, D_ATTN), BF16),
            jax.ShapeDtypeStruct((rows, 2 * D_KV), F32),
            jax.ShapeDtypeStruct((rows, D_RNN), gate_dtype),
            jax.ShapeDtypeStruct((rows, D_ATTN), gate_dtype),
        ],
        compiler_params=_cparams(1),
        name="in_proj",
    )(x, g, w_in_bf)


def _softplus(x):
    return jnp.maximum(x, 0.0) + jnp.log1p(jnp.exp(-jnp.abs(x)))


def _lru_coeffs(u, wg_ref, brg, big, lam):
    ub = u.astype(BF16)
    sp = _softplus(-lam)
    a_parts, i_parts, m_parts = [], [], []
    for g in range(N_GATE_TILES):
        sl = slice(g * GATE_TILE, (g + 1) * GATE_TILE)
        zz = jnp.dot(ub[:, sl], wg_ref[g], preferred_element_type=F32)
        r = jax.nn.sigmoid(zz[:, :GATE_TILE] + brg[:, sl])
        i = jax.nn.sigmoid(zz[:, GATE_TILE:] + big[:, sl])
        log_a = (-LRU_C) * r * sp[:, sl]
        a = jnp.exp(log_a)
        m = jnp.sqrt(1.0 - a * a)
        a_parts.append(a)
        i_parts.append(i)
        m_parts.append(m)
    cat = lambda ps: jnp.concatenate(ps, axis=-1)
    return cat(a_parts), cat(i_parts), cat(m_parts)


def _conv(ext, t, cw, cb):
    out = cb
    for j in range(CONV_W):
        s = CONV_W - 1 - j
        out = out + cw[j:j + 1, :] * ext[8 - s:8 - s + t, :]
    return out


def _rnn_meta_kernel(xr_ref, cw_ref, cb_ref, wg_ref, brg_ref, big_ref, lam_ref, h_ref):
    x = xr_ref[...]
    ext = jnp.concatenate([jnp.zeros((8, D_RNN), F32), x], axis=0)
    u = _conv(ext, N_META, cw_ref[...], cb_ref[...])
    a, i, m = _lru_coeffs(u, wg_ref, brg_ref[...], big_ref[...], lam_ref[...])
    first = lax.broadcasted_iota(jnp.int32, (N_META, 1), 0) == 0
    b = jnp.where(first, 1.0, m) * i * u
    h = jnp.zeros((1, D_RNN), F32)
    for t in range(N_META):
        h = a[t:t + 1, :] * h + b[t:t + 1, :]
    h_ref[...] = h


def _linear_scan(a, b, h_prev):
    t, d = a.shape
    g = t // SUBLANES
    a3 = a.reshape(g, SUBLANES, d)
    b3 = b.reshape(g, SUBLANES, d)
    row = lax.broadcasted_iota(jnp.int32, (g, SUBLANES, d), 1)
    step = 1
    while step < SUBLANES:
        keep = row >= step
        a_up = jnp.where(keep, pltpu.roll(a3, step, axis=1), 1.0)
        b_up = jnp.where(keep, pltpu.roll(b3, step, axis=1), 0.0)
        b3 = a3 * b_up + b3
        a3 = a3 * a_up
        step *= 2
    hs = []
    h = h_prev
    for k in range(g):
        hk = a3[k] * h + b3[k]
        hs.append(hk)
        h = hk[SUBLANES - 1:SUBLANES, :]
    return jnp.concatenate(hs, axis=0), h


def _rnn_prompt_kernel(xr_ref, gr_ref, xm_ref, h0_ref, cw_ref, cb_ref, wg_ref, brg_ref, big_ref,
                       lam_ref, m_ref, hl_ref, halo_s, h_s, *, n_b, tt):
    j = pl.program_id(0)

    @pl.when(j == 0)
    def _():
        h_s[...] = jnp.broadcast_to(h0_ref[...], (n_b, D_RNN))
        for b in range(n_b):
            halo_s[b] = xm_ref[N_META - 8:N_META, :]

    cw = cw_ref[...]
    cb = cb_ref[...]

    def per_batch(b, c):
        x = xr_ref[b]
        ext = jnp.concatenate([halo_s[b], x], axis=0)
        halo_s[b] = x[tt - 8:tt, :]
        u = _conv(ext, tt, cw, cb)
        a, i, m = _lru_coeffs(u, wg_ref, brg_ref[...], big_ref[...], lam_ref[...])
        y, h = _linear_scan(a, m * i * u, h_s[pl.ds(b, 1), :])
        h_s[pl.ds(b, 1), :] = h
        m_ref[b] = (jax.nn.sigmoid(gr_ref[b].astype(F32)) * y).astype(BF16)
        return c

    lax.fori_loop(0, n_b, per_batch, 0)
    hl_ref[...] = h_s[...]


def _rnn_sample_kernel(xr_ref, gr_ref, sc_ref, h0_ref, cw_ref, cb_ref, wg_ref, brg_ref, big_ref,
                       lam_ref, m_ref, hn_ref):
    cw = cw_ref[...]
    u = cb_ref[...] + cw[CONV_W - 1:CONV_W, :] * xr_ref[...]
    for j in range(CONV_W - 1):
        u = u + cw[j:j + 1, :] * sc_ref[j]
    a, i, m = _lru_coeffs(u, wg_ref, brg_ref[...], big_ref[...], lam_ref[...])
    h = a * h0_ref[...] + m * i * u
    hn_ref[...] = h
    m_ref[...] = (jax.nn.sigmoid(gr_ref[...].astype(F32)) * h).astype(BF16)


def _alibi_slope(h):
    return 2.0 ** (-8.0 * (h + 1) / N_HEADS)


def _attn_bias_table():
    assert WINDOW == BLOCK
    qi = np.arange(BLOCK, dtype=np.int32)[:, None]
    ci = np.arange(BLOCK, dtype=np.int32)[None, :]
    own = ci <= qi
    dist = np.where(own, qi - ci, qi + BLOCK - ci).astype(np.float32)
    slopes = np.asarray([_alibi_slope(h) for h in range(N_HEADS)], np.float32)[:, None, None]
    ali = -slopes * dist[None]
    tables = [np.where((own | (ci >= c_min))[None], ali, np.float32(NEG_BIG))
              for c_min in (BLOCK - N_META, 0)]
    return jnp.asarray(np.stack(tables).astype(np.float32))


def _attn_prompt_kernel(sink_ref, q_ref, kc_ref, kp_ref, km_ref, bias_ref, ga_ref, o_ref):
    first = pl.program_id(1) == 0
    kv_prev = jnp.where(first, km_ref[...], kp_ref[...])
    kv = jnp.concatenate([kv_prev, kc_ref[...]], axis=0).astype(BF16)
    q = q_ref[...]
    qi = lax.broadcasted_iota(jnp.int32, (BLOCK, BLOCK), 0)
    ci = lax.broadcasted_iota(jnp.int32, (BLOCK, BLOCK), 1)
    own = ci <= qi
    zero = jnp.zeros((BLOCK, BLOCK), BF16)
    outs = []
    for h in range(N_HEADS):
        g = h // GROUP
        kh = kv[:, g * HEAD_DIM:(g + 1) * HEAD_DIM]
        vh = kv[:, D_KV + g * HEAD_DIM:D_KV + (g + 1) * HEAD_DIM]
        qh = q[:, h * HEAD_DIM:(h + 1) * HEAD_DIM]
        s2 = lax.dot_general(qh, kh, (((1,), (1,)), ((), ())), preferred_element_type=F32)
        s = jnp.where(own, s2[:, BLOCK:], s2[:, :BLOCK]) + bias_ref[0, h]
        sink = sink_ref[h]
        mx = jnp.maximum(jnp.max(s, axis=-1, keepdims=True), sink)
        p = jnp.exp(s - mx)
        den = jnp.sum(p, axis=-1, keepdims=True) + jnp.exp(sink - mx)
        pb = p.astype(BF16)
        p2 = jnp.concatenate([jnp.where(own, zero, pb), jnp.where(own, pb, zero)], axis=1)
        o = jnp.dot(p2, vh, preferred_element_type=F32)
        outs.append(o / den)
    y = jnp.concatenate(outs, axis=-1)
    o_ref[...] = (jax.nn.sigmoid(ga_ref[...].astype(F32)) * y).astype(BF16)


SAMPLES_PER_STEP = 8


def _attn_sample_kernel(sink_ref, q_ref, kvn_ref, ck_ref, cv_ref, ga_ref, o_ref):
    w_buf = ck_ref.shape[1]
    n_rows = 2 * N_HEADS
    n_blocks = 2 * N_KV
    ci = lax.broadcasted_iota(jnp.int32, (n_rows, w_buf), 1)
    dist = w_buf - ci
    valid = dist < WINDOW
    row = lax.broadcasted_iota(jnp.int32, (n_rows, 1), 0)
    slope = jnp.zeros((n_rows, 1), F32)
    sink = jnp.zeros((n_rows, 1), F32)
    for h in range(N_HEADS):
        slope = jnp.where(row % N_HEADS == h, _alibi_slope(h), slope)
        sink = jnp.where(row % N_HEADS == h, sink_ref[h], sink)
    bias = jnp.where(valid, -slope * dist.astype(F32), NEG_BIG)
    row_block = lax.broadcasted_iota(jnp.int32, (n_rows, n_blocks * HEAD_DIM), 0) // GROUP
    lane_block = lax.broadcasted_iota(jnp.int32, (n_rows, n_blocks * HEAD_DIM), 1) // HEAD_DIM
    own_block = row_block == lane_block
    for n in range(0, SAMPLES_PER_STEP, 2):
        q2 = jnp.concatenate([q_ref[n], q_ref[n + 1]], axis=0)
        qd = jnp.where(own_block, jnp.concatenate([q2] * n_blocks, axis=1), jnp.zeros((), BF16))
        kk = jnp.concatenate([ck_ref[n], ck_ref[n + 1]], axis=1).astype(BF16)
        vv = jnp.concatenate([cv_ref[n], cv_ref[n + 1]], axis=1).astype(BF16)
        kvn = [kvn_ref[n + t].astype(BF16).astype(F32) for t in range(2)]
        k_new = jnp.concatenate([kvn[0][:, :D_KV], kvn[1][:, :D_KV]], axis=1)
        v_new = jnp.concatenate([kvn[0][:, D_KV:], kvn[1][:, D_KV:]], axis=1)
        s = lax.dot_general(qd, kk, (((1,), (1,)), ((), ())), preferred_element_type=F32) + bias
        sn = jnp.sum(qd.astype(F32) * k_new, axis=-1, keepdims=True)
        mx = jnp.maximum(jnp.maximum(jnp.max(s, axis=-1, keepdims=True), sn), sink)
        p = jnp.exp(s - mx)
        pn = jnp.exp(sn - mx)
        den = jnp.sum(p, axis=-1, keepdims=True) + pn + jnp.exp(sink - mx)
        o = jnp.dot(p.astype(BF16), vv, preferred_element_type=F32)
        o = o + pn.astype(BF16).astype(F32) * v_new
        y = jnp.concatenate([o[j * GROUP:(j + 1) * GROUP, j * HEAD_DIM:(j + 1) * HEAD_DIM]
                             for j in range(n_blocks)], axis=0) / den
        for t in range(2):
            rows = slice(t * N_HEADS, (t + 1) * N_HEADS)
            o_ref[n + t] = (jax.nn.sigmoid(ga_ref[n + t].astype(F32)) * y[rows]).astype(BF16)


def _merge_kernel(mr_ref, ma_ref, x_ref, wo_ref, nf_ref, wr_ref, br_ref, *rest, n_steps):
    x1_ref, hn_ref, ti_ref, tw_ref = rest[-4:]

    @pl.when(pl.program_id(0) >= n_steps)
    def _():
        hn_ref[...] = jnp.zeros_like(hn_ref)

    @pl.when(pl.program_id(0) < n_steps)
    def _():
        _merge_tile(mr_ref, ma_ref, x_ref, wo_ref, nf_ref, wr_ref, br_ref, x1_ref, hn_ref, ti_ref,
                    tw_ref)


def _merge_tile(mr_ref, ma_ref, x_ref, wo_ref, nf_ref, wr_ref, br_ref, x1_ref, hn_ref, ti_ref,
                tw_ref):
    mm = jnp.dot(mr_ref[...], wo_ref[:D_RNN, :], preferred_element_type=F32)
    mm = mm + jnp.dot(ma_ref[...], wo_ref[D_RNN:, :], preferred_element_type=F32)
    x1 = x_ref[...] + mm
    x1_ref[...] = x1
    ms = jnp.mean(x1 * x1, axis=-1, keepdims=True)
    hn = x1 * lax.rsqrt(ms + EPS) * nf_ref[...]
    _store_row_tiled(hn, hn_ref)
    tm = hn.shape[0]
    l = lax.dot_general(wr_ref[...], hn.astype(BF16), (((1,), (1,)), ((), ())),
                        preferred_element_type=F32) + br_ref[...]
    eid = lax.broadcasted_iota(jnp.int32, l.shape, 0)
    vals, idxs = [], []
    for k in range(TOP_K):
        mx = jnp.max(l, axis=0, keepdims=True)
        idx = jnp.min(jnp.where(l == mx, eid, N_EXPERTS), axis=0, keepdims=True)
        vals.append(mx)
        idxs.append(idx)
        l = jnp.where(eid == idx, NEG_BIG, l)
    es = [jnp.exp(v - vals[0]) for v in vals]
    tot = es[0] + es[1] + es[2] + es[3]
    pad = SUBLANES - TOP_K
    ti_ref[...] = jnp.concatenate(idxs + [jnp.zeros((pad, tm), jnp.int32)], axis=0)
    tw_t = jnp.concatenate([e / tot for e in es] + [jnp.zeros((pad, tm), F32)], axis=0)
    eye = (lax.broadcasted_iota(jnp.int32, (tm, tm), 0)
           == lax.broadcasted_iota(jnp.int32, (tm, tm), 1)).astype(F32)
    tw_rows = lax.dot_general(eye, tw_t, (((1,), (1,)), ((), ())), preferred_element_type=F32,
                              precision=lax.Precision.HIGHEST)
    tw_ref[...] = jnp.concatenate([tw_rows, jnp.zeros((tm, LANES - SUBLANES), F32)], axis=1)


def _merge(m_rnn, m_attn, x, w_out_bf, nf, wr_pad, br_pad, hn_all, n_all, row_block0, tm):
    rows = x.shape[0]
    n_steps = rows // tm
    n_zero_steps = 0 if hn_all is not None else n_all // tm - n_steps
    row = lambda w: pl.BlockSpec((tm, w), lambda i: (jnp.minimum(i, n_steps - 1), 0))
    in_specs = [row(D_RNN), row(D_ATTN), row(D_MODEL), _full((D_RNN + D_ATTN, D_MODEL)),
                _full((1, D_MODEL)), _full((N_EXPERTS, D_MODEL)), _full((N_EXPERTS, 1))]
    args = [m_rnn, m_attn, x, w_out_bf, nf, wr_pad, br_pad]
    aliases = {}
    if hn_all is not None:
        in_specs.append(pl.BlockSpec(memory_space=pl.ANY))
        args.append(hn_all)
        aliases = {len(args) - 1: 1}
    return pl.pallas_call(
        functools.partial(_merge_kernel, n_steps=n_steps),
        grid=(n_steps + n_zero_steps,),
        in_specs=in_specs,
        out_specs=[row(D_MODEL),
                   pl.BlockSpec((tm,) + ROW_TILE, lambda i: (i + row_block0, 0, 0)),
                   pl.BlockSpec((SUBLANES, tm), lambda i: (0, jnp.minimum(i, n_steps - 1))),
                   row(LANES)],
        out_shape=[
            jax.ShapeDtypeStruct((rows, D_MODEL), F32),
            jax.ShapeDtypeStruct((n_all,) + ROW_TILE, F32),
            jax.ShapeDtypeStruct((SUBLANES, rows), jnp.int32),
            jax.ShapeDtypeStruct((rows, LANES), F32),
        ],
        input_output_aliases=aliases,
        compiler_params=_cparams(1),
        name="merge_router",
    )(*args)


UP_TILE = 256


def _deinterleave_matrix():
    half = UP_TILE // 2
    r = lax.broadcasted_iota(jnp.int32, (UP_TILE, UP_TILE), 0)
    c = lax.broadcasted_iota(jnp.int32, (UP_TILE, UP_TILE), 1)
    src = jnp.where(c < half, 2 * c, 2 * (c - half) + 1)
    return jnp.where(r == src, 1.0, 0.0).astype(BF16)


GATHER_AHEAD = 2
GATHER_DMA_PRIORITY = 0
OTHER_DMA_PRIORITY = 1


def _moe_kernel(be_ref, nxt_ref, nval_ref, nu_ref, tok0_ref, tok1_ref, tok2_ref, asg_ref, hn_hbm,
                wu_hbm, bu_ref, wd_hbm, bd_ref, out_hbm, xbuf, xb_s, ystage, wu_f, wd_f, wu_s, wd_s,
                sems, ssems, wsems):
    i = pl.program_id(0)
    nu = nu_ref[0]
    half = UP_TILE // 2
    n_up = 2 * D_FF // UP_TILE
    slot = i % GATHER_AHEAD
    prev_slot = 1 - slot
    n_prev = nval_ref[i]

    def row_copy(tok_ref, r, s):
        return pltpu.make_async_copy(hn_hbm.at[pl.ds(tok_ref[0, 0, r], 1)],
                                     xbuf.at[s, pl.ds(r, 1)], sems.at[s])

    def weight_copies(e):
        return (pltpu.make_async_copy(wu_hbm.at[e], wu_f, wsems.at[0]),
                pltpu.make_async_copy(wd_hbm.at[e], wd_f, wsems.at[1]))

    def out_copy(r):
        return pltpu.make_async_copy(ystage.at[prev_slot, pl.ds(r, 1)],
                                     out_hbm.at[pl.ds(asg_ref[0, 0, r], 1)], ssems.at[prev_slot])

    @pl.when(i == 0)
    def _():
        for cp in weight_copies(be_ref[0]):
            cp.start(priority=OTHER_DMA_PRIORITY)

        def issue(r, c):
            row_copy(tok0_ref, r, 0).start(priority=GATHER_DMA_PRIORITY)
            row_copy(tok1_ref, r, 1).start(priority=GATHER_DMA_PRIORITY)
            return c

        lax.fori_loop(0, TM_MOE, issue, 0)

    @pl.when(i < nu + GATHER_AHEAD)
    def _():
        pltpu.make_async_copy(hn_hbm.at[pl.ds(0, TM_MOE)], xbuf.at[slot], sems.at[slot]).wait()

    @pl.when((i >= 2) & (i < nu + 2))
    def _():
        n_sent = nval_ref[i - 1]
        n_tiled = pl.multiple_of((n_sent // SUBLANES) * SUBLANES, SUBLANES)

        @pl.when(n_tiled > 0)
        def _():
            pltpu.make_async_copy(ystage.at[slot, pl.ds(0, n_tiled)],
                                  out_hbm.at[pl.ds(0, n_tiled)], ssems.at[slot]).wait()

        def wait_row(r, c):
            pltpu.make_async_copy(ystage.at[slot, pl.ds(0, 1)], out_hbm.at[pl.ds(0, 1)],
                                  ssems.at[slot]).wait()
            return c

        lax.fori_loop(0, n_sent - n_tiled, wait_row, 0)

    @pl.when(i < nu)
    def _():
        xb_s[...] = _load_row_tiled(xbuf.at[slot], TM_MOE).astype(BF16)

        @pl.when((i == 0) | (be_ref[i] != be_ref[jnp.maximum(i - 1, 0)]))
        def _():
            for cp in weight_copies(be_ref[i]):
                cp.wait()
            perm = _deinterleave_matrix()
            for c in range(n_up):
                cols = slice(c * UP_TILE, (c + 1) * UP_TILE)
                blk = wu_f[:, cols].astype(BF16)
                wu_s[:, cols] = jnp.dot(blk, perm, preferred_element_type=F32).astype(BF16)
            wd_s[...] = wd_f[...].astype(BF16)

            @pl.when(nxt_ref[i] >= 0)
            def _():
                for cp in weight_copies(nxt_ref[i]):
                    cp.start(priority=OTHER_DMA_PRIORITY)

    def send_some_rows():
        def send(r, c):
            out_copy(r).start(priority=OTHER_DMA_PRIORITY)
            return c

        lax.fori_loop(0, n_prev, send, 0)

    def run_block(send_all_rows):
        for r in range(TM_MOE):
            row_copy(tok2_ref, r, slot).start(priority=GATHER_DMA_PRIORITY)
            if send_all_rows:
                out_copy(r).start(priority=OTHER_DMA_PRIORITY)
        x = xb_s[...]
        z = jnp.dot(x, wu_s[...], preferred_element_type=F32) + bu_ref[0]
        zg = jnp.concatenate([z[:, c * UP_TILE:c * UP_TILE + half] for c in range(n_up)], axis=-1)
        zl = jnp.concatenate([z[:, c * UP_TILE + half:(c + 1) * UP_TILE] for c in range(n_up)],
                             axis=-1)
        xg = jnp.minimum(zg, SWIGLU_LIMIT)
        xl = jnp.clip(zl, -SWIGLU_LIMIT, SWIGLU_LIMIT)
        act = xg * jax.nn.sigmoid(SWIGLU_ALPHA * xg) * (xl + 1.0)
        y = jnp.dot(act.astype(BF16), wd_s[...], preferred_element_type=F32) + bd_ref[0]
        _store_row_tiled(y, ystage.at[slot])

    @pl.when((i < nu) & (n_prev == TM_MOE))
    def _():
        run_block(send_all_rows=True)

    @pl.when((i < nu) & (n_prev != TM_MOE))
    def _():
        send_some_rows()
        run_block(send_all_rows=False)

    @pl.when(i == nu)
    def _():
        send_some_rows()


def _moe(blk_expert, next_expert, n_valid, n_used, buf_tok, buf_asg, hn_all, w_up, b_up_perm,
         w_down, b_down, n_blk, n_assign):
    ew = lambda r, c: pl.BlockSpec((1, r, c), lambda i, be, nxt, nv, nu: (be[i], 0, 0))
    tok = lambda f: pl.BlockSpec((1, 1, TM_MOE), lambda i, be, nxt, nv, nu: (f(i), 0, 0),
                                 memory_space=pltpu.SMEM)
    hbm = pl.BlockSpec(memory_space=pl.ANY)
    return pl.pallas_call(
        _moe_kernel,
        grid_spec=pltpu.PrefetchScalarGridSpec(
            num_scalar_prefetch=4,
            grid=(n_blk,),
            in_specs=[
                tok(lambda i: 0), tok(lambda i: 1),
                tok(lambda i: jnp.minimum(i + GATHER_AHEAD, n_blk - 1)), tok(lambda i: i),
                hbm, hbm, ew(1, 2 * D_FF), hbm, ew(1, D_MODEL),
            ],
            out_specs=hbm,
            scratch_shapes=[pltpu.VMEM((GATHER_AHEAD, TM_MOE) + ROW_TILE, F32),
                            pltpu.VMEM((TM_MOE, D_MODEL), BF16),
                            pltpu.VMEM((2, TM_MOE) + ROW_TILE, F32),
                            pltpu.VMEM((D_MODEL, 2 * D_FF), F32),
                            pltpu.VMEM((D_FF, D_MODEL), F32),
                            pltpu.VMEM((D_MODEL, 2 * D_FF), BF16),
                            pltpu.VMEM((D_FF, D_MODEL), BF16),
                            pltpu.SemaphoreType.DMA((GATHER_AHEAD,)),
                            pltpu.SemaphoreType.DMA((2,)),
                            pltpu.SemaphoreType.DMA((2,))],
        ),
        out_shape=jax.ShapeDtypeStruct((n_assign,) + ROW_TILE, F32),
        compiler_params=_cparams(1),
        name="moe_experts",
    )(blk_expert, next_expert, n_valid, n_used, buf_tok, buf_tok, buf_tok, buf_asg, hn_all, w_up,
      b_up_perm, w_down, b_down)


def _final_kernel(x1_ref, tw_ref, nf_ref, *rest):
    y_refs, o_ref = rest[:TOP_K], rest[TOP_K]
    tw = tw_ref[...]
    x = x1_ref[...]
    for k in range(TOP_K):
        x = x + tw[:, k:k + 1] * _load_row_tiled(y_refs[k].at[0], x.shape[0])
    ms = jnp.mean(x * x, axis=-1, keepdims=True)
    o_ref[...] = x * lax.rsqrt(ms + EPS) * nf_ref[...]


def _final(x1, tw, nf, y_choice, row0):
    rows = x1.shape[0]
    tm = min(TM_FIN, rows)
    row_block0 = row0 // tm
    row = lambda w: pl.BlockSpec((tm, w), lambda i: (i, 0))
    y_spec = lambda k: pl.BlockSpec((1, tm) + ROW_TILE, lambda i: (k, i + row_block0, 0, 0))
    return pl.pallas_call(
        _final_kernel,
        grid=(rows // tm,),
        in_specs=[row(D_MODEL), row(LANES), _full((1, D_MODEL))]
        + [y_spec(k) for k in range(TOP_K)],
        out_specs=row(D_MODEL),
        out_shape=jax.ShapeDtypeStruct((rows, D_MODEL), F32),
        compiler_params=_cparams(1),
        name="combine_final",
    )(x1, tw, nf, *([y_choice] * TOP_K))


def _block_diag_tiles(w):
    per = GATE_TILE // GATE_BW
    w4 = w.reshape(N_GATE_TILES, per, GATE_BW, GATE_BW)
    eye = jnp.eye(per, dtype=w.dtype)
    return jnp.einsum("gacd,ab->gacbd", w4, eye).reshape(N_GATE_TILES, GATE_TILE, GATE_TILE)


def _route(top_i_groups, n_blk):
    e_ids = jnp.arange(N_EXPERTS, dtype=jnp.int32)
    n_tokens = sum(t.shape[1] for t in top_i_groups)
    a = n_tokens * TOP_K
    tables, cnts, bases = [], [], []
    tok0 = 0
    for ti in top_i_groups:
        n_g = ti.shape[1]
        bits = max(n_g - 1, 1).bit_length()
        keys = (ti << bits) | jnp.arange(n_g, dtype=jnp.int32)[None, :]
        tok = (jnp.sort(keys, axis=1) & ((1 << bits) - 1)) + tok0
        tables.append((tok + jnp.arange(TOP_K, dtype=jnp.int32)[:, None] * n_tokens).reshape(-1))
        cnts.append(jnp.sum((ti[:, :, None] == e_ids[None, None, :]).astype(jnp.int32), axis=1))
        bases += [tok0 * TOP_K + k * n_g for k in range(TOP_K)]
        tok0 += n_g
    table = jnp.concatenate(tables)
    cnt = jnp.concatenate(cnts, axis=0)
    col_off = jnp.cumsum(cnt, axis=1) - cnt
    col_cum = jnp.cumsum(cnt, axis=0) - cnt
    counts = jnp.sum(cnt, axis=0)
    col_shift = jnp.asarray(bases, jnp.int32)[:, None] + col_off - col_cum
    col_step = col_shift - jnp.concatenate([jnp.zeros((1, N_EXPERTS), jnp.int32), col_shift[:-1]])
    padded = ((counts + (TM_MOE - 1)) // TM_MOE) * TM_MOE
    ends = jnp.cumsum(padded).astype(jnp.int32)
    start_padded = ends - padded
    blk_start = jnp.arange(n_blk, dtype=jnp.int32) * TM_MOE
    blk_expert = jnp.minimum(
        jnp.sum((ends[None, :] <= blk_start[:, None]).astype(jnp.int32), axis=1), N_EXPERTS - 1)
    n_used = (ends[-1:] // TM_MOE).astype(jnp.int32)
    blk_onehot = blk_expert[:, None] == e_ids[None, :]

    def per_block(tab):
        return jnp.sum(jnp.where(blk_onehot, tab[..., None, :], 0), axis=-1).astype(jnp.int32)

    n_valid = jnp.clip(per_block(start_padded + counts) - blk_start, 0, TM_MOE).astype(jnp.int32)
    per_row = lambda v: jnp.repeat(v, TM_MOE, axis=-1)
    row_id = jnp.arange(n_blk * TM_MOE, dtype=jnp.int32)
    j = row_id - per_row(per_block(start_padded))
    pos = j + jnp.sum(jnp.where(j[None, :] >= per_row(per_block(col_cum)),
                                per_row(per_block(col_step)), 0), axis=0)
    pos = jnp.where(j < per_row(per_block(counts)), pos, row_id % a)
    buf_asg = table[pos]
    buf_tok = buf_asg % n_tokens
    n_valid = jnp.concatenate([jnp.zeros((1,), jnp.int32), n_valid])
    buf_asg = jnp.concatenate([jnp.zeros((TM_MOE,), jnp.int32), buf_asg])
    later_used = (e_ids[None, :] > e_ids[:, None]) & (counts[None, :] > 0)
    next_used = jnp.min(jnp.where(later_used, e_ids[None, :], N_EXPERTS), axis=1)
    next_used = jnp.where(next_used == N_EXPERTS, -1, next_used).astype(jnp.int32)
    return (buf_tok.reshape(n_blk, 1, TM_MOE), buf_asg.reshape(n_blk + 1, 1, TM_MOE),
            blk_expert.astype(jnp.int32), per_block(next_used), n_valid, n_used)


def kernel(x_prompt, x_sample, cache_k, cache_v, state_conv, state_h, meta_tokens, norm_mix, w_in, conv_w, conv_b, w_rg, b_rg, w_ig, b_ig, lru_lambda, attn_sinks, w_out, norm_ffn, w_router, b_router, w_up, b_up, w_down, b_down, norm_final):
    n_b, seq = x_prompt.shape[0], x_prompt.shape[1]
    n_s = x_sample.shape[0]
    w_buf = cache_k.shape[2]
    n_p = n_b * seq
    n_tok = n_p + n_s
    row1 = lambda v: v.reshape(1, -1)

    w_in_bf = w_in[0].astype(BF16)
    w_out_bf = w_out[0].astype(BF16)
    wg_tiles = jnp.concatenate([_block_diag_tiles(w_rg[0]), _block_diag_tiles(w_ig[0])],
                               axis=-1).astype(BF16)
    wr_pad = w_router[0].T.astype(BF16)
    br_pad = b_router[0].reshape(N_EXPERTS, 1)
    b_up_perm = jnp.swapaxes(b_up[0].reshape(N_EXPERTS, 2 * D_FF // UP_TILE, UP_TILE // 2, 2),
                             2, 3).reshape(N_EXPERTS, 1, 2 * D_FF)
    b_dn = b_down[0].reshape(N_EXPERTS, 1, D_MODEL)
    nm, nf, nfin = row1(norm_mix[0]), row1(norm_ffn[0]), row1(norm_final)
    cw, cb = conv_w[0], row1(conv_b[0])
    brg, big, lam = row1(b_rg[0]), row1(b_ig[0]), row1(lru_lambda[0])
    sinks = attn_sinks[0]
    rnn_w = (cw, cb, wg_tiles, brg, big, lam)
    rnn_w_specs = [_full((CONV_W, D_RNN)), _full((1, D_RNN)),
                   _full((N_GATE_TILES, GATE_TILE, 2 * GATE_TILE)),
                   _full((1, D_RNN)), _full((1, D_RNN)), _full((1, D_RNN))]

    xp2 = x_prompt.reshape(n_p, D_MODEL)
    xr_p, q_p, kv_p, gr_p, ga_p = _in_proj(xp2, nm, w_in_bf, TM_IN, BF16)
    x_sm = jnp.concatenate([x_sample.reshape(n_s, D_MODEL), meta_tokens], axis=0)
    xr_sm, q_sm, kv_sm, gr_sm, ga_sm = _in_proj(x_sm, nm, w_in_bf, n_s + N_META, F32)
    xr_s, q_s, kv_s, gr_s, ga_s = (t[:n_s] for t in (xr_sm, q_sm, kv_sm, gr_sm, ga_sm))
    xr_m, kv_m = xr_sm[n_s:], kv_sm[n_s:]

    h_meta = pl.pallas_call(
        _rnn_meta_kernel,
        in_specs=[_full((N_META, D_RNN))] + rnn_w_specs,
        out_specs=_full((1, D_RNN)),
        out_shape=jax.ShapeDtypeStruct((1, D_RNN), F32),
        grid=(1,),
        compiler_params=_cparams(1),
        name="rnn_meta",
    )(xr_m, *rnn_w)

    tt = TT_RNN
    blk3 = pl.BlockSpec((n_b, tt, D_RNN), lambda j: (0, j, 0))
    m_rnn_p, h_last_p = pl.pallas_call(
        functools.partial(_rnn_prompt_kernel, n_b=n_b, tt=tt),
        grid=(seq // tt,),
        in_specs=[blk3, blk3, _full((N_META, D_RNN)), _full((1, D_RNN))] + rnn_w_specs,
        out_specs=[blk3, _full((n_b, D_RNN))],
        out_shape=[jax.ShapeDtypeStruct((n_b, seq, D_RNN), BF16),
                   jax.ShapeDtypeStruct((n_b, D_RNN), F32)],
        scratch_shapes=[pltpu.VMEM((n_b, 8, D_RNN), F32), pltpu.VMEM((n_b, D_RNN), F32)],
        compiler_params=_cparams(1),
        name="rnn_prompt",
    )(xr_p.reshape(n_b, seq, D_RNN), gr_p.reshape(n_b, seq, D_RNN), xr_m, h_meta, *rnn_w)

    sc_t = jnp.swapaxes(state_conv[0], 0, 1)
    m_rnn_s, h_new_s = pl.pallas_call(
        _rnn_sample_kernel,
        grid=(1,),
        in_specs=[_full((n_s, D_RNN)), _full((n_s, D_RNN)), _full((CONV_W - 1, n_s, D_RNN)),
                  _full((n_s, D_RNN))] + rnn_w_specs,
        out_specs=[_full((n_s, D_RNN)), _full((n_s, D_RNN))],
        out_shape=[jax.ShapeDtypeStruct((n_s, D_RNN), BF16),
                   jax.ShapeDtypeStruct((n_s, D_RNN), F32)],
        compiler_params=_cparams(1),
        name="rnn_sample",
    )(xr_s, gr_s, sc_t, state_h[0], *rnn_w)

    n_blk_seq = seq // BLOCK
    kv_meta_blk = jnp.pad(kv_m, ((BLOCK - N_META, 0), (0, 0)))
    smem_spec = pl.BlockSpec(memory_space=pltpu.SMEM)
    rb = lambda w: pl.BlockSpec((BLOCK, w), lambda b, j: (b * n_blk_seq + j, 0))
    m_attn_p = pl.pallas_call(
        _attn_prompt_kernel,
        grid=(n_b, n_blk_seq),
        in_specs=[smem_spec, rb(D_ATTN), rb(2 * D_KV),
                  pl.BlockSpec((BLOCK, 2 * D_KV),
                               lambda b, j: (jnp.maximum(b * n_blk_seq + j - 1, 0), 0)),
                  _full((BLOCK, 2 * D_KV)),
                  pl.BlockSpec((1, N_HEADS, BLOCK, BLOCK),
                               lambda b, j: (jnp.minimum(j, 1), 0, 0, 0)),
                  rb(D_ATTN)],
        out_specs=rb(D_ATTN),
        out_shape=jax.ShapeDtypeStruct((n_p, D_ATTN), BF16),
        compiler_params=_cparams(2),
        name="attn_prompt",
    )(sinks, q_p, kv_p, kv_p, kv_meta_blk, _attn_bias_table(), ga_p)

    ck = cache_k[0].reshape(n_s, w_buf, D_KV)
    cv = cache_v[0].reshape(n_s, w_buf, D_KV)
    per_s = lambda a, b: pl.BlockSpec((SAMPLES_PER_STEP, a, b), lambda i: (i, 0, 0))
    m_attn_s = pl.pallas_call(
        _attn_sample_kernel,
        grid=(n_s // SAMPLES_PER_STEP,),
        in_specs=[smem_spec, per_s(N_HEADS, HEAD_DIM), per_s(1, 2 * D_KV), per_s(w_buf, D_KV),
                  per_s(w_buf, D_KV), per_s(N_HEADS, HEAD_DIM)],
        out_specs=per_s(N_HEADS, HEAD_DIM),
        out_shape=jax.ShapeDtypeStruct((n_s, N_HEADS, HEAD_DIM), BF16),
        compiler_params=_cparams(1),
        name="attn_sample",
    )(sinks, q_s.reshape(n_s, N_HEADS, HEAD_DIM), kv_s.reshape(n_s, 1, 2 * D_KV), ck, cv,
      ga_s.reshape(n_s, N_HEADS, HEAD_DIM)).reshape(n_s, D_ATTN)

    n_hn = n_p + -(-n_s // TM_MERGE) * TM_MERGE
    x1_p, hn_all, ti_p, tw_p = _merge(m_rnn_p.reshape(n_p, D_RNN), m_attn_p, xp2, w_out_bf, nf,
                                      wr_pad, br_pad, None, n_hn, 0, TM_MERGE)
    x1_s, hn_all, ti_s, tw_s = _merge(m_rnn_s, m_attn_s, x_sample.reshape(n_s, D_MODEL), w_out_bf,
                                      nf, wr_pad, br_pad, hn_all, n_hn, n_p // n_s, n_s)

    n_assign = n_tok * TOP_K
    n_blk = n_assign // TM_MOE + N_EXPERTS - 1 + GATHER_AHEAD
    buf_tok, buf_asg, blk_expert, next_expert, n_valid, n_used = _route(
        [ti_p[:TOP_K], ti_s[:TOP_K]], n_blk)

    y_tok = _moe(blk_expert, next_expert, n_valid, n_used, buf_tok, buf_asg, hn_all, w_up[0],
                 b_up_perm, w_down[0], b_dn, n_blk, n_assign).reshape((TOP_K, n_tok) + ROW_TILE)
    y_p = _final(x1_p, tw_p, nfin, y_tok, 0)
    y_s = _final(x1_s, tw_s, nfin, y_tok, n_p)

    kv_p3 = kv_p.reshape(n_b, seq, 2 * D_KV)
    w_p = min(WINDOW, seq + N_META)
    new_k_p = kv_p3[:, seq - w_p:, :D_KV].reshape(1, n_b, w_p, N_KV, HEAD_DIM)
    new_v_p = kv_p3[:, seq - w_p:, D_KV:].reshape(1, n_b, w_p, N_KV, HEAD_DIM)
    new_conv_p = xr_p.reshape(n_b, seq, D_RNN)[:, seq - (CONV_W - 1):][None]
    k_new = kv_s[:, :D_KV].reshape(n_s, 1, N_KV, HEAD_DIM)
    v_new = kv_s[:, D_KV:].reshape(n_s, 1, N_KV, HEAD_DIM)
    new_k_s = jnp.concatenate([cache_k[0], k_new], axis=1)[:, -w_buf:][None]
    new_v_s = jnp.concatenate([cache_v[0], v_new], axis=1)[:, -w_buf:][None]
    new_conv_s = jnp.concatenate([state_conv[0], xr_s[:, None, :]], axis=1)[:, -(CONV_W - 1):][None]
    return (y_p.reshape(n_b, seq, D_MODEL), y_s.reshape(n_s, 1, D_MODEL), new_k_p, new_v_p,
            new_conv_p, h_last_p[None], new_k_s, new_v_s, new_conv_s, h_new_s[None])
```

```python
import functools

import jax
import jax.numpy as jnp
import numpy as np
from jax import lax
from jax.experimental import pallas as pl
from jax.experimental.pallas import tpu as pltpu

F32 = jnp.float32
BF16 = jnp.bfloat16

D_MODEL = 1024
N_META = 16
D_RNN = 1024
N_GATE_BLOCKS = 16
GATE_BW = D_RNN // N_GATE_BLOCKS
CONV_W = 4
LRU_C = 8.0
N_HEADS = 16
HEAD_DIM = 64
N_KV = 2
GROUP = N_HEADS // N_KV
D_ATTN = N_HEADS * HEAD_DIM
D_KV = N_KV * HEAD_DIM
WINDOW = 128
BLOCK = 128
PAST_LEN = 16384
N_EXPERTS = 32
TOP_K = 4
D_FF = 1024
SWIGLU_LIMIT = 7.0
SWIGLU_ALPHA = 1.702
EPS = 1e-6

GATE_TILE = 256
N_GATE_TILES = D_RNN // GATE_TILE
LANES = 128
NEG_BIG = -1e30

TM_IN = 512
TT_RNN = 128
TM_MERGE = 256
TM_MOE = 256
TM_FIN = 256
VMEM_LIMIT = 56 * 1024 * 1024


def _cparams(n_grid_dims):
    return pltpu.CompilerParams(
        dimension_semantics=("arbitrary",) * n_grid_dims, vmem_limit_bytes=VMEM_LIMIT)


def _full(shape):
    return pl.BlockSpec(shape, lambda *_: (0,) * len(shape))


SUBLANES = 8
ROW_TILE = (SUBLANES, D_MODEL // SUBLANES)


def _transpose_vreg_group(v):
    sub = lax.broadcasted_iota(jnp.int32, v[0].shape, 0)
    v = list(v)
    d = SUBLANES // 2
    while d >= 1:
        hi = (sub & d) != 0
        nv = list(v)
        for a in range(SUBLANES):
            if a & d:
                continue
            b = a | d
            nv[a] = jnp.where(hi, pltpu.roll(v[b], d, axis=0), v[a])
            nv[b] = jnp.where(hi, v[b], pltpu.roll(v[a], SUBLANES - d, axis=0))
        v = nv
        d //= 2
    return v


def _store_row_tiled(x, dst):
    for g in range(x.shape[0] // SUBLANES):
        rows = slice(g * SUBLANES, (g + 1) * SUBLANES)
        w = _transpose_vreg_group([x[rows, s * LANES:(s + 1) * LANES] for s in range(SUBLANES)])
        for j in range(SUBLANES):
            dst[g * SUBLANES + j] = w[j]


def _load_row_tiled(src, n_rows):
    groups = []
    for g in range(n_rows // SUBLANES):
        v = _transpose_vreg_group([src[g * SUBLANES + j] for j in range(SUBLANES)])
        groups.append(jnp.concatenate(v, axis=-1))
    return jnp.concatenate(groups, axis=0)


def _in_proj_kernel(x_ref, g_ref, w_ref, xr_ref, q_ref, kv_ref, gr_ref, ga_ref):
    x = x_ref[...]
    ms = jnp.mean(x * x, axis=-1, keepdims=True)
    h = (x * lax.rsqrt(ms + EPS) * g_ref[...]).astype(BF16)

    def proj(lo, hi):
        return jnp.dot(h, w_ref[:, lo:hi], preferred_element_type=F32)

    o_q = D_RNN
    o_k = o_q + D_ATTN
    o_gr = o_k + 2 * D_KV
    o_ga = o_gr + D_RNN
    xr_ref[...] = proj(0, o_q)
    q_ref[...] = (proj(o_q, o_k) * (HEAD_DIM ** -0.5)).astype(BF16)
    kv_ref[...] = proj(o_k, o_gr)
    gr_ref[...] = proj(o_gr, o_ga).astype(gr_ref.dtype)
    ga_ref[...] = proj(o_ga, o_ga + D_ATTN).astype(ga_ref.dtype)


def _in_proj(x, g, w_in_bf, tm, gate_dtype):
    rows = x.shape[0]
    d_in = w_in_bf.shape[1]
    row = lambda w: pl.BlockSpec((tm, w), lambda i: (i, 0))
    return pl.pallas_call(
        _in_proj_kernel,
        grid=(rows // tm,),
        in_specs=[row(D_MODEL), _full((1, D_MODEL)), _full((D_MODEL, d_in))],
        out_specs=[row(D_RNN), row(D_ATTN), row(2 * D_KV), row(D_RNN), row(D_ATTN)],
        out_shape=[
            jax.ShapeDtypeStruct((rows, D_RNN), F32),
            jax.ShapeDtypeStruct((rows, D_ATTN), BF16),
            jax.ShapeDtypeStruct((rows, 2 * D_KV), F32),
            jax.ShapeDtypeStruct((rows, D_RNN), gate_dtype),
            jax.ShapeDtypeStruct((rows, D_ATTN), gate_dtype),
        ],
        compiler_params=_cparams(1),
        name="in_proj",
    )(x, g, w_in_bf)


def _softplus(x):
    return jnp.maximum(x, 0.0) + jnp.log1p(jnp.exp(-jnp.abs(x)))


def _lru_coeffs(u, wg_ref, brg, big, lam):
    ub = u.astype(BF16)
    sp = _softplus(-lam)
    a_parts, i_parts, m_parts = [], [], []
    for g in range(N_GATE_TILES):
        sl = slice(g * GATE_TILE, (g + 1) * GATE_TILE)
        zz = jnp.dot(ub[:, sl], wg_ref[g], preferred_element_type=F32)
        r = jax.nn.sigmoid(zz[:, :GATE_TILE] + brg[:, sl])
        i = jax.nn.sigmoid(zz[:, GATE_TILE:] + big[:, sl])
        log_a = (-LRU_C) * r * sp[:, sl]
        a = jnp.exp(log_a)
        m = jnp.sqrt(1.0 - a * a)
        a_parts.append(a)
        i_parts.append(i)
        m_parts.append(m)
    cat = lambda ps: jnp.concatenate(ps, axis=-1)
    return cat(a_parts), cat(i_parts), cat(m_parts)


def _conv(ext, t, cw, cb):
    out = cb
    for j in range(CONV_W):
        s = CONV_W - 1 - j
        out = out + cw[j:j + 1, :] * ext[8 - s:8 - s + t, :]
    return out


def _rnn_meta_kernel(xr_ref, cw_ref, cb_ref, wg_ref, brg_ref, big_ref, lam_ref, h_ref):
    x = xr_ref[...]
    ext = jnp.concatenate([jnp.zeros((8, D_RNN), F32), x], axis=0)
    u = _conv(ext, N_META, cw_ref[...], cb_ref[...])
    a, i, m = _lru_coeffs(u, wg_ref, brg_ref[...], big_ref[...], lam_ref[...])
    first = lax.broadcasted_iota(jnp.int32, (N_META, 1), 0) == 0
    b = jnp.where(first, 1.0, m) * i * u
    h = jnp.zeros((1, D_RNN), F32)
    for t in range(N_META):
        h = a[t:t + 1, :] * h + b[t:t + 1, :]
    h_ref[...] = h


def _linear_scan(a, b, h_prev):
    t, d = a.shape
    g = t // SUBLANES
    a3 = a.reshape(g, SUBLANES, d)
    b3 = b.reshape(g, SUBLANES, d)
    row = lax.broadcasted_iota(jnp.int32, (g, SUBLANES, d), 1)
    step = 1
    while step < SUBLANES:
        keep = row >= step
        a_up = jnp.where(keep, pltpu.roll(a3, step, axis=1), 1.0)
        b_up = jnp.where(keep, pltpu.roll(b3, step, axis=1), 0.0)
        b3 = a3 * b_up + b3
        a3 = a3 * a_up
        step *= 2
    hs = []
    h = h_prev
    for k in range(g):
        hk = a3[k] * h + b3[k]
        hs.append(hk)
        h = hk[SUBLANES - 1:SUBLANES, :]
    return jnp.concatenate(hs, axis=0), h


def _rnn_prompt_kernel(xr_ref, gr_ref, xm_ref, h0_ref, cw_ref, cb_ref, wg_ref, brg_ref, big_ref,
                       lam_ref, m_ref, hl_ref, halo_s, h_s, *, n_b, tt):
    j = pl.program_id(0)

    @pl.when(j == 0)
    def _():
        h_s[...] = jnp.broadcast_to(h0_ref[...], (n_b, D_RNN))
        for b in range(n_b):
            halo_s[b] = xm_ref[N_META - 8:N_META, :]

    cw = cw_ref[...]
    cb = cb_ref[...]

    def per_batch(b, c):
        x = xr_ref[b]
        ext = jnp.concatenate([halo_s[b], x], axis=0)
        halo_s[b] = x[tt - 8:tt, :]
        u = _conv(ext, tt, cw, cb)
        a, i, m = _lru_coeffs(u, wg_ref, brg_ref[...], big_ref[...], lam_ref[...])
        y, h = _linear_scan(a, m * i * u, h_s[pl.ds(b, 1), :])
        h_s[pl.ds(b, 1), :] = h
        m_ref[b] = (jax.nn.sigmoid(gr_ref[b].astype(F32)) * y).astype(BF16)
        return c

    lax.fori_loop(0, n_b, per_batch, 0)
    hl_ref[...] = h_s[...]


def _rnn_sample_kernel(xr_ref, gr_ref, sc_ref, h0_ref, cw_ref, cb_ref, wg_ref, brg_ref, big_ref,
                       lam_ref, m_ref, hn_ref):
    cw = cw_ref[...]
    u = cb_ref[...] + cw[CONV_W - 1:CONV_W, :] * xr_ref[...]
    for j in range(CONV_W - 1):
        u = u + cw[j:j + 1, :] * sc_ref[j]
    a, i, m = _lru_coeffs(u, wg_ref, brg_ref[...], big_ref[...], lam_ref[...])
    h = a * h0_ref[...] + m * i * u
    hn_ref[...] = h
    m_ref[...] = (jax.nn.sigmoid(gr_ref[...].astype(F32)) * h).astype(BF16)


def _alibi_slope(h):
    return 2.0 ** (-8.0 * (h + 1) / N_HEADS)


def _attn_bias_table():
    assert WINDOW == BLOCK
    qi = np.arange(BLOCK, dtype=np.int32)[:, None]
    ci = np.arange(BLOCK, dtype=np.int32)[None, :]
    own = ci <= qi
    dist = np.where(own, qi - ci, qi + BLOCK - ci).astype(np.float32)
    slopes = np.asarray([_alibi_slope(h) for h in range(N_HEADS)], np.float32)[:, None, None]
    ali = -slopes * dist[None]
    tables = [np.where((own | (ci >= c_min))[None], ali, np.float32(NEG_BIG))
              for c_min in (BLOCK - N_META, 0)]
    return jnp.asarray(np.stack(tables).astype(np.float32))


def _attn_prompt_kernel(sink_ref, q_ref, kc_ref, kp_ref, km_ref, bias_ref, ga_ref, o_ref):
    first = pl.program_id(1) == 0
    kv_prev = jnp.where(first, km_ref[...], kp_ref[...])
    kv = jnp.concatenate([kv_prev, kc_ref[...]], axis=0).astype(BF16)
    q = q_ref[...]
    qi = lax.broadcasted_iota(jnp.int32, (BLOCK, BLOCK), 0)
    ci = lax.broadcasted_iota(jnp.int32, (BLOCK, BLOCK), 1)
    own = ci <= qi
    zero = jnp.zeros((BLOCK, BLOCK), BF16)
    outs = []
    for h in range(N_HEADS):
        g = h // GROUP
        kh = kv[:, g * HEAD_DIM:(g + 1) * HEAD_DIM]
        vh = kv[:, D_KV + g * HEAD_DIM:D_KV + (g + 1) * HEAD_DIM]
        qh = q[:, h * HEAD_DIM:(h + 1) * HEAD_DIM]
        s2 = lax.dot_general(qh, kh, (((1,), (1,)), ((), ())), preferred_element_type=F32)
        s = jnp.where(own, s2[:, BLOCK:], s2[:, :BLOCK]) + bias_ref[0, h]
        sink = sink_ref[h]
        mx = jnp.maximum(jnp.max(s, axis=-1, keepdims=True), sink)
        p = jnp.exp(s - mx)
        den = jnp.sum(p, axis=-1, keepdims=True) + jnp.exp(sink - mx)
        pb = p.astype(BF16)
        p2 = jnp.concatenate([jnp.where(own, zero, pb), jnp.where(own, pb, zero)], axis=1)
        o = jnp.dot(p2, vh, preferred_element_type=F32)
        outs.append(o / den)
    y = jnp.concatenate(outs, axis=-1)
    o_ref[...] = (jax.nn.sigmoid(ga_ref[...].astype(F32)) * y).astype(BF16)


SAMPLES_PER_STEP = 8


def _attn_sample_kernel(sink_ref, q_ref, kvn_ref, ck_ref, cv_ref, ga_ref, o_ref):
    w_buf = ck_ref.shape[1]
    n_rows = 2 * N_HEADS
    n_blocks = 2 * N_KV
    ci = lax.broadcasted_iota(jnp.int32, (n_rows, w_buf), 1)
    dist = w_buf - ci
    valid = dist < WINDOW
    row = lax.broadcasted_iota(jnp.int32, (n_rows, 1), 0)
    slope = jnp.zeros((n_rows, 1), F32)
    sink = jnp.zeros((n_rows, 1), F32)
    for h in range(N_HEADS):
        slope = jnp.where(row % N_HEADS == h, _alibi_slope(h), slope)
        sink = jnp.where(row % N_HEADS == h, sink_ref[h], sink)
    bias = jnp.where(valid, -slope * dist.astype(F32), NEG_BIG)
    row_block = lax.broadcasted_iota(jnp.int32, (n_rows, n_blocks * HEAD_DIM), 0) // GROUP
    lane_block = lax.broadcasted_iota(jnp.int32, (n_rows, n_blocks * HEAD_DIM), 1) // HEAD_DIM
    own_block = row_block == lane_block
    for n in range(0, SAMPLES_PER_STEP, 2):
        q2 = jnp.concatenate([q_ref[n], q_ref[n + 1]], axis=0)
        qd = jnp.where(own_block, jnp.concatenate([q2] * n_blocks, axis=1), jnp.zeros((), BF16))
        kk = jnp.concatenate([ck_ref[n], ck_ref[n + 1]], axis=1).astype(BF16)
        vv = jnp.concatenate([cv_ref[n], cv_ref[n + 1]], axis=1).astype(BF16)
        kvn = [kvn_ref[n + t].astype(BF16).astype(F32) for t in range(2)]
        k_new = jnp.concatenate([kvn[0][:, :D_KV], kvn[1][:, :D_KV]], axis=1)
        v_new = jnp.concatenate([kvn[0][:, D_KV:], kvn[1][:, D_KV:]], axis=1)
        s = lax.dot_general(qd, kk, (((1,), (1,)), ((), ())), preferred_element_type=F32) + bias
        sn = jnp.sum(qd.astype(F32) * k_new, axis=-1, keepdims=True)
        mx = jnp.maximum(jnp.maximum(jnp.max(s, axis=-1, keepdims=True), sn), sink)
        p = jnp.exp(s - mx)
        pn = jnp.exp(sn - mx)
        den = jnp.sum(p, axis=-1, keepdims=True) + pn + jnp.exp(sink - mx)
        o = jnp.dot(p.astype(BF16), vv, preferred_element_type=F32)
        o = o + pn.astype(BF16).astype(F32) * v_new
        y = jnp.concatenate([o[j * GROUP:(j + 1) * GROUP, j * HEAD_DIM:(j + 1) * HEAD_DIM]
                             for j in range(n_blocks)], axis=0) / den
        for t in range(2):
            rows = slice(t * N_HEADS, (t + 1) * N_HEADS)
            o_ref[n + t] = (jax.nn.sigmoid(ga_ref[n + t].astype(F32)) * y[rows]).astype(BF16)


def _merge_kernel(mr_ref, ma_ref, x_ref, wo_ref, nf_ref, wr_ref, br_ref, *rest, n_steps):
    x1_ref, hn_ref, ti_ref, tw_ref = rest[-4:]

    @pl.when(pl.program_id(0) >= n_steps)
    def _():
        hn_ref[...] = jnp.zeros_like(hn_ref)

    @pl.when(pl.program_id(0) < n_steps)
    def _():
        _merge_tile(mr_ref, ma_ref, x_ref, wo_ref, nf_ref, wr_ref, br_ref, x1_ref, hn_ref, ti_ref,
                    tw_ref)


def _merge_tile(mr_ref, ma_ref, x_ref, wo_ref, nf_ref, wr_ref, br_ref, x1_ref, hn_ref, ti_ref,
                tw_ref):
    mm = jnp.dot(mr_ref[...], wo_ref[:D_RNN, :], preferred_element_type=F32)
    mm = mm + jnp.dot(ma_ref[...], wo_ref[D_RNN:, :], preferred_element_type=F32)
    x1 = x_ref[...] + mm
    x1_ref[...] = x1
    ms = jnp.mean(x1 * x1, axis=-1, keepdims=True)
    hn = x1 * lax.rsqrt(ms + EPS) * nf_ref[...]
    _store_row_tiled(hn, hn_ref)
    tm = hn.shape[0]
    l = lax.dot_general(wr_ref[...], hn.astype(BF16), (((1,), (1,)), ((), ())),
                        preferred_element_type=F32) + br_ref[...]
    eid = lax.broadcasted_iota(jnp.int32, l.shape, 0)
    vals, idxs = [], []
    for k in range(TOP_K):
        mx = jnp.max(l, axis=0, keepdims=True)
        idx = jnp.min(jnp.where(l == mx, eid, N_EXPERTS), axis=0, keepdims=True)
        vals.append(mx)
        idxs.append(idx)
        l = jnp.where(eid == idx, NEG_BIG, l)
    es = [jnp.exp(v - vals[0]) for v in vals]
    tot = es[0] + es[1] + es[2] + es[3]
    pad = SUBLANES - TOP_K
    ti_ref[...] = jnp.concatenate(idxs + [jnp.zeros((pad, tm), jnp.int32)], axis=0)
    tw_t = jnp.concatenate([e / tot for e in es] + [jnp.zeros((pad, tm), F32)], axis=0)
    eye = (lax.broadcasted_iota(jnp.int32, (tm, tm), 0)
           == lax.broadcasted_iota(jnp.int32, (tm, tm), 1)).astype(F32)
    tw_rows = lax.dot_general(eye, tw_t, (((1,), (1,)), ((), ())), preferred_element_type=F32,
                              precision=lax.Precision.HIGHEST)
    tw_ref[...] = jnp.concatenate([tw_rows, jnp.zeros((tm, LANES - SUBLANES), F32)], axis=1)


def _merge(m_rnn, m_attn, x, w_out_bf, nf, wr_pad, br_pad, hn_all, n_all, row_block0, tm):
    rows = x.shape[0]
    n_steps = rows // tm
    n_zero_steps = 0 if hn_all is not None else n_all // tm - n_steps
    row = lambda w: pl.BlockSpec((tm, w), lambda i: (jnp.minimum(i, n_steps - 1), 0))
    in_specs = [row(D_RNN), row(D_ATTN), row(D_MODEL), _full((D_RNN + D_ATTN, D_MODEL)),
                _full((1, D_MODEL)), _full((N_EXPERTS, D_MODEL)), _full((N_EXPERTS, 1))]
    args = [m_rnn, m_attn, x, w_out_bf, nf, wr_pad, br_pad]
    aliases = {}
    if hn_all is not None:
        in_specs.append(pl.BlockSpec(memory_space=pl.ANY))
        args.append(hn_all)
        aliases = {len(args) - 1: 1}
    return pl.pallas_call(
        functools.partial(_merge_kernel, n_steps=n_steps),
        grid=(n_steps + n_zero_steps,),
        in_specs=in_specs,
        out_specs=[row(D_MODEL),
                   pl.BlockSpec((tm,) + ROW_TILE, lambda i: (i + row_block0, 0, 0)),
                   pl.BlockSpec((SUBLANES, tm), lambda i: (0, jnp.minimum(i, n_steps - 1))),
                   row(LANES)],
        out_shape=[
            jax.ShapeDtypeStruct((rows, D_MODEL), F32),
            jax.ShapeDtypeStruct((n_all,) + ROW_TILE, F32),
            jax.ShapeDtypeStruct((SUBLANES, rows), jnp.int32),
            jax.ShapeDtypeStruct((rows, LANES), F32),
        ],
        input_output_aliases=aliases,
        compiler_params=_cparams(1),
        name="merge_router",
    )(*args)


UP_TILE = 256


def _deinterleave_matrix():
    half = UP_TILE // 2
    r = lax.broadcasted_iota(jnp.int32, (UP_TILE, UP_TILE), 0)
    c = lax.broadcasted_iota(jnp.int32, (UP_TILE, UP_TILE), 1)
    src = jnp.where(c < half, 2 * c, 2 * (c - half) + 1)
    return jnp.where(r == src, 1.0, 0.0).astype(BF16)


GATHER_AHEAD = 2
GATHER_DMA_PRIORITY = 0
OTHER_DMA_PRIORITY = 1


def _moe_kernel(be_ref, nxt_ref, nval_ref, nu_ref, tok0_ref, tok1_ref, tok2_ref, asg_ref, hn_hbm,
                wu_hbm, bu_ref, wd_hbm, bd_ref, out_hbm, xbuf, xb_s, ystage, wu_f, wd_f, wu_s, wd_s,
                sems, ssems, wsems):
    i = pl.program_id(0)
    nu = nu_ref[0]
    half = UP_TILE // 2
    n_up = 2 * D_FF // UP_TILE
    slot = i % GATHER_AHEAD
    prev_slot = 1 - slot
    n_prev = nval_ref[i]

    def row_copy(tok_ref, r, s):
        return pltpu.make_async_copy(hn_hbm.at[pl.ds(tok_ref[0, 0, r], 1)],
                                     xbuf.at[s, pl.ds(r, 1)], sems.at[s])

    def weight_copies(e):
        return (pltpu.make_async_copy(wu_hbm.at[e], wu_f, wsems.at[0]),
                pltpu.make_async_copy(wd_hbm.at[e], wd_f, wsems.at[1]))

    def out_copy(r):
        return pltpu.make_async_copy(ystage.at[prev_slot, pl.ds(r, 1)],
                                     out_hbm.at[pl.ds(asg_ref[0, 0, r], 1)], ssems.at[prev_slot])

    @pl.when(i == 0)
    def _():
        for cp in weight_copies(be_ref[0]):
            cp.start(priority=OTHER_DMA_PRIORITY)

        def issue(r, c):
            row_copy(tok0_ref, r, 0).start(priority=GATHER_DMA_PRIORITY)
            row_copy(tok1_ref, r, 1).start(priority=GATHER_DMA_PRIORITY)
            return c

        lax.fori_loop(0, TM_MOE, issue, 0)

    @pl.when(i < nu + GATHER_AHEAD)
    def _():
        pltpu.make_async_copy(hn_hbm.at[pl.ds(0, TM_MOE)], xbuf.at[slot], sems.at[slot]).wait()

    @pl.when((i >= 2) & (i < nu + 2))
    def _():
        n_sent = nval_ref[i - 1]
        n_tiled = pl.multiple_of((n_sent // SUBLANES) * SUBLANES, SUBLANES)

        @pl.when(n_tiled > 0)
        def _():
            pltpu.make_async_copy(ystage.at[slot, pl.ds(0, n_tiled)],
                                  out_hbm.at[pl.ds(0, n_tiled)], ssems.at[slot]).wait()

        def wait_row(r, c):
            pltpu.make_async_copy(ystage.at[slot, pl.ds(0, 1)], out_hbm.at[pl.ds(0, 1)],
                                  ssems.at[slot]).wait()
            return c

        lax.fori_loop(0, n_sent - n_tiled, wait_row, 0)

    @pl.when(i < nu)
    def _():
        xb_s[...] = _load_row_tiled(xbuf.at[slot], TM_MOE).astype(BF16)

        @pl.when((i == 0) | (be_ref[i] != be_ref[jnp.maximum(i - 1, 0)]))
        def _():
            for cp in weight_copies(be_ref[i]):
                cp.wait()
            perm = _deinterleave_matrix()
            for c in range(n_up):
                cols = slice(c * UP_TILE, (c + 1) * UP_TILE)
                blk = wu_f[:, cols].astype(BF16)
                wu_s[:, cols] = jnp.dot(blk, perm, preferred_element_type=F32).astype(BF16)
            wd_s[...] = wd_f[...].astype(BF16)

            @pl.when(nxt_ref[i] >= 0)
            def _():
                for cp in weight_copies(nxt_ref[i]):
                    cp.start(priority=OTHER_DMA_PRIORITY)

    def send_some_rows():
        def send(r, c):
            out_copy(r).start(priority=OTHER_DMA_PRIORITY)
            return c

        lax.fori_loop(0, n_prev, send, 0)

    def run_block(send_all_rows):
        for r in range(TM_MOE):
            row_copy(tok2_ref, r, slot).start(priority=GATHER_DMA_PRIORITY)
            if send_all_rows:
                out_copy(r).start(priority=OTHER_DMA_PRIORITY)
        x = xb_s[...]
        z = jnp.dot(x, wu_s[...], preferred_element_type=F32) + bu_ref[0]
        zg = jnp.concatenate([z[:, c * UP_TILE:c * UP_TILE + half] for c in range(n_up)], axis=-1)
        zl = jnp.concatenate([z[:, c * UP_TILE + half:(c + 1) * UP_TILE] for c in range(n_up)],
                             axis=-1)
        xg = jnp.minimum(zg, SWIGLU_LIMIT)
        xl = jnp.clip(zl, -SWIGLU_LIMIT, SWIGLU_LIMIT)
        act = xg * jax.nn.sigmoid(SWIGLU_ALPHA * xg) * (xl + 1.0)
        y = jnp.dot(act.astype(BF16), wd_s[...], preferred_element_type=F32) + bd_ref[0]
        _store_row_tiled(y, ystage.at[slot])

    @pl.when((i < nu) & (n_prev == TM_MOE))
    def _():
        run_block(send_all_rows=True)

    @pl.when((i < nu) & (n_prev != TM_MOE))
    def _():
        send_some_rows()
        run_block(send_all_rows=False)

    @pl.when(i == nu)
    def _():
        send_some_rows()


def _moe(blk_expert, next_expert, n_valid, n_used, buf_tok, buf_asg, hn_all, w_up, b_up_perm,
         w_down, b_down, n_blk, n_assign):
    ew = lambda r, c: pl.BlockSpec((1, r, c), lambda i, be, nxt, nv, nu: (be[i], 0, 0))
    tok = lambda f: pl.BlockSpec((1, 1, TM_MOE), lambda i, be, nxt, nv, nu: (f(i), 0, 0),
                                 memory_space=pltpu.SMEM)
    hbm = pl.BlockSpec(memory_space=pl.ANY)
    return pl.pallas_call(
        _moe_kernel,
        grid_spec=pltpu.PrefetchScalarGridSpec(
            num_scalar_prefetch=4,
            grid=(n_blk,),
            in_specs=[
                tok(lambda i: 0), tok(lambda i: 1),
                tok(lambda i: jnp.minimum(i + GATHER_AHEAD, n_blk - 1)), tok(lambda i: i),
                hbm, hbm, ew(1, 2 * D_FF), hbm, ew(1, D_MODEL),
            ],
            out_specs=hbm,
            scratch_shapes=[pltpu.VMEM((GATHER_AHEAD, TM_MOE) + ROW_TILE, F32),
                            pltpu.VMEM((TM_MOE, D_MODEL), BF16),
                            pltpu.VMEM((2, TM_MOE) + ROW_TILE, F32),
                            pltpu.VMEM((D_MODEL, 2 * D_FF), F32),
                            pltpu.VMEM((D_FF, D_MODEL), F32),
                            pltpu.VMEM((D_MODEL, 2 * D_FF), BF16),
                            pltpu.VMEM((D_FF, D_MODEL), BF16),
                            pltpu.SemaphoreType.DMA((GATHER_AHEAD,)),
                            pltpu.SemaphoreType.DMA((2,)),
                            pltpu.SemaphoreType.DMA((2,))],
        ),
        out_shape=jax.ShapeDtypeStruct((n_assign,) + ROW_TILE, F32),
        compiler_params=_cparams(1),
        name="moe_experts",
    )(blk_expert, next_expert, n_valid, n_used, buf_tok, buf_tok, buf_tok, buf_asg, hn_all, w_up,
      b_up_perm, w_down, b_down)


FIN_BUFFERS = 3


def _final_kernel(x1_ref, tw_ref, nf_ref, y_hbm, o_ref, ybuf, sems, *, n_steps, row0):
    i = pl.program_id(0)
    tm = x1_ref.shape[0]

    def tile_copies(step, slot):
        start = row0 + step * tm
        if not isinstance(step, int):
            start = pl.multiple_of(start, tm)
        return [pltpu.make_async_copy(y_hbm.at[k, pl.ds(start, tm)], ybuf.at[slot, k],
                                      sems.at[slot]) for k in range(TOP_K)]

    @pl.when(i == 0)
    def _():
        for step in range(min(FIN_BUFFERS - 1, n_steps)):
            for cp in tile_copies(step, step):
                cp.start()

    ahead = i + FIN_BUFFERS - 1

    @pl.when(ahead < n_steps)
    def _():
        for cp in tile_copies(ahead, ahead % FIN_BUFFERS):
            cp.start()

    slot = i % FIN_BUFFERS
    for cp in tile_copies(i, slot):
        cp.wait()
    tw = tw_ref[...]
    x = x1_ref[...]
    for k in range(TOP_K):
        x = x + tw[:, k:k + 1] * _load_row_tiled(ybuf.at[slot, k], tm)
    ms = jnp.mean(x * x, axis=-1, keepdims=True)
    o_ref[...] = x * lax.rsqrt(ms + EPS) * nf_ref[...]


def _final(x1, tw, nf, y_choice, row0):
    rows = x1.shape[0]
    tm = min(TM_FIN, rows)
    n_steps = rows // tm
    row = lambda w: pl.BlockSpec((tm, w), lambda i: (i, 0))
    return pl.pallas_call(
        functools.partial(_final_kernel, n_steps=n_steps, row0=row0),
        grid=(n_steps,),
        in_specs=[row(D_MODEL), row(LANES), _full((1, D_MODEL)),
                  pl.BlockSpec(memory_space=pl.ANY)],
        out_specs=row(D_MODEL),
        out_shape=jax.ShapeDtypeStruct((rows, D_MODEL), F32),
        scratch_shapes=[pltpu.VMEM((FIN_BUFFERS, TOP_K, tm) + ROW_TILE, F32),
                        pltpu.SemaphoreType.DMA((FIN_BUFFERS,))],
        compiler_params=_cparams(1),
        name="combine_final",
    )(x1, tw, nf, y_choice)


def _block_diag_tiles(w):
    per = GATE_TILE // GATE_BW
    w4 = w.reshape(N_GATE_TILES, per, GATE_BW, GATE_BW)
    eye = jnp.eye(per, dtype=w.dtype)
    return jnp.einsum("gacd,ab->gacbd", w4, eye).reshape(N_GATE_TILES, GATE_TILE, GATE_TILE)


def _route(top_i_groups, n_blk):
    e_ids = jnp.arange(N_EXPERTS, dtype=jnp.int32)
    n_tokens = sum(t.shape[1] for t in top_i_groups)
    a = n_tokens * TOP_K
    tables, cnts, bases = [], [], []
    tok0 = 0
    for ti in top_i_groups:
        n_g = ti.shape[1]
        bits = max(n_g - 1, 1).bit_length()
        keys = (ti << bits) | jnp.arange(n_g, dtype=jnp.int32)[None, :]
        tok = (jnp.sort(keys, axis=1) & ((1 << bits) - 1)) + tok0
        tables.append((tok + jnp.arange(TOP_K, dtype=jnp.int32)[:, None] * n_tokens).reshape(-1))
        cnts.append(jnp.sum((ti[:, :, None] == e_ids[None, None, :]).astype(jnp.int32), axis=1))
        bases += [tok0 * TOP_K + k * n_g for k in range(TOP_K)]
        tok0 += n_g
    table = jnp.concatenate(tables)
    cnt = jnp.concatenate(cnts, axis=0)
    col_off = jnp.cumsum(cnt, axis=1) - cnt
    col_cum = jnp.cumsum(cnt, axis=0) - cnt
    counts = jnp.sum(cnt, axis=0)
    col_shift = jnp.asarray(bases, jnp.int32)[:, None] + col_off - col_cum
    col_step = col_shift - jnp.concatenate([jnp.zeros((1, N_EXPERTS), jnp.int32), col_shift[:-1]])
    padded = ((counts + (TM_MOE - 1)) // TM_MOE) * TM_MOE
    ends = jnp.cumsum(padded).astype(jnp.int32)
    start_padded = ends - padded
    blk_start = jnp.arange(n_blk, dtype=jnp.int32) * TM_MOE
    blk_expert = jnp.minimum(
        jnp.sum((ends[None, :] <= blk_start[:, None]).astype(jnp.int32), axis=1), N_EXPERTS - 1)
    n_used = (ends[-1:] // TM_MOE).astype(jnp.int32)
    blk_onehot = blk_expert[:, None] == e_ids[None, :]

    def per_block(tab):
        return jnp.sum(jnp.where(blk_onehot, tab[..., None, :], 0), axis=-1).astype(jnp.int32)

    n_valid = jnp.clip(per_block(start_padded + counts) - blk_start, 0, TM_MOE).astype(jnp.int32)
    per_row = lambda v: jnp.repeat(v, TM_MOE, axis=-1)
    row_id = jnp.arange(n_blk * TM_MOE, dtype=jnp.int32)
    j = row_id - per_row(per_block(start_padded))
    pos = j + jnp.sum(jnp.where(j[None, :] >= per_row(per_block(col_cum)),
                                per_row(per_block(col_step)), 0), axis=0)
    pos = jnp.where(j < per_row(per_block(counts)), pos, row_id % a)
    buf_asg = table[pos]
    buf_tok = buf_asg % n_tokens
    n_valid = jnp.concatenate([jnp.zeros((1,), jnp.int32), n_valid])
    buf_asg = jnp.concatenate([jnp.zeros((TM_MOE,), jnp.int32), buf_asg])
    later_used = (e_ids[None, :] > e_ids[:, None]) & (counts[None, :] > 0)
    next_used = jnp.min(jnp.where(later_used, e_ids[None, :], N_EXPERTS), axis=1)
    next_used = jnp.where(next_used == N_EXPERTS, -1, next_used).astype(jnp.int32)
    return (buf_tok.reshape(n_blk, 1, TM_MOE), buf_asg.reshape(n_blk + 1, 1, TM_MOE),
            blk_expert.astype(jnp.int32), per_block(next_used), n_valid, n_used)


def kernel(x_prompt, x_sample, cache_k, cache_v, state_conv, state_h, meta_tokens, norm_mix, w_in, conv_w, conv_b, w_rg, b_rg, w_ig, b_ig, lru_lambda, attn_sinks, w_out, norm_ffn, w_router, b_router, w_up, b_up, w_down, b_down, norm_final):
    n_b, seq = x_prompt.shape[0], x_prompt.shape[1]
    n_s = x_sample.shape[0]
    w_buf = cache_k.shape[2]
    n_p = n_b * seq
    n_tok = n_p + n_s
    row1 = lambda v: v.reshape(1, -1)

    w_in_bf = w_in[0].astype(BF16)
    w_out_bf = w_out[0].astype(BF16)
    wg_tiles = jnp.concatenate([_block_diag_tiles(w_rg[0]), _block_diag_tiles(w_ig[0])],
                               axis=-1).astype(BF16)
    wr_pad = w_router[0].T.astype(BF16)
    br_pad = b_router[0].reshape(N_EXPERTS, 1)
    b_up_perm = jnp.swapaxes(b_up[0].reshape(N_EXPERTS, 2 * D_FF // UP_TILE, UP_TILE // 2, 2),
                             2, 3).reshape(N_EXPERTS, 1, 2 * D_FF)
    b_dn = b_down[0].reshape(N_EXPERTS, 1, D_MODEL)
    nm, nf, nfin = row1(norm_mix[0]), row1(norm_ffn[0]), row1(norm_final)
    cw, cb = conv_w[0], row1(conv_b[0])
    brg, big, lam = row1(b_rg[0]), row1(b_ig[0]), row1(lru_lambda[0])
    sinks = attn_sinks[0]
    rnn_w = (cw, cb, wg_tiles, brg, big, lam)
    rnn_w_specs = [_full((CONV_W, D_RNN)), _full((1, D_RNN)),
                   _full((N_GATE_TILES, GATE_TILE, 2 * GATE_TILE)),
                   _full((1, D_RNN)), _full((1, D_RNN)), _full((1, D_RNN))]

    xp2 = x_prompt.reshape(n_p, D_MODEL)
    xr_p, q_p, kv_p, gr_p, ga_p = _in_proj(xp2, nm, w_in_bf, TM_IN, BF16)
    x_sm = jnp.concatenate([x_sample.reshape(n_s, D_MODEL), meta_tokens], axis=0)
    xr_sm, q_sm, kv_sm, gr_sm, ga_sm = _in_proj(x_sm, nm, w_in_bf, n_s + N_META, F32)
    xr_s, q_s, kv_s, gr_s, ga_s = (t[:n_s] for t in (xr_sm, q_sm, kv_sm, gr_sm, ga_sm))
    xr_m, kv_m = xr_sm[n_s:], kv_sm[n_s:]

    h_meta = pl.pallas_call(
        _rnn_meta_kernel,
        in_specs=[_full((N_META, D_RNN))] + rnn_w_specs,
        out_specs=_full((1, D_RNN)),
        out_shape=jax.ShapeDtypeStruct((1, D_RNN), F32),
        grid=(1,),
        compiler_params=_cparams(1),
        name="rnn_meta",
    )(xr_m, *rnn_w)

    tt = TT_RNN
    blk3 = pl.BlockSpec((n_b, tt, D_RNN), lambda j: (0, j, 0))
    m_rnn_p, h_last_p = pl.pallas_call(
        functools.partial(_rnn_prompt_kernel, n_b=n_b, tt=tt),
        grid=(seq // tt,),
        in_specs=[blk3, blk3, _full((N_META, D_RNN)), _full((1, D_RNN))] + rnn_w_specs,
        out_specs=[blk3, _full((n_b, D_RNN))],
        out_shape=[jax.ShapeDtypeStruct((n_b, seq, D_RNN), BF16),
                   jax.ShapeDtypeStruct((n_b, D_RNN), F32)],
        scratch_shapes=[pltpu.VMEM((n_b, 8, D_RNN), F32), pltpu.VMEM((n_b, D_RNN), F32)],
        compiler_params=_cparams(1),
        name="rnn_prompt",
    )(xr_p.reshape(n_b, seq, D_RNN), gr_p.reshape(n_b, seq, D_RNN), xr_m, h_meta, *rnn_w)

    sc_t = jnp.swapaxes(state_conv[0], 0, 1)
    m_rnn_s, h_new_s = pl.pallas_call(
        _rnn_sample_kernel,
        grid=(1,),
        in_specs=[_full((n_s, D_RNN)), _full((n_s, D_RNN)), _full((CONV_W - 1, n_s, D_RNN)),
                  _full((n_s, D_RNN))] + rnn_w_specs,
        out_specs=[_full((n_s, D_RNN)), _full((n_s, D_RNN))],
        out_shape=[jax.ShapeDtypeStruct((n_s, D_RNN), BF16),
                   jax.ShapeDtypeStruct((n_s, D_RNN), F32)],
        compiler_params=_cparams(1),
        name="rnn_sample",
    )(xr_s, gr_s, sc_t, state_h[0], *rnn_w)

    n_blk_seq = seq // BLOCK
    kv_meta_blk = jnp.pad(kv_m, ((BLOCK - N_META, 0), (0, 0)))
    smem_spec = pl.BlockSpec(memory_space=pltpu.SMEM)
    rb = lambda w: pl.BlockSpec((BLOCK, w), lambda b, j: (b * n_blk_seq + j, 0))
    m_attn_p = pl.pallas_call(
        _attn_prompt_kernel,
        grid=(n_b, n_blk_seq),
        in_specs=[smem_spec, rb(D_ATTN), rb(2 * D_KV),
                  pl.BlockSpec((BLOCK, 2 * D_KV),
                               lambda b, j: (jnp.maximum(b * n_blk_seq + j - 1, 0), 0)),
                  _full((BLOCK, 2 * D_KV)),
                  pl.BlockSpec((1, N_HEADS, BLOCK, BLOCK),
                               lambda b, j: (jnp.minimum(j, 1), 0, 0, 0)),
                  rb(D_ATTN)],
        out_specs=rb(D_ATTN),
        out_shape=jax.ShapeDtypeStruct((n_p, D_ATTN), BF16),
        compiler_params=_cparams(2),
        name="attn_prompt",
    )(sinks, q_p, kv_p, kv_p, kv_meta_blk, _attn_bias_table(), ga_p)

    ck = cache_k[0].reshape(n_s, w_buf, D_KV)
    cv = cache_v[0].reshape(n_s, w_buf, D_KV)
    per_s = lambda a, b: pl.BlockSpec((SAMPLES_PER_STEP, a, b), lambda i: (i, 0, 0))
    m_attn_s = pl.pallas_call(
        _attn_sample_kernel,
        grid=(n_s // SAMPLES_PER_STEP,),
        in_specs=[smem_spec, per_s(N_HEADS, HEAD_DIM), per_s(1, 2 * D_KV), per_s(w_buf, D_KV),
                  per_s(w_buf, D_KV), per_s(N_HEADS, HEAD_DIM)],
        out_specs=per_s(N_HEADS, HEAD_DIM),
        out_shape=jax.ShapeDtypeStruct((n_s, N_HEADS, HEAD_DIM), BF16),
        compiler_params=_cparams(1),
        name="attn_sample",
    )(sinks, q_s.reshape(n_s, N_HEADS, HEAD_DIM), kv_s.reshape(n_s, 1, 2 * D_KV), ck, cv,
      ga_s.reshape(n_s, N_HEADS, HEAD_DIM)).reshape(n_s, D_ATTN)

    n_hn = n_p + -(-n_s // TM_MERGE) * TM_MERGE
    x1_p, hn_all, ti_p, tw_p = _merge(m_rnn_p.reshape(n_p, D_RNN), m_attn_p, xp2, w_out_bf, nf,
                                      wr_pad, br_pad, None, n_hn, 0, TM_MERGE)
    x1_s, hn_all, ti_s, tw_s = _merge(m_rnn_s, m_attn_s, x_sample.reshape(n_s, D_MODEL), w_out_bf,
                                      nf, wr_pad, br_pad, hn_all, n_hn, n_p // n_s, n_s)

    n_assign = n_tok * TOP_K
    n_blk = n_assign // TM_MOE + N_EXPERTS - 1 + GATHER_AHEAD
    buf_tok, buf_asg, blk_expert, next_expert, n_valid, n_used = _route(
        [ti_p[:TOP_K], ti_s[:TOP_K]], n_blk)

    y_tok = _moe(blk_expert, next_expert, n_valid, n_used, buf_tok, buf_asg, hn_all, w_up[0],
                 b_up_perm, w_down[0], b_dn, n_blk, n_assign).reshape((TOP_K, n_tok) + ROW_TILE)
    y_p = _final(x1_p, tw_p, nfin, y_tok, 0)
    y_s = _final(x1_s, tw_s, nfin, y_tok, n_p)

    kv_p3 = kv_p.reshape(n_b, seq, 2 * D_KV)
    w_p = min(WINDOW, seq + N_META)
    new_k_p = kv_p3[:, seq - w_p:, :D_KV].reshape(1, n_b, w_p, N_KV, HEAD_DIM)
    new_v_p = kv_p3[:, seq - w_p:, D_KV:].reshape(1, n_b, w_p, N_KV, HEAD_DIM)
    new_conv_p = xr_p.reshape(n_b, seq, D_RNN)[:, seq - (CONV_W - 1):][None]
    k_new = kv_s[:, :D_KV].reshape(n_s, 1, N_KV, HEAD_DIM)
    v_new = kv_s[:, D_KV:].reshape(n_s, 1, N_KV, HEAD_DIM)
    new_k_s = jnp.concatenate([cache_k[0], k_new], axis=1)[:, -w_buf:][None]
    new_v_s = jnp.concatenate([cache_v[0], v_new], axis=1)[:, -w_buf:][None]
    new_conv_s = jnp.concatenate([state_conv[0], xr_s[:, None, :]], axis=1)[:, -(CONV_W - 1):][None]
    return (y_p.reshape(n_b, seq, D_MODEL), y_s.reshape(n_s, 1, D_MODEL), new_k_p, new_v_p,
            new_conv_p, h_last_p[None], new_k_s, new_v_s, new_conv_s, h_new_s[None])
```
